```python
import math
import jax, jax.numpy as jnp
from jax import lax
import numpy as np

D_MODEL = 1024
BATCH = 2
SEQ = 8192
DEPTH = 1

MIX_WIDTH = D_MODEL
ATTN_WIDTH = MIX_WIDTH // 2
CONV_WIDTH = MIX_WIDTH - ATTN_WIDTH
N_DIFF_HEADS = 4
DIFF_HEAD_DIM = ATTN_WIDTH // (2 * N_DIFF_HEADS)
V_HEAD_DIM = 2 * DIFF_HEAD_DIM
IN_PROJ_WIDTH = 3 * ATTN_WIDTH + 3 * CONV_WIDTH
CONV_K = 3
N_BUCKETS = 32
MAX_DISTANCE = 128
Q_BLOCK = 128
N_GROUPS = 4
EXPERTS_PER_GROUP = 4
N_EXPERTS = N_GROUPS * EXPERTS_PER_GROUP
TOP_K = 2
D_EXPERT = D_MODEL // 2
NORM_EPS = 1e-6
SUBLN_EPS = 1e-5
NEG_INF = -1e30

kernel_name = "hybrid_diffattn_shortconv_hmoe_adaln"


def rmsnorm(x, g, eps=NORM_EPS):
    xf = x.astype(jnp.float32)
    y = xf * lax.rsqrt(jnp.mean(xf * xf, axis=-1, keepdims=True) + eps) * g.astype(jnp.float32)
    return y.astype(x.dtype)


def lambda_init_fn(layer_idx):
    return 0.8 - 0.6 * math.exp(-0.3 * layer_idx)


def rel_bucket(n):
    n = jnp.maximum(n, 0)
    max_exact = N_BUCKETS // 2
    is_small = n < max_exact
    nf = jnp.maximum(n, 1).astype(jnp.float32)
    large = max_exact + (jnp.log(nf / max_exact) / math.log(MAX_DISTANCE / max_exact)
                         * (N_BUCKETS - max_exact)).astype(jnp.int32)
    large = jnp.minimum(large, N_BUCKETS - 1)
    return jnp.where(is_small, n, large)


def diff_attention(q, k, v, positions, rel_bias, lq1, lk1, lq2, lk2, subln_g, lam_init):
    B, S, _ = q.shape
    H2 = 2 * N_DIFF_HEADS
    qh = q.reshape(B, S, H2, DIFF_HEAD_DIM).transpose(0, 2, 1, 3).astype(jnp.float32)
    kh = k.reshape(B, S, H2, DIFF_HEAD_DIM).transpose(0, 2, 1, 3).astype(jnp.float32)
    vh = v.reshape(B, S, N_DIFF_HEADS, V_HEAD_DIM).transpose(0, 2, 1, 3).astype(jnp.float32)
    lam = (jnp.exp(jnp.sum(lq1.astype(jnp.float32) * lk1.astype(jnp.float32)))
           - jnp.exp(jnp.sum(lq2.astype(jnp.float32) * lk2.astype(jnp.float32))) + lam_init)
    scale = DIFF_HEAD_DIM ** -0.5
    nb = S // Q_BLOCK
    q_blk = qh.reshape(B, H2, nb, Q_BLOCK, DIFF_HEAD_DIM).transpose(2, 0, 1, 3, 4)
    pos_blk = positions.reshape(B, nb, Q_BLOCK).transpose(1, 0, 2)
    starts = jnp.arange(nb, dtype=jnp.int32) * Q_BLOCK
    kidx = jnp.arange(S, dtype=jnp.int32)
    bias_table = rel_bias.astype(jnp.float32).T

    def block(args):
        qb, pq, s0 = args
        logits = jnp.einsum('bhqd,bhkd->bhqk', qb, kh) * scale
        bucket = rel_bucket(pq[:, :, None] - positions[:, None, :])
        bias = jnp.take(bias_table, bucket, axis=1).transpose(1, 0, 2, 3)
        qidx = s0 + jnp.arange(Q_BLOCK, dtype=jnp.int32)
        causal = kidx[None, :] <= qidx[:, None]
        logits = jnp.where(causal[None, None], logits + bias, NEG_INF)
        p = jax.nn.softmax(logits, axis=-1).reshape(B, N_DIFF_HEADS, 2, Q_BLOCK, S)
        a = p[:, :, 0] - lam * p[:, :, 1]
        return jnp.einsum('bhqk,bhkv->bhqv', a, vh)

    o = lax.map(block, (q_blk, pos_blk, starts))
    o = o.transpose(1, 0, 3, 2, 4).reshape(B, S, N_DIFF_HEADS, V_HEAD_DIM)
    o = o * lax.rsqrt(jnp.mean(o * o, axis=-1, keepdims=True) + SUBLN_EPS)
    o = o * subln_g.astype(jnp.float32) * (1.0 - lam_init)
    return o.reshape(B, S, ATTN_WIDTH).astype(q.dtype)


def short_conv(gate_b, gate_c, h_in, conv_w):
    u = gate_c * h_in
    conv = lax.conv_general_dilated(
        u, conv_w[:, None, :].astype(u.dtype), window_strides=(1,), padding=[(CONV_K - 1, 0)],
        dimension_numbers=('NWC', 'WIO', 'NWC'), feature_group_count=CONV_WIDTH)
    return gate_b * conv


def hier_moe(h, rgw, rgb, rew, reb, w_gate, w_up, w_down):
    B, S, D = h.shape
    ht = h.reshape(B * S, D)
    glog = (ht @ rgw + rgb).astype(jnp.float32)
    pg = jax.nn.softmax(glog, axis=-1)
    gsel = jnp.argmax(glog, axis=-1)
    pg_sel = jnp.take_along_axis(pg, gsel[:, None], axis=1)[:, 0]
    elog = (ht @ rew + reb).astype(jnp.float32).reshape(-1, N_GROUPS, EXPERTS_PER_GROUP)
    sel = jnp.take_along_axis(elog, gsel[:, None, None], axis=1)[:, 0]
    vals, idx = lax.top_k(sel, TOP_K)
    w = jax.nn.softmax(vals, axis=-1) * pg_sel[:, None]
    eid = gsel[:, None] * EXPERTS_PER_GROUP + idx
    gates = jnp.einsum('tkn,tk->tn', jax.nn.one_hot(eid, N_EXPERTS, dtype=jnp.float32), w)
    y = jnp.zeros((B * S, D), jnp.float32)
    for e in range(N_EXPERTS):
        hid = jax.nn.silu(ht @ w_gate[e]) * (ht @ w_up[e])
        y = y + gates[:, e:e + 1] * (hid @ w_down[e]).astype(jnp.float32)
    return y.astype(h.dtype).reshape(B, S, D)


def setup_inputs(seed: int = 0) -> dict:
    key = jax.random.key(seed)
    ks = jax.random.split(key, 24)
    L, D = DEPTH, D_MODEL
    nrm = lambda k, shape, s: jax.random.normal(k, shape, jnp.float32) * s
    return {
        "x": nrm(ks[0], (BATCH, SEQ, D), 1.0),
        "c": nrm(ks[1], (BATCH, D), 1.0),
        "positions": (jnp.arange(SEQ, dtype=jnp.int32)[None, :]
                      + jax.random.randint(ks[2], (BATCH, 1), 0, 4096, dtype=jnp.int32)),
        "rel_bias": nrm(ks[3], (N_BUCKETS, 2 * N_DIFF_HEADS), 0.5),
        "ada_w": nrm(ks[4], (L, D, 6 * D), 0.5 * D ** -0.5),
        "ada_b": nrm(ks[5], (L, 6 * D), 0.02),
        "norm1_g": 1.0 + nrm(ks[6], (L, D), 0.02),
        "w_in": nrm(ks[7], (L, D, IN_PROJ_WIDTH), D ** -0.5),
        "lambda_q1": nrm(ks[8], (L, DIFF_HEAD_DIM), 0.1),
        "lambda_k1": nrm(ks[9], (L, DIFF_HEAD_DIM), 0.1),
        "lambda_q2": nrm(ks[10], (L, DIFF_HEAD_DIM), 0.1),
        "lambda_k2": nrm(ks[11], (L, DIFF_HEAD_DIM), 0.1),
        "subln_g": 1.0 + nrm(ks[12], (L, V_HEAD_DIM), 0.02),
        "conv_w": nrm(ks[13], (L, CONV_K, CONV_WIDTH), CONV_K ** -0.5),
        "w_out": nrm(ks[14], (L, MIX_WIDTH, D), MIX_WIDTH ** -0.5),
        "norm2_g": 1.0 + nrm(ks[15], (L, D), 0.02),
        "router_group_w": nrm(ks[16], (L, D, N_GROUPS), D ** -0.5),
        "router_group_b": nrm(ks[17], (L, N_GROUPS), 0.01),
        "router_expert_w": nrm(ks[18], (L, D, N_EXPERTS), D ** -0.5),
        "router_expert_b": nrm(ks[19], (L, N_EXPERTS), 0.01),
        "expert_w_gate": nrm(ks[20], (L, N_EXPERTS, D, D_EXPERT), D ** -0.5),
        "expert_w_up": nrm(ks[21], (L, N_EXPERTS, D, D_EXPERT), D ** -0.5),
        "expert_w_down": nrm(ks[22], (L, N_EXPERTS, D_EXPERT, D), D_EXPERT ** -0.5),
        "final_g": 1.0 + nrm(ks[23], (D,), 0.02),
    }


def reference(x, c, positions, rel_bias, ada_w, ada_b, norm1_g, w_in, lambda_q1, lambda_k1,
              lambda_q2, lambda_k2, subln_g, conv_w, w_out, norm2_g, router_group_w,
              router_group_b, router_expert_w, router_expert_b, expert_w_gate, expert_w_up,
              expert_w_down, final_g):
    A, C = ATTN_WIDTH, CONV_WIDTH
    split_pts = [A, 2 * A, 3 * A, 3 * A + C, 3 * A + 2 * C]
    for l in range(DEPTH):
        lam_init = lambda_init_fn(l)
        ada = jax.nn.silu(c) @ ada_w[l] + ada_b[l]
        sh1, sc1, g1, sh2, sc2, g2 = jnp.split(ada[:, None, :], 6, axis=-1)
        h = rmsnorm(x, norm1_g[l]) * (1.0 + sc1) + sh1
        proj = h @ w_in[l]
        q, k, v, gb, gc, hc = jnp.split(proj, split_pts, axis=-1)
        attn = diff_attention(q, k, v, positions, rel_bias, lambda_q1[l], lambda_k1[l],
                              lambda_q2[l], lambda_k2[l], subln_g[l], lam_init)
        conv = short_conv(gb, gc, hc, conv_w[l])
        mix = jnp.concatenate([attn, conv], axis=-1) @ w_out[l]
        x = x + g1 * mix
        h = rmsnorm(x, norm2_g[l]) * (1.0 + sc2) + sh2
        x = x + g2 * hier_moe(h, router_group_w[l], router_group_b[l], router_expert_w[l],
                              router_expert_b[l], expert_w_gate[l], expert_w_up[l],
                              expert_w_down[l])
    return rmsnorm(x, final_g)
```

```python
import functools
import math

import jax
import jax.numpy as jnp
from jax import lax
from jax.experimental import pallas as pl
from jax.experimental.pallas import tpu as pltpu

F32 = jnp.float32
BF16 = jnp.bfloat16
I32 = jnp.int32
U32 = jnp.uint32

D_MODEL = 1024
ATTN_WIDTH = 512
CONV_WIDTH = 512
N_DIFF_HEADS = 4
DIFF_HEAD_DIM = 64
V_HEAD_DIM = 128
IN_PROJ_WIDTH = 3 * ATTN_WIDTH + 3 * CONV_WIDTH
CONV_K = 3
N_BUCKETS = 32
MAX_DISTANCE = 128
N_GROUPS = 4
EXPERTS_PER_GROUP = 4
N_EXPERTS = 16
D_EXPERT = 512
NORM_EPS = 1e-6
SUBLN_EPS = 1e-5
NEG_INF = -1e30
LAMBDA_INIT = 0.8 - 0.6 * math.exp(-0.3 * 0)
QK_SCALE = DIFF_HEAD_DIM ** -0.5

BIAS_LUT = 128

ROW_TILE = 512
ATT_TQ = 256
ATT_TK = 256
MOE_TM = 256
COMB_TM = 256
ROUTER_ROWS = 32
VMEM_LIMIT = 56 * 1024 * 1024


def _silu(x):
    return x * (1.0 / (1.0 + jnp.exp(-x)))


def _ada_kernel(c_ref, w_ref, b_ref, o_ref):
    s = _silu(c_ref[...])
    o_ref[...] = jnp.dot(s, w_ref[...], preferred_element_type=F32,
                         precision=lax.Precision.HIGHEST) + b_ref[...]


def _ada(c_pad, w, b):
    n = w.shape[1]
    bn = 1024
    return pl.pallas_call(
        _ada_kernel,
        grid=(n // bn,),
        in_specs=[pl.BlockSpec((8, D_MODEL), lambda j: (0, 0)),
                  pl.BlockSpec((D_MODEL, bn), lambda j: (0, j)),
                  pl.BlockSpec((1, bn), lambda j: (0, j))],
        out_specs=pl.BlockSpec((8, bn), lambda j: (0, j)),
        out_shape=jax.ShapeDtypeStruct((8, n), F32),
        name="ada",
    )(c_pad, w, b)


def _inproj_kernel(x_ref, sc_ref, sh_ref, g_ref, w_ref, cw_ref,
                   q_ref, k_ref, v_ref, conv_ref, carry_ref):
    j = pl.program_id(1)
    tm = x_ref.shape[1]
    x = x_ref[0]
    ms = jnp.mean(x * x, axis=-1, keepdims=True)
    h = x * lax.rsqrt(ms + NORM_EPS) * g_ref[...]
    h = h * (1.0 + sc_ref[0]) + sh_ref[0]
    hb = h.astype(BF16)

    def proj(c0):
        return jnp.dot(hb, w_ref[:, c0:c0 + 512], preferred_element_type=F32)

    q_ref[0] = (proj(0) * QK_SCALE).astype(BF16)
    k_ref[0] = proj(512).astype(BF16)
    v_ref[0] = proj(1024).astype(BF16)
    gate_b = proj(1536)
    u = proj(2048) * proj(2560)

    @pl.when(j == 0)
    def _():
        carry_ref[...] = jnp.zeros_like(carry_ref)

    prev = carry_ref[...]
    row = lax.broadcasted_iota(I32, u.shape, 0)
    u1 = pltpu.roll(u, 1, axis=0)
    u2 = pltpu.roll(u, 2, axis=0)
    u1 = jnp.where(row == 0, prev[7:8, :], u1)
    u2 = jnp.where(row == 0, prev[6:7, :], jnp.where(row == 1, prev[7:8, :], u2))
    conv = cw_ref[0:1, :] * u2 + cw_ref[1:2, :] * u1 + cw_ref[2:3, :] * u
    conv_ref[0] = (gate_b * conv).astype(BF16)
    carry_ref[...] = u[tm - 8:tm, :]


def _inproj(x, sc1, sh1, g1n, w_in_bf, conv_w):
    B, S, D = x.shape
    tm = ROW_TILE
    out = jax.ShapeDtypeStruct((B, S, 512), BF16)
    row_spec = pl.BlockSpec((1, tm, 512), lambda b, j: (b, j, 0))
    mod_spec = pl.BlockSpec((1, 1, D), lambda b, j: (b, 0, 0))
    return pl.pallas_call(
        _inproj_kernel,
        grid=(B, S // tm),
        in_specs=[pl.BlockSpec((1, tm, D), lambda b, j: (b, j, 0)),
                  mod_spec, mod_spec,
                  pl.BlockSpec((1, D), lambda b, j: (0, 0)),
                  pl.BlockSpec((D, IN_PROJ_WIDTH), lambda b, j: (0, 0)),
                  pl.BlockSpec((CONV_K, CONV_WIDTH), lambda b, j: (0, 0))],
        out_specs=[row_spec, row_spec, row_spec, row_spec],
        out_shape=[out, out, out, out],
        scratch_shapes=[pltpu.VMEM((8, CONV_WIDTH), F32)],
        compiler_params=pltpu.CompilerParams(
            dimension_semantics=("arbitrary", "arbitrary"), vmem_limit_bytes=VMEM_LIMIT),
        name="inproj",
    )(x, sc1, sh1, g1n, w_in_bf, conv_w)


def _attn_kernel(qmin_ref, kmax_ref, q_ref, k_ref, v_ref, pc_ref, pr_ref, lut_ref, lam_ref, sg_ref,
                 o_ref):
    b = pl.program_id(0)
    qi = pl.program_id(2)
    nq = pl.num_programs(2)
    tq, tk = ATT_TQ, ATT_TK
    q = q_ref[0]
    q1 = q[:, :DIFF_HEAD_DIM]
    q2 = q[:, DIFF_HEAD_DIM:]
    lut1 = lut_ref[0, 0:1, :]
    lut2 = lut_ref[0, 1:2, :]
    far1 = lut1[:, BIAS_LUT - 1:BIAS_LUT]
    far2 = lut2[:, BIAS_LUT - 1:BIAS_LUT]
    pq = pc_ref[0]
    qmin = qmin_ref[b * nq + qi]
    nt = (((1,), (1,)), ((), ()))

    def logits(j):
        ks = pl.multiple_of(j * tk, tk)
        kb = k_ref[0, pl.ds(ks, tk), :]
        vb = v_ref[0, pl.ds(ks, tk), :]
        s1 = lax.dot_general(q1, kb[:, :DIFF_HEAD_DIM], nt, preferred_element_type=F32)
        s2 = lax.dot_general(q2, kb[:, DIFF_HEAD_DIM:], nt, preferred_element_type=F32)
        return s1, s2, vb, ks

    def near_bias(ks):
        pk = pr_ref[0, :, pl.ds(ks, tk)]
        dist = jnp.clip(pq - pk, 0, BIAS_LUT - 1)
        t1 = jnp.broadcast_to(lut1, (tq, BIAS_LUT))
        t2 = jnp.broadcast_to(lut2, (tq, BIAS_LUT))
        b1, b2 = [], []
        for c in range(tk // 128):
            idx = dist[:, c * 128:(c + 1) * 128]
            b1.append(jnp.take_along_axis(t1, idx, axis=1))
            b2.append(jnp.take_along_axis(t2, idx, axis=1))
        return jnp.concatenate(b1, axis=1), jnp.concatenate(b2, axis=1)

    def update(s, m, l, a, vb):
        mn = jnp.maximum(m, jnp.max(s, axis=-1, keepdims=True))
        alpha = jnp.exp(m - mn)
        p = jnp.exp(s - mn)
        l = alpha * l + jnp.sum(p, axis=-1, keepdims=True)
        a = alpha * a + jnp.dot(p.astype(BF16), vb, preferred_element_type=F32)
        return mn, l, a

    def body(j, carry):
        m1, l1, a1, m2, l2, a2 = carry
        s1, s2, vb, ks = logits(j)
        is_far = (qmin - kmax_ref[b * nq + j]) >= BIAS_LUT - 1

        def far_fn():
            return s1 + far1, s2 + far2

        def near_fn():
            b1, b2 = near_bias(ks)
            return s1 + b1, s2 + b2

        s1, s2 = lax.cond(is_far, far_fn, near_fn)
        m1, l1, a1 = update(s1, m1, l1, a1, vb)
        m2, l2, a2 = update(s2, m2, l2, a2, vb)
        return m1, l1, a1, m2, l2, a2

    m0 = jnp.full((tq, 1), NEG_INF, F32)
    l0 = jnp.zeros((tq, 1), F32)
    a0 = jnp.zeros((tq, V_HEAD_DIM), F32)
    carry = lax.fori_loop(0, qi, body, (m0, l0, a0, m0, l0, a0))

    m1, l1, a1, m2, l2, a2 = carry
    s1, s2, vb, ks = logits(qi)
    b1, b2 = near_bias(ks)
    causal = (lax.broadcasted_iota(I32, (tq, tk), 1) <= lax.broadcasted_iota(I32, (tq, tk), 0))
    s1 = jnp.where(causal, s1 + b1, NEG_INF)
    s2 = jnp.where(causal, s2 + b2, NEG_INF)
    m1, l1, a1 = update(s1, m1, l1, a1, vb)
    m2, l2, a2 = update(s2, m2, l2, a2, vb)

    lam = (jnp.exp(jnp.sum(lam_ref[0:1, :] * lam_ref[1:2, :], axis=-1, keepdims=True))
           - jnp.exp(jnp.sum(lam_ref[2:3, :] * lam_ref[3:4, :], axis=-1, keepdims=True))
           + LAMBDA_INIT)
    o = a1 / l1 - lam * (a2 / l2)
    o = o * lax.rsqrt(jnp.mean(o * o, axis=-1, keepdims=True) + SUBLN_EPS)
    o = o * sg_ref[...] * (1.0 - LAMBDA_INIT)
    o_ref[0] = o.astype(BF16)


def _attention(q, k, v, positions, lut, lam_params, subln_g):
    B, S, _ = q.shape
    tq = ATT_TQ
    nq = S // tq
    pos_col = positions.reshape(B, S, 1)
    pos_row = positions.reshape(B, 1, S)
    pos_blk = positions.reshape(B * nq, tq)
    qmin = jnp.min(pos_blk, axis=1)
    kmax = jnp.max(pos_blk, axis=1)
    grid_spec = pltpu.PrefetchScalarGridSpec(
        num_scalar_prefetch=2,
        grid=(B, N_DIFF_HEADS, nq),
        in_specs=[pl.BlockSpec((1, tq, 128), lambda b, h, i, *_: (b, i, h)),
                  pl.BlockSpec((1, S, 128), lambda b, h, i, *_: (b, 0, h)),
                  pl.BlockSpec((1, S, 128), lambda b, h, i, *_: (b, 0, h)),
                  pl.BlockSpec((1, tq, 1), lambda b, h, i, *_: (b, i, 0)),
                  pl.BlockSpec((1, 1, S), lambda b, h, i, *_: (b, 0, 0)),
                  pl.BlockSpec((1, 2, BIAS_LUT), lambda b, h, i, *_: (h, 0, 0)),
                  pl.BlockSpec((4, DIFF_HEAD_DIM), lambda b, h, i, *_: (0, 0)),
                  pl.BlockSpec((1, V_HEAD_DIM), lambda b, h, i, *_: (0, 0))],
        out_specs=pl.BlockSpec((1, tq, 128), lambda b, h, i, *_: (b, i, h)),
    )
    return pl.pallas_call(
        _attn_kernel,
        grid_spec=grid_spec,
        out_shape=jax.ShapeDtypeStruct((B, S, ATTN_WIDTH), BF16),
        compiler_params=pltpu.CompilerParams(
            dimension_semantics=("arbitrary", "arbitrary", "arbitrary"),
            vmem_limit_bytes=VMEM_LIMIT),
        name="diffattn",
    )(qmin, kmax, q, k, v, pos_col, pos_row, lut, lam_params, subln_g)


def _outproj_kernel(at_ref, cv_ref, x_ref, g1_ref, sc_ref, sh_ref, gn_ref, wo_ref, wr_ref, rb_ref,
                    x1_ref, hp_ref, ri_ref, rw_ref):
    tm = x_ref.shape[1]
    mix = (jnp.dot(at_ref[0], wo_ref[0:ATTN_WIDTH, :], preferred_element_type=F32)
           + jnp.dot(cv_ref[0], wo_ref[ATTN_WIDTH:, :], preferred_element_type=F32))
    x1 = x_ref[0] + g1_ref[0] * mix
    x1_ref[0] = x1
    ms = jnp.mean(x1 * x1, axis=-1, keepdims=True)
    h = x1 * lax.rsqrt(ms + NORM_EPS) * gn_ref[...]
    h = h * (1.0 + sc_ref[0]) + sh_ref[0]
    hb = h.astype(BF16)

    half = D_MODEL // 2
    lo = pltpu.bitcast(hb[:, :half].astype(F32), U32) >> 16
    hi = pltpu.bitcast(hb[:, half:].astype(F32), U32) & jnp.uint32(0xFFFF0000)
    hp_ref[0] = lo | hi

    lg_all = lax.dot_general(wr_ref[...], hb, (((1,), (1,)), ((), ())),
                             preferred_element_type=F32) + rb_ref[...]
    lg = lg_all[0:N_GROUPS, :]
    le = lg_all[N_GROUPS:N_GROUPS + N_EXPERTS, :]
    row4 = lax.broadcasted_iota(I32, (N_GROUPS, tm), 0)
    gmax = jnp.max(lg, axis=0, keepdims=True)
    pg_sel = 1.0 / jnp.sum(jnp.exp(lg - gmax), axis=0, keepdims=True)
    gsel = jnp.min(jnp.where(lg == gmax, row4, N_GROUPS), axis=0, keepdims=True)
    sel = jnp.zeros((EXPERTS_PER_GROUP, tm), F32)
    for g in range(N_GROUPS):
        sel = jnp.where(gsel == g, le[g * EXPERTS_PER_GROUP:(g + 1) * EXPERTS_PER_GROUP, :], sel)
    v1 = jnp.max(sel, axis=0, keepdims=True)
    i1 = jnp.min(jnp.where(sel == v1, row4, EXPERTS_PER_GROUP), axis=0, keepdims=True)
    rest = jnp.where(row4 == i1, -jnp.inf, sel)
    v2 = jnp.max(rest, axis=0, keepdims=True)
    i2 = jnp.min(jnp.where(rest == v2, row4, EXPERTS_PER_GROUP), axis=0, keepdims=True)
    e2 = jnp.exp(v2 - v1)
    w1 = pg_sel / (1.0 + e2)
    w2 = pg_sel * e2 / (1.0 + e2)
    row8 = lax.broadcasted_iota(I32, (8, tm), 0)
    eid1 = gsel * EXPERTS_PER_GROUP + i1
    eid2 = gsel * EXPERTS_PER_GROUP + i2
    ri_ref[0] = jnp.where(row8 == 0, eid1, jnp.where(row8 == 1, eid2, 0))
    rw_ref[0] = jnp.where(row8 == 0, w1, jnp.where(row8 == 1, w2, 0.0))


def _outproj(attn, conv, x, g1, sc2, sh2, g2n, w_out_bf, wr_t, rb):
    B, S, D = x.shape
    tm = ROW_TILE
    half_spec = pl.BlockSpec((1, tm, 512), lambda b, j: (b, j, 0))
    full_spec = pl.BlockSpec((1, tm, D), lambda b, j: (b, j, 0))
    mod_spec = pl.BlockSpec((1, 1, D), lambda b, j: (b, 0, 0))
    rt_spec = pl.BlockSpec((1, 8, tm), lambda b, j: (b, 0, j))
    return pl.pallas_call(
        _outproj_kernel,
        grid=(B, S // tm),
        in_specs=[half_spec, half_spec, full_spec, mod_spec, mod_spec, mod_spec,
                  pl.BlockSpec((1, D), lambda b, j: (0, 0)),
                  pl.BlockSpec((D, D), lambda b, j: (0, 0)),
                  pl.BlockSpec((ROUTER_ROWS, D), lambda b, j: (0, 0)),
                  pl.BlockSpec((ROUTER_ROWS, 1), lambda b, j: (0, 0))],
        out_specs=[full_spec, half_spec, rt_spec, rt_spec],
        out_shape=[jax.ShapeDtypeStruct((B, S, D), F32),
                   jax.ShapeDtypeStruct((B, S, 512), U32),
                   jax.ShapeDtypeStruct((B, 8, S), I32),
                   jax.ShapeDtypeStruct((B, 8, S), F32)],
        compiler_params=pltpu.CompilerParams(
            dimension_semantics=("arbitrary", "arbitrary"), vmem_limit_bytes=VMEM_LIMIT),
        name="outproj",
    )(attn, conv, x, g1, sc2, sh2, g2n, w_out_bf, wr_t, rb)


def _row_gather_start(src_hbm, idx_ref, dst, sem, n_rows):
    def body(r, c):
        t = idx_ref[0, 0, r]
        pltpu.make_async_copy(src_hbm.at[pl.ds(t, 1), :], dst.at[pl.ds(r, 1), :], sem).start()
        return c
    lax.fori_loop(0, n_rows, body, 0, unroll=8)


def _row_gather_wait(src_hbm, dst, sem, n_rows):
    pltpu.make_async_copy(src_hbm.at[pl.ds(0, n_rows), :], dst, sem).wait()


def _moe_kernel(te_ref, tv_ref, tok_ref, tokn_ref, hp_hbm, wg_ref, wu_ref, wd_ref, y_ref,
                xbuf, sem):
    i = pl.program_id(0)
    n = pl.num_programs(0)
    slot = lax.rem(i, 2)
    nslot = 1 - slot
    tm = MOE_TM

    @pl.when(i == 0)
    def _():
        _row_gather_start(hp_hbm, tok_ref, xbuf.at[0], sem.at[0], tm)

    nxt = jnp.minimum(i + 1, n - 1)

    @pl.when(jnp.logical_and(i + 1 < n, tv_ref[nxt] == 1))
    def _():
        _row_gather_start(hp_hbm, tokn_ref, xbuf.at[nslot], sem.at[nslot], tm)

    @pl.when(tv_ref[i] == 1)
    def _():
        _row_gather_wait(hp_hbm, xbuf.at[slot], sem.at[slot], tm)
        xp = xbuf[slot]
        half = D_MODEL // 2
        lo = pltpu.bitcast(xp << 16, F32).astype(BF16)
        hi = pltpu.bitcast(xp & jnp.uint32(0xFFFF0000), F32).astype(BF16)
        g = (jnp.dot(lo, wg_ref[0, 0:half, :], preferred_element_type=F32)
             + jnp.dot(hi, wg_ref[0, half:, :], preferred_element_type=F32))
        u = (jnp.dot(lo, wu_ref[0, 0:half, :], preferred_element_type=F32)
             + jnp.dot(hi, wu_ref[0, half:, :], preferred_element_type=F32))
        hid = (_silu(g) * u).astype(BF16)
        y_ref[...] = jnp.dot(hid, wd_ref[0], preferred_element_type=F32)

    @pl.when(tv_ref[i] == 0)
    def _():
        y_ref[...] = jnp.zeros_like(y_ref)


def _moe(tile_expert, tile_valid, slot_token, hp, wg_bf, wu_bf, wd_bf):
    nt = tile_expert.shape[0]
    tm = MOE_TM
    D = D_MODEL
    grid_spec = pltpu.PrefetchScalarGridSpec(
        num_scalar_prefetch=2,
        grid=(nt,),
        in_specs=[pl.BlockSpec((1, 1, tm), lambda i, te, tv: (i, 0, 0), memory_space=pltpu.SMEM),
                  pl.BlockSpec((1, 1, tm), lambda i, te, tv: (jnp.minimum(i + 1, nt - 1), 0, 0),
                               memory_space=pltpu.SMEM),
                  pl.BlockSpec(memory_space=pl.ANY),
                  pl.BlockSpec((1, D, D_EXPERT), lambda i, te, tv: (te[i], 0, 0)),
                  pl.BlockSpec((1, D, D_EXPERT), lambda i, te, tv: (te[i], 0, 0)),
                  pl.BlockSpec((1, D_EXPERT, D), lambda i, te, tv: (te[i], 0, 0))],
        out_specs=pl.BlockSpec((tm, D), lambda i, te, tv: (i, 0)),
        scratch_shapes=[pltpu.VMEM((2, tm, D // 2), U32),
                        pltpu.SemaphoreType.DMA((2,))],
    )
    return pl.pallas_call(
        _moe_kernel,
        grid_spec=grid_spec,
        out_shape=jax.ShapeDtypeStruct((nt * tm, D), F32),
        compiler_params=pltpu.CompilerParams(
            dimension_semantics=("arbitrary",), vmem_limit_bytes=VMEM_LIMIT),
        name="moe",
    )(tile_expert, tile_valid, slot_token, slot_token, hp, wg_bf, wu_bf, wd_bf)


def _combine_kernel(pos_ref, posn_ref, ys_hbm, x1_ref, g2_ref, w_ref, fg_ref, o_ref, rbuf, sem):
    i = pl.program_id(0)
    n = pl.num_programs(0)
    slot = lax.rem(i, 2)
    nslot = 1 - slot
    tm = COMB_TM

    @pl.when(i == 0)
    def _():
        _row_gather_start(ys_hbm, pos_ref, rbuf.at[0], sem.at[0], 2 * tm)

    @pl.when(i + 1 < n)
    def _():
        _row_gather_start(ys_hbm, posn_ref, rbuf.at[nslot], sem.at[nslot], 2 * tm)

    _row_gather_wait(ys_hbm, rbuf.at[slot], sem.at[slot], 2 * tm)
    r = rbuf[slot]
    w = w_ref[...]
    moe = w[:, 0:1] * r[0:tm, :] + w[:, 1:2] * r[tm:, :]
    y = x1_ref[...] + g2_ref[0] * moe
    ms = jnp.mean(y * y, axis=-1, keepdims=True)
    o_ref[...] = y * lax.rsqrt(ms + NORM_EPS) * fg_ref[...]


def _combine(pos_tiles, ys, x1, g2, w_tok, final_g, seq_len):
    T, D = x1.shape
    tm = COMB_TM
    nt = T // tm
    per_b = seq_len // tm
    return pl.pallas_call(
        _combine_kernel,
        grid=(nt,),
        in_specs=[pl.BlockSpec((1, 1, 2 * tm), lambda i: (i, 0, 0), memory_space=pltpu.SMEM),
                  pl.BlockSpec((1, 1, 2 * tm), lambda i: (jnp.minimum(i + 1, nt - 1), 0, 0),
                               memory_space=pltpu.SMEM),
                  pl.BlockSpec(memory_space=pl.ANY),
                  pl.BlockSpec((tm, D), lambda i: (i, 0)),
                  pl.BlockSpec((1, 1, D), lambda i: (i // per_b, 0, 0)),
                  pl.BlockSpec((tm, 2), lambda i: (i, 0)),
                  pl.BlockSpec((1, D), lambda i: (0, 0))],
        out_specs=pl.BlockSpec((tm, D), lambda i: (i, 0)),
        out_shape=jax.ShapeDtypeStruct((T, D), F32),
        scratch_shapes=[pltpu.VMEM((2, 2 * tm, D), F32),
                        pltpu.SemaphoreType.DMA((2,))],
        compiler_params=pltpu.CompilerParams(
            dimension_semantics=("arbitrary",), vmem_limit_bytes=VMEM_LIMIT),
        name="combine",
    )(pos_tiles, pos_tiles, ys, x1, g2, w_tok, final_g)


def _rel_bucket_table():
    n = jnp.arange(BIAS_LUT, dtype=I32)
    max_exact = N_BUCKETS // 2
    nf = jnp.maximum(n, 1).astype(F32)
    large = max_exact + (jnp.log(nf / max_exact) / math.log(MAX_DISTANCE / max_exact)
                         * (N_BUCKETS - max_exact)).astype(I32)
    large = jnp.minimum(large, N_BUCKETS - 1)
    return jnp.where(n < max_exact, n, large)


def _route_plan(eid, n_tiles, tm):
    two, T = eid.shape
    e_flat = eid.reshape(-1)
    onehot = (e_flat[:, None] == jnp.arange(N_EXPERTS, dtype=I32)[None, :]).astype(I32)
    csum = jnp.cumsum(onehot, axis=0)
    rank = jnp.sum((csum - onehot) * onehot, axis=1)
    counts = csum[-1]
    ptiles = (counts + tm - 1) // tm
    tend = jnp.cumsum(ptiles)
    tstart = tend - ptiles
    slot = jnp.sum(onehot * tstart[None, :], axis=1) * tm + rank
    total = tend[-1]
    tile_ids = jnp.arange(n_tiles, dtype=I32)
    tile_valid = (tile_ids < total).astype(I32)
    tile_expert = jnp.sum((tile_ids[:, None] >= tend[None, :]).astype(I32), axis=1)
    last_expert = jnp.sum((total - 1 >= tend).astype(I32))
    tile_expert = jnp.minimum(tile_expert, last_expert).astype(I32)
    tok = jnp.tile(jnp.arange(T, dtype=I32), two)
    slot_token = jnp.zeros((n_tiles * tm,), I32).at[slot].set(tok, unique_indices=True)
    return slot.astype(I32), slot_token, tile_expert, tile_valid


def kernel(x, c, positions, rel_bias, ada_w, ada_b, norm1_g, w_in, lambda_q1, lambda_k1, lambda_q2,
           lambda_k2, subln_g, conv_w, w_out, norm2_g, router_group_w, router_group_b,
           router_expert_w, router_expert_b, expert_w_gate, expert_w_up, expert_w_down, final_g):
    B, S, D = x.shape
    T = B * S
    l = 0

    c_pad = jnp.zeros((8, D), F32).at[:B].set(c)
    ada = _ada(c_pad, ada_w[l], ada_b[l].reshape(1, -1))[:B]
    sh1, sc1, g1, sh2, sc2, g2 = [a.reshape(B, 1, D) for a in jnp.split(ada, 6, axis=-1)]

    q, k, v, conv = _inproj(x, sc1, sh1, norm1_g[l].reshape(1, D), w_in[l].astype(BF16), conv_w[l])
    lut = rel_bias.astype(F32)[_rel_bucket_table(), :].T.reshape(N_DIFF_HEADS, 2, BIAS_LUT)
    lam_params = jnp.stack([lambda_q1[l], lambda_k1[l], lambda_q2[l], lambda_k2[l]]).astype(F32)
    attn = _attention(q, k, v, positions, lut, lam_params, subln_g[l].reshape(1, V_HEAD_DIM))

    wr_t = jnp.zeros((ROUTER_ROWS, D), F32)
    wr_t = wr_t.at[0:N_GROUPS].set(router_group_w[l].T)
    wr_t = wr_t.at[N_GROUPS:N_GROUPS + N_EXPERTS].set(router_expert_w[l].T).astype(BF16)
    rb = jnp.zeros((ROUTER_ROWS, 1), F32)
    rb = rb.at[0:N_GROUPS, 0].set(router_group_b[l])
    rb = rb.at[N_GROUPS:N_GROUPS + N_EXPERTS, 0].set(router_expert_b[l])
    x1, hp, ri, rw = _outproj(attn, conv, x, g1, sc2, sh2, norm2_g[l].reshape(1, D),
                              w_out[l].astype(BF16), wr_t, rb)

    eid = ri[:, 0:2, :].transpose(1, 0, 2).reshape(2, T)
    w_tok = rw[:, 0:2, :].transpose(0, 2, 1).reshape(T, 2)
    n_tiles = 2 * T // MOE_TM + N_EXPERTS
    slot, slot_token, tile_expert, tile_valid = _route_plan(eid, n_tiles, MOE_TM)

    ys = _moe(tile_expert, tile_valid, slot_token.reshape(n_tiles, 1, MOE_TM), hp.reshape(T, D // 2),
              expert_w_gate[l].astype(BF16), expert_w_up[l].astype(BF16),
              expert_w_down[l].astype(BF16))

    nct = T // COMB_TM
    pos = slot.reshape(2, nct, 1, COMB_TM)
    pos_tiles = jnp.concatenate([pos[0], pos[1]], axis=2)
    out = _combine(pos_tiles, ys, x1.reshape(T, D), g2, w_tok, final_g.reshape(1, D), S)
    return out.reshape(B, S, D)
```

```python
import functools
import math

import jax
import jax.numpy as jnp
from jax import lax
from jax.experimental import pallas as pl
from jax.experimental.pallas import tpu as pltpu

F32 = jnp.float32
BF16 = jnp.bfloat16
I32 = jnp.int32
U32 = jnp.uint32

D_MODEL = 1024
ATTN_WIDTH = 512
CONV_WIDTH = 512
N_DIFF_HEADS = 4
DIFF_HEAD_DIM = 64
V_HEAD_DIM = 128
IN_PROJ_WIDTH = 3 * ATTN_WIDTH + 3 * CONV_WIDTH
CONV_K = 3
N_BUCKETS = 32
MAX_DISTANCE = 128
N_GROUPS = 4
EXPERTS_PER_GROUP = 4
N_EXPERTS = 16
D_EXPERT = 512
NORM_EPS = 1e-6
SUBLN_EPS = 1e-5
NEG_INF = -1e30
LAMBDA_INIT = 0.8 - 0.6 * math.exp(-0.3 * 0)
QK_SCALE = DIFF_HEAD_DIM ** -0.5
LOG2E = math.log2(math.e)

BIAS_LUT = 128

ROW_TILE = 512
ATT_TQ = 512
ATT_CHAIN = 256
ATT_GROUP = 4
ATT_LOOKAHEAD = 4
ATT_TK = 256
MOE_TM = 256
COMB_TM = 256
ROUTER_ROWS = 32
VMEM_LIMIT = 56 * 1024 * 1024


def _silu(x):
    return x * (1.0 / (1.0 + jnp.exp(-x)))


def _ada_kernel(c_ref, w_ref, b_ref, o_ref):
    s = _silu(c_ref[...])
    o_ref[...] = jnp.dot(s, w_ref[...], preferred_element_type=F32,
                         precision=lax.Precision.HIGHEST) + b_ref[...]


def _ada(c_pad, w, b):
    n = w.shape[1]
    bn = 1024
    return pl.pallas_call(
        _ada_kernel,
        grid=(n // bn,),
        in_specs=[pl.BlockSpec((8, D_MODEL), lambda j: (0, 0)),
                  pl.BlockSpec((D_MODEL, bn), lambda j: (0, j)),
                  pl.BlockSpec((1, bn), lambda j: (0, j))],
        out_specs=pl.BlockSpec((8, bn), lambda j: (0, j)),
        out_shape=jax.ShapeDtypeStruct((8, n), F32),
        name="ada",
    )(c_pad, w, b)


def _inproj_kernel(x_ref, sc_ref, sh_ref, g_ref, wqt_ref, wk_ref, wvt_ref, wc_ref, cw_ref,
                   qt_ref, k_ref, vt_ref, conv_ref, carry_ref):
    j = pl.program_id(1)
    tm = x_ref.shape[1]
    x = x_ref[0]
    ms = jnp.mean(x * x, axis=-1, keepdims=True)
    h = x * lax.rsqrt(ms + NORM_EPS) * g_ref[...]
    h = h * (1.0 + sc_ref[0]) + sh_ref[0]
    hb = h.astype(BF16)
    nt = (((1,), (1,)), ((), ()))

    def proj(c0):
        return jnp.dot(hb, wc_ref[:, c0:c0 + 512], preferred_element_type=F32)

    qt = lax.dot_general(wqt_ref[...], hb, nt, preferred_element_type=F32)
    qt_ref[0] = (qt * (QK_SCALE * LOG2E)).astype(BF16)
    k_ref[0] = jnp.dot(hb, wk_ref[...], preferred_element_type=F32).astype(BF16)
    vt_ref[0] = lax.dot_general(wvt_ref[...], hb, nt, preferred_element_type=F32).astype(BF16)
    gate_b = proj(0)
    u = proj(512) * proj(1024)

    @pl.when(j == 0)
    def _():
        carry_ref[...] = jnp.zeros_like(carry_ref)

    prev = carry_ref[...]
    row = lax.broadcasted_iota(I32, u.shape, 0)
    u1 = pltpu.roll(u, 1, axis=0)
    u2 = pltpu.roll(u, 2, axis=0)
    u1 = jnp.where(row == 0, prev[7:8, :], u1)
    u2 = jnp.where(row == 0, prev[6:7, :], jnp.where(row == 1, prev[7:8, :], u2))
    conv = cw_ref[0:1, :] * u2 + cw_ref[1:2, :] * u1 + cw_ref[2:3, :] * u
    conv_ref[0] = (gate_b * conv).astype(BF16)
    carry_ref[...] = u[tm - 8:tm, :]


def _inproj(x, sc1, sh1, g1n, wq_t, wk, wv_t, wc, conv_w):
    B, S, D = x.shape
    tm = ROW_TILE
    row_out = jax.ShapeDtypeStruct((B, S, 512), BF16)
    col_out = jax.ShapeDtypeStruct((B, 512, S), BF16)
    row_spec = pl.BlockSpec((1, tm, 512), lambda b, j: (b, j, 0))
    col_spec = pl.BlockSpec((1, 512, tm), lambda b, j: (b, 0, j))
    mod_spec = pl.BlockSpec((1, 1, D), lambda b, j: (b, 0, 0))
    const2 = lambda b, j: (0, 0)
    return pl.pallas_call(
        _inproj_kernel,
        grid=(B, S // tm),
        in_specs=[pl.BlockSpec((1, tm, D), lambda b, j: (b, j, 0)),
                  mod_spec, mod_spec,
                  pl.BlockSpec((1, D), const2),
                  pl.BlockSpec((ATTN_WIDTH, D), const2),
                  pl.BlockSpec((D, ATTN_WIDTH), const2),
                  pl.BlockSpec((ATTN_WIDTH, D), const2),
                  pl.BlockSpec((D, 3 * CONV_WIDTH), const2),
                  pl.BlockSpec((CONV_K, CONV_WIDTH), const2)],
        out_specs=[col_spec, row_spec, col_spec, row_spec],
        out_shape=[col_out, row_out, col_out, row_out],
        scratch_shapes=[pltpu.VMEM((8, CONV_WIDTH), F32)],
        compiler_params=pltpu.CompilerParams(
            dimension_semantics=("arbitrary", "arbitrary"), vmem_limit_bytes=VMEM_LIMIT),
        name="inproj",
    )(x, sc1, sh1, g1n, wq_t, wk, wv_t, wc, conv_w)


def _attn_kernel(qmin_ref, kmax_ref, qt_ref, k_ref, vt_ref, pr_ref, pc_ref, lut_ref, lam_ref, sg_ref,
                 o_ref, acc_ref):
    b = pl.program_id(0)
    qi = pl.program_id(2)
    nq = pl.num_programs(2)
    tq, tk = ATT_TQ, ATT_TK
    hw = ATT_CHAIN
    n_half = tq // hw
    nk = nq * (tq // tk)
    qt = qt_ref[0]
    feat = lax.broadcasted_iota(I32, qt.shape, 0)
    zero = jnp.zeros_like(qt)
    qts = (jnp.where(feat < DIFF_HEAD_DIM, qt, zero), jnp.where(feat >= DIFF_HEAD_DIM, qt, zero))
    luts = (lut_ref[0, 0:1, :], lut_ref[0, 1:2, :])
    fars = tuple(t[:, BIAS_LUT - 1:BIAS_LUT] for t in luts)
    pq = pr_ref[0]
    qmin = qmin_ref[b * nq + qi]
    czero = jnp.zeros((1, 1), F32)
    chains = [(mi, hi) for mi in range(2) for hi in range(n_half)]

    def run_blocks(blocks, state):
        loaded = []
        for (j, mode, halves, masked) in blocks:
            ks = pl.multiple_of(j * tk, tk)
            kb = k_ref[0, pl.ds(ks, tk), :]
            vtb = vt_ref[0, :, pl.ds(ks, tk)]
            dist = None
            if mode != "far":
                pk = pc_ref[0, pl.ds(ks, tk), :]
                dist = jnp.clip(pq - pk, 0, BIAS_LUT - 1)
            loaded.append((kb, vtb, dist))
        items = [(bi, n) for bi, blk in enumerate(blocks) for n, (mi, hi) in enumerate(chains)
                 if hi in blk[2]]
        scores = {}

        def issue_qk(t):
            bi, n = items[t]
            mi, hi = chains[n]
            scores[t] = jnp.dot(loaded[bi][0], qts[mi][:, hi * hw:(hi + 1) * hw],
                                preferred_element_type=F32)

        state = list(state)
        for t in range(min(ATT_LOOKAHEAD, len(items))):
            issue_qk(t)
        for t, (bi, n) in enumerate(items):
            if t + ATT_LOOKAHEAD < len(items):
                issue_qk(t + ATT_LOOKAHEAD)
            _, mode, _, masked = blocks[bi]
            _, vtb, dist = loaded[bi]
            mi, hi = chains[n]
            cols = slice(hi * hw, (hi + 1) * hw)
            m, l = state[n]
            s = scores.pop(t)
            if mode == "far":
                c = fars[mi]
            else:
                c = czero
                table = jnp.broadcast_to(luts[mi], (tk, BIAS_LUT))
                bias = [jnp.take_along_axis(table, dist[:, o:o + 128], axis=1)
                        for o in range(hi * hw, (hi + 1) * hw, 128)]
                s = s + jnp.concatenate(bias, axis=1)
            if hi in masked:
                keep = (lax.broadcasted_iota(I32, (tk, hw), 0)
                        <= lax.broadcasted_iota(I32, (tk, hw), 1))
                s = jnp.where(keep, s, NEG_INF)
            mn = jnp.maximum(m, jnp.max(s, axis=0, keepdims=True) + c)
            alpha = jnp.exp2(m - mn)
            p = jnp.exp2(s - (mn - c))
            l = alpha * l + jnp.sum(p, axis=0, keepdims=True)
            acc_ref[mi, :, cols] = alpha * acc_ref[mi, :, cols] + jnp.dot(
                vtb, p.astype(BF16), preferred_element_type=F32)
            state[n] = (mn, l)
        return tuple(state)

    all_halves = tuple(range(n_half))

    def group_body(width):
        def body(g, carry):
            j0, state = carry
            is_far = qmin - kmax_ref[b * nk + j0] >= BIAS_LUT - 1
            for u in range(1, width):
                is_far = jnp.logical_and(is_far, qmin - kmax_ref[b * nk + j0 + u] >= BIAS_LUT - 1)
            far_blocks = [(j0 + u, "far", all_halves, ()) for u in range(width)]
            near_blocks = [(j0 + u, "near", all_halves, ()) for u in range(width)]
            state = lax.cond(is_far, lambda st: run_blocks(far_blocks, st),
                             lambda st: run_blocks(near_blocks, st), state)
            return j0 + width, state
        return body

    acc_ref[...] = jnp.zeros_like(acc_ref)
    m0 = jnp.full((1, hw), NEG_INF, F32)
    l0 = jnp.zeros((1, hw), F32)
    state = tuple((m0, l0) for _ in chains)
    n_full = qi * (tq // tk)
    n_main = jnp.maximum(n_full - 1, 0)
    j0 = jnp.int32(0)
    width = ATT_GROUP
    j0, state = lax.fori_loop(0, n_main // width, group_body(width), (j0, state))
    rem = n_main % width
    while width > 1:
        width //= 2
        j0, state = lax.fori_loop(0, (rem // width) % 2, group_body(width), (j0, state))

    diag = []
    for d in range(tq // tk):
        halves = tuple(hi for hi in range(n_half) if d * tk < (hi + 1) * hw)
        masked = tuple(hi for hi in halves if (d + 1) * tk - 1 > hi * hw)
        assert all(d * tk == hi * hw for hi in masked)
        diag.append((n_full + d, "near", halves, masked))
    state = lax.cond(qi > 0,
                     lambda st: run_blocks([(n_full - 1, "near", all_halves, ())] + diag, st),
                     lambda st: run_blocks(diag, st), state)

    l1 = jnp.concatenate([state[n][1] for n, (mi, hi) in enumerate(chains) if mi == 0], axis=1)
    l2 = jnp.concatenate([state[n][1] for n, (mi, hi) in enumerate(chains) if mi == 1], axis=1)
    lam = (jnp.exp(jnp.sum(lam_ref[0:1, :] * lam_ref[1:2, :], axis=-1, keepdims=True))
           - jnp.exp(jnp.sum(lam_ref[2:3, :] * lam_ref[3:4, :], axis=-1, keepdims=True))
           + LAMBDA_INIT)
    ot = acc_ref[0] * (1.0 / l1) - (lam * (1.0 / l2)) * acc_ref[1]
    ot = ot * lax.rsqrt(jnp.mean(ot * ot, axis=0, keepdims=True) + SUBLN_EPS)
    ot = ot * (sg_ref[...] * (1.0 - LAMBDA_INIT))
    o_ref[0] = ot.T.astype(BF16)


def _attention(qt, k, vt, positions, lut, lam_params, subln_g_col):
    B, S, _ = k.shape
    tq = ATT_TQ
    nq = S // tq
    pos_col = positions.reshape(B, S, 1)
    pos_row = positions.reshape(B, 1, S)
    qmin = jnp.min(positions.reshape(B * nq, tq), axis=1)
    kmax = jnp.max(positions.reshape(B * (S // ATT_TK), ATT_TK), axis=1)
    grid_spec = pltpu.PrefetchScalarGridSpec(
        num_scalar_prefetch=2,
        grid=(B, N_DIFF_HEADS, nq),
        in_specs=[pl.BlockSpec((1, 128, tq), lambda b, h, i, *_: (b, h, i)),
                  pl.BlockSpec((1, S, 128), lambda b, h, i, *_: (b, 0, h)),
                  pl.BlockSpec((1, 128, S), lambda b, h, i, *_: (b, h, 0)),
                  pl.BlockSpec((1, 1, tq), lambda b, h, i, *_: (b, 0, i)),
                  pl.BlockSpec((1, S, 1), lambda b, h, i, *_: (b, 0, 0)),
                  pl.BlockSpec((1, 2, BIAS_LUT), lambda b, h, i, *_: (h, 0, 0)),
                  pl.BlockSpec((4, DIFF_HEAD_DIM), lambda b, h, i, *_: (0, 0)),
                  pl.BlockSpec((V_HEAD_DIM, 1), lambda b, h, i, *_: (0, 0))],
        out_specs=pl.BlockSpec((1, tq, 128), lambda b, h, i, *_: (b, i, h)),
        scratch_shapes=[pltpu.VMEM((2, V_HEAD_DIM, tq), F32)],
    )
    return pl.pallas_call(
        _attn_kernel,
        grid_spec=grid_spec,
        out_shape=jax.ShapeDtypeStruct((B, S, ATTN_WIDTH), BF16),
        compiler_params=pltpu.CompilerParams(
            dimension_semantics=("arbitrary", "arbitrary", "arbitrary"),
            vmem_limit_bytes=VMEM_LIMIT),
        name="diffattn",
    )(qmin, kmax, qt, k, vt, pos_row, pos_col, lut, lam_params, subln_g_col)


def _outproj_kernel(at_ref, cv_ref, x_ref, g1_ref, sc_ref, sh_ref, gn_ref, wo_ref, wr_ref, rb_ref,
                    x1_ref, hp_ref, ri_ref, rw_ref):
    tm = x_ref.shape[1]
    mix = (jnp.dot(at_ref[0], wo_ref[0:ATTN_WIDTH, :], preferred_element_type=F32)
           + jnp.dot(cv_ref[0], wo_ref[ATTN_WIDTH:, :], preferred_element_type=F32))
    x1 = x_ref[0] + g1_ref[0] * mix
    x1_ref[0] = x1
    ms = jnp.mean(x1 * x1, axis=-1, keepdims=True)
    h = x1 * lax.rsqrt(ms + NORM_EPS) * gn_ref[...]
    h = h * (1.0 + sc_ref[0]) + sh_ref[0]
    hb = h.astype(BF16)

    half = D_MODEL // 2
    lo = pltpu.bitcast(hb[:, :half].astype(F32), U32) >> 16
    hi = pltpu.bitcast(hb[:, half:].astype(F32), U32) & jnp.uint32(0xFFFF0000)
    hp_ref[0] = lo | hi

    lg_all = lax.dot_general(wr_ref[...], hb, (((1,), (1,)), ((), ())),
                             preferred_element_type=F32) + rb_ref[...]
    lg = lg_all[0:N_GROUPS, :]
    le = lg_all[N_GROUPS:N_GROUPS + N_EXPERTS, :]
    row4 = lax.broadcasted_iota(I32, (N_GROUPS, tm), 0)
    gmax = jnp.max(lg, axis=0, keepdims=True)
    pg_sel = 1.0 / jnp.sum(jnp.exp(lg - gmax), axis=0, keepdims=True)
    gsel = jnp.min(jnp.where(lg == gmax, row4, N_GROUPS), axis=0, keepdims=True)
    sel = jnp.zeros((EXPERTS_PER_GROUP, tm), F32)
    for g in range(N_GROUPS):
        sel = jnp.where(gsel == g, le[g * EXPERTS_PER_GROUP:(g + 1) * EXPERTS_PER_GROUP, :], sel)
    v1 = jnp.max(sel, axis=0, keepdims=True)
    i1 = jnp.min(jnp.where(sel == v1, row4, EXPERTS_PER_GROUP), axis=0, keepdims=True)
    rest = jnp.where(row4 == i1, -jnp.inf, sel)
    v2 = jnp.max(rest, axis=0, keepdims=True)
    i2 = jnp.min(jnp.where(rest == v2, row4, EXPERTS_PER_GROUP), axis=0, keepdims=True)
    e2 = jnp.exp(v2 - v1)
    w1 = pg_sel / (1.0 + e2)
    w2 = pg_sel * e2 / (1.0 + e2)
    row8 = lax.broadcasted_iota(I32, (8, tm), 0)
    eid1 = gsel * EXPERTS_PER_GROUP + i1
    eid2 = gsel * EXPERTS_PER_GROUP + i2
    ri_ref[0] = jnp.where(row8 == 0, eid1, jnp.where(row8 == 1, eid2, 0))
    rw_ref[0] = jnp.where(row8 == 0, w1, jnp.where(row8 == 1, w2, 0.0))


def _outproj(attn, conv, x, g1, sc2, sh2, g2n, w_out_bf, wr_t, rb):
    B, S, D = x.shape
    tm = ROW_TILE
    half_spec = pl.BlockSpec((1, tm, 512), lambda b, j: (b, j, 0))
    full_spec = pl.BlockSpec((1, tm, D), lambda b, j: (b, j, 0))
    mod_spec = pl.BlockSpec((1, 1, D), lambda b, j: (b, 0, 0))
    rt_spec = pl.BlockSpec((1, 8, tm), lambda b, j: (b, 0, j))
    return pl.pallas_call(
        _outproj_kernel,
        grid=(B, S // tm),
        in_specs=[half_spec, half_spec, full_spec, mod_spec, mod_spec, mod_spec,
                  pl.BlockSpec((1, D), lambda b, j: (0, 0)),
                  pl.BlockSpec((D, D), lambda b, j: (0, 0)),
                  pl.BlockSpec((ROUTER_ROWS, D), lambda b, j: (0, 0)),
                  pl.BlockSpec((ROUTER_ROWS, 1), lambda b, j: (0, 0))],
        out_specs=[full_spec, half_spec, rt_spec, rt_spec],
        out_shape=[jax.ShapeDtypeStruct((B, S, D), F32),
                   jax.ShapeDtypeStruct((B, S, 512), U32),
                   jax.ShapeDtypeStruct((B, 8, S), I32),
                   jax.ShapeDtypeStruct((B, 8, S), F32)],
        compiler_params=pltpu.CompilerParams(
            dimension_semantics=("arbitrary", "arbitrary"), vmem_limit_bytes=VMEM_LIMIT),
        name="outproj",
    )(attn, conv, x, g1, sc2, sh2, g2n, w_out_bf, wr_t, rb)


def _row_gather_start(src_hbm, idx_ref, dst, sem, n_rows):
    def body(r, c):
        t = idx_ref[0, 0, r]
        pltpu.make_async_copy(src_hbm.at[pl.ds(t, 1), :], dst.at[pl.ds(r, 1), :], sem).start()
        return c
    lax.fori_loop(0, n_rows, body, 0, unroll=8)


def _row_gather_wait(src_hbm, dst, sem, n_rows):
    pltpu.make_async_copy(src_hbm.at[pl.ds(0, n_rows), :], dst, sem).wait()


def _moe_kernel(te_ref, tv_ref, tok_ref, tokn_ref, hp_hbm, wg_ref, wu_ref, wd_ref, y_ref,
                xbuf, sem):
    i = pl.program_id(0)
    n = pl.num_programs(0)
    slot = lax.rem(i, 2)
    nslot = 1 - slot
    tm = MOE_TM

    @pl.when(i == 0)
    def _():
        _row_gather_start(hp_hbm, tok_ref, xbuf.at[0], sem.at[0], tm)

    nxt = jnp.minimum(i + 1, n - 1)

    @pl.when(jnp.logical_and(i + 1 < n, tv_ref[nxt] == 1))
    def _():
        _row_gather_start(hp_hbm, tokn_ref, xbuf.at[nslot], sem.at[nslot], tm)

    @pl.when(tv_ref[i] == 1)
    def _():
        _row_gather_wait(hp_hbm, xbuf.at[slot], sem.at[slot], tm)
        xp = xbuf[slot]
        half = D_MODEL // 2
        lo = pltpu.bitcast(xp << 16, F32).astype(BF16)
        hi = pltpu.bitcast(xp & jnp.uint32(0xFFFF0000), F32).astype(BF16)
        g = (jnp.dot(lo, wg_ref[0, 0:half, :], preferred_element_type=F32)
             + jnp.dot(hi, wg_ref[0, half:, :], preferred_element_type=F32))
        u = (jnp.dot(lo, wu_ref[0, 0:half, :], preferred_element_type=F32)
             + jnp.dot(hi, wu_ref[0, half:, :], preferred_element_type=F32))
        hid = (_silu(g) * u).astype(BF16)
        y_ref[...] = jnp.dot(hid, wd_ref[0], preferred_element_type=F32)

    @pl.when(tv_ref[i] == 0)
    def _():
        y_ref[...] = jnp.zeros_like(y_ref)


def _moe(tile_expert, tile_valid, slot_token, hp, wg_bf, wu_bf, wd_bf):
    nt = tile_expert.shape[0]
    tm = MOE_TM
    D = D_MODEL
    grid_spec = pltpu.PrefetchScalarGridSpec(
        num_scalar_prefetch=2,
        grid=(nt,),
        in_specs=[pl.BlockSpec((1, 1, tm), lambda i, te, tv: (i, 0, 0), memory_space=pltpu.SMEM),
                  pl.BlockSpec((1, 1, tm), lambda i, te, tv: (jnp.minimum(i + 1, nt - 1), 0, 0),
                               memory_space=pltpu.SMEM),
                  pl.BlockSpec(memory_space=pl.ANY),
                  pl.BlockSpec((1, D, D_EXPERT), lambda i, te, tv: (te[i], 0, 0)),
                  pl.BlockSpec((1, D, D_EXPERT), lambda i, te, tv: (te[i], 0, 0)),
                  pl.BlockSpec((1, D_EXPERT, D), lambda i, te, tv: (te[i], 0, 0))],
        out_specs=pl.BlockSpec((tm, D), lambda i, te, tv: (i, 0)),
        scratch_shapes=[pltpu.VMEM((2, tm, D // 2), U32),
                        pltpu.SemaphoreType.DMA((2,))],
    )
    return pl.pallas_call(
        _moe_kernel,
        grid_spec=grid_spec,
        out_shape=jax.ShapeDtypeStruct((nt * tm, D), F32),
        compiler_params=pltpu.CompilerParams(
            dimension_semantics=("arbitrary",), vmem_limit_bytes=VMEM_LIMIT),
        name="moe",
    )(tile_expert, tile_valid, slot_token, slot_token, hp, wg_bf, wu_bf, wd_bf)


def _combine_kernel(pos_ref, posn_ref, ys_hbm, x1_ref, g2_ref, w_ref, fg_ref, o_ref, rbuf, sem):
    i = pl.program_id(0)
    n = pl.num_programs(0)
    slot = lax.rem(i, 2)
    nslot = 1 - slot
    tm = COMB_TM

    @pl.when(i == 0)
    def _():
        _row_gather_start(ys_hbm, pos_ref, rbuf.at[0], sem.at[0], 2 * tm)

    @pl.when(i + 1 < n)
    def _():
        _row_gather_start(ys_hbm, posn_ref, rbuf.at[nslot], sem.at[nslot], 2 * tm)

    _row_gather_wait(ys_hbm, rbuf.at[slot], sem.at[slot], 2 * tm)
    r = rbuf[slot]
    w = w_ref[...]
    moe = w[:, 0:1] * r[0:tm, :] + w[:, 1:2] * r[tm:, :]
    y = x1_ref[...] + g2_ref[0] * moe
    ms = jnp.mean(y * y, axis=-1, keepdims=True)
    o_ref[...] = y * lax.rsqrt(ms + NORM_EPS) * fg_ref[...]


def _combine(pos_tiles, ys, x1, g2, w_tok, final_g, seq_len):
    T, D = x1.shape
    tm = COMB_TM
    nt = T // tm
    per_b = seq_len // tm
    return pl.pallas_call(
        _combine_kernel,
        grid=(nt,),
        in_specs=[pl.BlockSpec((1, 1, 2 * tm), lambda i: (i, 0, 0), memory_space=pltpu.SMEM),
                  pl.BlockSpec((1, 1, 2 * tm), lambda i: (jnp.minimum(i + 1, nt - 1), 0, 0),
                               memory_space=pltpu.SMEM),
                  pl.BlockSpec(memory_space=pl.ANY),
                  pl.BlockSpec((tm, D), lambda i: (i, 0)),
                  pl.BlockSpec((1, 1, D), lambda i: (i // per_b, 0, 0)),
                  pl.BlockSpec((tm, 2), lambda i: (i, 0)),
                  pl.BlockSpec((1, D), lambda i: (0, 0))],
        out_specs=pl.BlockSpec((tm, D), lambda i: (i, 0)),
        out_shape=jax.ShapeDtypeStruct((T, D), F32),
        scratch_shapes=[pltpu.VMEM((2, 2 * tm, D), F32),
                        pltpu.SemaphoreType.DMA((2,))],
        compiler_params=pltpu.CompilerParams(
            dimension_semantics=("arbitrary",), vmem_limit_bytes=VMEM_LIMIT),
        name="combine",
    )(pos_tiles, pos_tiles, ys, x1, g2, w_tok, final_g)


def _rel_bucket_table():
    n = jnp.arange(BIAS_LUT, dtype=I32)
    max_exact = N_BUCKETS // 2
    nf = jnp.maximum(n, 1).astype(F32)
    large = max_exact + (jnp.log(nf / max_exact) / math.log(MAX_DISTANCE / max_exact)
                         * (N_BUCKETS - max_exact)).astype(I32)
    large = jnp.minimum(large, N_BUCKETS - 1)
    return jnp.where(n < max_exact, n, large)


def _route_plan(eid, n_tiles, tm):
    two, T = eid.shape
    e_flat = eid.reshape(-1)
    onehot = (e_flat[:, None] == jnp.arange(N_EXPERTS, dtype=I32)[None, :]).astype(I32)
    csum = jnp.cumsum(onehot, axis=0)
    rank = jnp.sum((csum - onehot) * onehot, axis=1)
    counts = csum[-1]
    ptiles = (counts + tm - 1) // tm
    tend = jnp.cumsum(ptiles)
    tstart = tend - ptiles
    slot = jnp.sum(onehot * tstart[None, :], axis=1) * tm + rank
    total = tend[-1]
    tile_ids = jnp.arange(n_tiles, dtype=I32)
    tile_valid = (tile_ids < total).astype(I32)
    tile_expert = jnp.sum((tile_ids[:, None] >= tend[None, :]).astype(I32), axis=1)
    last_expert = jnp.sum((total - 1 >= tend).astype(I32))
    tile_expert = jnp.minimum(tile_expert, last_expert).astype(I32)
    tok = jnp.tile(jnp.arange(T, dtype=I32), two)
    slot_token = jnp.zeros((n_tiles * tm,), I32).at[slot].set(tok, unique_indices=True)
    return slot.astype(I32), slot_token, tile_expert, tile_valid


def kernel(x, c, positions, rel_bias, ada_w, ada_b, norm1_g, w_in, lambda_q1, lambda_k1, lambda_q2,
           lambda_k2, subln_g, conv_w, w_out, norm2_g, router_group_w, router_group_b,
           router_expert_w, router_expert_b, expert_w_gate, expert_w_up, expert_w_down, final_g):
    B, S, D = x.shape
    T = B * S
    l = 0

    c_pad = jnp.zeros((8, D), F32).at[:B].set(c)
    ada = _ada(c_pad, ada_w[l], ada_b[l].reshape(1, -1))[:B]
    sh1, sc1, g1, sh2, sc2, g2 = [a.reshape(B, 1, D) for a in jnp.split(ada, 6, axis=-1)]

    w_in_bf = w_in[l].astype(BF16)
    A = ATTN_WIDTH
    qt, k, vt, conv = _inproj(x, sc1, sh1, norm1_g[l].reshape(1, D), w_in_bf[:, 0:A].T,
                              w_in_bf[:, A:2 * A], w_in_bf[:, 2 * A:3 * A].T, w_in_bf[:, 3 * A:],
                              conv_w[l])
    lut = (rel_bias.astype(F32)[_rel_bucket_table(), :].T * LOG2E).reshape(N_DIFF_HEADS, 2, BIAS_LUT)
    lam_params = jnp.stack([lambda_q1[l], lambda_k1[l], lambda_q2[l], lambda_k2[l]]).astype(F32)
    attn = _attention(qt, k, vt, positions, lut, lam_params, subln_g[l].reshape(V_HEAD_DIM, 1))

    wr_t = jnp.zeros((ROUTER_ROWS, D), F32)
    wr_t = wr_t.at[0:N_GROUPS].set(router_group_w[l].T)
    wr_t = wr_t.at[N_GROUPS:N_GROUPS + N_EXPERTS].set(router_expert_w[l].T).astype(BF16)
    rb = jnp.zeros((ROUTER_ROWS, 1), F32)
    rb = rb.at[0:N_GROUPS, 0].set(router_group_b[l])
    rb = rb.at[N_GROUPS:N_GROUPS + N_EXPERTS, 0].set(router_expert_b[l])
    x1, hp, ri, rw = _outproj(attn, conv, x, g1, sc2, sh2, norm2_g[l].reshape(1, D),
                              w_out[l].astype(BF16), wr_t, rb)

    eid = ri[:, 0:2, :].transpose(1, 0, 2).reshape(2, T)
    w_tok = rw[:, 0:2, :].transpose(0, 2, 1).reshape(T, 2)
    n_tiles = 2 * T // MOE_TM + N_EXPERTS
    slot, slot_token, tile_expert, tile_valid = _route_plan(eid, n_tiles, MOE_TM)

    ys = _moe(tile_expert, tile_valid, slot_token.reshape(n_tiles, 1, MOE_TM), hp.reshape(T, D // 2),
              expert_w_gate[l].astype(BF16), expert_w_up[l].astype(BF16),
              expert_w_down[l].astype(BF16))

    nct = T // COMB_TM
    pos = slot.reshape(2, nct, 1, COMB_TM)
    pos_tiles = jnp.concatenate([pos[0], pos[1]], axis=2)
    out = _combine(pos_tiles, ys, x1.reshape(T, D), g2, w_tok, final_g.reshape(1, D), S)
    return out.reshape(B, S, D)
```

```python
import functools
import math

import jax
import jax.numpy as jnp
from jax import lax
from jax.experimental import pallas as pl
from jax.experimental.pallas import tpu as pltpu

F32 = jnp.float32
BF16 = jnp.bfloat16
I32 = jnp.int32
U32 = jnp.uint32

D_MODEL = 1024
ATTN_WIDTH = 512
CONV_WIDTH = 512
N_DIFF_HEADS = 4
DIFF_HEAD_DIM = 64
V_HEAD_DIM = 128
IN_PROJ_WIDTH = 3 * ATTN_WIDTH + 3 * CONV_WIDTH
CONV_K = 3
N_BUCKETS = 32
MAX_DISTANCE = 128
N_GROUPS = 4
EXPERTS_PER_GROUP = 4
N_EXPERTS = 16
D_EXPERT = 512
NORM_EPS = 1e-6
SUBLN_EPS = 1e-5
NEG_INF = -1e30
LAMBDA_INIT = 0.8 - 0.6 * math.exp(-0.3 * 0)
QK_SCALE = DIFF_HEAD_DIM ** -0.5
LOG2E = math.log2(math.e)

BIAS_LUT = 128

ROW_TILE = 512
ATT_TQ = 512
ATT_CHAIN = 256
ATT_GROUP = 4
ATT_LOOKAHEAD = 4
ATT_TK = 256
MOE_TM = 256
COMB_TM = 256
ROUTER_ROWS = 32
ROW_SLAB = D_MODEL // 256
VMEM_LIMIT = 56 * 1024 * 1024


def _silu(x):
    return x * (1.0 / (1.0 + jnp.exp(-x)))


def _ada_kernel(c_ref, w_ref, b_ref, o_ref):
    s = _silu(c_ref[...])
    o_ref[...] = jnp.dot(s, w_ref[...], preferred_element_type=F32,
                         precision=lax.Precision.HIGHEST) + b_ref[...]


def _ada(c_pad, w, b):
    n = w.shape[1]
    bn = 1024
    return pl.pallas_call(
        _ada_kernel,
        grid=(n // bn,),
        in_specs=[pl.BlockSpec((8, D_MODEL), lambda j: (0, 0)),
                  pl.BlockSpec((D_MODEL, bn), lambda j: (0, j)),
                  pl.BlockSpec((1, bn), lambda j: (0, j))],
        out_specs=pl.BlockSpec((8, bn), lambda j: (0, j)),
        out_shape=jax.ShapeDtypeStruct((8, n), F32),
        name="ada",
    )(c_pad, w, b)


def _inproj_kernel(x_ref, sc_ref, sh_ref, g_ref, wqt_ref, wk_ref, wvt_ref, wc_ref, cw_ref,
                   qt_ref, k_ref, vt_ref, conv_ref, carry_ref):
    j = pl.program_id(1)
    tm = x_ref.shape[1]
    x = x_ref[0]
    ms = jnp.mean(x * x, axis=-1, keepdims=True)
    h = x * lax.rsqrt(ms + NORM_EPS) * g_ref[...]
    h = h * (1.0 + sc_ref[0]) + sh_ref[0]
    hb = h.astype(BF16)
    nt = (((1,), (1,)), ((), ()))

    def proj(c0):
        return jnp.dot(hb, wc_ref[:, c0:c0 + 512], preferred_element_type=F32)

    qt = lax.dot_general(wqt_ref[...], hb, nt, preferred_element_type=F32)
    qt_ref[0] = (qt * (QK_SCALE * LOG2E)).astype(BF16)
    k_ref[0] = jnp.dot(hb, wk_ref[...], preferred_element_type=F32).astype(BF16)
    vt_ref[0] = lax.dot_general(wvt_ref[...], hb, nt, preferred_element_type=F32).astype(BF16)
    gate_b = proj(0)
    u = proj(512) * proj(1024)

    @pl.when(j == 0)
    def _():
        carry_ref[...] = jnp.zeros_like(carry_ref)

    prev = carry_ref[...]
    row = lax.broadcasted_iota(I32, u.shape, 0)
    u1 = pltpu.roll(u, 1, axis=0)
    u2 = pltpu.roll(u, 2, axis=0)
    u1 = jnp.where(row == 0, prev[7:8, :], u1)
    u2 = jnp.where(row == 0, prev[6:7, :], jnp.where(row == 1, prev[7:8, :], u2))
    conv = cw_ref[0:1, :] * u2 + cw_ref[1:2, :] * u1 + cw_ref[2:3, :] * u
    conv_ref[0] = (gate_b * conv).astype(BF16)
    carry_ref[...] = u[tm - 8:tm, :]


def _inproj(x, sc1, sh1, g1n, wq_t, wk, wv_t, wc, conv_w):
    B, S, D = x.shape
    tm = ROW_TILE
    row_out = jax.ShapeDtypeStruct((B, S, 512), BF16)
    col_out = jax.ShapeDtypeStruct((B, 512, S), BF16)
    row_spec = pl.BlockSpec((1, tm, 512), lambda b, j: (b, j, 0))
    col_spec = pl.BlockSpec((1, 512, tm), lambda b, j: (b, 0, j))
    mod_spec = pl.BlockSpec((1, 1, D), lambda b, j: (b, 0, 0))
    const2 = lambda b, j: (0, 0)
    return pl.pallas_call(
        _inproj_kernel,
        grid=(B, S // tm),
        in_specs=[pl.BlockSpec((1, tm, D), lambda b, j: (b, j, 0)),
                  mod_spec, mod_spec,
                  pl.BlockSpec((1, D), const2),
                  pl.BlockSpec((ATTN_WIDTH, D), const2),
                  pl.BlockSpec((D, ATTN_WIDTH), const2),
                  pl.BlockSpec((ATTN_WIDTH, D), const2),
                  pl.BlockSpec((D, 3 * CONV_WIDTH), const2),
                  pl.BlockSpec((CONV_K, CONV_WIDTH), const2)],
        out_specs=[col_spec, row_spec, col_spec, row_spec],
        out_shape=[col_out, row_out, col_out, row_out],
        scratch_shapes=[pltpu.VMEM((8, CONV_WIDTH), F32)],
        compiler_params=pltpu.CompilerParams(
            dimension_semantics=("arbitrary", "arbitrary"), vmem_limit_bytes=VMEM_LIMIT),
        name="inproj",
    )(x, sc1, sh1, g1n, wq_t, wk, wv_t, wc, conv_w)


def _attn_kernel(qmin_ref, kmax_ref, qt_ref, k_ref, vt_ref, pr_ref, pc_ref, lut_ref, lam_ref, sg_ref,
                 o_ref, acc_ref):
    b = pl.program_id(0)
    qi = pl.program_id(2)
    nq = pl.num_programs(2)
    tq, tk = ATT_TQ, ATT_TK
    hw = ATT_CHAIN
    n_half = tq // hw
    nk = nq * (tq // tk)
    qt = qt_ref[0]
    feat = lax.broadcasted_iota(I32, qt.shape, 0)
    zero = jnp.zeros_like(qt)
    qts = (jnp.where(feat < DIFF_HEAD_DIM, qt, zero), jnp.where(feat >= DIFF_HEAD_DIM, qt, zero))
    luts = (lut_ref[0, 0:1, :], lut_ref[0, 1:2, :])
    fars = tuple(t[:, BIAS_LUT - 1:BIAS_LUT] for t in luts)
    pq = pr_ref[0]
    qmin = qmin_ref[b * nq + qi]
    czero = jnp.zeros((1, 1), F32)
    chains = [(mi, hi) for mi in range(2) for hi in range(n_half)]

    def run_blocks(blocks, state):
        loaded = []
        for (j, mode, halves, masked) in blocks:
            ks = pl.multiple_of(j * tk, tk)
            kb = k_ref[0, pl.ds(ks, tk), :]
            vtb = vt_ref[0, :, pl.ds(ks, tk)]
            dist = None
            if mode != "far":
                pk = pc_ref[0, pl.ds(ks, tk), :]
                dist = jnp.clip(pq - pk, 0, BIAS_LUT - 1)
            loaded.append((kb, vtb, dist))
        items = [(bi, n) for bi, blk in enumerate(blocks) for n, (mi, hi) in enumerate(chains)
                 if hi in blk[2]]
        scores = {}

        def issue_qk(t):
            bi, n = items[t]
            mi, hi = chains[n]
            scores[t] = jnp.dot(loaded[bi][0], qts[mi][:, hi * hw:(hi + 1) * hw],
                                preferred_element_type=F32)

        state = list(state)
        for t in range(min(ATT_LOOKAHEAD, len(items))):
            issue_qk(t)
        for t, (bi, n) in enumerate(items):
            if t + ATT_LOOKAHEAD < len(items):
                issue_qk(t + ATT_LOOKAHEAD)
            _, mode, _, masked = blocks[bi]
            _, vtb, dist = loaded[bi]
            mi, hi = chains[n]
            cols = slice(hi * hw, (hi + 1) * hw)
            m, l = state[n]
            s = scores.pop(t)
            if mode == "far":
                c = fars[mi]
            else:
                c = czero
                table = jnp.broadcast_to(luts[mi], (tk, BIAS_LUT))
                bias = [jnp.take_along_axis(table, dist[:, o:o + 128], axis=1)
                        for o in range(hi * hw, (hi + 1) * hw, 128)]
                s = s + jnp.concatenate(bias, axis=1)
            if hi in masked:
                keep = (lax.broadcasted_iota(I32, (tk, hw), 0)
                        <= lax.broadcasted_iota(I32, (tk, hw), 1))
                s = jnp.where(keep, s, NEG_INF)
            mn = jnp.maximum(m, jnp.max(s, axis=0, keepdims=True) + c)
            alpha = jnp.exp2(m - mn)
            p = jnp.exp2(s - (mn - c))
            l = alpha * l + jnp.sum(p, axis=0, keepdims=True)
            acc_ref[mi, :, cols] = alpha * acc_ref[mi, :, cols] + jnp.dot(
                vtb, p.astype(BF16), preferred_element_type=F32)
            state[n] = (mn, l)
        return tuple(state)

    all_halves = tuple(range(n_half))

    def group_body(width):
        def body(g, carry):
            j0, state = carry
            is_far = qmin - kmax_ref[b * nk + j0] >= BIAS_LUT - 1
            for u in range(1, width):
                is_far = jnp.logical_and(is_far, qmin - kmax_ref[b * nk + j0 + u] >= BIAS_LUT - 1)
            far_blocks = [(j0 + u, "far", all_halves, ()) for u in range(width)]
            near_blocks = [(j0 + u, "near", all_halves, ()) for u in range(width)]
            state = lax.cond(is_far, lambda st: run_blocks(far_blocks, st),
                             lambda st: run_blocks(near_blocks, st), state)
            return j0 + width, state
        return body

    acc_ref[...] = jnp.zeros_like(acc_ref)
    m0 = jnp.full((1, hw), NEG_INF, F32)
    l0 = jnp.zeros((1, hw), F32)
    state = tuple((m0, l0) for _ in chains)
    n_full = qi * (tq // tk)
    n_main = jnp.maximum(n_full - 1, 0)
    j0 = jnp.int32(0)
    width = ATT_GROUP
    j0, state = lax.fori_loop(0, n_main // width, group_body(width), (j0, state))
    rem = n_main % width
    while width > 1:
        width //= 2
        j0, state = lax.fori_loop(0, (rem // width) % 2, group_body(width), (j0, state))

    diag = []
    for d in range(tq // tk):
        halves = tuple(hi for hi in range(n_half) if d * tk < (hi + 1) * hw)
        masked = tuple(hi for hi in halves if (d + 1) * tk - 1 > hi * hw)
        assert all(d * tk == hi * hw for hi in masked)
        diag.append((n_full + d, "near", halves, masked))
    state = lax.cond(qi > 0,
                     lambda st: run_blocks([(n_full - 1, "near", all_halves, ())] + diag, st),
                     lambda st: run_blocks(diag, st), state)

    l1 = jnp.concatenate([state[n][1] for n, (mi, hi) in enumerate(chains) if mi == 0], axis=1)
    l2 = jnp.concatenate([state[n][1] for n, (mi, hi) in enumerate(chains) if mi == 1], axis=1)
    lam = (jnp.exp(jnp.sum(lam_ref[0:1, :] * lam_ref[1:2, :], axis=-1, keepdims=True))
           - jnp.exp(jnp.sum(lam_ref[2:3, :] * lam_ref[3:4, :], axis=-1, keepdims=True))
           + LAMBDA_INIT)
    ot = acc_ref[0] * (1.0 / l1) - (lam * (1.0 / l2)) * acc_ref[1]
    ot = ot * lax.rsqrt(jnp.mean(ot * ot, axis=0, keepdims=True) + SUBLN_EPS)
    ot = ot * (sg_ref[...] * (1.0 - LAMBDA_INIT))
    o_ref[0] = ot.T.astype(BF16)


def _attention(qt, k, vt, positions, lut, lam_params, subln_g_col):
    B, S, _ = k.shape
    tq = ATT_TQ
    nq = S // tq
    pos_col = positions.reshape(B, S, 1)
    pos_row = positions.reshape(B, 1, S)
    qmin = jnp.min(positions.reshape(B * nq, tq), axis=1)
    kmax = jnp.max(positions.reshape(B * (S // ATT_TK), ATT_TK), axis=1)
    grid_spec = pltpu.PrefetchScalarGridSpec(
        num_scalar_prefetch=2,
        grid=(B, N_DIFF_HEADS, nq),
        in_specs=[pl.BlockSpec((1, 128, tq), lambda b, h, i, *_: (b, h, i)),
                  pl.BlockSpec((1, S, 128), lambda b, h, i, *_: (b, 0, h)),
                  pl.BlockSpec((1, 128, S), lambda b, h, i, *_: (b, h, 0)),
                  pl.BlockSpec((1, 1, tq), lambda b, h, i, *_: (b, 0, i)),
                  pl.BlockSpec((1, S, 1), lambda b, h, i, *_: (b, 0, 0)),
                  pl.BlockSpec((1, 2, BIAS_LUT), lambda b, h, i, *_: (h, 0, 0)),
                  pl.BlockSpec((4, DIFF_HEAD_DIM), lambda b, h, i, *_: (0, 0)),
                  pl.BlockSpec((V_HEAD_DIM, 1), lambda b, h, i, *_: (0, 0))],
        out_specs=pl.BlockSpec((1, tq, 128), lambda b, h, i, *_: (b, i, h)),
        scratch_shapes=[pltpu.VMEM((2, V_HEAD_DIM, tq), F32)],
    )
    return pl.pallas_call(
        _attn_kernel,
        grid_spec=grid_spec,
        out_shape=jax.ShapeDtypeStruct((B, S, ATTN_WIDTH), BF16),
        compiler_params=pltpu.CompilerParams(
            dimension_semantics=("arbitrary", "arbitrary", "arbitrary"),
            vmem_limit_bytes=VMEM_LIMIT),
        name="diffattn",
    )(qmin, kmax, qt, k, vt, pos_row, pos_col, lut, lam_params, subln_g_col)


def _outproj_kernel(at_ref, cv_ref, x_ref, g1_ref, sc_ref, sh_ref, gn_ref, wo_ref, wr_ref, rb_ref,
                    x1_ref, hp_ref, ri_ref, rw_ref):
    tm = x_ref.shape[1]
    mix = (jnp.dot(at_ref[0], wo_ref[0:ATTN_WIDTH, :], preferred_element_type=F32)
           + jnp.dot(cv_ref[0], wo_ref[ATTN_WIDTH:, :], preferred_element_type=F32))
    x1 = x_ref[0] + g1_ref[0] * mix
    x1_ref[0] = x1
    ms = jnp.mean(x1 * x1, axis=-1, keepdims=True)
    h = x1 * lax.rsqrt(ms + NORM_EPS) * gn_ref[...]
    h = h * (1.0 + sc_ref[0]) + sh_ref[0]
    hb = h.astype(BF16)

    _store_packed_rows(hp_ref.at[0], h)

    lg_all = lax.dot_general(wr_ref[...], hb, (((1,), (1,)), ((), ())),
                             preferred_element_type=F32) + rb_ref[...]
    lg = lg_all[0:N_GROUPS, :]
    le = lg_all[N_GROUPS:N_GROUPS + N_EXPERTS, :]
    row4 = lax.broadcasted_iota(I32, (N_GROUPS, tm), 0)
    gmax = jnp.max(lg, axis=0, keepdims=True)
    pg_sel = 1.0 / jnp.sum(jnp.exp(lg - gmax), axis=0, keepdims=True)
    gsel = jnp.min(jnp.where(lg == gmax, row4, N_GROUPS), axis=0, keepdims=True)
    sel = jnp.zeros((EXPERTS_PER_GROUP, tm), F32)
    for g in range(N_GROUPS):
        sel = jnp.where(gsel == g, le[g * EXPERTS_PER_GROUP:(g + 1) * EXPERTS_PER_GROUP, :], sel)
    v1 = jnp.max(sel, axis=0, keepdims=True)
    i1 = jnp.min(jnp.where(sel == v1, row4, EXPERTS_PER_GROUP), axis=0, keepdims=True)
    rest = jnp.where(row4 == i1, -jnp.inf, sel)
    v2 = jnp.max(rest, axis=0, keepdims=True)
    i2 = jnp.min(jnp.where(rest == v2, row4, EXPERTS_PER_GROUP), axis=0, keepdims=True)
    e2 = jnp.exp(v2 - v1)
    w1 = pg_sel / (1.0 + e2)
    w2 = pg_sel * e2 / (1.0 + e2)
    row8 = lax.broadcasted_iota(I32, (8, tm), 0)
    eid1 = gsel * EXPERTS_PER_GROUP + i1
    eid2 = gsel * EXPERTS_PER_GROUP + i2
    ri_ref[0] = jnp.where(row8 == 0, eid1, jnp.where(row8 == 1, eid2, 0))
    rw_ref[0] = jnp.where(row8 == 0, w1, jnp.where(row8 == 1, w2, 0.0))


def _outproj(attn, conv, x, g1, sc2, sh2, g2n, w_out_bf, wr_t, rb):
    B, S, D = x.shape
    tm = ROW_TILE
    half_spec = pl.BlockSpec((1, tm, 512), lambda b, j: (b, j, 0))
    full_spec = pl.BlockSpec((1, tm, D), lambda b, j: (b, j, 0))
    mod_spec = pl.BlockSpec((1, 1, D), lambda b, j: (b, 0, 0))
    rt_spec = pl.BlockSpec((1, 8, tm), lambda b, j: (b, 0, j))
    return pl.pallas_call(
        _outproj_kernel,
        grid=(B, S // tm),
        in_specs=[half_spec, half_spec, full_spec, mod_spec, mod_spec, mod_spec,
                  pl.BlockSpec((1, D), lambda b, j: (0, 0)),
                  pl.BlockSpec((D, D), lambda b, j: (0, 0)),
                  pl.BlockSpec((ROUTER_ROWS, D), lambda b, j: (0, 0)),
                  pl.BlockSpec((ROUTER_ROWS, 1), lambda b, j: (0, 0))],
        out_specs=[full_spec, pl.BlockSpec((1, tm * ROW_SLAB, 128), lambda b, j: (b, j, 0)),
                   rt_spec, rt_spec],
        out_shape=[jax.ShapeDtypeStruct((B, S, D), F32),
                   jax.ShapeDtypeStruct((B, S * ROW_SLAB, 128), U32),
                   jax.ShapeDtypeStruct((B, 8, S), I32),
                   jax.ShapeDtypeStruct((B, 8, S), F32)],
        compiler_params=pltpu.CompilerParams(
            dimension_semantics=("arbitrary", "arbitrary"), vmem_limit_bytes=VMEM_LIMIT),
        name="outproj",
    )(attn, conv, x, g1, sc2, sh2, g2n, w_out_bf, wr_t, rb)


def _store_packed_rows(dst, x):
    half = D_MODEL // 2
    xb = x.astype(BF16).astype(F32)
    packed = (pltpu.bitcast(xb[:, :half], U32) >> 16) | (
        pltpu.bitcast(xb[:, half:], U32) & jnp.uint32(0xFFFF0000))
    rows = x.shape[0]
    for c in range(ROW_SLAB):
        dst[pl.ds(c, rows, stride=ROW_SLAB), :] = packed[:, c * 128:(c + 1) * 128]


def _load_packed_rows(src, row0, rows):
    packed = jnp.concatenate(
        [src[pl.ds(row0 * ROW_SLAB + c, rows, stride=ROW_SLAB), :] for c in range(ROW_SLAB)], axis=1)
    lo = pltpu.bitcast(packed << 16, F32)
    hi = pltpu.bitcast(packed & jnp.uint32(0xFFFF0000), F32)
    return lo, hi


def _row_gather_start(src_hbm, idx_ref, dst, sem, n_rows):
    def body(r, c):
        t = idx_ref[0, 0, r]
        pltpu.make_async_copy(src_hbm.at[pl.ds(pl.multiple_of(t * ROW_SLAB, ROW_SLAB), ROW_SLAB)],
                              dst.at[pl.ds(pl.multiple_of(r * ROW_SLAB, ROW_SLAB), ROW_SLAB)],
                              sem).start()
        return c
    lax.fori_loop(0, n_rows, body, 0, unroll=8)


def _row_gather_wait(src_hbm, dst, sem, n_rows):
    pltpu.make_async_copy(src_hbm.at[pl.ds(0, n_rows * ROW_SLAB)], dst, sem).wait()


def _moe_kernel(te_ref, tv_ref, tok_ref, tokn_ref, hp_hbm, wg_ref, wu_ref, wd_ref, y_ref,
                xbuf, sem):
    i = pl.program_id(0)
    n = pl.num_programs(0)
    slot = lax.rem(i, 2)
    nslot = 1 - slot
    tm = MOE_TM

    @pl.when(i == 0)
    def _():
        _row_gather_start(hp_hbm, tok_ref, xbuf.at[0], sem.at[0], tm)

    nxt = jnp.minimum(i + 1, n - 1)

    @pl.when(jnp.logical_and(i + 1 < n, tv_ref[nxt] == 1))
    def _():
        _row_gather_start(hp_hbm, tokn_ref, xbuf.at[nslot], sem.at[nslot], tm)

    @pl.when(tv_ref[i] == 1)
    def _():
        _row_gather_wait(hp_hbm, xbuf.at[slot], sem.at[slot], tm)
        half = D_MODEL // 2
        lo, hi = _load_packed_rows(xbuf.at[slot], 0, tm)
        lo = lo.astype(BF16)
        hi = hi.astype(BF16)
        g = (jnp.dot(lo, wg_ref[0, 0:half, :], preferred_element_type=F32)
             + jnp.dot(hi, wg_ref[0, half:, :], preferred_element_type=F32))
        u = (jnp.dot(lo, wu_ref[0, 0:half, :], preferred_element_type=F32)
             + jnp.dot(hi, wu_ref[0, half:, :], preferred_element_type=F32))
        hid = (_silu(g) * u).astype(BF16)
        _store_packed_rows(y_ref, jnp.dot(hid, wd_ref[0], preferred_element_type=F32))

    @pl.when(tv_ref[i] == 0)
    def _():
        y_ref[...] = jnp.zeros_like(y_ref)


def _moe(tile_expert, tile_valid, slot_token, hp, wg_bf, wu_bf, wd_bf):
    nt = tile_expert.shape[0]
    tm = MOE_TM
    D = D_MODEL
    grid_spec = pltpu.PrefetchScalarGridSpec(
        num_scalar_prefetch=2,
        grid=(nt,),
        in_specs=[pl.BlockSpec((1, 1, tm), lambda i, te, tv: (i, 0, 0), memory_space=pltpu.SMEM),
                  pl.BlockSpec((1, 1, tm), lambda i, te, tv: (jnp.minimum(i + 1, nt - 1), 0, 0),
                               memory_space=pltpu.SMEM),
                  pl.BlockSpec(memory_space=pl.ANY),
                  pl.BlockSpec((1, D, D_EXPERT), lambda i, te, tv: (te[i], 0, 0)),
                  pl.BlockSpec((1, D, D_EXPERT), lambda i, te, tv: (te[i], 0, 0)),
                  pl.BlockSpec((1, D_EXPERT, D), lambda i, te, tv: (te[i], 0, 0))],
        out_specs=pl.BlockSpec((tm * ROW_SLAB, 128), lambda i, te, tv: (i, 0)),
        scratch_shapes=[pltpu.VMEM((2, tm * ROW_SLAB, 128), U32),
                        pltpu.SemaphoreType.DMA((2,))],
    )
    return pl.pallas_call(
        _moe_kernel,
        grid_spec=grid_spec,
        out_shape=jax.ShapeDtypeStruct((nt * tm * ROW_SLAB, 128), U32),
        compiler_params=pltpu.CompilerParams(
            dimension_semantics=("arbitrary",), vmem_limit_bytes=VMEM_LIMIT),
        name="moe",
    )(tile_expert, tile_valid, slot_token, slot_token, hp, wg_bf, wu_bf, wd_bf)


def _combine_kernel(pos_ref, posn_ref, ys_hbm, x1_ref, g2_ref, w_ref, fg_ref, o_ref, rbuf, sem):
    i = pl.program_id(0)
    n = pl.num_programs(0)
    slot = lax.rem(i, 2)
    nslot = 1 - slot
    tm = COMB_TM

    @pl.when(i == 0)
    def _():
        _row_gather_start(ys_hbm, pos_ref, rbuf.at[0], sem.at[0], 2 * tm)

    @pl.when(i + 1 < n)
    def _():
        _row_gather_start(ys_hbm, posn_ref, rbuf.at[nslot], sem.at[nslot], 2 * tm)

    _row_gather_wait(ys_hbm, rbuf.at[slot], sem.at[slot], 2 * tm)
    w = w_ref[...]
    r1 = jnp.concatenate(_load_packed_rows(rbuf.at[slot], 0, tm), axis=1)
    r2 = jnp.concatenate(_load_packed_rows(rbuf.at[slot], tm, tm), axis=1)
    moe = w[:, 0:1] * r1 + w[:, 1:2] * r2
    y = x1_ref[...] + g2_ref[0] * moe
    ms = jnp.mean(y * y, axis=-1, keepdims=True)
    o_ref[...] = y * lax.rsqrt(ms + NORM_EPS) * fg_ref[...]


def _combine(pos_tiles, ys, x1, g2, w_tok, final_g, seq_len):
    T, D = x1.shape
    tm = COMB_TM
    nt = T // tm
    per_b = seq_len // tm
    return pl.pallas_call(
        _combine_kernel,
        grid=(nt,),
        in_specs=[pl.BlockSpec((1, 1, 2 * tm), lambda i: (i, 0, 0), memory_space=pltpu.SMEM),
                  pl.BlockSpec((1, 1, 2 * tm), lambda i: (jnp.minimum(i + 1, nt - 1), 0, 0),
                               memory_space=pltpu.SMEM),
                  pl.BlockSpec(memory_space=pl.ANY),
                  pl.BlockSpec((tm, D), lambda i: (i, 0)),
                  pl.BlockSpec((1, 1, D), lambda i: (i // per_b, 0, 0)),
                  pl.BlockSpec((tm, 2), lambda i: (i, 0)),
                  pl.BlockSpec((1, D), lambda i: (0, 0))],
        out_specs=pl.BlockSpec((tm, D), lambda i: (i, 0)),
        out_shape=jax.ShapeDtypeStruct((T, D), F32),
        scratch_shapes=[pltpu.VMEM((2, 2 * tm * ROW_SLAB, 128), U32),
                        pltpu.SemaphoreType.DMA((2,))],
        compiler_params=pltpu.CompilerParams(
            dimension_semantics=("arbitrary",), vmem_limit_bytes=VMEM_LIMIT),
        name="combine",
    )(pos_tiles, pos_tiles, ys, x1, g2, w_tok, final_g)


def _rel_bucket_table():
    n = jnp.arange(BIAS_LUT, dtype=I32)
    max_exact = N_BUCKETS // 2
    nf = jnp.maximum(n, 1).astype(F32)
    large = max_exact + (jnp.log(nf / max_exact) / math.log(MAX_DISTANCE / max_exact)
                         * (N_BUCKETS - max_exact)).astype(I32)
    large = jnp.minimum(large, N_BUCKETS - 1)
    return jnp.where(n < max_exact, n, large)


def _route_plan(eid, n_tiles, tm):
    two, T = eid.shape
    e_flat = eid.reshape(-1)
    onehot = (e_flat[:, None] == jnp.arange(N_EXPERTS, dtype=I32)[None, :]).astype(I32)
    csum = jnp.cumsum(onehot, axis=0)
    rank = jnp.sum((csum - onehot) * onehot, axis=1)
    counts = csum[-1]
    ptiles = (counts + tm - 1) // tm
    tend = jnp.cumsum(ptiles)
    tstart = tend - ptiles
    slot = jnp.sum(onehot * tstart[None, :], axis=1) * tm + rank
    total = tend[-1]
    tile_ids = jnp.arange(n_tiles, dtype=I32)
    tile_valid = (tile_ids < total).astype(I32)
    tile_expert = jnp.sum((tile_ids[:, None] >= tend[None, :]).astype(I32), axis=1)
    last_expert = jnp.sum((total - 1 >= tend).astype(I32))
    tile_expert = jnp.minimum(tile_expert, last_expert).astype(I32)
    tok = jnp.tile(jnp.arange(T, dtype=I32), two)
    slot_token = jnp.zeros((n_tiles * tm,), I32).at[slot].set(tok, unique_indices=True)
    return slot.astype(I32), slot_token, tile_expert, tile_valid


def kernel(x, c, positions, rel_bias, ada_w, ada_b, norm1_g, w_in, lambda_q1, lambda_k1, lambda_q2,
           lambda_k2, subln_g, conv_w, w_out, norm2_g, router_group_w, router_group_b,
           router_expert_w, router_expert_b, expert_w_gate, expert_w_up, expert_w_down, final_g):
    B, S, D = x.shape
    T = B * S
    l = 0

    c_pad = jnp.zeros((8, D), F32).at[:B].set(c)
    ada = _ada(c_pad, ada_w[l], ada_b[l].reshape(1, -1))[:B]
    sh1, sc1, g1, sh2, sc2, g2 = [a.reshape(B, 1, D) for a in jnp.split(ada, 6, axis=-1)]

    w_in_bf = w_in[l].astype(BF16)
    A = ATTN_WIDTH
    qt, k, vt, conv = _inproj(x, sc1, sh1, norm1_g[l].reshape(1, D), w_in_bf[:, 0:A].T,
                              w_in_bf[:, A:2 * A], w_in_bf[:, 2 * A:3 * A].T, w_in_bf[:, 3 * A:],
                              conv_w[l])
    lut = (rel_bias.astype(F32)[_rel_bucket_table(), :].T * LOG2E).reshape(N_DIFF_HEADS, 2, BIAS_LUT)
    lam_params = jnp.stack([lambda_q1[l], lambda_k1[l], lambda_q2[l], lambda_k2[l]]).astype(F32)
    attn = _attention(qt, k, vt, positions, lut, lam_params, subln_g[l].reshape(V_HEAD_DIM, 1))

    wr_t = jnp.zeros((ROUTER_ROWS, D), F32)
    wr_t = wr_t.at[0:N_GROUPS].set(router_group_w[l].T)
    wr_t = wr_t.at[N_GROUPS:N_GROUPS + N_EXPERTS].set(router_expert_w[l].T).astype(BF16)
    rb = jnp.zeros((ROUTER_ROWS, 1), F32)
    rb = rb.at[0:N_GROUPS, 0].set(router_group_b[l])
    rb = rb.at[N_GROUPS:N_GROUPS + N_EXPERTS, 0].set(router_expert_b[l])
    x1, hp, ri, rw = _outproj(attn, conv, x, g1, sc2, sh2, norm2_g[l].reshape(1, D),
                              w_out[l].astype(BF16), wr_t, rb)

    eid = ri[:, 0:2, :].transpose(1, 0, 2).reshape(2, T)
    w_tok = rw[:, 0:2, :].transpose(0, 2, 1).reshape(T, 2)
    n_tiles = 2 * T // MOE_TM + N_EXPERTS
    slot, slot_token, tile_expert, tile_valid = _route_plan(eid, n_tiles, MOE_TM)

    ys = _moe(tile_expert, tile_valid, slot_token.reshape(n_tiles, 1, MOE_TM), hp.reshape(T * ROW_SLAB, 128),
              expert_w_gate[l].astype(BF16), expert_w_up[l].astype(BF16),
              expert_w_down[l].astype(BF16))

    nct = T // COMB_TM
    pos = slot.reshape(2, nct, 1, COMB_TM)
    pos_tiles = jnp.concatenate([pos[0], pos[1]], axis=2)
    out = _combine(pos_tiles, ys, x1.reshape(T, D), g2, w_tok, final_g.reshape(1, D), S)
    return out.reshape(B, S, D)
```

```python
import functools
import math

import jax
import jax.numpy as jnp
from jax import lax
from jax.experimental import pallas as pl
from jax.experimental.pallas import tpu as pltpu

F32 = jnp.float32
BF16 = jnp.bfloat16
I32 = jnp.int32
U32 = jnp.uint32

D_MODEL = 1024
ATTN_WIDTH = 512
CONV_WIDTH = 512
N_DIFF_HEADS = 4
DIFF_HEAD_DIM = 64
V_HEAD_DIM = 128
IN_PROJ_WIDTH = 3 * ATTN_WIDTH + 3 * CONV_WIDTH
CONV_K = 3
N_BUCKETS = 32
MAX_DISTANCE = 128
N_GROUPS = 4
EXPERTS_PER_GROUP = 4
N_EXPERTS = 16
D_EXPERT = 512
NORM_EPS = 1e-6
SUBLN_EPS = 1e-5
NEG_INF = -1e30
LAMBDA_INIT = 0.8 - 0.6 * math.exp(-0.3 * 0)
QK_SCALE = DIFF_HEAD_DIM ** -0.5
LOG2E = math.log2(math.e)

BIAS_LUT = 128

ROW_TILE = 512
ATT_TQ = 512
ATT_CHAIN = 256
ATT_GROUP = 8
ATT_LOOKAHEAD = 4
ATT_TK = 256
MOE_TM = 256
COMB_TM = 256
ROUTER_ROWS = 32
ROW_SLAB = D_MODEL // 256
VMEM_LIMIT = 56 * 1024 * 1024


def _silu(x):
    return x * (1.0 / (1.0 + jnp.exp(-x)))


def _ada_kernel(c_ref, w_ref, b_ref, o_ref):
    s = _silu(c_ref[...])
    o_ref[...] = jnp.dot(s, w_ref[...], preferred_element_type=F32,
                         precision=lax.Precision.HIGHEST) + b_ref[...]


def _ada(c_pad, w, b):
    n = w.shape[1]
    bn = 1024
    return pl.pallas_call(
        _ada_kernel,
        grid=(n // bn,),
        in_specs=[pl.BlockSpec((8, D_MODEL), lambda j: (0, 0)),
                  pl.BlockSpec((D_MODEL, bn), lambda j: (0, j)),
                  pl.BlockSpec((1, bn), lambda j: (0, j))],
        out_specs=pl.BlockSpec((8, bn), lambda j: (0, j)),
        out_shape=jax.ShapeDtypeStruct((8, n), F32),
        name="ada",
    )(c_pad, w, b)


def _inproj_kernel(x_ref, sc_ref, sh_ref, g_ref, wqt_ref, wk_ref, wvt_ref, wc_ref, cw_ref,
                   qt_ref, k_ref, vt_ref, conv_ref, carry_ref):
    j = pl.program_id(1)
    tm = x_ref.shape[1]
    x = x_ref[0]
    ms = jnp.mean(x * x, axis=-1, keepdims=True)
    h = x * lax.rsqrt(ms + NORM_EPS) * g_ref[...]
    h = h * (1.0 + sc_ref[0]) + sh_ref[0]
    hb = h.astype(BF16)
    nt = (((1,), (1,)), ((), ()))

    def proj(c0):
        return jnp.dot(hb, wc_ref[:, c0:c0 + 512], preferred_element_type=F32)

    qt = lax.dot_general(wqt_ref[...], hb, nt, preferred_element_type=F32)
    qt_ref[0] = (qt * (QK_SCALE * LOG2E)).astype(BF16)
    k_ref[0] = jnp.dot(hb, wk_ref[...], preferred_element_type=F32).astype(BF16)
    vt_ref[0] = lax.dot_general(wvt_ref[...], hb, nt, preferred_element_type=F32).astype(BF16)
    gate_b = proj(0)
    u = proj(512) * proj(1024)

    @pl.when(j == 0)
    def _():
        carry_ref[...] = jnp.zeros_like(carry_ref)

    prev = carry_ref[...]
    row = lax.broadcasted_iota(I32, u.shape, 0)
    u1 = pltpu.roll(u, 1, axis=0)
    u2 = pltpu.roll(u, 2, axis=0)
    u1 = jnp.where(row == 0, prev[7:8, :], u1)
    u2 = jnp.where(row == 0, prev[6:7, :], jnp.where(row == 1, prev[7:8, :], u2))
    conv = cw_ref[0:1, :] * u2 + cw_ref[1:2, :] * u1 + cw_ref[2:3, :] * u
    conv_ref[0] = (gate_b * conv).astype(BF16)
    carry_ref[...] = u[tm - 8:tm, :]


def _inproj(x, sc1, sh1, g1n, wq_t, wk, wv_t, wc, conv_w):
    B, S, D = x.shape
    tm = ROW_TILE
    row_out = jax.ShapeDtypeStruct((B, S, 512), BF16)
    col_out = jax.ShapeDtypeStruct((B, 512, S), BF16)
    row_spec = pl.BlockSpec((1, tm, 512), lambda b, j: (b, j, 0))
    col_spec = pl.BlockSpec((1, 512, tm), lambda b, j: (b, 0, j))
    mod_spec = pl.BlockSpec((1, 1, D), lambda b, j: (b, 0, 0))
    const2 = lambda b, j: (0, 0)
    return pl.pallas_call(
        _inproj_kernel,
        grid=(B, S // tm),
        in_specs=[pl.BlockSpec((1, tm, D), lambda b, j: (b, j, 0)),
                  mod_spec, mod_spec,
                  pl.BlockSpec((1, D), const2),
                  pl.BlockSpec((ATTN_WIDTH, D), const2),
                  pl.BlockSpec((D, ATTN_WIDTH), const2),
                  pl.BlockSpec((ATTN_WIDTH, D), const2),
                  pl.BlockSpec((D, 3 * CONV_WIDTH), const2),
                  pl.BlockSpec((CONV_K, CONV_WIDTH), const2)],
        out_specs=[col_spec, row_spec, col_spec, row_spec],
        out_shape=[col_out, row_out, col_out, row_out],
        scratch_shapes=[pltpu.VMEM((8, CONV_WIDTH), F32)],
        compiler_params=pltpu.CompilerParams(
            dimension_semantics=("arbitrary", "arbitrary"), vmem_limit_bytes=VMEM_LIMIT),
        name="inproj",
    )(x, sc1, sh1, g1n, wq_t, wk, wv_t, wc, conv_w)


def _attn_kernel(qmin_ref, kmax_ref, consec_ref, qt_ref, k_ref, vt_ref, pr_ref, pc_ref, lut_ref,
                 lam_ref, sg_ref, o_ref, acc_ref, tz_ref):
    b = pl.program_id(0)
    qi = pl.program_id(2)
    nq = pl.num_programs(2)
    tq, tk = ATT_TQ, ATT_TK
    hw = ATT_CHAIN
    n_half = tq // hw
    nk = nq * (tq // tk)
    qt = qt_ref[0]
    feat = lax.broadcasted_iota(I32, qt.shape, 0)
    zero = jnp.zeros_like(qt)
    qts = (jnp.where(feat < DIFF_HEAD_DIM, qt, zero), jnp.where(feat >= DIFF_HEAD_DIM, qt, zero))
    luts = (lut_ref[0, 0:1, :], lut_ref[0, 1:2, :])
    fars = tuple(t[:, BIAS_LUT - 1:BIAS_LUT] for t in luts)
    pq = pr_ref[0]
    qmin = qmin_ref[b * nq + qi]
    czero = jnp.zeros((1, 1), F32)
    chains = [(mi, hi) for mi in range(2) for hi in range(n_half)]

    def gather_bias(mi, dist):
        table = jnp.broadcast_to(luts[mi], (tk, BIAS_LUT))
        return jnp.concatenate([jnp.take_along_axis(table, dist[:, o:o + 128], axis=1)
                                for o in range(0, hw, 128)], axis=1)

    @pl.when(qi == 0)
    def _():
        delta = (lax.broadcasted_iota(I32, (tk, hw), 1) - lax.broadcasted_iota(I32, (tk, hw), 0))
        for mi in range(2):
            diag_bias = gather_bias(mi, jnp.clip(delta, 0, BIAS_LUT - 1))
            tz_ref[mi, 0] = jnp.where(delta >= 0, diag_bias, NEG_INF)
            tz_ref[mi, 1] = gather_bias(mi, jnp.clip(delta + tk, 0, BIAS_LUT - 1))

    def run_blocks(blocks, state):
        loaded = []
        for (j, kinds) in blocks:
            ks = pl.multiple_of(j * tk, tk)
            kb = k_ref[0, pl.ds(ks, tk), :]
            vtb = vt_ref[0, :, pl.ds(ks, tk)]
            dist = None
            if any(kd is not None and kd.startswith("near") for kd in kinds):
                pk = pc_ref[0, pl.ds(ks, tk), :]
                dist = jnp.clip(pq - pk, 0, BIAS_LUT - 1)
            loaded.append((kb, vtb, dist))
        items = [(bi, n) for bi, blk in enumerate(blocks) for n, (mi, hi) in enumerate(chains)
                 if blk[1][hi] is not None]
        scores = {}

        def issue_qk(t):
            bi, n = items[t]
            mi, hi = chains[n]
            scores[t] = jnp.dot(loaded[bi][0], qts[mi][:, hi * hw:(hi + 1) * hw],
                                preferred_element_type=F32)

        state = list(state)
        for t in range(min(ATT_LOOKAHEAD, len(items))):
            issue_qk(t)
        for t, (bi, n) in enumerate(items):
            if t + ATT_LOOKAHEAD < len(items):
                issue_qk(t + ATT_LOOKAHEAD)
            _, vtb, dist = loaded[bi]
            mi, hi = chains[n]
            kind = blocks[bi][1][hi]
            cols = slice(hi * hw, (hi + 1) * hw)
            m, l = state[n]
            s = scores.pop(t)
            c = czero
            if kind == "far":
                c = fars[mi]
            elif kind == "tz_diag":
                s = tz_ref[mi, 0] + s
            elif kind == "tz_sub":
                s = tz_ref[mi, 1] + s
            else:
                s = gather_bias(mi, dist[:, cols]) + s
                if kind == "near_masked":
                    keep = (lax.broadcasted_iota(I32, (tk, hw), 0)
                            <= lax.broadcasted_iota(I32, (tk, hw), 1))
                    s = jnp.where(keep, s, NEG_INF)
            mn = jnp.maximum(m, jnp.max(s, axis=0, keepdims=True) + c)
            alpha = jnp.exp2(m - mn)
            p = jnp.exp2(s - (mn - c))
            l = alpha * l + jnp.sum(p, axis=0, keepdims=True)
            acc_ref[mi, :, cols] = alpha * acc_ref[mi, :, cols] + jnp.dot(
                vtb, p.astype(BF16), preferred_element_type=F32)
            state[n] = (mn, l)
        return tuple(state)

    def block_is_far(j):
        return qmin - kmax_ref[b * nk + j] >= BIAS_LUT - 1

    def one_block(j, state):
        return lax.cond(block_is_far(j), lambda st: run_blocks([(j, ("far",) * n_half)], st),
                        lambda st: run_blocks([(j, ("near",) * n_half)], st), state)

    def group_body(width):
        def body(g, carry):
            j0, state = carry
            all_far = block_is_far(j0)
            for u in range(1, width):
                all_far = jnp.logical_and(all_far, block_is_far(j0 + u))
            far_blocks = [(j0 + u, ("far",) * n_half) for u in range(width)]
            state = lax.cond(
                all_far, lambda st: run_blocks(far_blocks, st),
                lambda st: lax.fori_loop(0, width, lambda u, s2: one_block(j0 + u, s2), st), state)
            return j0 + width, state
        return body

    acc_ref[...] = jnp.zeros_like(acc_ref)
    m0 = jnp.full((1, hw), NEG_INF, F32)
    l0 = jnp.zeros((1, hw), F32)
    state = tuple((m0, l0) for _ in chains)
    assert tq == 2 * tk and hw == tk
    n_full = 2 * qi
    n_main = jnp.maximum(n_full - 2, 0)
    j0 = jnp.int32(0)
    width = ATT_GROUP
    j0, state = lax.fori_loop(0, n_main // width, group_body(width), (j0, state))
    rem = n_main % width
    while width > 2:
        width //= 2
        j0, state = lax.fori_loop(0, (rem // width) % 2, group_body(width), (j0, state))

    consec = consec_ref[b * nq + qi] == 1
    diag_fast = [(n_full, ("tz_diag", "tz_sub")), (n_full + 1, (None, "tz_diag"))]
    diag_any = [(n_full, ("near_masked", "near")), (n_full + 1, (None, "near_masked"))]
    below_fast = [(n_full - 2, ("far", "far")), (n_full - 1, ("tz_sub", "far"))]
    below_any = [(n_full - 2, ("near", "near")), (n_full - 1, ("near", "near"))]

    def tail(below, diag):
        return lambda st: lax.cond(qi > 0, lambda s2: run_blocks(below + diag, s2),
                                   lambda s2: run_blocks(diag, s2), st)

    state = lax.cond(consec, tail(below_fast, diag_fast), tail(below_any, diag_any), state)

    l1 = jnp.concatenate([state[n][1] for n, (mi, hi) in enumerate(chains) if mi == 0], axis=1)
    l2 = jnp.concatenate([state[n][1] for n, (mi, hi) in enumerate(chains) if mi == 1], axis=1)
    lam = (jnp.exp(jnp.sum(lam_ref[0:1, :] * lam_ref[1:2, :], axis=-1, keepdims=True))
           - jnp.exp(jnp.sum(lam_ref[2:3, :] * lam_ref[3:4, :], axis=-1, keepdims=True))
           + LAMBDA_INIT)
    ot = acc_ref[0] * (1.0 / l1) - (lam * (1.0 / l2)) * acc_ref[1]
    ot = ot * lax.rsqrt(jnp.mean(ot * ot, axis=0, keepdims=True) + SUBLN_EPS)
    ot = ot * (sg_ref[...] * (1.0 - LAMBDA_INIT))
    o_ref[0] = ot.T.astype(BF16)


def _attention(qt, k, vt, positions, lut, lam_params, subln_g_col):
    B, S, _ = k.shape
    tq = ATT_TQ
    nq = S // tq
    pos_col = positions.reshape(B, S, 1)
    pos_row = positions.reshape(B, 1, S)
    qmin = jnp.min(positions.reshape(B * nq, tq), axis=1)
    kmax = jnp.max(positions.reshape(B * (S // ATT_TK), ATT_TK), axis=1)
    step_ok = jnp.concatenate([positions[:, 1:] - positions[:, :-1] == 1,
                               jnp.ones((B, 1), jnp.bool_)], axis=1).reshape(B, nq, tq)
    inner_ok = jnp.all(step_ok[:, :, :tq - 1], axis=2)
    link_ok = jnp.concatenate([jnp.ones((B, 1), jnp.bool_), step_ok[:, :-1, tq - 1]], axis=1)
    prev_ok = jnp.concatenate([jnp.ones((B, 1), jnp.bool_), inner_ok[:, :-1]], axis=1)
    consec = (inner_ok & link_ok & prev_ok).astype(I32).reshape(B * nq)
    grid_spec = pltpu.PrefetchScalarGridSpec(
        num_scalar_prefetch=3,
        grid=(B, N_DIFF_HEADS, nq),
        in_specs=[pl.BlockSpec((1, 128, tq), lambda b, h, i, *_: (b, h, i)),
                  pl.BlockSpec((1, S, 128), lambda b, h, i, *_: (b, 0, h)),
                  pl.BlockSpec((1, 128, S), lambda b, h, i, *_: (b, h, 0)),
                  pl.BlockSpec((1, 1, tq), lambda b, h, i, *_: (b, 0, i)),
                  pl.BlockSpec((1, S, 1), lambda b, h, i, *_: (b, 0, 0)),
                  pl.BlockSpec((1, 2, BIAS_LUT), lambda b, h, i, *_: (h, 0, 0)),
                  pl.BlockSpec((4, DIFF_HEAD_DIM), lambda b, h, i, *_: (0, 0)),
                  pl.BlockSpec((V_HEAD_DIM, 1), lambda b, h, i, *_: (0, 0))],
        out_specs=pl.BlockSpec((1, tq, 128), lambda b, h, i, *_: (b, i, h)),
        scratch_shapes=[pltpu.VMEM((2, V_HEAD_DIM, tq), F32),
                        pltpu.VMEM((2, 2, ATT_TK, ATT_CHAIN), F32)],
    )
    return pl.pallas_call(
        _attn_kernel,
        grid_spec=grid_spec,
        out_shape=jax.ShapeDtypeStruct((B, S, ATTN_WIDTH), BF16),
        compiler_params=pltpu.CompilerParams(
            dimension_semantics=("arbitrary", "arbitrary", "arbitrary"),
            vmem_limit_bytes=VMEM_LIMIT),
        name="diffattn",
    )(qmin, kmax, consec, qt, k, vt, pos_row, pos_col, lut, lam_params, subln_g_col)


def _outproj_kernel(at_ref, cv_ref, x_ref, g1_ref, sc_ref, sh_ref, gn_ref, wo_ref, wr_ref, rb_ref,
                    x1_ref, hp_ref, ri_ref, rw_ref):
    tm = x_ref.shape[1]
    mix = (jnp.dot(at_ref[0], wo_ref[0:ATTN_WIDTH, :], preferred_element_type=F32)
           + jnp.dot(cv_ref[0], wo_ref[ATTN_WIDTH:, :], preferred_element_type=F32))
    x1 = x_ref[0] + g1_ref[0] * mix
    x1_ref[0] = x1
    ms = jnp.mean(x1 * x1, axis=-1, keepdims=True)
    h = x1 * lax.rsqrt(ms + NORM_EPS) * gn_ref[...]
    h = h * (1.0 + sc_ref[0]) + sh_ref[0]
    hb = h.astype(BF16)

    _store_packed_rows(hp_ref.at[0], h)

    lg_all = lax.dot_general(wr_ref[...], hb, (((1,), (1,)), ((), ())),
                             preferred_element_type=F32) + rb_ref[...]
    lg = lg_all[0:N_GROUPS, :]
    le = lg_all[N_GROUPS:N_GROUPS + N_EXPERTS, :]
    row4 = lax.broadcasted_iota(I32, (N_GROUPS, tm), 0)
    gmax = jnp.max(lg, axis=0, keepdims=True)
    pg_sel = 1.0 / jnp.sum(jnp.exp(lg - gmax), axis=0, keepdims=True)
    gsel = jnp.min(jnp.where(lg == gmax, row4, N_GROUPS), axis=0, keepdims=True)
    sel = jnp.zeros((EXPERTS_PER_GROUP, tm), F32)
    for g in range(N_GROUPS):
        sel = jnp.where(gsel == g, le[g * EXPERTS_PER_GROUP:(g + 1) * EXPERTS_PER_GROUP, :], sel)
    v1 = jnp.max(sel, axis=0, keepdims=True)
    i1 = jnp.min(jnp.where(sel == v1, row4, EXPERTS_PER_GROUP), axis=0, keepdims=True)
    rest = jnp.where(row4 == i1, -jnp.inf, sel)
    v2 = jnp.max(rest, axis=0, keepdims=True)
    i2 = jnp.min(jnp.where(rest == v2, row4, EXPERTS_PER_GROUP), axis=0, keepdims=True)
    e2 = jnp.exp(v2 - v1)
    w1 = pg_sel / (1.0 + e2)
    w2 = pg_sel * e2 / (1.0 + e2)
    row8 = lax.broadcasted_iota(I32, (8, tm), 0)
    eid1 = gsel * EXPERTS_PER_GROUP + i1
    eid2 = gsel * EXPERTS_PER_GROUP + i2
    ri_ref[0] = jnp.where(row8 == 0, eid1, jnp.where(row8 == 1, eid2, 0))
    rw_ref[0] = jnp.where(row8 == 0, w1, jnp.where(row8 == 1, w2, 0.0))


def _outproj(attn, conv, x, g1, sc2, sh2, g2n, w_out_bf, wr_t, rb):
    B, S, D = x.shape
    tm = ROW_TILE
    half_spec = pl.BlockSpec((1, tm, 512), lambda b, j: (b, j, 0))
    full_spec = pl.BlockSpec((1, tm, D), lambda b, j: (b, j, 0))
    mod_spec = pl.BlockSpec((1, 1, D), lambda b, j: (b, 0, 0))
    rt_spec = pl.BlockSpec((1, 8, tm), lambda b, j: (b, 0, j))
    return pl.pallas_call(
        _outproj_kernel,
        grid=(B, S // tm),
        in_specs=[half_spec, half_spec, full_spec, mod_spec, mod_spec, mod_spec,
                  pl.BlockSpec((1, D), lambda b, j: (0, 0)),
                  pl.BlockSpec((D, D), lambda b, j: (0, 0)),
                  pl.BlockSpec((ROUTER_ROWS, D), lambda b, j: (0, 0)),
                  pl.BlockSpec((ROUTER_ROWS, 1), lambda b, j: (0, 0))],
        out_specs=[full_spec, pl.BlockSpec((1, tm * ROW_SLAB, 128), lambda b, j: (b, j, 0)),
                   rt_spec, rt_spec],
        out_shape=[jax.ShapeDtypeStruct((B, S, D), F32),
                   jax.ShapeDtypeStruct((B, S * ROW_SLAB, 128), U32),
                   jax.ShapeDtypeStruct((B, 8, S), I32),
                   jax.ShapeDtypeStruct((B, 8, S), F32)],
        compiler_params=pltpu.CompilerParams(
            dimension_semantics=("arbitrary", "arbitrary"), vmem_limit_bytes=VMEM_LIMIT),
        name="outproj",
    )(attn, conv, x, g1, sc2, sh2, g2n, w_out_bf, wr_t, rb)


def _store_packed_rows(dst, x):
    half = D_MODEL // 2
    xb = x.astype(BF16).astype(F32)
    packed = (pltpu.bitcast(xb[:, :half], U32) >> 16) | (
        pltpu.bitcast(xb[:, half:], U32) & jnp.uint32(0xFFFF0000))
    rows = x.shape[0]
    for c in range(ROW_SLAB):
        dst[pl.ds(c, rows, stride=ROW_SLAB), :] = packed[:, c * 128:(c + 1) * 128]


def _load_packed_rows(src, row0, rows):
    packed = jnp.concatenate(
        [src[pl.ds(row0 * ROW_SLAB + c, rows, stride=ROW_SLAB), :] for c in range(ROW_SLAB)], axis=1)
    lo = pltpu.bitcast(packed << 16, F32)
    hi = pltpu.bitcast(packed & jnp.uint32(0xFFFF0000), F32)
    return lo, hi


def _row_gather_start(src_hbm, idx_ref, dst, sem, n_rows):
    def body(r, c):
        t = idx_ref[0, 0, r]
        pltpu.make_async_copy(src_hbm.at[pl.ds(pl.multiple_of(t * ROW_SLAB, ROW_SLAB), ROW_SLAB)],
                              dst.at[pl.ds(pl.multiple_of(r * ROW_SLAB, ROW_SLAB), ROW_SLAB)],
                              sem).start()
        return c
    lax.fori_loop(0, n_rows, body, 0, unroll=8)


def _row_gather_wait(src_hbm, dst, sem, n_rows):
    pltpu.make_async_copy(src_hbm.at[pl.ds(0, n_rows * ROW_SLAB)], dst, sem).wait()


def _moe_kernel(te_ref, tv_ref, tok_ref, tokn_ref, hp_hbm, wg_ref, wu_ref, wd_ref, y_ref,
                xbuf, sem):
    i = pl.program_id(0)
    n = pl.num_programs(0)
    slot = lax.rem(i, 2)
    nslot = 1 - slot
    tm = MOE_TM

    @pl.when(i == 0)
    def _():
        _row_gather_start(hp_hbm, tok_ref, xbuf.at[0], sem.at[0], tm)

    nxt = jnp.minimum(i + 1, n - 1)

    @pl.when(jnp.logical_and(i + 1 < n, tv_ref[nxt] == 1))
    def _():
        _row_gather_start(hp_hbm, tokn_ref, xbuf.at[nslot], sem.at[nslot], tm)

    @pl.when(tv_ref[i] == 1)
    def _():
        _row_gather_wait(hp_hbm, xbuf.at[slot], sem.at[slot], tm)
        half = D_MODEL // 2
        lo, hi = _load_packed_rows(xbuf.at[slot], 0, tm)
        lo = lo.astype(BF16)
        hi = hi.astype(BF16)
        g = (jnp.dot(lo, wg_ref[0, 0:half, :], preferred_element_type=F32)
             + jnp.dot(hi, wg_ref[0, half:, :], preferred_element_type=F32))
        u = (jnp.dot(lo, wu_ref[0, 0:half, :], preferred_element_type=F32)
             + jnp.dot(hi, wu_ref[0, half:, :], preferred_element_type=F32))
        hid = (_silu(g) * u).astype(BF16)
        _store_packed_rows(y_ref, jnp.dot(hid, wd_ref[0], preferred_element_type=F32))

    @pl.when(tv_ref[i] == 0)
    def _():
        y_ref[...] = jnp.zeros_like(y_ref)


def _moe(tile_expert, tile_valid, slot_token, hp, wg_bf, wu_bf, wd_bf):
    nt = tile_expert.shape[0]
    tm = MOE_TM
    D = D_MODEL
    grid_spec = pltpu.PrefetchScalarGridSpec(
        num_scalar_prefetch=2,
        grid=(nt,),
        in_specs=[pl.BlockSpec((1, 1, tm), lambda i, te, tv: (i, 0, 0), memory_space=pltpu.SMEM),
                  pl.BlockSpec((1, 1, tm), lambda i, te, tv: (jnp.minimum(i + 1, nt - 1), 0, 0),
                               memory_space=pltpu.SMEM),
                  pl.BlockSpec(memory_space=pl.ANY),
                  pl.BlockSpec((1, D, D_EXPERT), lambda i, te, tv: (te[i], 0, 0)),
                  pl.BlockSpec((1, D, D_EXPERT), lambda i, te, tv: (te[i], 0, 0)),
                  pl.BlockSpec((1, D_EXPERT, D), lambda i, te, tv: (te[i], 0, 0))],
        out_specs=pl.BlockSpec((tm * ROW_SLAB, 128), lambda i, te, tv: (i, 0)),
        scratch_shapes=[pltpu.VMEM((2, tm * ROW_SLAB, 128), U32),
                        pltpu.SemaphoreType.DMA((2,))],
    )
    return pl.pallas_call(
        _moe_kernel,
        grid_spec=grid_spec,
        out_shape=jax.ShapeDtypeStruct((nt * tm * ROW_SLAB, 128), U32),
        compiler_params=pltpu.CompilerParams(
            dimension_semantics=("arbitrary",), vmem_limit_bytes=VMEM_LIMIT),
        name="moe",
    )(tile_expert, tile_valid, slot_token, slot_token, hp, wg_bf, wu_bf, wd_bf)


def _combine_kernel(pos_ref, posn_ref, ys_hbm, x1_ref, g2_ref, w_ref, fg_ref, o_ref, rbuf, sem):
    i = pl.program_id(0)
    n = pl.num_programs(0)
    slot = lax.rem(i, 2)
    nslot = 1 - slot
    tm = COMB_TM

    @pl.when(i == 0)
    def _():
        _row_gather_start(ys_hbm, pos_ref, rbuf.at[0], sem.at[0], 2 * tm)

    @pl.when(i + 1 < n)
    def _():
        _row_gather_start(ys_hbm, posn_ref, rbuf.at[nslot], sem.at[nslot], 2 * tm)

    _row_gather_wait(ys_hbm, rbuf.at[slot], sem.at[slot], 2 * tm)
    w = w_ref[...]
    r1 = jnp.concatenate(_load_packed_rows(rbuf.at[slot], 0, tm), axis=1)
    r2 = jnp.concatenate(_load_packed_rows(rbuf.at[slot], tm, tm), axis=1)
    moe = w[:, 0:1] * r1 + w[:, 1:2] * r2
    y = x1_ref[...] + g2_ref[0] * moe
    ms = jnp.mean(y * y, axis=-1, keepdims=True)
    o_ref[...] = y * lax.rsqrt(ms + NORM_EPS) * fg_ref[...]


def _combine(pos_tiles, ys, x1, g2, w_tok, final_g, seq_len):
    T, D = x1.shape
    tm = COMB_TM
    nt = T // tm
    per_b = seq_len // tm
    return pl.pallas_call(
        _combine_kernel,
        grid=(nt,),
        in_specs=[pl.BlockSpec((1, 1, 2 * tm), lambda i: (i, 0, 0), memory_space=pltpu.SMEM),
                  pl.BlockSpec((1, 1, 2 * tm), lambda i: (jnp.minimum(i + 1, nt - 1), 0, 0),
                               memory_space=pltpu.SMEM),
                  pl.BlockSpec(memory_space=pl.ANY),
                  pl.BlockSpec((tm, D), lambda i: (i, 0)),
                  pl.BlockSpec((1, 1, D), lambda i: (i // per_b, 0, 0)),
                  pl.BlockSpec((tm, 2), lambda i: (i, 0)),
                  pl.BlockSpec((1, D), lambda i: (0, 0))],
        out_specs=pl.BlockSpec((tm, D), lambda i: (i, 0)),
        out_shape=jax.ShapeDtypeStruct((T, D), F32),
        scratch_shapes=[pltpu.VMEM((2, 2 * tm * ROW_SLAB, 128), U32),
                        pltpu.SemaphoreType.DMA((2,))],
        compiler_params=pltpu.CompilerParams(
            dimension_semantics=("arbitrary",), vmem_limit_bytes=VMEM_LIMIT),
        name="combine",
    )(pos_tiles, pos_tiles, ys, x1, g2, w_tok, final_g)


def _rel_bucket_table():
    n = jnp.arange(BIAS_LUT, dtype=I32)
    max_exact = N_BUCKETS // 2
    nf = jnp.maximum(n, 1).astype(F32)
    large = max_exact + (jnp.log(nf / max_exact) / math.log(MAX_DISTANCE / max_exact)
                         * (N_BUCKETS - max_exact)).astype(I32)
    large = jnp.minimum(large, N_BUCKETS - 1)
    return jnp.where(n < max_exact, n, large)


def _route_plan(eid, n_tiles, tm):
    two, T = eid.shape
    e_flat = eid.reshape(-1)
    onehot = (e_flat[:, None] == jnp.arange(N_EXPERTS, dtype=I32)[None, :]).astype(I32)
    csum = jnp.cumsum(onehot, axis=0)
    rank = jnp.sum((csum - onehot) * onehot, axis=1)
    counts = csum[-1]
    ptiles = (counts + tm - 1) // tm
    tend = jnp.cumsum(ptiles)
    tstart = tend - ptiles
    slot = jnp.sum(onehot * tstart[None, :], axis=1) * tm + rank
    total = tend[-1]
    tile_ids = jnp.arange(n_tiles, dtype=I32)
    tile_valid = (tile_ids < total).astype(I32)
    tile_expert = jnp.sum((tile_ids[:, None] >= tend[None, :]).astype(I32), axis=1)
    last_expert = jnp.sum((total - 1 >= tend).astype(I32))
    tile_expert = jnp.minimum(tile_expert, last_expert).astype(I32)
    tok = jnp.tile(jnp.arange(T, dtype=I32), two)
    slot_token = jnp.zeros((n_tiles * tm,), I32).at[slot].set(tok, unique_indices=True)
    return slot.astype(I32), slot_token, tile_expert, tile_valid


def kernel(x, c, positions, rel_bias, ada_w, ada_b, norm1_g, w_in, lambda_q1, lambda_k1, lambda_q2,
           lambda_k2, subln_g, conv_w, w_out, norm2_g, router_group_w, router_group_b,
           router_expert_w, router_expert_b, expert_w_gate, expert_w_up, expert_w_down, final_g):
    B, S, D = x.shape
    T = B * S
    l = 0

    c_pad = jnp.zeros((8, D), F32).at[:B].set(c)
    ada = _ada(c_pad, ada_w[l], ada_b[l].reshape(1, -1))[:B]
    sh1, sc1, g1, sh2, sc2, g2 = [a.reshape(B, 1, D) for a in jnp.split(ada, 6, axis=-1)]

    w_in_bf = w_in[l].astype(BF16)
    A = ATTN_WIDTH
    qt, k, vt, conv = _inproj(x, sc1, sh1, norm1_g[l].reshape(1, D), w_in_bf[:, 0:A].T,
                              w_in_bf[:, A:2 * A], w_in_bf[:, 2 * A:3 * A].T, w_in_bf[:, 3 * A:],
                              conv_w[l])
    lut = (rel_bias.astype(F32)[_rel_bucket_table(), :].T * LOG2E).reshape(N_DIFF_HEADS, 2, BIAS_LUT)
    lam_params = jnp.stack([lambda_q1[l], lambda_k1[l], lambda_q2[l], lambda_k2[l]]).astype(F32)
    attn = _attention(qt, k, vt, positions, lut, lam_params, subln_g[l].reshape(V_HEAD_DIM, 1))

    wr_t = jnp.zeros((ROUTER_ROWS, D), F32)
    wr_t = wr_t.at[0:N_GROUPS].set(router_group_w[l].T)
    wr_t = wr_t.at[N_GROUPS:N_GROUPS + N_EXPERTS].set(router_expert_w[l].T).astype(BF16)
    rb = jnp.zeros((ROUTER_ROWS, 1), F32)
    rb = rb.at[0:N_GROUPS, 0].set(router_group_b[l])
    rb = rb.at[N_GROUPS:N_GROUPS + N_EXPERTS, 0].set(router_expert_b[l])
    x1, hp, ri, rw = _outproj(attn, conv, x, g1, sc2, sh2, norm2_g[l].reshape(1, D),
                              w_out[l].astype(BF16), wr_t, rb)

    eid = ri[:, 0:2, :].transpose(1, 0, 2).reshape(2, T)
    w_tok = rw[:, 0:2, :].transpose(0, 2, 1).reshape(T, 2)
    n_tiles = 2 * T // MOE_TM + N_EXPERTS
    slot, slot_token, tile_expert, tile_valid = _route_plan(eid, n_tiles, MOE_TM)

    ys = _moe(tile_expert, tile_valid, slot_token.reshape(n_tiles, 1, MOE_TM), hp.reshape(T * ROW_SLAB, 128),
              expert_w_gate[l].astype(BF16), expert_w_up[l].astype(BF16),
              expert_w_down[l].astype(BF16))

    nct = T // COMB_TM
    pos = slot.reshape(2, nct, 1, COMB_TM)
    pos_tiles = jnp.concatenate([pos[0], pos[1]], axis=2)
    out = _combine(pos_tiles, ys, x1.reshape(T, D), g2, w_tok, final_g.reshape(1, D), S)
    return out.reshape(B, S, D)
```

```python
import functools
import math

import jax
import jax.numpy as jnp
from jax import lax
from jax.experimental import pallas as pl
from jax.experimental.pallas import tpu as pltpu

F32 = jnp.float32
BF16 = jnp.bfloat16
I32 = jnp.int32
U32 = jnp.uint32

D_MODEL = 1024
ATTN_WIDTH = 512
CONV_WIDTH = 512
N_DIFF_HEADS = 4
DIFF_HEAD_DIM = 64
V_HEAD_DIM = 128
IN_PROJ_WIDTH = 3 * ATTN_WIDTH + 3 * CONV_WIDTH
CONV_K = 3
N_BUCKETS = 32
MAX_DISTANCE = 128
N_GROUPS = 4
EXPERTS_PER_GROUP = 4
N_EXPERTS = 16
D_EXPERT = 512
NORM_EPS = 1e-6
SUBLN_EPS = 1e-5
NEG_INF = -1e30
LAMBDA_INIT = 0.8 - 0.6 * math.exp(-0.3 * 0)
QK_SCALE = DIFF_HEAD_DIM ** -0.5
LOG2E = math.log2(math.e)

BIAS_LUT = 128

ROW_TILE = 512
ATT_TQ = 512
ATT_CHAIN = 256
ATT_GROUP = 8
ATT_LOOKAHEAD = 4
ATT_TK = 256
MOE_TM = 256
COMB_TM = 256
ROUTER_ROWS = 32
ROW_SLAB = D_MODEL // 256
VMEM_LIMIT = 56 * 1024 * 1024


def _silu(x):
    return x * (1.0 / (1.0 + jnp.exp(-x)))


def _ada_kernel(c_ref, w_ref, b_ref, o_ref):
    s = _silu(c_ref[...])
    o_ref[...] = jnp.dot(s, w_ref[...], preferred_element_type=F32,
                         precision=lax.Precision.HIGHEST) + b_ref[...]


def _ada(c_pad, w, b):
    n = w.shape[1]
    bn = 1024
    return pl.pallas_call(
        _ada_kernel,
        grid=(n // bn,),
        in_specs=[pl.BlockSpec((8, D_MODEL), lambda j: (0, 0)),
                  pl.BlockSpec((D_MODEL, bn), lambda j: (0, j)),
                  pl.BlockSpec((1, bn), lambda j: (0, j))],
        out_specs=pl.BlockSpec((8, bn), lambda j: (0, j)),
        out_shape=jax.ShapeDtypeStruct((8, n), F32),
        name="ada",
    )(c_pad, w, b)


def _inproj_kernel(x_ref, sc_ref, sh_ref, g_ref, wqt_ref, wk_ref, wvt_ref, wc_ref, cw_ref,
                   qt_ref, k_ref, vt_ref, conv_ref, carry_ref):
    j = pl.program_id(1)
    tm = x_ref.shape[1]
    x = x_ref[0]
    ms = jnp.mean(x * x, axis=-1, keepdims=True)
    h = x * lax.rsqrt(ms + NORM_EPS) * g_ref[...]
    h = h * (1.0 + sc_ref[0]) + sh_ref[0]
    hb = h.astype(BF16)
    nt = (((1,), (1,)), ((), ()))

    def proj(c0):
        return jnp.dot(hb, wc_ref[:, c0:c0 + 512], preferred_element_type=F32)

    qt = lax.dot_general(wqt_ref[...], hb, nt, preferred_element_type=F32)
    qt_ref[0] = (qt * (QK_SCALE * LOG2E)).astype(BF16)
    k_ref[0] = jnp.dot(hb, wk_ref[...], preferred_element_type=F32).astype(BF16)
    vt_ref[0] = lax.dot_general(wvt_ref[...], hb, nt, preferred_element_type=F32).astype(BF16)
    gate_b = proj(0)
    u = proj(512) * proj(1024)

    @pl.when(j == 0)
    def _():
        carry_ref[...] = jnp.zeros_like(carry_ref)

    prev = carry_ref[...]
    row = lax.broadcasted_iota(I32, u.shape, 0)
    u1 = pltpu.roll(u, 1, axis=0)
    u2 = pltpu.roll(u, 2, axis=0)
    u1 = jnp.where(row == 0, prev[7:8, :], u1)
    u2 = jnp.where(row == 0, prev[6:7, :], jnp.where(row == 1, prev[7:8, :], u2))
    conv = cw_ref[0:1, :] * u2 + cw_ref[1:2, :] * u1 + cw_ref[2:3, :] * u
    conv_ref[0] = (gate_b * conv).astype(BF16)
    carry_ref[...] = u[tm - 8:tm, :]


def _inproj(x, sc1, sh1, g1n, wq_t, wk, wv_t, wc, conv_w):
    B, S, D = x.shape
    tm = ROW_TILE
    row_out = jax.ShapeDtypeStruct((B, S, 512), BF16)
    col_out = jax.ShapeDtypeStruct((B, 512, S), BF16)
    row_spec = pl.BlockSpec((1, tm, 512), lambda b, j: (b, j, 0))
    col_spec = pl.BlockSpec((1, 512, tm), lambda b, j: (b, 0, j))
    mod_spec = pl.BlockSpec((1, 1, D), lambda b, j: (b, 0, 0))
    const2 = lambda b, j: (0, 0)
    return pl.pallas_call(
        _inproj_kernel,
        grid=(B, S // tm),
        in_specs=[pl.BlockSpec((1, tm, D), lambda b, j: (b, j, 0)),
                  mod_spec, mod_spec,
                  pl.BlockSpec((1, D), const2),
                  pl.BlockSpec((ATTN_WIDTH, D), const2),
                  pl.BlockSpec((D, ATTN_WIDTH), const2),
                  pl.BlockSpec((ATTN_WIDTH, D), const2),
                  pl.BlockSpec((D, 3 * CONV_WIDTH), const2),
                  pl.BlockSpec((CONV_K, CONV_WIDTH), const2)],
        out_specs=[col_spec, row_spec, col_spec, row_spec],
        out_shape=[col_out, row_out, col_out, row_out],
        scratch_shapes=[pltpu.VMEM((8, CONV_WIDTH), F32)],
        compiler_params=pltpu.CompilerParams(
            dimension_semantics=("arbitrary", "arbitrary"), vmem_limit_bytes=VMEM_LIMIT),
        name="inproj",
    )(x, sc1, sh1, g1n, wq_t, wk, wv_t, wc, conv_w)


def _attn_kernel(qmin_ref, kmax_ref, consec_ref, qt_ref, k_ref, vt_ref, pr_ref, pc_ref, lut_ref,
                 lam_ref, sg_ref, o_ref, acc_ref, tz_ref):
    b = pl.program_id(0)
    qi = pl.program_id(2)
    nq = pl.num_programs(2)
    tq, tk = ATT_TQ, ATT_TK
    hw = ATT_CHAIN
    n_half = tq // hw
    nk = nq * (tq // tk)
    qt = qt_ref[0]
    feat = lax.broadcasted_iota(I32, qt.shape, 0)
    zero = jnp.zeros_like(qt)
    qts = (jnp.where(feat < DIFF_HEAD_DIM, qt, zero), jnp.where(feat >= DIFF_HEAD_DIM, qt, zero))
    luts = (lut_ref[0, 0:1, :], lut_ref[0, 1:2, :])
    fars = tuple(t[:, BIAS_LUT - 1:BIAS_LUT] for t in luts)
    pq = pr_ref[0]
    qmin = qmin_ref[b * nq + qi]
    czero = jnp.zeros((1, 1), F32)
    chains = [(mi, hi) for mi in range(2) for hi in range(n_half)]

    def gather_bias(mi, dist):
        table = jnp.broadcast_to(luts[mi], (tk, BIAS_LUT))
        return jnp.concatenate([jnp.take_along_axis(table, dist[:, o:o + 128], axis=1)
                                for o in range(0, hw, 128)], axis=1)

    @pl.when(qi == 0)
    def _():
        delta = (lax.broadcasted_iota(I32, (tk, hw), 1) - lax.broadcasted_iota(I32, (tk, hw), 0))
        for mi in range(2):
            diag_bias = gather_bias(mi, jnp.clip(delta, 0, BIAS_LUT - 1))
            tz_ref[mi, 0] = jnp.where(delta >= 0, diag_bias, NEG_INF)
            tz_ref[mi, 1] = gather_bias(mi, jnp.clip(delta + tk, 0, BIAS_LUT - 1))

    def run_blocks(blocks, state):
        loaded = []
        for (j, kinds) in blocks:
            ks = pl.multiple_of(j * tk, tk)
            kb = k_ref[0, pl.ds(ks, tk), :]
            vtb = vt_ref[0, :, pl.ds(ks, tk)]
            dist = None
            if any(kd is not None and kd.startswith("near") for kd in kinds):
                pk = pc_ref[0, pl.ds(ks, tk), :]
                dist = jnp.clip(pq - pk, 0, BIAS_LUT - 1)
            loaded.append((kb, vtb, dist))
        items = [(bi, n) for bi, blk in enumerate(blocks) for n, (mi, hi) in enumerate(chains)
                 if blk[1][hi] is not None]
        scores = {}

        def issue_qk(t):
            bi, n = items[t]
            mi, hi = chains[n]
            scores[t] = jnp.dot(loaded[bi][0], qts[mi][:, hi * hw:(hi + 1) * hw],
                                preferred_element_type=F32)

        state = list(state)
        for t in range(min(ATT_LOOKAHEAD, len(items))):
            issue_qk(t)
        for t, (bi, n) in enumerate(items):
            if t + ATT_LOOKAHEAD < len(items):
                issue_qk(t + ATT_LOOKAHEAD)
            _, vtb, dist = loaded[bi]
            mi, hi = chains[n]
            kind = blocks[bi][1][hi]
            cols = slice(hi * hw, (hi + 1) * hw)
            m, l = state[n]
            s = scores.pop(t)
            c = czero
            if kind == "far":
                c = fars[mi]
            elif kind == "tz_diag":
                s = tz_ref[mi, 0] + s
            elif kind == "tz_sub":
                s = tz_ref[mi, 1] + s
            else:
                s = gather_bias(mi, dist[:, cols]) + s
                if kind == "near_masked":
                    keep = (lax.broadcasted_iota(I32, (tk, hw), 0)
                            <= lax.broadcasted_iota(I32, (tk, hw), 1))
                    s = jnp.where(keep, s, NEG_INF)
            mn = jnp.maximum(m, jnp.max(s, axis=0, keepdims=True) + c)
            alpha = jnp.exp2(m - mn)
            p = jnp.exp2(s - (mn - c))
            l = alpha * l + jnp.sum(p, axis=0, keepdims=True)
            acc_ref[mi, :, cols] = alpha * acc_ref[mi, :, cols] + jnp.dot(
                vtb, p.astype(BF16), preferred_element_type=F32)
            state[n] = (mn, l)
        return tuple(state)

    def block_is_far(j):
        return qmin - kmax_ref[b * nk + j] >= BIAS_LUT - 1

    def one_block(j, state):
        return lax.cond(block_is_far(j), lambda st: run_blocks([(j, ("far",) * n_half)], st),
                        lambda st: run_blocks([(j, ("near",) * n_half)], st), state)

    def group_body(width):
        def body(g, carry):
            j0, state = carry
            all_far = block_is_far(j0)
            for u in range(1, width):
                all_far = jnp.logical_and(all_far, block_is_far(j0 + u))
            far_blocks = [(j0 + u, ("far",) * n_half) for u in range(width)]
            state = lax.cond(
                all_far, lambda st: run_blocks(far_blocks, st),
                lambda st: lax.fori_loop(0, width, lambda u, s2: one_block(j0 + u, s2), st), state)
            return j0 + width, state
        return body

    acc_ref[...] = jnp.zeros_like(acc_ref)
    m0 = jnp.full((1, hw), NEG_INF, F32)
    l0 = jnp.zeros((1, hw), F32)
    state = tuple((m0, l0) for _ in chains)
    assert tq == 2 * tk and hw == tk
    n_full = 2 * qi
    n_main = jnp.maximum(n_full - 2, 0)
    j0 = jnp.int32(0)
    width = ATT_GROUP
    j0, state = lax.fori_loop(0, n_main // width, group_body(width), (j0, state))
    rem = n_main % width
    while width > 2:
        width //= 2
        j0, state = lax.fori_loop(0, (rem // width) % 2, group_body(width), (j0, state))

    consec = consec_ref[b * nq + qi] == 1
    diag_fast = [(n_full, ("tz_diag", "tz_sub")), (n_full + 1, (None, "tz_diag"))]
    diag_any = [(n_full, ("near_masked", "near")), (n_full + 1, (None, "near_masked"))]
    below_fast = [(n_full - 2, ("far", "far")), (n_full - 1, ("tz_sub", "far"))]
    below_any = [(n_full - 2, ("near", "near")), (n_full - 1, ("near", "near"))]

    def tail(below, diag):
        return lambda st: lax.cond(qi > 0, lambda s2: run_blocks(below + diag, s2),
                                   lambda s2: run_blocks(diag, s2), st)

    state = lax.cond(consec, tail(below_fast, diag_fast), tail(below_any, diag_any), state)

    l1 = jnp.concatenate([state[n][1] for n, (mi, hi) in enumerate(chains) if mi == 0], axis=1)
    l2 = jnp.concatenate([state[n][1] for n, (mi, hi) in enumerate(chains) if mi == 1], axis=1)
    lam = (jnp.exp(jnp.sum(lam_ref[0:1, :] * lam_ref[1:2, :], axis=-1, keepdims=True))
           - jnp.exp(jnp.sum(lam_ref[2:3, :] * lam_ref[3:4, :], axis=-1, keepdims=True))
           + LAMBDA_INIT)
    ot = acc_ref[0] * (1.0 / l1) - (lam * (1.0 / l2)) * acc_ref[1]
    ot = ot * lax.rsqrt(jnp.mean(ot * ot, axis=0, keepdims=True) + SUBLN_EPS)
    ot = ot * (sg_ref[...] * (1.0 - LAMBDA_INIT))
    o_ref[0] = ot.T.astype(BF16)


def _attention(qt, k, vt, positions, lut, lam_params, subln_g_col):
    B, S, _ = k.shape
    tq = ATT_TQ
    nq = S // tq
    pos_col = positions.reshape(B, S, 1)
    pos_row = positions.reshape(B, 1, S)
    qmin = jnp.min(positions.reshape(B * nq, tq), axis=1)
    kmax = jnp.max(positions.reshape(B * (S // ATT_TK), ATT_TK), axis=1)
    step_ok = jnp.concatenate([positions[:, 1:] - positions[:, :-1] == 1,
                               jnp.ones((B, 1), jnp.bool_)], axis=1).reshape(B, nq, tq)
    inner_ok = jnp.all(step_ok[:, :, :tq - 1], axis=2)
    link_ok = jnp.concatenate([jnp.ones((B, 1), jnp.bool_), step_ok[:, :-1, tq - 1]], axis=1)
    prev_ok = jnp.concatenate([jnp.ones((B, 1), jnp.bool_), inner_ok[:, :-1]], axis=1)
    consec = (inner_ok & link_ok & prev_ok).astype(I32).reshape(B * nq)
    grid_spec = pltpu.PrefetchScalarGridSpec(
        num_scalar_prefetch=3,
        grid=(B, N_DIFF_HEADS, nq),
        in_specs=[pl.BlockSpec((1, 128, tq), lambda b, h, i, *_: (b, h, i)),
                  pl.BlockSpec((1, S, 128), lambda b, h, i, *_: (b, 0, h)),
                  pl.BlockSpec((1, 128, S), lambda b, h, i, *_: (b, h, 0)),
                  pl.BlockSpec((1, 1, tq), lambda b, h, i, *_: (b, 0, i)),
                  pl.BlockSpec((1, S, 1), lambda b, h, i, *_: (b, 0, 0)),
                  pl.BlockSpec((1, 2, BIAS_LUT), lambda b, h, i, *_: (h, 0, 0)),
                  pl.BlockSpec((4, DIFF_HEAD_DIM), lambda b, h, i, *_: (0, 0)),
                  pl.BlockSpec((V_HEAD_DIM, 1), lambda b, h, i, *_: (0, 0))],
        out_specs=pl.BlockSpec((1, tq, 128), lambda b, h, i, *_: (b, i, h)),
        scratch_shapes=[pltpu.VMEM((2, V_HEAD_DIM, tq), F32),
                        pltpu.VMEM((2, 2, ATT_TK, ATT_CHAIN), F32)],
    )
    return pl.pallas_call(
        _attn_kernel,
        grid_spec=grid_spec,
        out_shape=jax.ShapeDtypeStruct((B, S, ATTN_WIDTH), BF16),
        compiler_params=pltpu.CompilerParams(
            dimension_semantics=("arbitrary", "arbitrary", "arbitrary"),
            vmem_limit_bytes=VMEM_LIMIT),
        name="diffattn",
    )(qmin, kmax, consec, qt, k, vt, pos_row, pos_col, lut, lam_params, subln_g_col)


def _outproj_kernel(at_ref, cv_ref, x_ref, g1_ref, sc_ref, sh_ref, gn_ref, wo_ref, wr_ref, rb_ref,
                    x1_ref, hp_ref, ri_ref, rw_ref):
    tm = x_ref.shape[1]
    mix = (jnp.dot(at_ref[0], wo_ref[0:ATTN_WIDTH, :], preferred_element_type=F32)
           + jnp.dot(cv_ref[0], wo_ref[ATTN_WIDTH:, :], preferred_element_type=F32))
    x1 = x_ref[0] + g1_ref[0] * mix
    x1_ref[0] = x1
    ms = jnp.mean(x1 * x1, axis=-1, keepdims=True)
    h = x1 * lax.rsqrt(ms + NORM_EPS) * gn_ref[...]
    h = h * (1.0 + sc_ref[0]) + sh_ref[0]
    hb = h.astype(BF16)

    _store_packed_rows(hp_ref.at[0], h)

    lg_all = lax.dot_general(wr_ref[...], hb, (((1,), (1,)), ((), ())),
                             preferred_element_type=F32) + rb_ref[...]
    lg = lg_all[0:N_GROUPS, :]
    le = lg_all[N_GROUPS:N_GROUPS + N_EXPERTS, :]
    row4 = lax.broadcasted_iota(I32, (N_GROUPS, tm), 0)
    gmax = jnp.max(lg, axis=0, keepdims=True)
    pg_sel = 1.0 / jnp.sum(jnp.exp(lg - gmax), axis=0, keepdims=True)
    gsel = jnp.min(jnp.where(lg == gmax, row4, N_GROUPS), axis=0, keepdims=True)
    sel = jnp.zeros((EXPERTS_PER_GROUP, tm), F32)
    for g in range(N_GROUPS):
        sel = jnp.where(gsel == g, le[g * EXPERTS_PER_GROUP:(g + 1) * EXPERTS_PER_GROUP, :], sel)
    v1 = jnp.max(sel, axis=0, keepdims=True)
    i1 = jnp.min(jnp.where(sel == v1, row4, EXPERTS_PER_GROUP), axis=0, keepdims=True)
    rest = jnp.where(row4 == i1, -jnp.inf, sel)
    v2 = jnp.max(rest, axis=0, keepdims=True)
    i2 = jnp.min(jnp.where(rest == v2, row4, EXPERTS_PER_GROUP), axis=0, keepdims=True)
    e2 = jnp.exp(v2 - v1)
    w1 = pg_sel / (1.0 + e2)
    w2 = pg_sel * e2 / (1.0 + e2)
    row8 = lax.broadcasted_iota(I32, (8, tm), 0)
    eid1 = gsel * EXPERTS_PER_GROUP + i1
    eid2 = gsel * EXPERTS_PER_GROUP + i2
    ri_ref[0] = jnp.where(row8 == 0, eid1, jnp.where(row8 == 1, eid2, 0))
    rw_ref[0] = jnp.where(row8 == 0, w1, jnp.where(row8 == 1, w2, 0.0))


def _outproj(attn, conv, x, g1, sc2, sh2, g2n, w_out_bf, wr_t, rb):
    B, S, D = x.shape
    tm = ROW_TILE
    half_spec = pl.BlockSpec((1, tm, 512), lambda b, j: (b, j, 0))
    full_spec = pl.BlockSpec((1, tm, D), lambda b, j: (b, j, 0))
    mod_spec = pl.BlockSpec((1, 1, D), lambda b, j: (b, 0, 0))
    rt_spec = pl.BlockSpec((1, 8, tm), lambda b, j: (b, 0, j))
    return pl.pallas_call(
        _outproj_kernel,
        grid=(B, S // tm),
        in_specs=[half_spec, half_spec, full_spec, mod_spec, mod_spec, mod_spec,
                  pl.BlockSpec((1, D), lambda b, j: (0, 0)),
                  pl.BlockSpec((D, D), lambda b, j: (0, 0)),
                  pl.BlockSpec((ROUTER_ROWS, D), lambda b, j: (0, 0)),
                  pl.BlockSpec((ROUTER_ROWS, 1), lambda b, j: (0, 0))],
        out_specs=[full_spec, pl.BlockSpec((1, tm * ROW_SLAB, 128), lambda b, j: (b, j, 0)),
                   rt_spec, rt_spec],
        out_shape=[jax.ShapeDtypeStruct((B, S, D), F32),
                   jax.ShapeDtypeStruct((B, S * ROW_SLAB, 128), U32),
                   jax.ShapeDtypeStruct((B, 8, S), I32),
                   jax.ShapeDtypeStruct((B, 8, S), F32)],
        compiler_params=pltpu.CompilerParams(
            dimension_semantics=("arbitrary", "arbitrary"), vmem_limit_bytes=VMEM_LIMIT),
        name="outproj",
    )(attn, conv, x, g1, sc2, sh2, g2n, w_out_bf, wr_t, rb)


def _store_packed_rows(dst, x):
    half = D_MODEL // 2
    xb = x.astype(BF16).astype(F32)
    packed = (pltpu.bitcast(xb[:, :half], U32) >> 16) | (
        pltpu.bitcast(xb[:, half:], U32) & jnp.uint32(0xFFFF0000))
    rows = x.shape[0]
    for c in range(ROW_SLAB):
        dst[pl.ds(c, rows, stride=ROW_SLAB), :] = packed[:, c * 128:(c + 1) * 128]


def _load_packed_rows(src, row0, rows):
    packed = jnp.concatenate(
        [src[pl.ds(row0 * ROW_SLAB + c, rows, stride=ROW_SLAB), :] for c in range(ROW_SLAB)], axis=1)
    lo = pltpu.bitcast(packed << 16, F32)
    hi = pltpu.bitcast(packed & jnp.uint32(0xFFFF0000), F32)
    return lo, hi


def _row_gather_start(src_hbm, idx_ref, dst, sem, n_rows):
    for r in range(n_rows):
        off = pl.multiple_of(idx_ref[0, 0, r], ROW_SLAB)
        pltpu.make_async_copy(src_hbm.at[pl.ds(off, ROW_SLAB)],
                              dst.at[pl.ds(r * ROW_SLAB, ROW_SLAB)], sem).start()


def _row_gather_wait(src_hbm, dst, sem, n_rows):
    pltpu.make_async_copy(src_hbm.at[pl.ds(0, n_rows * ROW_SLAB)], dst, sem).wait()


def _dispatch_kernel(fill_ref, pos_ref, hp_ref, xs_hbm, zbuf, sem):
    i = pl.program_id(0)
    tm = COMB_TM
    tile_rows = MOE_TM * ROW_SLAB

    @pl.when(i == 0)
    def _():
        zbuf[...] = jnp.zeros_like(zbuf)

        def fill_copy(t):
            return pltpu.make_async_copy(
                zbuf, xs_hbm.at[pl.ds(pl.multiple_of(t * tile_rows, tile_rows), tile_rows)],
                sem.at[0])

        def start(t, c):
            @pl.when(fill_ref[t] == 1)
            def _():
                fill_copy(t).start()
            return c

        def wait(t, c):
            @pl.when(fill_ref[t] == 1)
            def _():
                fill_copy(t).wait()
            return c

        lax.fori_loop(0, fill_ref.shape[0], start, 0)
        lax.fori_loop(0, fill_ref.shape[0], wait, 0)

    for r in range(2 * tm):
        off = pl.multiple_of(pos_ref[0, 0, r], ROW_SLAB)
        pltpu.make_async_copy(hp_ref.at[pl.ds((r % tm) * ROW_SLAB, ROW_SLAB)],
                              xs_hbm.at[pl.ds(off, ROW_SLAB)], sem.at[1]).start()
    for _ in range(2):
        pltpu.make_async_copy(hp_ref, xs_hbm.at[pl.ds(0, tm * ROW_SLAB)], sem.at[1]).wait()


def _dispatch(tile_fill, pos_tiles, hp, n_tiles):
    T = hp.shape[0] // ROW_SLAB
    tm = COMB_TM
    grid_spec = pltpu.PrefetchScalarGridSpec(
        num_scalar_prefetch=1,
        grid=(T // tm,),
        in_specs=[pl.BlockSpec((1, 1, 2 * tm), lambda i, f: (i, 0, 0), memory_space=pltpu.SMEM),
                  pl.BlockSpec((tm * ROW_SLAB, 128), lambda i, f: (i, 0))],
        out_specs=pl.BlockSpec(memory_space=pl.ANY),
        scratch_shapes=[pltpu.VMEM((MOE_TM * ROW_SLAB, 128), U32),
                        pltpu.SemaphoreType.DMA((2,))],
    )
    return pl.pallas_call(
        _dispatch_kernel,
        grid_spec=grid_spec,
        out_shape=jax.ShapeDtypeStruct((n_tiles * MOE_TM * ROW_SLAB, 128), U32),
        compiler_params=pltpu.CompilerParams(
            dimension_semantics=("arbitrary",), vmem_limit_bytes=VMEM_LIMIT),
        name="dispatch",
    )(tile_fill, pos_tiles, hp)


def _moe_kernel(te_ref, tv_ref, xs_ref, wg_ref, wu_ref, wd_ref, y_ref):
    i = pl.program_id(0)
    tm = MOE_TM

    @pl.when(tv_ref[i] == 1)
    def _():
        half = D_MODEL // 2
        lo, hi = _load_packed_rows(xs_ref, 0, tm)
        lo = lo.astype(BF16)
        hi = hi.astype(BF16)
        g = (jnp.dot(lo, wg_ref[0, 0:half, :], preferred_element_type=F32)
             + jnp.dot(hi, wg_ref[0, half:, :], preferred_element_type=F32))
        u = (jnp.dot(lo, wu_ref[0, 0:half, :], preferred_element_type=F32)
             + jnp.dot(hi, wu_ref[0, half:, :], preferred_element_type=F32))
        hid = (_silu(g) * u).astype(BF16)
        _store_packed_rows(y_ref, jnp.dot(hid, wd_ref[0], preferred_element_type=F32))

    @pl.when(tv_ref[i] == 0)
    def _():
        y_ref[...] = jnp.zeros_like(y_ref)


def _moe(tile_expert, tile_valid, xs, wg_bf, wu_bf, wd_bf):
    nt = tile_expert.shape[0]
    tm = MOE_TM
    D = D_MODEL
    grid_spec = pltpu.PrefetchScalarGridSpec(
        num_scalar_prefetch=2,
        grid=(nt,),
        in_specs=[pl.BlockSpec((tm * ROW_SLAB, 128), lambda i, te, tv: (i, 0)),
                  pl.BlockSpec((1, D, D_EXPERT), lambda i, te, tv: (te[i], 0, 0)),
                  pl.BlockSpec((1, D, D_EXPERT), lambda i, te, tv: (te[i], 0, 0)),
                  pl.BlockSpec((1, D_EXPERT, D), lambda i, te, tv: (te[i], 0, 0))],
        out_specs=pl.BlockSpec((tm * ROW_SLAB, 128), lambda i, te, tv: (i, 0)),
    )
    return pl.pallas_call(
        _moe_kernel,
        grid_spec=grid_spec,
        out_shape=jax.ShapeDtypeStruct((nt * tm * ROW_SLAB, 128), U32),
        compiler_params=pltpu.CompilerParams(
            dimension_semantics=("arbitrary",), vmem_limit_bytes=VMEM_LIMIT),
        name="moe",
    )(tile_expert, tile_valid, xs, wg_bf, wu_bf, wd_bf)


def _combine_kernel(pos_ref, posn_ref, ys_hbm, x1_ref, g2_ref, w_ref, fg_ref, o_ref, rbuf, sem):
    i = pl.program_id(0)
    n = pl.num_programs(0)
    slot = lax.rem(i, 2)
    nslot = 1 - slot
    tm = COMB_TM

    @pl.when(i == 0)
    def _():
        _row_gather_start(ys_hbm, pos_ref, rbuf.at[0], sem.at[0], 2 * tm)

    @pl.when(i + 1 < n)
    def _():
        _row_gather_start(ys_hbm, posn_ref, rbuf.at[nslot], sem.at[nslot], 2 * tm)

    _row_gather_wait(ys_hbm, rbuf.at[slot], sem.at[slot], 2 * tm)
    w = w_ref[...]
    r1 = jnp.concatenate(_load_packed_rows(rbuf.at[slot], 0, tm), axis=1)
    r2 = jnp.concatenate(_load_packed_rows(rbuf.at[slot], tm, tm), axis=1)
    moe = w[:, 0:1] * r1 + w[:, 1:2] * r2
    y = x1_ref[...] + g2_ref[0] * moe
    ms = jnp.mean(y * y, axis=-1, keepdims=True)
    o_ref[...] = y * lax.rsqrt(ms + NORM_EPS) * fg_ref[...]


def _combine(pos_tiles, ys, x1, g2, w_tok, final_g, seq_len):
    T, D = x1.shape
    tm = COMB_TM
    nt = T // tm
    per_b = seq_len // tm
    return pl.pallas_call(
        _combine_kernel,
        grid=(nt,),
        in_specs=[pl.BlockSpec((1, 1, 2 * tm), lambda i: (i, 0, 0), memory_space=pltpu.SMEM),
                  pl.BlockSpec((1, 1, 2 * tm), lambda i: (jnp.minimum(i + 1, nt - 1), 0, 0),
                               memory_space=pltpu.SMEM),
                  pl.BlockSpec(memory_space=pl.ANY),
                  pl.BlockSpec((tm, D), lambda i: (i, 0)),
                  pl.BlockSpec((1, 1, D), lambda i: (i // per_b, 0, 0)),
                  pl.BlockSpec((tm, 2), lambda i: (i, 0)),
                  pl.BlockSpec((1, D), lambda i: (0, 0))],
        out_specs=pl.BlockSpec((tm, D), lambda i: (i, 0)),
        out_shape=jax.ShapeDtypeStruct((T, D), F32),
        scratch_shapes=[pltpu.VMEM((2, 2 * tm * ROW_SLAB, 128), U32),
                        pltpu.SemaphoreType.DMA((2,))],
        compiler_params=pltpu.CompilerParams(
            dimension_semantics=("arbitrary",), vmem_limit_bytes=VMEM_LIMIT),
        name="combine",
    )(pos_tiles, pos_tiles, ys, x1, g2, w_tok, final_g)


def _rel_bucket_table():
    n = jnp.arange(BIAS_LUT, dtype=I32)
    max_exact = N_BUCKETS // 2
    nf = jnp.maximum(n, 1).astype(F32)
    large = max_exact + (jnp.log(nf / max_exact) / math.log(MAX_DISTANCE / max_exact)
                         * (N_BUCKETS - max_exact)).astype(I32)
    large = jnp.minimum(large, N_BUCKETS - 1)
    return jnp.where(n < max_exact, n, large)


def _route_plan(eid, n_tiles, tm):
    two, T = eid.shape
    e_flat = eid.reshape(-1)
    onehot = (e_flat[:, None] == jnp.arange(N_EXPERTS, dtype=I32)[None, :]).astype(I32)
    csum = jnp.cumsum(onehot, axis=0)
    rank = jnp.sum((csum - onehot) * onehot, axis=1)
    counts = csum[-1]
    ptiles = (counts + tm - 1) // tm
    tend = jnp.cumsum(ptiles)
    tstart = tend - ptiles
    slot = jnp.sum(onehot * tstart[None, :], axis=1) * tm + rank
    total = tend[-1]
    tile_ids = jnp.arange(n_tiles, dtype=I32)
    tile_valid = (tile_ids < total).astype(I32)
    tile_expert = jnp.sum((tile_ids[:, None] >= tend[None, :]).astype(I32), axis=1)
    last_expert = jnp.sum((total - 1 >= tend).astype(I32))
    tile_expert = jnp.minimum(tile_expert, last_expert).astype(I32)
    partial_last = jnp.any((tile_ids[:, None] == tend[None, :] - 1)
                           & (ptiles[None, :] > 0) & (counts[None, :] % tm != 0), axis=1)
    tile_fill = (partial_last | (tile_ids >= total)).astype(I32)
    return slot.astype(I32), tile_expert, tile_valid, tile_fill


def kernel(x, c, positions, rel_bias, ada_w, ada_b, norm1_g, w_in, lambda_q1, lambda_k1, lambda_q2,
           lambda_k2, subln_g, conv_w, w_out, norm2_g, router_group_w, router_group_b,
           router_expert_w, router_expert_b, expert_w_gate, expert_w_up, expert_w_down, final_g):
    B, S, D = x.shape
    T = B * S
    l = 0

    c_pad = jnp.zeros((8, D), F32).at[:B].set(c)
    ada = _ada(c_pad, ada_w[l], ada_b[l].reshape(1, -1))[:B]
    sh1, sc1, g1, sh2, sc2, g2 = [a.reshape(B, 1, D) for a in jnp.split(ada, 6, axis=-1)]

    w_in_bf = w_in[l].astype(BF16)
    A = ATTN_WIDTH
    qt, k, vt, conv = _inproj(x, sc1, sh1, norm1_g[l].reshape(1, D), w_in_bf[:, 0:A].T,
                              w_in_bf[:, A:2 * A], w_in_bf[:, 2 * A:3 * A].T, w_in_bf[:, 3 * A:],
                              conv_w[l])
    lut = (rel_bias.astype(F32)[_rel_bucket_table(), :].T * LOG2E).reshape(N_DIFF_HEADS, 2, BIAS_LUT)
    lam_params = jnp.stack([lambda_q1[l], lambda_k1[l], lambda_q2[l], lambda_k2[l]]).astype(F32)
    attn = _attention(qt, k, vt, positions, lut, lam_params, subln_g[l].reshape(V_HEAD_DIM, 1))

    wr_t = jnp.zeros((ROUTER_ROWS, D), F32)
    wr_t = wr_t.at[0:N_GROUPS].set(router_group_w[l].T)
    wr_t = wr_t.at[N_GROUPS:N_GROUPS + N_EXPERTS].set(router_expert_w[l].T).astype(BF16)
    rb = jnp.zeros((ROUTER_ROWS, 1), F32)
    rb = rb.at[0:N_GROUPS, 0].set(router_group_b[l])
    rb = rb.at[N_GROUPS:N_GROUPS + N_EXPERTS, 0].set(router_expert_b[l])
    x1, hp, ri, rw = _outproj(attn, conv, x, g1, sc2, sh2, norm2_g[l].reshape(1, D),
                              w_out[l].astype(BF16), wr_t, rb)

    eid = ri[:, 0:2, :].transpose(1, 0, 2).reshape(2, T)
    w_tok = rw[:, 0:2, :].transpose(0, 2, 1).reshape(T, 2)
    n_tiles = 2 * T // MOE_TM + N_EXPERTS
    slot, tile_expert, tile_valid, tile_fill = _route_plan(eid, n_tiles, MOE_TM)
    nct = T // COMB_TM
    pos = (slot * ROW_SLAB).reshape(2, nct, 1, COMB_TM)
    pos_tiles = jnp.concatenate([pos[0], pos[1]], axis=2)

    xs = _dispatch(tile_fill, pos_tiles, hp.reshape(T * ROW_SLAB, 128), n_tiles)
    ys = _moe(tile_expert, tile_valid, xs, expert_w_gate[l].astype(BF16),
              expert_w_up[l].astype(BF16), expert_w_down[l].astype(BF16))

    out = _combine(pos_tiles, ys, x1.reshape(T, D), g2, w_tok, final_g.reshape(1, D), S)
    return out.reshape(B, S, D)
```

```python
import functools
import math

import jax
import jax.numpy as jnp
from jax import lax
from jax.experimental import pallas as pl
from jax.experimental.pallas import tpu as pltpu

F32 = jnp.float32
BF16 = jnp.bfloat16
I32 = jnp.int32
U32 = jnp.uint32

D_MODEL = 1024
ATTN_WIDTH = 512
CONV_WIDTH = 512
N_DIFF_HEADS = 4
DIFF_HEAD_DIM = 64
V_HEAD_DIM = 128
IN_PROJ_WIDTH = 3 * ATTN_WIDTH + 3 * CONV_WIDTH
CONV_K = 3
N_BUCKETS = 32
MAX_DISTANCE = 128
N_GROUPS = 4
EXPERTS_PER_GROUP = 4
N_EXPERTS = 16
D_EXPERT = 512
NORM_EPS = 1e-6
SUBLN_EPS = 1e-5
NEG_INF = -1e30
LAMBDA_INIT = 0.8 - 0.6 * math.exp(-0.3 * 0)
QK_SCALE = DIFF_HEAD_DIM ** -0.5
LOG2E = math.log2(math.e)

BIAS_LUT = 128

ROW_TILE = 512
ATT_TQ = 512
ATT_CHAIN = 256
ATT_GROUP = 8
ATT_LOOKAHEAD = 4
ATT_TK = 256
MOE_TM = 256
COMB_TM = 256
ROUTER_ROWS = 32
ROW_SLAB = D_MODEL // 256
VMEM_LIMIT = 56 * 1024 * 1024


def _silu(x):
    return x * (1.0 / (1.0 + jnp.exp(-x)))


def _ada_kernel(c_ref, w_ref, b_ref, o_ref):
    s = _silu(c_ref[...])
    o_ref[...] = jnp.dot(s, w_ref[...], preferred_element_type=F32,
                         precision=lax.Precision.HIGHEST) + b_ref[...]


def _ada(c_pad, w, b):
    n = w.shape[1]
    bn = 1024
    return pl.pallas_call(
        _ada_kernel,
        grid=(n // bn,),
        in_specs=[pl.BlockSpec((8, D_MODEL), lambda j: (0, 0)),
                  pl.BlockSpec((D_MODEL, bn), lambda j: (0, j)),
                  pl.BlockSpec((1, bn), lambda j: (0, j))],
        out_specs=pl.BlockSpec((8, bn), lambda j: (0, j)),
        out_shape=jax.ShapeDtypeStruct((8, n), F32),
        name="ada",
    )(c_pad, w, b)


def _inproj_kernel(x_ref, sc_ref, sh_ref, g_ref, wqt_ref, wk_ref, wvt_ref, wc_ref, cw_ref,
                   qt_ref, k_ref, vt_ref, conv_ref, carry_ref):
    j = pl.program_id(1)
    tm = x_ref.shape[1]
    x = x_ref[0]
    ms = jnp.mean(x * x, axis=-1, keepdims=True)
    h = x * lax.rsqrt(ms + NORM_EPS) * g_ref[...]
    h = h * (1.0 + sc_ref[0]) + sh_ref[0]
    hb = h.astype(BF16)
    nt = (((1,), (1,)), ((), ()))

    def proj(c0):
        return jnp.dot(hb, wc_ref[:, c0:c0 + 512], preferred_element_type=F32)

    qt = lax.dot_general(wqt_ref[...], hb, nt, preferred_element_type=F32)
    qt_ref[0] = (qt * (QK_SCALE * LOG2E)).astype(BF16)
    k_ref[0] = jnp.dot(hb, wk_ref[...], preferred_element_type=F32).astype(BF16)
    vt_ref[0] = lax.dot_general(wvt_ref[...], hb, nt, preferred_element_type=F32).astype(BF16)
    gate_b = proj(0)
    u = proj(512) * proj(1024)

    @pl.when(j == 0)
    def _():
        carry_ref[...] = jnp.zeros_like(carry_ref)

    prev = carry_ref[...]
    row = lax.broadcasted_iota(I32, u.shape, 0)
    u1 = pltpu.roll(u, 1, axis=0)
    u2 = pltpu.roll(u, 2, axis=0)
    u1 = jnp.where(row == 0, prev[7:8, :], u1)
    u2 = jnp.where(row == 0, prev[6:7, :], jnp.where(row == 1, prev[7:8, :], u2))
    conv = cw_ref[0:1, :] * u2 + cw_ref[1:2, :] * u1 + cw_ref[2:3, :] * u
    conv_ref[0] = (gate_b * conv).astype(BF16)
    carry_ref[...] = u[tm - 8:tm, :]


def _inproj(x, sc1, sh1, g1n, wq_t, wk, wv_t, wc, conv_w):
    B, S, D = x.shape
    tm = ROW_TILE
    row_out = jax.ShapeDtypeStruct((B, S, 512), BF16)
    col_out = jax.ShapeDtypeStruct((B, 512, S), BF16)
    row_spec = pl.BlockSpec((1, tm, 512), lambda b, j: (b, j, 0))
    col_spec = pl.BlockSpec((1, 512, tm), lambda b, j: (b, 0, j))
    mod_spec = pl.BlockSpec((1, 1, D), lambda b, j: (b, 0, 0))
    const2 = lambda b, j: (0, 0)
    return pl.pallas_call(
        _inproj_kernel,
        grid=(B, S // tm),
        in_specs=[pl.BlockSpec((1, tm, D), lambda b, j: (b, j, 0)),
                  mod_spec, mod_spec,
                  pl.BlockSpec((1, D), const2),
                  pl.BlockSpec((ATTN_WIDTH, D), const2),
                  pl.BlockSpec((D, ATTN_WIDTH), const2),
                  pl.BlockSpec((ATTN_WIDTH, D), const2),
                  pl.BlockSpec((D, 3 * CONV_WIDTH), const2),
                  pl.BlockSpec((CONV_K, CONV_WIDTH), const2)],
        out_specs=[col_spec, row_spec, col_spec, row_spec],
        out_shape=[col_out, row_out, col_out, row_out],
        scratch_shapes=[pltpu.VMEM((8, CONV_WIDTH), F32)],
        compiler_params=pltpu.CompilerParams(
            dimension_semantics=("arbitrary", "arbitrary"), vmem_limit_bytes=VMEM_LIMIT),
        name="inproj",
    )(x, sc1, sh1, g1n, wq_t, wk, wv_t, wc, conv_w)


def _attn_kernel(qmin_ref, kmax_ref, consec_ref, qt_ref, k_ref, vt_ref, pr_ref, pc_ref, lut_ref,
                 lam_ref, sg_ref, o_ref, acc_ref, tz_ref):
    b = pl.program_id(0)
    qi = pl.program_id(2)
    nq = pl.num_programs(2)
    tq, tk = ATT_TQ, ATT_TK
    hw = ATT_CHAIN
    n_half = tq // hw
    nk = nq * (tq // tk)
    qt = qt_ref[0]
    feat = lax.broadcasted_iota(I32, qt.shape, 0)
    zero = jnp.zeros_like(qt)
    qts = (jnp.where(feat < DIFF_HEAD_DIM, qt, zero), jnp.where(feat >= DIFF_HEAD_DIM, qt, zero))
    luts = (lut_ref[0, 0:1, :], lut_ref[0, 1:2, :])
    fars = tuple(t[:, BIAS_LUT - 1:BIAS_LUT] for t in luts)
    pq = pr_ref[0]
    qmin = qmin_ref[b * nq + qi]
    czero = jnp.zeros((1, 1), F32)
    chains = [(mi, hi) for mi in range(2) for hi in range(n_half)]

    def gather_bias(mi, dist):
        table = jnp.broadcast_to(luts[mi], (tk, BIAS_LUT))
        return jnp.concatenate([jnp.take_along_axis(table, dist[:, o:o + 128], axis=1)
                                for o in range(0, hw, 128)], axis=1)

    @pl.when(qi == 0)
    def _():
        delta = (lax.broadcasted_iota(I32, (tk, hw), 1) - lax.broadcasted_iota(I32, (tk, hw), 0))
        for mi in range(2):
            diag_bias = gather_bias(mi, jnp.clip(delta, 0, BIAS_LUT - 1))
            tz_ref[mi, 0] = jnp.where(delta >= 0, diag_bias, NEG_INF)
            tz_ref[mi, 1] = gather_bias(mi, jnp.clip(delta + tk, 0, BIAS_LUT - 1))

    def run_blocks(blocks, state):
        loaded = []
        for (j, kinds) in blocks:
            ks = pl.multiple_of(j * tk, tk)
            kb = k_ref[0, pl.ds(ks, tk), :]
            vtb = vt_ref[0, :, pl.ds(ks, tk)]
            dist = None
            if any(kd is not None and kd.startswith("near") for kd in kinds):
                pk = pc_ref[0, pl.ds(ks, tk), :]
                dist = jnp.clip(pq - pk, 0, BIAS_LUT - 1)
            loaded.append((kb, vtb, dist))
        items = [(bi, n) for bi, blk in enumerate(blocks) for n, (mi, hi) in enumerate(chains)
                 if blk[1][hi] is not None]
        scores = {}

        def issue_qk(t):
            bi, n = items[t]
            mi, hi = chains[n]
            scores[t] = jnp.dot(loaded[bi][0], qts[mi][:, hi * hw:(hi + 1) * hw],
                                preferred_element_type=F32)

        state = list(state)
        for t in range(min(ATT_LOOKAHEAD, len(items))):
            issue_qk(t)
        for t, (bi, n) in enumerate(items):
            if t + ATT_LOOKAHEAD < len(items):
                issue_qk(t + ATT_LOOKAHEAD)
            _, vtb, dist = loaded[bi]
            mi, hi = chains[n]
            kind = blocks[bi][1][hi]
            cols = slice(hi * hw, (hi + 1) * hw)
            m, l = state[n]
            s = scores.pop(t)
            c = czero
            if kind == "far":
                c = fars[mi]
            elif kind == "tz_diag":
                s = tz_ref[mi, 0] + s
            elif kind == "tz_sub":
                s = tz_ref[mi, 1] + s
            else:
                s = gather_bias(mi, dist[:, cols]) + s
                if kind == "near_masked":
                    keep = (lax.broadcasted_iota(I32, (tk, hw), 0)
                            <= lax.broadcasted_iota(I32, (tk, hw), 1))
                    s = jnp.where(keep, s, NEG_INF)
            mn = jnp.maximum(m, jnp.max(s, axis=0, keepdims=True) + c)
            alpha = jnp.exp2(m - mn)
            p = jnp.exp2(s - (mn - c))
            l = alpha * l + jnp.sum(p, axis=0, keepdims=True)
            acc_ref[mi, :, cols] = alpha * acc_ref[mi, :, cols] + jnp.dot(
                vtb, p.astype(BF16), preferred_element_type=F32)
            state[n] = (mn, l)
        return tuple(state)

    def block_is_far(j):
        return qmin - kmax_ref[b * nk + j] >= BIAS_LUT - 1

    def one_block(j, state):
        return lax.cond(block_is_far(j), lambda st: run_blocks([(j, ("far",) * n_half)], st),
                        lambda st: run_blocks([(j, ("near",) * n_half)], st), state)

    def group_body(width):
        def body(g, carry):
            j0, state = carry
            all_far = block_is_far(j0)
            for u in range(1, width):
                all_far = jnp.logical_and(all_far, block_is_far(j0 + u))
            far_blocks = [(j0 + u, ("far",) * n_half) for u in range(width)]
            state = lax.cond(
                all_far, lambda st: run_blocks(far_blocks, st),
                lambda st: lax.fori_loop(0, width, lambda u, s2: one_block(j0 + u, s2), st), state)
            return j0 + width, state
        return body

    acc_ref[...] = jnp.zeros_like(acc_ref)
    m0 = jnp.full((1, hw), NEG_INF, F32)
    l0 = jnp.zeros((1, hw), F32)
    state = tuple((m0, l0) for _ in chains)
    assert tq == 2 * tk and hw == tk
    n_full = 2 * qi
    n_main = jnp.maximum(n_full - 2, 0)
    j0 = jnp.int32(0)
    width = ATT_GROUP
    j0, state = lax.fori_loop(0, n_main // width, group_body(width), (j0, state))
    rem = n_main % width
    while width > 2:
        width //= 2
        j0, state = lax.fori_loop(0, (rem // width) % 2, group_body(width), (j0, state))

    consec = consec_ref[b * nq + qi] == 1
    diag_fast = [(n_full, ("tz_diag", "tz_sub")), (n_full + 1, (None, "tz_diag"))]
    diag_any = [(n_full, ("near_masked", "near")), (n_full + 1, (None, "near_masked"))]
    below_fast = [(n_full - 2, ("far", "far")), (n_full - 1, ("tz_sub", "far"))]
    below_any = [(n_full - 2, ("near", "near")), (n_full - 1, ("near", "near"))]

    def tail(below, diag):
        return lambda st: lax.cond(qi > 0, lambda s2: run_blocks(below + diag, s2),
                                   lambda s2: run_blocks(diag, s2), st)

    state = lax.cond(consec, tail(below_fast, diag_fast), tail(below_any, diag_any), state)

    l1 = jnp.concatenate([state[n][1] for n, (mi, hi) in enumerate(chains) if mi == 0], axis=1)
    l2 = jnp.concatenate([state[n][1] for n, (mi, hi) in enumerate(chains) if mi == 1], axis=1)
    lam = (jnp.exp(jnp.sum(lam_ref[0:1, :] * lam_ref[1:2, :], axis=-1, keepdims=True))
           - jnp.exp(jnp.sum(lam_ref[2:3, :] * lam_ref[3:4, :], axis=-1, keepdims=True))
           + LAMBDA_INIT)
    ot = acc_ref[0] * (1.0 / l1) - (lam * (1.0 / l2)) * acc_ref[1]
    ot = ot * lax.rsqrt(jnp.mean(ot * ot, axis=0, keepdims=True) + SUBLN_EPS)
    ot = ot * (sg_ref[...] * (1.0 - LAMBDA_INIT))
    o_ref[0] = ot.T.astype(BF16)


def _attention(qt, k, vt, positions, lut, lam_params, subln_g_col):
    B, S, _ = k.shape
    tq = ATT_TQ
    nq = S // tq
    pos_col = positions.reshape(B, S, 1)
    pos_row = positions.reshape(B, 1, S)
    qmin = jnp.min(positions.reshape(B * nq, tq), axis=1)
    kmax = jnp.max(positions.reshape(B * (S // ATT_TK), ATT_TK), axis=1)
    step_ok = jnp.concatenate([positions[:, 1:] - positions[:, :-1] == 1,
                               jnp.ones((B, 1), jnp.bool_)], axis=1).reshape(B, nq, tq)
    inner_ok = jnp.all(step_ok[:, :, :tq - 1], axis=2)
    link_ok = jnp.concatenate([jnp.ones((B, 1), jnp.bool_), step_ok[:, :-1, tq - 1]], axis=1)
    prev_ok = jnp.concatenate([jnp.ones((B, 1), jnp.bool_), inner_ok[:, :-1]], axis=1)
    consec = (inner_ok & link_ok & prev_ok).astype(I32).reshape(B * nq)
    grid_spec = pltpu.PrefetchScalarGridSpec(
        num_scalar_prefetch=3,
        grid=(B, N_DIFF_HEADS, nq),
        in_specs=[pl.BlockSpec((1, 128, tq), lambda b, h, i, *_: (b, h, i)),
                  pl.BlockSpec((1, S, 128), lambda b, h, i, *_: (b, 0, h)),
                  pl.BlockSpec((1, 128, S), lambda b, h, i, *_: (b, h, 0)),
                  pl.BlockSpec((1, 1, tq), lambda b, h, i, *_: (b, 0, i)),
                  pl.BlockSpec((1, S, 1), lambda b, h, i, *_: (b, 0, 0)),
                  pl.BlockSpec((1, 2, BIAS_LUT), lambda b, h, i, *_: (h, 0, 0)),
                  pl.BlockSpec((4, DIFF_HEAD_DIM), lambda b, h, i, *_: (0, 0)),
                  pl.BlockSpec((V_HEAD_DIM, 1), lambda b, h, i, *_: (0, 0))],
        out_specs=pl.BlockSpec((1, tq, 128), lambda b, h, i, *_: (b, i, h)),
        scratch_shapes=[pltpu.VMEM((2, V_HEAD_DIM, tq), F32),
                        pltpu.VMEM((2, 2, ATT_TK, ATT_CHAIN), F32)],
    )
    return pl.pallas_call(
        _attn_kernel,
        grid_spec=grid_spec,
        out_shape=jax.ShapeDtypeStruct((B, S, ATTN_WIDTH), BF16),
        compiler_params=pltpu.CompilerParams(
            dimension_semantics=("arbitrary", "arbitrary", "arbitrary"),
            vmem_limit_bytes=VMEM_LIMIT),
        name="diffattn",
    )(qmin, kmax, consec, qt, k, vt, pos_row, pos_col, lut, lam_params, subln_g_col)


def _outproj_kernel(at_ref, cv_ref, x_ref, g1_ref, sc_ref, sh_ref, gn_ref, wo_ref, wr_ref, rb_ref,
                    x1_ref, hp_ref, ri_ref, rw_ref):
    tm = x_ref.shape[1]
    mix = (jnp.dot(at_ref[0], wo_ref[0:ATTN_WIDTH, :], preferred_element_type=F32)
           + jnp.dot(cv_ref[0], wo_ref[ATTN_WIDTH:, :], preferred_element_type=F32))
    x1 = x_ref[0] + g1_ref[0] * mix
    x1_ref[0] = x1
    ms = jnp.mean(x1 * x1, axis=-1, keepdims=True)
    h = x1 * lax.rsqrt(ms + NORM_EPS) * gn_ref[...]
    h = h * (1.0 + sc_ref[0]) + sh_ref[0]
    hb = h.astype(BF16)

    _store_packed_rows(hp_ref.at[0], h)

    lg_all = lax.dot_general(wr_ref[...], hb, (((1,), (1,)), ((), ())),
                             preferred_element_type=F32) + rb_ref[...]
    lg = lg_all[0:N_GROUPS, :]
    le = lg_all[N_GROUPS:N_GROUPS + N_EXPERTS, :]
    row4 = lax.broadcasted_iota(I32, (N_GROUPS, tm), 0)
    gmax = jnp.max(lg, axis=0, keepdims=True)
    pg_sel = 1.0 / jnp.sum(jnp.exp(lg - gmax), axis=0, keepdims=True)
    gsel = jnp.min(jnp.where(lg == gmax, row4, N_GROUPS), axis=0, keepdims=True)
    sel = jnp.zeros((EXPERTS_PER_GROUP, tm), F32)
    for g in range(N_GROUPS):
        sel = jnp.where(gsel == g, le[g * EXPERTS_PER_GROUP:(g + 1) * EXPERTS_PER_GROUP, :], sel)
    v1 = jnp.max(sel, axis=0, keepdims=True)
    i1 = jnp.min(jnp.where(sel == v1, row4, EXPERTS_PER_GROUP), axis=0, keepdims=True)
    rest = jnp.where(row4 == i1, -jnp.inf, sel)
    v2 = jnp.max(rest, axis=0, keepdims=True)
    i2 = jnp.min(jnp.where(rest == v2, row4, EXPERTS_PER_GROUP), axis=0, keepdims=True)
    e2 = jnp.exp(v2 - v1)
    w1 = pg_sel / (1.0 + e2)
    w2 = pg_sel * e2 / (1.0 + e2)
    row8 = lax.broadcasted_iota(I32, (8, tm), 0)
    eid1 = gsel * EXPERTS_PER_GROUP + i1
    eid2 = gsel * EXPERTS_PER_GROUP + i2
    ri_ref[0] = jnp.where(row8 == 0, eid1, jnp.where(row8 == 1, eid2, 0))
    rw_ref[0] = jnp.where(row8 == 0, w1, jnp.where(row8 == 1, w2, 0.0))


def _outproj(attn, conv, x, g1, sc2, sh2, g2n, w_out_bf, wr_t, rb):
    B, S, D = x.shape
    tm = ROW_TILE
    half_spec = pl.BlockSpec((1, tm, 512), lambda b, j: (b, j, 0))
    full_spec = pl.BlockSpec((1, tm, D), lambda b, j: (b, j, 0))
    mod_spec = pl.BlockSpec((1, 1, D), lambda b, j: (b, 0, 0))
    rt_spec = pl.BlockSpec((1, 8, tm), lambda b, j: (b, 0, j))
    return pl.pallas_call(
        _outproj_kernel,
        grid=(B, S // tm),
        in_specs=[half_spec, half_spec, full_spec, mod_spec, mod_spec, mod_spec,
                  pl.BlockSpec((1, D), lambda b, j: (0, 0)),
                  pl.BlockSpec((D, D), lambda b, j: (0, 0)),
                  pl.BlockSpec((ROUTER_ROWS, D), lambda b, j: (0, 0)),
                  pl.BlockSpec((ROUTER_ROWS, 1), lambda b, j: (0, 0))],
        out_specs=[full_spec, pl.BlockSpec((1, tm * ROW_SLAB, 128), lambda b, j: (b, j, 0)),
                   rt_spec, rt_spec],
        out_shape=[jax.ShapeDtypeStruct((B, S, D), F32),
                   jax.ShapeDtypeStruct((B, S * ROW_SLAB, 128), U32),
                   jax.ShapeDtypeStruct((B, 8, S), I32),
                   jax.ShapeDtypeStruct((B, 8, S), F32)],
        compiler_params=pltpu.CompilerParams(
            dimension_semantics=("arbitrary", "arbitrary"), vmem_limit_bytes=VMEM_LIMIT),
        name="outproj",
    )(attn, conv, x, g1, sc2, sh2, g2n, w_out_bf, wr_t, rb)


def _store_packed_rows(dst, x):
    half = D_MODEL // 2
    xb = x.astype(BF16).astype(F32)
    packed = (pltpu.bitcast(xb[:, :half], U32) >> 16) | (
        pltpu.bitcast(xb[:, half:], U32) & jnp.uint32(0xFFFF0000))
    rows = x.shape[0]
    for c in range(ROW_SLAB):
        dst[pl.ds(c, rows, stride=ROW_SLAB), :] = packed[:, c * 128:(c + 1) * 128]


def _load_packed_rows(src, row0, rows):
    packed = jnp.concatenate(
        [src[pl.ds(row0 * ROW_SLAB + c, rows, stride=ROW_SLAB), :] for c in range(ROW_SLAB)], axis=1)
    lo = pltpu.bitcast(packed << 16, F32)
    hi = pltpu.bitcast(packed & jnp.uint32(0xFFFF0000), F32)
    return lo, hi


def _row_gather_start(src_hbm, idx_ref, dst, sem, n_rows):
    for r in range(n_rows):
        off = pl.multiple_of(idx_ref[0, 0, r], ROW_SLAB)
        pltpu.make_async_copy(src_hbm.at[pl.ds(off, ROW_SLAB)],
                              dst.at[pl.ds(r * ROW_SLAB, ROW_SLAB)], sem).start(priority=r % 2)


def _row_gather_wait(src_hbm, dst, sem, n_rows):
    pltpu.make_async_copy(src_hbm.at[pl.ds(0, n_rows * ROW_SLAB)], dst, sem).wait()


def _dispatch_kernel(fill_ref, pos_ref, hp_ref, xs_hbm, zbuf, sem):
    i = pl.program_id(0)
    tm = COMB_TM
    tile_rows = MOE_TM * ROW_SLAB

    @pl.when(i == 0)
    def _():
        zbuf[...] = jnp.zeros_like(zbuf)

        def fill_copy(t):
            return pltpu.make_async_copy(
                zbuf, xs_hbm.at[pl.ds(pl.multiple_of(t * tile_rows, tile_rows), tile_rows)],
                sem.at[0])

        def start(t, c):
            @pl.when(fill_ref[t] == 1)
            def _():
                fill_copy(t).start()
            return c

        def wait(t, c):
            @pl.when(fill_ref[t] == 1)
            def _():
                fill_copy(t).wait()
            return c

        lax.fori_loop(0, fill_ref.shape[0], start, 0)
        lax.fori_loop(0, fill_ref.shape[0], wait, 0)

    par = lax.rem(i, 2)
    base = pl.multiple_of(i * (tm * ROW_SLAB), tm * ROW_SLAB)
    for r in range(2 * tm):
        off = pl.multiple_of(pos_ref[0, 0, r], ROW_SLAB)
        pltpu.make_async_copy(hp_ref.at[pl.ds(base + (r % tm) * ROW_SLAB, ROW_SLAB)],
                              xs_hbm.at[pl.ds(off, ROW_SLAB)],
                              sem.at[1 + par]).start(priority=r % 2)

    def drain(parity):
        for _ in range(2):
            pltpu.make_async_copy(hp_ref.at[pl.ds(0, tm * ROW_SLAB)],
                                  xs_hbm.at[pl.ds(0, tm * ROW_SLAB)], sem.at[1 + parity]).wait()

    @pl.when(i > 0)
    def _():
        drain(1 - par)

    @pl.when(i == pl.num_programs(0) - 1)
    def _():
        drain(par)


def _dispatch(tile_fill, pos_tiles, hp, n_tiles):
    T = hp.shape[0] // ROW_SLAB
    tm = COMB_TM
    grid_spec = pltpu.PrefetchScalarGridSpec(
        num_scalar_prefetch=1,
        grid=(T // tm,),
        in_specs=[pl.BlockSpec((1, 1, 2 * tm), lambda i, f: (i, 0, 0), memory_space=pltpu.SMEM),
                  pl.BlockSpec(memory_space=pltpu.VMEM)],
        out_specs=pl.BlockSpec(memory_space=pl.ANY),
        scratch_shapes=[pltpu.VMEM((MOE_TM * ROW_SLAB, 128), U32),
                        pltpu.SemaphoreType.DMA((3,))],
    )
    return pl.pallas_call(
        _dispatch_kernel,
        grid_spec=grid_spec,
        out_shape=jax.ShapeDtypeStruct((n_tiles * MOE_TM * ROW_SLAB, 128), U32),
        compiler_params=pltpu.CompilerParams(
            dimension_semantics=("arbitrary",), vmem_limit_bytes=VMEM_LIMIT),
        name="dispatch",
    )(tile_fill, pos_tiles, hp)


def _moe_kernel(te_ref, tv_ref, xs_ref, wg_ref, wu_ref, wd_ref, y_ref):
    i = pl.program_id(0)
    tm = MOE_TM

    @pl.when(tv_ref[i] == 1)
    def _():
        half = D_MODEL // 2
        lo, hi = _load_packed_rows(xs_ref, 0, tm)
        lo = lo.astype(BF16)
        hi = hi.astype(BF16)
        g = (jnp.dot(lo, wg_ref[0, 0:half, :], preferred_element_type=F32)
             + jnp.dot(hi, wg_ref[0, half:, :], preferred_element_type=F32))
        u = (jnp.dot(lo, wu_ref[0, 0:half, :], preferred_element_type=F32)
             + jnp.dot(hi, wu_ref[0, half:, :], preferred_element_type=F32))
        hid = (_silu(g) * u).astype(BF16)
        _store_packed_rows(y_ref, jnp.dot(hid, wd_ref[0], preferred_element_type=F32))

    @pl.when(tv_ref[i] == 0)
    def _():
        y_ref[...] = jnp.zeros_like(y_ref)


def _moe(tile_expert, tile_valid, xs, wg_bf, wu_bf, wd_bf):
    nt = tile_expert.shape[0]
    tm = MOE_TM
    D = D_MODEL
    grid_spec = pltpu.PrefetchScalarGridSpec(
        num_scalar_prefetch=2,
        grid=(nt,),
        in_specs=[pl.BlockSpec((tm * ROW_SLAB, 128), lambda i, te, tv: (i, 0)),
                  pl.BlockSpec((1, D, D_EXPERT), lambda i, te, tv: (te[i], 0, 0)),
                  pl.BlockSpec((1, D, D_EXPERT), lambda i, te, tv: (te[i], 0, 0)),
                  pl.BlockSpec((1, D_EXPERT, D), lambda i, te, tv: (te[i], 0, 0))],
        out_specs=pl.BlockSpec((tm * ROW_SLAB, 128), lambda i, te, tv: (i, 0)),
    )
    return pl.pallas_call(
        _moe_kernel,
        grid_spec=grid_spec,
        out_shape=jax.ShapeDtypeStruct((nt * tm * ROW_SLAB, 128), U32),
        compiler_params=pltpu.CompilerParams(
            dimension_semantics=("arbitrary",), vmem_limit_bytes=VMEM_LIMIT),
        name="moe",
    )(tile_expert, tile_valid, xs, wg_bf, wu_bf, wd_bf)


def _combine_kernel(pos_ref, posn_ref, ys_hbm, x1_ref, g2_ref, w_ref, fg_ref, o_ref, rbuf, sem):
    i = pl.program_id(0)
    n = pl.num_programs(0)
    slot = lax.rem(i, 2)
    nslot = 1 - slot
    tm = COMB_TM

    @pl.when(i == 0)
    def _():
        _row_gather_start(ys_hbm, pos_ref, rbuf.at[0], sem.at[0], 2 * tm)

    @pl.when(i + 1 < n)
    def _():
        _row_gather_start(ys_hbm, posn_ref, rbuf.at[nslot], sem.at[nslot], 2 * tm)

    _row_gather_wait(ys_hbm, rbuf.at[slot], sem.at[slot], 2 * tm)
    w = w_ref[...]
    r1 = jnp.concatenate(_load_packed_rows(rbuf.at[slot], 0, tm), axis=1)
    r2 = jnp.concatenate(_load_packed_rows(rbuf.at[slot], tm, tm), axis=1)
    moe = w[:, 0:1] * r1 + w[:, 1:2] * r2
    y = x1_ref[...] + g2_ref[0] * moe
    ms = jnp.mean(y * y, axis=-1, keepdims=True)
    o_ref[...] = y * lax.rsqrt(ms + NORM_EPS) * fg_ref[...]


def _combine(pos_tiles, ys, x1, g2, w_tok, final_g, seq_len):
    T, D = x1.shape
    tm = COMB_TM
    nt = T // tm
    per_b = seq_len // tm
    return pl.pallas_call(
        _combine_kernel,
        grid=(nt,),
        in_specs=[pl.BlockSpec((1, 1, 2 * tm), lambda i: (i, 0, 0), memory_space=pltpu.SMEM),
                  pl.BlockSpec((1, 1, 2 * tm), lambda i: (jnp.minimum(i + 1, nt - 1), 0, 0),
                               memory_space=pltpu.SMEM),
                  pl.BlockSpec(memory_space=pl.ANY),
                  pl.BlockSpec((tm, D), lambda i: (i, 0)),
                  pl.BlockSpec((1, 1, D), lambda i: (i // per_b, 0, 0)),
                  pl.BlockSpec((tm, 2), lambda i: (i, 0)),
                  pl.BlockSpec((1, D), lambda i: (0, 0))],
        out_specs=pl.BlockSpec((tm, D), lambda i: (i, 0)),
        out_shape=jax.ShapeDtypeStruct((T, D), F32),
        scratch_shapes=[pltpu.VMEM((2, 2 * tm * ROW_SLAB, 128), U32),
                        pltpu.SemaphoreType.DMA((2,))],
        compiler_params=pltpu.CompilerParams(
            dimension_semantics=("arbitrary",), vmem_limit_bytes=VMEM_LIMIT),
        name="combine",
    )(pos_tiles, pos_tiles, ys, x1, g2, w_tok, final_g)


def _rel_bucket_table():
    n = jnp.arange(BIAS_LUT, dtype=I32)
    max_exact = N_BUCKETS // 2
    nf = jnp.maximum(n, 1).astype(F32)
    large = max_exact + (jnp.log(nf / max_exact) / math.log(MAX_DISTANCE / max_exact)
                         * (N_BUCKETS - max_exact)).astype(I32)
    large = jnp.minimum(large, N_BUCKETS - 1)
    return jnp.where(n < max_exact, n, large)


def _route_plan(eid, n_tiles, tm):
    two, T = eid.shape
    e_flat = eid.reshape(-1)
    onehot = (e_flat[:, None] == jnp.arange(N_EXPERTS, dtype=I32)[None, :]).astype(I32)
    csum = jnp.cumsum(onehot, axis=0)
    rank = jnp.sum((csum - onehot) * onehot, axis=1)
    counts = csum[-1]
    ptiles = (counts + tm - 1) // tm
    tend = jnp.cumsum(ptiles)
    tstart = tend - ptiles
    slot = jnp.sum(onehot * tstart[None, :], axis=1) * tm + rank
    total = tend[-1]
    tile_ids = jnp.arange(n_tiles, dtype=I32)
    tile_valid = (tile_ids < total).astype(I32)
    tile_expert = jnp.sum((tile_ids[:, None] >= tend[None, :]).astype(I32), axis=1)
    last_expert = jnp.sum((total - 1 >= tend).astype(I32))
    tile_expert = jnp.minimum(tile_expert, last_expert).astype(I32)
    partial_last = jnp.any((tile_ids[:, None] == tend[None, :] - 1)
                           & (ptiles[None, :] > 0) & (counts[None, :] % tm != 0), axis=1)
    tile_fill = (partial_last | (tile_ids >= total)).astype(I32)
    return slot.astype(I32), tile_expert, tile_valid, tile_fill


def kernel(x, c, positions, rel_bias, ada_w, ada_b, norm1_g, w_in, lambda_q1, lambda_k1, lambda_q2,
           lambda_k2, subln_g, conv_w, w_out, norm2_g, router_group_w, router_group_b,
           router_expert_w, router_expert_b, expert_w_gate, expert_w_up, expert_w_down, final_g):
    B, S, D = x.shape
    T = B * S
    l = 0

    c_pad = jnp.zeros((8, D), F32).at[:B].set(c)
    ada = _ada(c_pad, ada_w[l], ada_b[l].reshape(1, -1))[:B]
    sh1, sc1, g1, sh2, sc2, g2 = [a.reshape(B, 1, D) for a in jnp.split(ada, 6, axis=-1)]

    w_in_bf = w_in[l].astype(BF16)
    A = ATTN_WIDTH
    qt, k, vt, conv = _inproj(x, sc1, sh1, norm1_g[l].reshape(1, D), w_in_bf[:, 0:A].T,
                              w_in_bf[:, A:2 * A], w_in_bf[:, 2 * A:3 * A].T, w_in_bf[:, 3 * A:],
                              conv_w[l])
    lut = (rel_bias.astype(F32)[_rel_bucket_table(), :].T * LOG2E).reshape(N_DIFF_HEADS, 2, BIAS_LUT)
    lam_params = jnp.stack([lambda_q1[l], lambda_k1[l], lambda_q2[l], lambda_k2[l]]).astype(F32)
    attn = _attention(qt, k, vt, positions, lut, lam_params, subln_g[l].reshape(V_HEAD_DIM, 1))

    wr_t = jnp.zeros((ROUTER_ROWS, D), F32)
    wr_t = wr_t.at[0:N_GROUPS].set(router_group_w[l].T)
    wr_t = wr_t.at[N_GROUPS:N_GROUPS + N_EXPERTS].set(router_expert_w[l].T).astype(BF16)
    rb = jnp.zeros((ROUTER_ROWS, 1), F32)
    rb = rb.at[0:N_GROUPS, 0].set(router_group_b[l])
    rb = rb.at[N_GROUPS:N_GROUPS + N_EXPERTS, 0].set(router_expert_b[l])
    x1, hp, ri, rw = _outproj(attn, conv, x, g1, sc2, sh2, norm2_g[l].reshape(1, D),
                              w_out[l].astype(BF16), wr_t, rb)

    eid = ri[:, 0:2, :].transpose(1, 0, 2).reshape(2, T)
    w_tok = rw[:, 0:2, :].transpose(0, 2, 1).reshape(T, 2)
    n_tiles = 2 * T // MOE_TM + N_EXPERTS
    slot, tile_expert, tile_valid, tile_fill = _route_plan(eid, n_tiles, MOE_TM)
    nct = T // COMB_TM
    pos = (slot * ROW_SLAB).reshape(2, nct, 1, COMB_TM)
    pos_tiles = jnp.concatenate([pos[0], pos[1]], axis=2)

    xs = _dispatch(tile_fill, pos_tiles, hp.reshape(T * ROW_SLAB, 128), n_tiles)
    ys = _moe(tile_expert, tile_valid, xs, expert_w_gate[l].astype(BF16),
              expert_w_up[l].astype(BF16), expert_w_down[l].astype(BF16))

    out = _combine(pos_tiles, ys, x1.reshape(T, D), g2, w_tok, final_g.reshape(1, D), S)
    return out.reshape(B, S, D)
```

```python
import functools
import math

import jax
import jax.numpy as jnp
from jax import lax
from jax.experimental import pallas as pl
from jax.experimental.pallas import tpu as pltpu

F32 = jnp.float32
BF16 = jnp.bfloat16
I32 = jnp.int32
U32 = jnp.uint32

D_MODEL = 1024
ATTN_WIDTH = 512
CONV_WIDTH = 512
N_DIFF_HEADS = 4
DIFF_HEAD_DIM = 64
V_HEAD_DIM = 128
IN_PROJ_WIDTH = 3 * ATTN_WIDTH + 3 * CONV_WIDTH
CONV_K = 3
N_BUCKETS = 32
MAX_DISTANCE = 128
N_GROUPS = 4
EXPERTS_PER_GROUP = 4
N_EXPERTS = 16
D_EXPERT = 512
NORM_EPS = 1e-6
SUBLN_EPS = 1e-5
NEG_INF = -1e30
LAMBDA_INIT = 0.8 - 0.6 * math.exp(-0.3 * 0)
QK_SCALE = DIFF_HEAD_DIM ** -0.5
LOG2E = math.log2(math.e)

BIAS_LUT = 128

ROW_TILE = 512
ATT_TQ = 512
ATT_CHAIN = 256
ATT_GROUP = 8
ATT_LOOKAHEAD = 8
ATT_TK = 256
MOE_TM = 256
COMB_TM = 256
ROUTER_ROWS = 32
ROW_SLAB = D_MODEL // 256
VMEM_LIMIT = 56 * 1024 * 1024


def _silu(x):
    return x * (1.0 / (1.0 + jnp.exp(-x)))


def _ada_kernel(c_ref, w_ref, b_ref, o_ref):
    s = _silu(c_ref[...])
    o_ref[...] = jnp.dot(s, w_ref[...], preferred_element_type=F32,
                         precision=lax.Precision.HIGHEST) + b_ref[...]


def _ada(c_pad, w, b):
    n = w.shape[1]
    bn = 1024
    return pl.pallas_call(
        _ada_kernel,
        grid=(n // bn,),
        in_specs=[pl.BlockSpec((8, D_MODEL), lambda j: (0, 0)),
                  pl.BlockSpec((D_MODEL, bn), lambda j: (0, j)),
                  pl.BlockSpec((1, bn), lambda j: (0, j))],
        out_specs=pl.BlockSpec((8, bn), lambda j: (0, j)),
        out_shape=jax.ShapeDtypeStruct((8, n), F32),
        name="ada",
    )(c_pad, w, b)


def _inproj_kernel(x_ref, sc_ref, sh_ref, g_ref, wqt_ref, wk_ref, wvt_ref, wc_ref, cw_ref,
                   qt_ref, k_ref, vt_ref, conv_ref, carry_ref):
    j = pl.program_id(1)
    tm = x_ref.shape[1]
    x = x_ref[0]
    ms = jnp.mean(x * x, axis=-1, keepdims=True)
    h = x * lax.rsqrt(ms + NORM_EPS) * g_ref[...]
    h = h * (1.0 + sc_ref[0]) + sh_ref[0]
    hb = h.astype(BF16)
    nt = (((1,), (1,)), ((), ()))

    def proj(c0):
        return jnp.dot(hb, wc_ref[:, c0:c0 + 512], preferred_element_type=F32)

    qt = lax.dot_general(wqt_ref[...], hb, nt, preferred_element_type=F32)
    qt_ref[0] = (qt * (QK_SCALE * LOG2E)).astype(BF16)
    k_ref[0] = jnp.dot(hb, wk_ref[...], preferred_element_type=F32).astype(BF16)
    vt_ref[0] = lax.dot_general(wvt_ref[...], hb, nt, preferred_element_type=F32).astype(BF16)
    gate_b = proj(0)
    u = proj(512) * proj(1024)

    @pl.when(j == 0)
    def _():
        carry_ref[...] = jnp.zeros_like(carry_ref)

    prev = carry_ref[...]
    row = lax.broadcasted_iota(I32, u.shape, 0)
    u1 = pltpu.roll(u, 1, axis=0)
    u2 = pltpu.roll(u, 2, axis=0)
    u1 = jnp.where(row == 0, prev[7:8, :], u1)
    u2 = jnp.where(row == 0, prev[6:7, :], jnp.where(row == 1, prev[7:8, :], u2))
    conv = cw_ref[0:1, :] * u2 + cw_ref[1:2, :] * u1 + cw_ref[2:3, :] * u
    conv_ref[0] = (gate_b * conv).astype(BF16)
    carry_ref[...] = u[tm - 8:tm, :]


def _inproj(x, sc1, sh1, g1n, wq_t, wk, wv_t, wc, conv_w):
    B, S, D = x.shape
    tm = ROW_TILE
    row_out = jax.ShapeDtypeStruct((B, S, 512), BF16)
    col_out = jax.ShapeDtypeStruct((B, 512, S), BF16)
    row_spec = pl.BlockSpec((1, tm, 512), lambda b, j: (b, j, 0))
    col_spec = pl.BlockSpec((1, 512, tm), lambda b, j: (b, 0, j))
    mod_spec = pl.BlockSpec((1, 1, D), lambda b, j: (b, 0, 0))
    const2 = lambda b, j: (0, 0)
    return pl.pallas_call(
        _inproj_kernel,
        grid=(B, S // tm),
        in_specs=[pl.BlockSpec((1, tm, D), lambda b, j: (b, j, 0)),
                  mod_spec, mod_spec,
                  pl.BlockSpec((1, D), const2),
                  pl.BlockSpec((ATTN_WIDTH, D), const2),
                  pl.BlockSpec((D, ATTN_WIDTH), const2),
                  pl.BlockSpec((ATTN_WIDTH, D), const2),
                  pl.BlockSpec((D, 3 * CONV_WIDTH), const2),
                  pl.BlockSpec((CONV_K, CONV_WIDTH), const2)],
        out_specs=[col_spec, row_spec, col_spec, row_spec],
        out_shape=[col_out, row_out, col_out, row_out],
        scratch_shapes=[pltpu.VMEM((8, CONV_WIDTH), F32)],
        compiler_params=pltpu.CompilerParams(
            dimension_semantics=("arbitrary", "arbitrary"), vmem_limit_bytes=VMEM_LIMIT),
        name="inproj",
    )(x, sc1, sh1, g1n, wq_t, wk, wv_t, wc, conv_w)


def _attn_kernel(qmin_ref, kmax_ref, consec_ref, qt_ref, k_ref, vt_ref, pr_ref, pc_ref, lut_ref,
                 lam_ref, sg_ref, o_ref, acc_ref, tz_ref):
    b = pl.program_id(0)
    qi = pl.program_id(2)
    nq = pl.num_programs(2)
    tq, tk = ATT_TQ, ATT_TK
    hw = ATT_CHAIN
    n_half = tq // hw
    nk = nq * (tq // tk)
    qt = qt_ref[0]
    feat = lax.broadcasted_iota(I32, qt.shape, 0)
    zero = jnp.zeros_like(qt)
    qts = (jnp.where(feat < DIFF_HEAD_DIM, qt, zero), jnp.where(feat >= DIFF_HEAD_DIM, qt, zero))
    luts = (lut_ref[0, 0:1, :], lut_ref[0, 1:2, :])
    fars = tuple(t[:, BIAS_LUT - 1:BIAS_LUT] for t in luts)
    pq = pr_ref[0]
    qmin = qmin_ref[b * nq + qi]
    czero = jnp.zeros((1, 1), F32)
    ones_rows = jnp.ones((16, tk), BF16)
    chains = [(mi, hi) for mi in range(2) for hi in range(n_half)]

    def gather_bias(mi, dist):
        table = jnp.broadcast_to(luts[mi], (tk, BIAS_LUT))
        return jnp.concatenate([jnp.take_along_axis(table, dist[:, o:o + 128], axis=1)
                                for o in range(0, hw, 128)], axis=1)

    @pl.when(qi == 0)
    def _():
        delta = (lax.broadcasted_iota(I32, (tk, hw), 1) - lax.broadcasted_iota(I32, (tk, hw), 0))
        for mi in range(2):
            diag_bias = gather_bias(mi, jnp.clip(delta, 0, BIAS_LUT - 1))
            tz_ref[mi, 0] = jnp.where(delta >= 0, diag_bias, NEG_INF)
            tz_ref[mi, 1] = gather_bias(mi, jnp.clip(delta + tk, 0, BIAS_LUT - 1))

    def run_blocks(blocks, state):
        loaded = []
        for (j, kinds) in blocks:
            ks = pl.multiple_of(j * tk, tk)
            kb = k_ref[0, pl.ds(ks, tk), :]
            vtb = jnp.concatenate([vt_ref[0, :, pl.ds(ks, tk)], ones_rows], axis=0)
            dist = None
            if any(kd is not None and kd.startswith("near") for kd in kinds):
                pk = pc_ref[0, pl.ds(ks, tk), :]
                dist = jnp.clip(pq - pk, 0, BIAS_LUT - 1)
            loaded.append((kb, vtb, dist))
        items = [(bi, n) for bi, blk in enumerate(blocks) for n, (mi, hi) in enumerate(chains)
                 if blk[1][hi] is not None]
        scores = {}

        def issue_qk(t):
            bi, n = items[t]
            mi, hi = chains[n]
            scores[t] = jnp.dot(loaded[bi][0], qts[mi][:, hi * hw:(hi + 1) * hw],
                                preferred_element_type=F32)

        state = list(state)
        for t in range(min(ATT_LOOKAHEAD, len(items))):
            issue_qk(t)
        for t, (bi, n) in enumerate(items):
            if t + ATT_LOOKAHEAD < len(items):
                issue_qk(t + ATT_LOOKAHEAD)
            _, vtb, dist = loaded[bi]
            mi, hi = chains[n]
            kind = blocks[bi][1][hi]
            cols = slice(hi * hw, (hi + 1) * hw)
            m, l = state[n]
            s = scores.pop(t)
            c = czero
            if kind == "far":
                c = fars[mi]
            elif kind == "tz_diag":
                s = tz_ref[mi, 0] + s
            elif kind == "tz_sub":
                s = tz_ref[mi, 1] + s
            else:
                s = gather_bias(mi, dist[:, cols]) + s
                if kind == "near_masked":
                    keep = (lax.broadcasted_iota(I32, (tk, hw), 0)
                            <= lax.broadcasted_iota(I32, (tk, hw), 1))
                    s = jnp.where(keep, s, NEG_INF)
            mn = jnp.maximum(m, jnp.max(s, axis=0, keepdims=True) + c)
            alpha = jnp.exp2(m - mn)
            p = jnp.exp2(s - (mn - c))
            pv = jnp.dot(vtb, p.astype(BF16), preferred_element_type=F32)
            l = alpha * l + pv[V_HEAD_DIM:V_HEAD_DIM + 1, :]
            acc_ref[mi, :, cols] = alpha * acc_ref[mi, :, cols] + pv[:V_HEAD_DIM, :]
            state[n] = (mn, l)
        return tuple(state)

    def block_is_far(j):
        return qmin - kmax_ref[b * nk + j] >= BIAS_LUT - 1

    def one_block(j, state):
        return lax.cond(block_is_far(j), lambda st: run_blocks([(j, ("far",) * n_half)], st),
                        lambda st: run_blocks([(j, ("near",) * n_half)], st), state)

    def group_body(width):
        def body(g, carry):
            j0, state = carry
            all_far = block_is_far(j0)
            for u in range(1, width):
                all_far = jnp.logical_and(all_far, block_is_far(j0 + u))
            far_blocks = [(j0 + u, ("far",) * n_half) for u in range(width)]
            state = lax.cond(
                all_far, lambda st: run_blocks(far_blocks, st),
                lambda st: lax.fori_loop(0, width, lambda u, s2: one_block(j0 + u, s2), st), state)
            return j0 + width, state
        return body

    acc_ref[...] = jnp.zeros_like(acc_ref)
    m0 = jnp.full((1, hw), NEG_INF, F32)
    l0 = jnp.zeros((1, hw), F32)
    state = tuple((m0, l0) for _ in chains)
    assert tq == 2 * tk and hw == tk
    n_full = 2 * qi
    n_main = jnp.maximum(n_full - 2, 0)
    j0 = jnp.int32(0)
    width = ATT_GROUP
    j0, state = lax.fori_loop(0, n_main // width, group_body(width), (j0, state))
    rem = n_main % width
    while width > 2:
        width //= 2
        j0, state = lax.fori_loop(0, (rem // width) % 2, group_body(width), (j0, state))

    consec = consec_ref[b * nq + qi] == 1
    diag_fast = [(n_full, ("tz_diag", "tz_sub")), (n_full + 1, (None, "tz_diag"))]
    diag_any = [(n_full, ("near_masked", "near")), (n_full + 1, (None, "near_masked"))]
    below_fast = [(n_full - 2, ("far", "far")), (n_full - 1, ("tz_sub", "far"))]
    below_any = [(n_full - 2, ("near", "near")), (n_full - 1, ("near", "near"))]

    def tail(below, diag):
        return lambda st: lax.cond(qi > 0, lambda s2: run_blocks(below + diag, s2),
                                   lambda s2: run_blocks(diag, s2), st)

    state = lax.cond(consec, tail(below_fast, diag_fast), tail(below_any, diag_any), state)

    l1 = jnp.concatenate([state[n][1] for n, (mi, hi) in enumerate(chains) if mi == 0], axis=1)
    l2 = jnp.concatenate([state[n][1] for n, (mi, hi) in enumerate(chains) if mi == 1], axis=1)
    lam = (jnp.exp(jnp.sum(lam_ref[0:1, :] * lam_ref[1:2, :], axis=-1, keepdims=True))
           - jnp.exp(jnp.sum(lam_ref[2:3, :] * lam_ref[3:4, :], axis=-1, keepdims=True))
           + LAMBDA_INIT)
    ot = acc_ref[0] * (1.0 / l1) - (lam * (1.0 / l2)) * acc_ref[1]
    ot = ot * lax.rsqrt(jnp.mean(ot * ot, axis=0, keepdims=True) + SUBLN_EPS)
    ot = ot * (sg_ref[...] * (1.0 - LAMBDA_INIT))
    o_ref[0] = ot.T.astype(BF16)


def _attention(qt, k, vt, positions, lut, lam_params, subln_g_col):
    B, S, _ = k.shape
    tq = ATT_TQ
    nq = S // tq
    pos_col = positions.reshape(B, S, 1)
    pos_row = positions.reshape(B, 1, S)
    qmin = jnp.min(positions.reshape(B * nq, tq), axis=1)
    kmax = jnp.max(positions.reshape(B * (S // ATT_TK), ATT_TK), axis=1)
    step_ok = jnp.concatenate([positions[:, 1:] - positions[:, :-1] == 1,
                               jnp.ones((B, 1), jnp.bool_)], axis=1).reshape(B, nq, tq)
    inner_ok = jnp.all(step_ok[:, :, :tq - 1], axis=2)
    link_ok = jnp.concatenate([jnp.ones((B, 1), jnp.bool_), step_ok[:, :-1, tq - 1]], axis=1)
    prev_ok = jnp.concatenate([jnp.ones((B, 1), jnp.bool_), inner_ok[:, :-1]], axis=1)
    consec = (inner_ok & link_ok & prev_ok).astype(I32).reshape(B * nq)
    grid_spec = pltpu.PrefetchScalarGridSpec(
        num_scalar_prefetch=3,
        grid=(B, N_DIFF_HEADS, nq),
        in_specs=[pl.BlockSpec((1, 128, tq), lambda b, h, i, *_: (b, h, i)),
                  pl.BlockSpec((1, S, 128), lambda b, h, i, *_: (b, 0, h)),
                  pl.BlockSpec((1, 128, S), lambda b, h, i, *_: (b, h, 0)),
                  pl.BlockSpec((1, 1, tq), lambda b, h, i, *_: (b, 0, i)),
                  pl.BlockSpec((1, S, 1), lambda b, h, i, *_: (b, 0, 0)),
                  pl.BlockSpec((1, 2, BIAS_LUT), lambda b, h, i, *_: (h, 0, 0)),
                  pl.BlockSpec((4, DIFF_HEAD_DIM), lambda b, h, i, *_: (0, 0)),
                  pl.BlockSpec((V_HEAD_DIM, 1), lambda b, h, i, *_: (0, 0))],
        out_specs=pl.BlockSpec((1, tq, 128), lambda b, h, i, *_: (b, i, h)),
        scratch_shapes=[pltpu.VMEM((2, V_HEAD_DIM, tq), F32),
                        pltpu.VMEM((2, 2, ATT_TK, ATT_CHAIN), F32)],
    )
    return pl.pallas_call(
        _attn_kernel,
        grid_spec=grid_spec,
        out_shape=jax.ShapeDtypeStruct((B, S, ATTN_WIDTH), BF16),
        compiler_params=pltpu.CompilerParams(
            dimension_semantics=("arbitrary", "arbitrary", "arbitrary"),
            vmem_limit_bytes=VMEM_LIMIT),
        name="diffattn",
    )(qmin, kmax, consec, qt, k, vt, pos_row, pos_col, lut, lam_params, subln_g_col)


def _outproj_kernel(at_ref, cv_ref, x_ref, g1_ref, sc_ref, sh_ref, gn_ref, wo_ref, wr_ref, rb_ref,
                    x1_ref, hp_ref, ri_ref, rw_ref):
    tm = x_ref.shape[1]
    mix = (jnp.dot(at_ref[0], wo_ref[0:ATTN_WIDTH, :], preferred_element_type=F32)
           + jnp.dot(cv_ref[0], wo_ref[ATTN_WIDTH:, :], preferred_element_type=F32))
    x1 = x_ref[0] + g1_ref[0] * mix
    x1_ref[0] = x1
    ms = jnp.mean(x1 * x1, axis=-1, keepdims=True)
    h = x1 * lax.rsqrt(ms + NORM_EPS) * gn_ref[...]
    h = h * (1.0 + sc_ref[0]) + sh_ref[0]
    hb = h.astype(BF16)

    _store_packed_rows(hp_ref.at[0], h)

    lg_all = lax.dot_general(wr_ref[...], hb, (((1,), (1,)), ((), ())),
                             preferred_element_type=F32) + rb_ref[...]
    lg = lg_all[0:N_GROUPS, :]
    le = lg_all[N_GROUPS:N_GROUPS + N_EXPERTS, :]
    row4 = lax.broadcasted_iota(I32, (N_GROUPS, tm), 0)
    gmax = jnp.max(lg, axis=0, keepdims=True)
    pg_sel = 1.0 / jnp.sum(jnp.exp(lg - gmax), axis=0, keepdims=True)
    gsel = jnp.min(jnp.where(lg == gmax, row4, N_GROUPS), axis=0, keepdims=True)
    sel = jnp.zeros((EXPERTS_PER_GROUP, tm), F32)
    for g in range(N_GROUPS):
        sel = jnp.where(gsel == g, le[g * EXPERTS_PER_GROUP:(g + 1) * EXPERTS_PER_GROUP, :], sel)
    v1 = jnp.max(sel, axis=0, keepdims=True)
    i1 = jnp.min(jnp.where(sel == v1, row4, EXPERTS_PER_GROUP), axis=0, keepdims=True)
    rest = jnp.where(row4 == i1, -jnp.inf, sel)
    v2 = jnp.max(rest, axis=0, keepdims=True)
    i2 = jnp.min(jnp.where(rest == v2, row4, EXPERTS_PER_GROUP), axis=0, keepdims=True)
    e2 = jnp.exp(v2 - v1)
    w1 = pg_sel / (1.0 + e2)
    w2 = pg_sel * e2 / (1.0 + e2)
    row8 = lax.broadcasted_iota(I32, (8, tm), 0)
    eid1 = gsel * EXPERTS_PER_GROUP + i1
    eid2 = gsel * EXPERTS_PER_GROUP + i2
    ri_ref[0] = jnp.where(row8 == 0, eid1, jnp.where(row8 == 1, eid2, 0))
    rw_ref[0] = jnp.where(row8 == 0, w1, jnp.where(row8 == 1, w2, 0.0))


def _outproj(attn, conv, x, g1, sc2, sh2, g2n, w_out_bf, wr_t, rb):
    B, S, D = x.shape
    tm = ROW_TILE
    half_spec = pl.BlockSpec((1, tm, 512), lambda b, j: (b, j, 0))
    full_spec = pl.BlockSpec((1, tm, D), lambda b, j: (b, j, 0))
    mod_spec = pl.BlockSpec((1, 1, D), lambda b, j: (b, 0, 0))
    rt_spec = pl.BlockSpec((1, 8, tm), lambda b, j: (b, 0, j))
    return pl.pallas_call(
        _outproj_kernel,
        grid=(B, S // tm),
        in_specs=[half_spec, half_spec, full_spec, mod_spec, mod_spec, mod_spec,
                  pl.BlockSpec((1, D), lambda b, j: (0, 0)),
                  pl.BlockSpec((D, D), lambda b, j: (0, 0)),
                  pl.BlockSpec((ROUTER_ROWS, D), lambda b, j: (0, 0)),
                  pl.BlockSpec((ROUTER_ROWS, 1), lambda b, j: (0, 0))],
        out_specs=[full_spec, pl.BlockSpec((1, tm * ROW_SLAB, 128), lambda b, j: (b, j, 0)),
                   rt_spec, rt_spec],
        out_shape=[jax.ShapeDtypeStruct((B, S, D), F32),
                   jax.ShapeDtypeStruct((B, S * ROW_SLAB, 128), U32),
                   jax.ShapeDtypeStruct((B, 8, S), I32),
                   jax.ShapeDtypeStruct((B, 8, S), F32)],
        compiler_params=pltpu.CompilerParams(
            dimension_semantics=("arbitrary", "arbitrary"), vmem_limit_bytes=VMEM_LIMIT),
        name="outproj",
    )(attn, conv, x, g1, sc2, sh2, g2n, w_out_bf, wr_t, rb)


def _store_packed_rows(dst, x):
    half = D_MODEL // 2
    xb = x.astype(BF16).astype(F32)
    packed = (pltpu.bitcast(xb[:, :half], U32) >> 16) | (
        pltpu.bitcast(xb[:, half:], U32) & jnp.uint32(0xFFFF0000))
    rows = x.shape[0]
    for c in range(ROW_SLAB):
        dst[pl.ds(c, rows, stride=ROW_SLAB), :] = packed[:, c * 128:(c + 1) * 128]


def _load_packed_rows(src, row0, rows):
    packed = jnp.concatenate(
        [src[pl.ds(row0 * ROW_SLAB + c, rows, stride=ROW_SLAB), :] for c in range(ROW_SLAB)], axis=1)
    lo = pltpu.bitcast(packed << 16, F32)
    hi = pltpu.bitcast(packed & jnp.uint32(0xFFFF0000), F32)
    return lo, hi


def _row_gather_start(src_hbm, idx_ref, dst, sem, n_rows):
    for r in range(n_rows):
        off = pl.multiple_of(idx_ref[0, 0, r], ROW_SLAB)
        pltpu.make_async_copy(src_hbm.at[pl.ds(off, ROW_SLAB)],
                              dst.at[pl.ds(r * ROW_SLAB, ROW_SLAB)], sem).start(priority=r % 2)


def _row_gather_wait(src_hbm, dst, sem, n_rows):
    pltpu.make_async_copy(src_hbm.at[pl.ds(0, n_rows * ROW_SLAB)], dst, sem).wait()


def _dispatch_kernel(fill_ref, pos_ref, hp_ref, xs_hbm, zbuf, sem):
    i = pl.program_id(0)
    tm = COMB_TM
    tile_rows = MOE_TM * ROW_SLAB

    @pl.when(i == 0)
    def _():
        zbuf[...] = jnp.zeros_like(zbuf)

        def fill_copy(t):
            return pltpu.make_async_copy(
                zbuf, xs_hbm.at[pl.ds(pl.multiple_of(t * tile_rows, tile_rows), tile_rows)],
                sem.at[0])

        def start(t, c):
            @pl.when(fill_ref[t] == 1)
            def _():
                fill_copy(t).start()
            return c

        def wait(t, c):
            @pl.when(fill_ref[t] == 1)
            def _():
                fill_copy(t).wait()
            return c

        lax.fori_loop(0, fill_ref.shape[0], start, 0)
        lax.fori_loop(0, fill_ref.shape[0], wait, 0)

    par = lax.rem(i, 2)
    base = pl.multiple_of(i * (tm * ROW_SLAB), tm * ROW_SLAB)
    for r in range(2 * tm):
        off = pl.multiple_of(pos_ref[0, 0, r], ROW_SLAB)
        pltpu.make_async_copy(hp_ref.at[pl.ds(base + (r % tm) * ROW_SLAB, ROW_SLAB)],
                              xs_hbm.at[pl.ds(off, ROW_SLAB)],
                              sem.at[1 + par]).start(priority=r % 2)

    def drain(parity):
        for _ in range(2):
            pltpu.make_async_copy(hp_ref.at[pl.ds(0, tm * ROW_SLAB)],
                                  xs_hbm.at[pl.ds(0, tm * ROW_SLAB)], sem.at[1 + parity]).wait()

    @pl.when(i > 0)
    def _():
        drain(1 - par)

    @pl.when(i == pl.num_programs(0) - 1)
    def _():
        drain(par)


def _dispatch(tile_fill, pos_tiles, hp, n_tiles):
    T = hp.shape[0] // ROW_SLAB
    tm = COMB_TM
    grid_spec = pltpu.PrefetchScalarGridSpec(
        num_scalar_prefetch=1,
        grid=(T // tm,),
        in_specs=[pl.BlockSpec((1, 1, 2 * tm), lambda i, f: (i, 0, 0), memory_space=pltpu.SMEM),
                  pl.BlockSpec(memory_space=pltpu.VMEM)],
        out_specs=pl.BlockSpec(memory_space=pl.ANY),
        scratch_shapes=[pltpu.VMEM((MOE_TM * ROW_SLAB, 128), U32),
                        pltpu.SemaphoreType.DMA((3,))],
    )
    return pl.pallas_call(
        _dispatch_kernel,
        grid_spec=grid_spec,
        out_shape=jax.ShapeDtypeStruct((n_tiles * MOE_TM * ROW_SLAB, 128), U32),
        compiler_params=pltpu.CompilerParams(
            dimension_semantics=("arbitrary",), vmem_limit_bytes=VMEM_LIMIT),
        name="dispatch",
    )(tile_fill, pos_tiles, hp)


def _moe_kernel(te_ref, tv_ref, xs_ref, wg_ref, wu_ref, wd_ref, y_ref, wg_bf, wu_bf, wd_bf):
    i = pl.program_id(0)
    tm = MOE_TM

    @pl.when(jnp.logical_or(i == 0, te_ref[i] != te_ref[jnp.maximum(i - 1, 0)]))
    def _():
        wg_bf[...] = wg_ref[0].astype(BF16)
        wu_bf[...] = wu_ref[0].astype(BF16)
        wd_bf[...] = wd_ref[0].astype(BF16)

    @pl.when(tv_ref[i] == 1)
    def _():
        half = D_MODEL // 2
        lo, hi = _load_packed_rows(xs_ref, 0, tm)
        lo = lo.astype(BF16)
        hi = hi.astype(BF16)
        g = (jnp.dot(lo, wg_bf[0:half, :], preferred_element_type=F32)
             + jnp.dot(hi, wg_bf[half:, :], preferred_element_type=F32))
        u = (jnp.dot(lo, wu_bf[0:half, :], preferred_element_type=F32)
             + jnp.dot(hi, wu_bf[half:, :], preferred_element_type=F32))
        hid = (_silu(g) * u).astype(BF16)
        _store_packed_rows(y_ref, jnp.dot(hid, wd_bf[...], preferred_element_type=F32))

    @pl.when(tv_ref[i] == 0)
    def _():
        y_ref[...] = jnp.zeros_like(y_ref)


def _moe(tile_expert, tile_valid, xs, wg, wu, wd):
    nt = tile_expert.shape[0]
    tm = MOE_TM
    D = D_MODEL
    grid_spec = pltpu.PrefetchScalarGridSpec(
        num_scalar_prefetch=2,
        grid=(nt,),
        in_specs=[pl.BlockSpec((tm * ROW_SLAB, 128), lambda i, te, tv: (i, 0)),
                  pl.BlockSpec((1, D, D_EXPERT), lambda i, te, tv: (te[i], 0, 0)),
                  pl.BlockSpec((1, D, D_EXPERT), lambda i, te, tv: (te[i], 0, 0)),
                  pl.BlockSpec((1, D_EXPERT, D), lambda i, te, tv: (te[i], 0, 0))],
        out_specs=pl.BlockSpec((tm * ROW_SLAB, 128), lambda i, te, tv: (i, 0)),
        scratch_shapes=[pltpu.VMEM((D, D_EXPERT), BF16), pltpu.VMEM((D, D_EXPERT), BF16),
                        pltpu.VMEM((D_EXPERT, D), BF16)],
    )
    return pl.pallas_call(
        _moe_kernel,
        grid_spec=grid_spec,
        out_shape=jax.ShapeDtypeStruct((nt * tm * ROW_SLAB, 128), U32),
        compiler_params=pltpu.CompilerParams(
            dimension_semantics=("arbitrary",), vmem_limit_bytes=VMEM_LIMIT),
        name="moe",
    )(tile_expert, tile_valid, xs, wg, wu, wd)


def _combine_kernel(pos_ref, posn_ref, ys_hbm, x1_ref, g2_ref, w_ref, fg_ref, o_ref, rbuf, sem):
    i = pl.program_id(0)
    n = pl.num_programs(0)
    slot = lax.rem(i, 2)
    nslot = 1 - slot
    tm = COMB_TM

    @pl.when(i == 0)
    def _():
        _row_gather_start(ys_hbm, pos_ref, rbuf.at[0], sem.at[0], 2 * tm)

    @pl.when(i + 1 < n)
    def _():
        _row_gather_start(ys_hbm, posn_ref, rbuf.at[nslot], sem.at[nslot], 2 * tm)

    _row_gather_wait(ys_hbm, rbuf.at[slot], sem.at[slot], 2 * tm)
    w = w_ref[...]
    r1 = jnp.concatenate(_load_packed_rows(rbuf.at[slot], 0, tm), axis=1)
    r2 = jnp.concatenate(_load_packed_rows(rbuf.at[slot], tm, tm), axis=1)
    moe = w[:, 0:1] * r1 + w[:, 1:2] * r2
    y = x1_ref[...] + g2_ref[0] * moe
    ms = jnp.mean(y * y, axis=-1, keepdims=True)
    o_ref[...] = y * lax.rsqrt(ms + NORM_EPS) * fg_ref[...]


def _combine(pos_tiles, ys, x1, g2, w_tok, final_g, seq_len):
    T, D = x1.shape
    tm = COMB_TM
    nt = T // tm
    per_b = seq_len // tm
    return pl.pallas_call(
        _combine_kernel,
        grid=(nt,),
        in_specs=[pl.BlockSpec((1, 1, 2 * tm), lambda i: (i, 0, 0), memory_space=pltpu.SMEM),
                  pl.BlockSpec((1, 1, 2 * tm), lambda i: (jnp.minimum(i + 1, nt - 1), 0, 0),
                               memory_space=pltpu.SMEM),
                  pl.BlockSpec(memory_space=pl.ANY),
                  pl.BlockSpec((tm, D), lambda i: (i, 0)),
                  pl.BlockSpec((1, 1, D), lambda i: (i // per_b, 0, 0)),
                  pl.BlockSpec((tm, 2), lambda i: (i, 0)),
                  pl.BlockSpec((1, D), lambda i: (0, 0))],
        out_specs=pl.BlockSpec((tm, D), lambda i: (i, 0)),
        out_shape=jax.ShapeDtypeStruct((T, D), F32),
        scratch_shapes=[pltpu.VMEM((2, 2 * tm * ROW_SLAB, 128), U32),
                        pltpu.SemaphoreType.DMA((2,))],
        compiler_params=pltpu.CompilerParams(
            dimension_semantics=("arbitrary",), vmem_limit_bytes=VMEM_LIMIT),
        name="combine",
    )(pos_tiles, pos_tiles, ys, x1, g2, w_tok, final_g)


def _rel_bucket_table():
    n = jnp.arange(BIAS_LUT, dtype=I32)
    max_exact = N_BUCKETS // 2
    nf = jnp.maximum(n, 1).astype(F32)
    large = max_exact + (jnp.log(nf / max_exact) / math.log(MAX_DISTANCE / max_exact)
                         * (N_BUCKETS - max_exact)).astype(I32)
    large = jnp.minimum(large, N_BUCKETS - 1)
    return jnp.where(n < max_exact, n, large)


def _route_plan(eid, n_tiles, tm):
    two, T = eid.shape
    e_flat = eid.reshape(-1)
    onehot = (e_flat[:, None] == jnp.arange(N_EXPERTS, dtype=I32)[None, :]).astype(I32)
    csum = jnp.cumsum(onehot, axis=0)
    rank = jnp.sum((csum - onehot) * onehot, axis=1)
    counts = csum[-1]
    ptiles = (counts + tm - 1) // tm
    tend = jnp.cumsum(ptiles)
    tstart = tend - ptiles
    slot = jnp.sum(onehot * tstart[None, :], axis=1) * tm + rank
    total = tend[-1]
    tile_ids = jnp.arange(n_tiles, dtype=I32)
    tile_valid = (tile_ids < total).astype(I32)
    tile_expert = jnp.sum((tile_ids[:, None] >= tend[None, :]).astype(I32), axis=1)
    last_expert = jnp.sum((total - 1 >= tend).astype(I32))
    tile_expert = jnp.minimum(tile_expert, last_expert).astype(I32)
    partial_last = jnp.any((tile_ids[:, None] == tend[None, :] - 1)
                           & (ptiles[None, :] > 0) & (counts[None, :] % tm != 0), axis=1)
    tile_fill = (partial_last | (tile_ids >= total)).astype(I32)
    return slot.astype(I32), tile_expert, tile_valid, tile_fill


def kernel(x, c, positions, rel_bias, ada_w, ada_b, norm1_g, w_in, lambda_q1, lambda_k1, lambda_q2,
           lambda_k2, subln_g, conv_w, w_out, norm2_g, router_group_w, router_group_b,
           router_expert_w, router_expert_b, expert_w_gate, expert_w_up, expert_w_down, final_g):
    B, S, D = x.shape
    T = B * S
    l = 0

    c_pad = jnp.zeros((8, D), F32).at[:B].set(c)
    ada = _ada(c_pad, ada_w[l], ada_b[l].reshape(1, -1))[:B]
    sh1, sc1, g1, sh2, sc2, g2 = [a.reshape(B, 1, D) for a in jnp.split(ada, 6, axis=-1)]

    w_in_bf = w_in[l].astype(BF16)
    A = ATTN_WIDTH
    qt, k, vt, conv = _inproj(x, sc1, sh1, norm1_g[l].reshape(1, D), w_in_bf[:, 0:A].T,
                              w_in_bf[:, A:2 * A], w_in_bf[:, 2 * A:3 * A].T, w_in_bf[:, 3 * A:],
                              conv_w[l])
    lut = (rel_bias.astype(F32)[_rel_bucket_table(), :].T * LOG2E).reshape(N_DIFF_HEADS, 2, BIAS_LUT)
    lam_params = jnp.stack([lambda_q1[l], lambda_k1[l], lambda_q2[l], lambda_k2[l]]).astype(F32)
    attn = _attention(qt, k, vt, positions, lut, lam_params, subln_g[l].reshape(V_HEAD_DIM, 1))

    wr_t = jnp.zeros((ROUTER_ROWS, D), F32)
    wr_t = wr_t.at[0:N_GROUPS].set(router_group_w[l].T)
    wr_t = wr_t.at[N_GROUPS:N_GROUPS + N_EXPERTS].set(router_expert_w[l].T).astype(BF16)
    rb = jnp.zeros((ROUTER_ROWS, 1), F32)
    rb = rb.at[0:N_GROUPS, 0].set(router_group_b[l])
    rb = rb.at[N_GROUPS:N_GROUPS + N_EXPERTS, 0].set(router_expert_b[l])
    x1, hp, ri, rw = _outproj(attn, conv, x, g1, sc2, sh2, norm2_g[l].reshape(1, D),
                              w_out[l].astype(BF16), wr_t, rb)

    eid = ri[:, 0:2, :].transpose(1, 0, 2).reshape(2, T)
    w_tok = rw[:, 0:2, :].transpose(0, 2, 1).reshape(T, 2)
    n_tiles = 2 * T // MOE_TM + N_EXPERTS
    slot, tile_expert, tile_valid, tile_fill = _route_plan(eid, n_tiles, MOE_TM)
    nct = T // COMB_TM
    pos = (slot * ROW_SLAB).reshape(2, nct, 1, COMB_TM)
    pos_tiles = jnp.concatenate([pos[0], pos[1]], axis=2)

    xs = _dispatch(tile_fill, pos_tiles, hp.reshape(T * ROW_SLAB, 128), n_tiles)
    ys = _moe(tile_expert, tile_valid, xs, expert_w_gate[l], expert_w_up[l], expert_w_down[l])

    out = _combine(pos_tiles, ys, x1.reshape(T, D), g2, w_tok, final_g.reshape(1, D), S)
    return out.reshape(B, S, D)
```

```python
import functools
import math

import jax
import jax.numpy as jnp
from jax import lax
from jax.experimental import pallas as pl
from jax.experimental.pallas import tpu as pltpu

F32 = jnp.float32
BF16 = jnp.bfloat16
I32 = jnp.int32
U32 = jnp.uint32

D_MODEL = 1024
ATTN_WIDTH = 512
CONV_WIDTH = 512
N_DIFF_HEADS = 4
DIFF_HEAD_DIM = 64
V_HEAD_DIM = 128
IN_PROJ_WIDTH = 3 * ATTN_WIDTH + 3 * CONV_WIDTH
CONV_K = 3
N_BUCKETS = 32
MAX_DISTANCE = 128
N_GROUPS = 4
EXPERTS_PER_GROUP = 4
N_EXPERTS = 16
D_EXPERT = 512
NORM_EPS = 1e-6
SUBLN_EPS = 1e-5
NEG_INF = -1e30
LAMBDA_INIT = 0.8 - 0.6 * math.exp(-0.3 * 0)
QK_SCALE = DIFF_HEAD_DIM ** -0.5
LOG2E = math.log2(math.e)

BIAS_LUT = 128

ROW_TILE = 512
ATT_TQ = 1024
ATT_CHAIN = 256
ATT_GROUP = 8
ATT_LOOKAHEAD = 8
ATT_TK = 256
MOE_TM = 256
COMB_TM = 256
ROUTER_ROWS = 32
ROW_SLAB = D_MODEL // 256
VMEM_LIMIT = 56 * 1024 * 1024


def _silu(x):
    return x * (1.0 / (1.0 + jnp.exp(-x)))


def _ada_kernel(c_ref, w_ref, b_ref, o_ref):
    s = _silu(c_ref[...])
    o_ref[...] = jnp.dot(s, w_ref[...], preferred_element_type=F32,
                         precision=lax.Precision.HIGHEST) + b_ref[...]


def _ada(c_pad, w, b):
    n = w.shape[1]
    bn = 1024
    return pl.pallas_call(
        _ada_kernel,
        grid=(n // bn,),
        in_specs=[pl.BlockSpec((8, D_MODEL), lambda j: (0, 0)),
                  pl.BlockSpec((D_MODEL, bn), lambda j: (0, j)),
                  pl.BlockSpec((1, bn), lambda j: (0, j))],
        out_specs=pl.BlockSpec((8, bn), lambda j: (0, j)),
        out_shape=jax.ShapeDtypeStruct((8, n), F32),
        name="ada",
    )(c_pad, w, b)


def _inproj_kernel(x_ref, sc_ref, sh_ref, g_ref, wqt_ref, wk_ref, wvt_ref, wc_ref, cw_ref,
                   qt_ref, k_ref, vt_ref, conv_ref, carry_ref):
    j = pl.program_id(1)
    tm = x_ref.shape[1]
    x = x_ref[0]
    ms = jnp.mean(x * x, axis=-1, keepdims=True)
    h = x * lax.rsqrt(ms + NORM_EPS) * g_ref[...]
    h = h * (1.0 + sc_ref[0]) + sh_ref[0]
    hb = h.astype(BF16)
    nt = (((1,), (1,)), ((), ()))

    def proj(c0):
        return jnp.dot(hb, wc_ref[:, c0:c0 + 512], preferred_element_type=F32)

    qt = lax.dot_general(wqt_ref[...], hb, nt, preferred_element_type=F32)
    qt_ref[0] = (qt * (QK_SCALE * LOG2E)).astype(BF16)
    k_ref[0] = jnp.dot(hb, wk_ref[...], preferred_element_type=F32).astype(BF16)
    vt_ref[0] = lax.dot_general(wvt_ref[...], hb, nt, preferred_element_type=F32).astype(BF16)
    gate_b = proj(0)
    u = proj(512) * proj(1024)

    @pl.when(j == 0)
    def _():
        carry_ref[...] = jnp.zeros_like(carry_ref)

    prev = carry_ref[...]
    row = lax.broadcasted_iota(I32, u.shape, 0)
    u1 = pltpu.roll(u, 1, axis=0)
    u2 = pltpu.roll(u, 2, axis=0)
    u1 = jnp.where(row == 0, prev[7:8, :], u1)
    u2 = jnp.where(row == 0, prev[6:7, :], jnp.where(row == 1, prev[7:8, :], u2))
    conv = cw_ref[0:1, :] * u2 + cw_ref[1:2, :] * u1 + cw_ref[2:3, :] * u
    conv_ref[0] = (gate_b * conv).astype(BF16)
    carry_ref[...] = u[tm - 8:tm, :]


def _inproj(x, sc1, sh1, g1n, wq_t, wk, wv_t, wc, conv_w):
    B, S, D = x.shape
    tm = ROW_TILE
    row_out = jax.ShapeDtypeStruct((B, S, 512), BF16)
    col_out = jax.ShapeDtypeStruct((B, 512, S), BF16)
    row_spec = pl.BlockSpec((1, tm, 512), lambda b, j: (b, j, 0))
    col_spec = pl.BlockSpec((1, 512, tm), lambda b, j: (b, 0, j))
    mod_spec = pl.BlockSpec((1, 1, D), lambda b, j: (b, 0, 0))
    const2 = lambda b, j: (0, 0)
    return pl.pallas_call(
        _inproj_kernel,
        grid=(B, S // tm),
        in_specs=[pl.BlockSpec((1, tm, D), lambda b, j: (b, j, 0)),
                  mod_spec, mod_spec,
                  pl.BlockSpec((1, D), const2),
                  pl.BlockSpec((ATTN_WIDTH, D), const2),
                  pl.BlockSpec((D, ATTN_WIDTH), const2),
                  pl.BlockSpec((ATTN_WIDTH, D), const2),
                  pl.BlockSpec((D, 3 * CONV_WIDTH), const2),
                  pl.BlockSpec((CONV_K, CONV_WIDTH), const2)],
        out_specs=[col_spec, row_spec, col_spec, row_spec],
        out_shape=[col_out, row_out, col_out, row_out],
        scratch_shapes=[pltpu.VMEM((8, CONV_WIDTH), F32)],
        compiler_params=pltpu.CompilerParams(
            dimension_semantics=("arbitrary", "arbitrary"), vmem_limit_bytes=VMEM_LIMIT),
        name="inproj",
    )(x, sc1, sh1, g1n, wq_t, wk, wv_t, wc, conv_w)


def _attn_kernel(qmin_ref, kmax_ref, consec_ref, qt_ref, k_ref, vt_ref, pr_ref, pc_ref, lut_ref,
                 lam_ref, sg_ref, o_ref, acc_ref, tz_ref):
    b = pl.program_id(0)
    qi = pl.program_id(2)
    nq = pl.num_programs(2)
    tq, tk = ATT_TQ, ATT_TK
    hw = ATT_CHAIN
    n_half = tq // hw
    nk = nq * (tq // tk)
    qt = qt_ref[0]
    feat = lax.broadcasted_iota(I32, qt.shape, 0)
    zero = jnp.zeros_like(qt)
    qts = (jnp.where(feat < DIFF_HEAD_DIM, qt, zero), jnp.where(feat >= DIFF_HEAD_DIM, qt, zero))
    luts = (lut_ref[0, 0:1, :], lut_ref[0, 1:2, :])
    fars = tuple(t[:, BIAS_LUT - 1:BIAS_LUT] for t in luts)
    pq = pr_ref[0]
    qmin = qmin_ref[b * nq + qi]
    czero = jnp.zeros((1, 1), F32)
    ones_rows = jnp.ones((16, tk), BF16)
    chains = [(mi, hi) for mi in range(2) for hi in range(n_half)]

    def gather_bias(mi, dist):
        table = jnp.broadcast_to(luts[mi], (tk, BIAS_LUT))
        return jnp.concatenate([jnp.take_along_axis(table, dist[:, o:o + 128], axis=1)
                                for o in range(0, hw, 128)], axis=1)

    @pl.when(qi == 0)
    def _():
        delta = (lax.broadcasted_iota(I32, (tk, hw), 1) - lax.broadcasted_iota(I32, (tk, hw), 0))
        for mi in range(2):
            diag_bias = gather_bias(mi, jnp.clip(delta, 0, BIAS_LUT - 1))
            tz_ref[mi, 0] = jnp.where(delta >= 0, diag_bias, NEG_INF)
            tz_ref[mi, 1] = gather_bias(mi, jnp.clip(delta + tk, 0, BIAS_LUT - 1))

    def run_blocks(blocks, state):
        loaded = []
        for (j, kinds) in blocks:
            ks = pl.multiple_of(j * tk, tk)
            kb = k_ref[0, pl.ds(ks, tk), :]
            vtb = jnp.concatenate([vt_ref[0, :, pl.ds(ks, tk)], ones_rows], axis=0)
            dist = None
            if any(kd is not None and kd.startswith("near") for kd in kinds):
                pk = pc_ref[0, pl.ds(ks, tk), :]
                dist = jnp.clip(pq - pk, 0, BIAS_LUT - 1)
            loaded.append((kb, vtb, dist))
        items = [(bi, n) for bi, blk in enumerate(blocks) for n, (mi, hi) in enumerate(chains)
                 if blk[1][hi] is not None]
        scores = {}

        def issue_qk(t):
            bi, n = items[t]
            mi, hi = chains[n]
            scores[t] = jnp.dot(loaded[bi][0], qts[mi][:, hi * hw:(hi + 1) * hw],
                                preferred_element_type=F32)

        state = list(state)
        for t in range(min(ATT_LOOKAHEAD, len(items))):
            issue_qk(t)
        for t, (bi, n) in enumerate(items):
            if t + ATT_LOOKAHEAD < len(items):
                issue_qk(t + ATT_LOOKAHEAD)
            _, vtb, dist = loaded[bi]
            mi, hi = chains[n]
            kind = blocks[bi][1][hi]
            cols = slice(hi * hw, (hi + 1) * hw)
            m, l = state[n]
            s = scores.pop(t)
            c = czero
            if kind == "far":
                c = fars[mi]
            elif kind == "tz_diag":
                s = tz_ref[mi, 0] + s
            elif kind == "tz_sub":
                s = tz_ref[mi, 1] + s
            else:
                s = gather_bias(mi, dist[:, cols]) + s
                if kind == "near_masked":
                    keep = (lax.broadcasted_iota(I32, (tk, hw), 0)
                            <= lax.broadcasted_iota(I32, (tk, hw), 1))
                    s = jnp.where(keep, s, NEG_INF)
            mn = jnp.maximum(m, jnp.max(s, axis=0, keepdims=True) + c)
            alpha = jnp.exp2(m - mn)
            p = jnp.exp2(s - (mn - c))
            pv = jnp.dot(vtb, p.astype(BF16), preferred_element_type=F32)
            l = alpha * l + pv[V_HEAD_DIM:V_HEAD_DIM + 1, :]
            acc_ref[mi, :, cols] = alpha * acc_ref[mi, :, cols] + pv[:V_HEAD_DIM, :]
            state[n] = (mn, l)
        return tuple(state)

    def block_is_far(j):
        return qmin - kmax_ref[b * nk + j] >= BIAS_LUT - 1

    def one_block(j, state):
        return lax.cond(block_is_far(j), lambda st: run_blocks([(j, ("far",) * n_half)], st),
                        lambda st: run_blocks([(j, ("near",) * n_half)], st), state)

    def group_body(width):
        def body(g, carry):
            j0, state = carry
            all_far = block_is_far(j0)
            for u in range(1, width):
                all_far = jnp.logical_and(all_far, block_is_far(j0 + u))
            far_blocks = [(j0 + u, ("far",) * n_half) for u in range(width)]
            state = lax.cond(
                all_far, lambda st: run_blocks(far_blocks, st),
                lambda st: lax.fori_loop(0, width, lambda u, s2: one_block(j0 + u, s2), st), state)
            return j0 + width, state
        return body

    acc_ref[...] = jnp.zeros_like(acc_ref)
    m0 = jnp.full((1, hw), NEG_INF, F32)
    l0 = jnp.zeros((1, hw), F32)
    state = tuple((m0, l0) for _ in chains)
    assert hw == tk and n_half % 2 == 0
    n_full = n_half * qi
    n_main = jnp.maximum(n_full - 2, 0)
    j0 = jnp.int32(0)
    width = ATT_GROUP
    j0, state = lax.fori_loop(0, n_main // width, group_body(width), (j0, state))
    rem = n_main % width
    while width > 2:
        width //= 2
        j0, state = lax.fori_loop(0, (rem // width) % 2, group_body(width), (j0, state))

    consec = consec_ref[b * nq + qi] == 1
    def diag_kinds(d, on_diag, below, further):
        return tuple(None if hi < d else on_diag if hi == d else below if hi == d + 1 else further
                     for hi in range(n_half))

    diag_fast = [(n_full + d, diag_kinds(d, "tz_diag", "tz_sub", "far")) for d in range(n_half)]
    diag_any = [(n_full + d, diag_kinds(d, "near_masked", "near", "near")) for d in range(n_half)]
    below_fast = [(n_full - 2, ("far",) * n_half),
                  (n_full - 1, ("tz_sub",) + ("far",) * (n_half - 1))]
    below_any = [(n_full - 2, ("near",) * n_half), (n_full - 1, ("near",) * n_half)]

    def tail(below, diag):
        return lambda st: lax.cond(qi > 0, lambda s2: run_blocks(below + diag, s2),
                                   lambda s2: run_blocks(diag, s2), st)

    state = lax.cond(consec, tail(below_fast, diag_fast), tail(below_any, diag_any), state)

    l1 = jnp.concatenate([state[n][1] for n, (mi, hi) in enumerate(chains) if mi == 0], axis=1)
    l2 = jnp.concatenate([state[n][1] for n, (mi, hi) in enumerate(chains) if mi == 1], axis=1)
    lam = (jnp.exp(jnp.sum(lam_ref[0:1, :] * lam_ref[1:2, :], axis=-1, keepdims=True))
           - jnp.exp(jnp.sum(lam_ref[2:3, :] * lam_ref[3:4, :], axis=-1, keepdims=True))
           + LAMBDA_INIT)
    ot = acc_ref[0] * (1.0 / l1) - (lam * (1.0 / l2)) * acc_ref[1]
    ot = ot * lax.rsqrt(jnp.mean(ot * ot, axis=0, keepdims=True) + SUBLN_EPS)
    ot = ot * (sg_ref[...] * (1.0 - LAMBDA_INIT))
    o_ref[0] = ot.T.astype(BF16)


def _attention(qt, k, vt, positions, lut, lam_params, subln_g_col):
    B, S, _ = k.shape
    tq = ATT_TQ
    nq = S // tq
    pos_col = positions.reshape(B, S, 1)
    pos_row = positions.reshape(B, 1, S)
    qmin = jnp.min(positions.reshape(B * nq, tq), axis=1)
    kmax = jnp.max(positions.reshape(B * (S // ATT_TK), ATT_TK), axis=1)
    step_ok = jnp.concatenate([positions[:, 1:] - positions[:, :-1] == 1,
                               jnp.ones((B, 1), jnp.bool_)], axis=1).reshape(B, nq, tq)
    inner_ok = jnp.all(step_ok[:, :, :tq - 1], axis=2)
    link_ok = jnp.concatenate([jnp.ones((B, 1), jnp.bool_), step_ok[:, :-1, tq - 1]], axis=1)
    prev_ok = jnp.concatenate([jnp.ones((B, 1), jnp.bool_), inner_ok[:, :-1]], axis=1)
    consec = (inner_ok & link_ok & prev_ok).astype(I32).reshape(B * nq)
    grid_spec = pltpu.PrefetchScalarGridSpec(
        num_scalar_prefetch=3,
        grid=(B, N_DIFF_HEADS, nq),
        in_specs=[pl.BlockSpec((1, 128, tq), lambda b, h, i, *_: (b, h, i)),
                  pl.BlockSpec((1, S, 128), lambda b, h, i, *_: (b, 0, h)),
                  pl.BlockSpec((1, 128, S), lambda b, h, i, *_: (b, h, 0)),
                  pl.BlockSpec((1, 1, tq), lambda b, h, i, *_: (b, 0, i)),
                  pl.BlockSpec((1, S, 1), lambda b, h, i, *_: (b, 0, 0)),
                  pl.BlockSpec((1, 2, BIAS_LUT), lambda b, h, i, *_: (h, 0, 0)),
                  pl.BlockSpec((4, DIFF_HEAD_DIM), lambda b, h, i, *_: (0, 0)),
                  pl.BlockSpec((V_HEAD_DIM, 1), lambda b, h, i, *_: (0, 0))],
        out_specs=pl.BlockSpec((1, tq, 128), lambda b, h, i, *_: (b, i, h)),
        scratch_shapes=[pltpu.VMEM((2, V_HEAD_DIM, tq), F32),
                        pltpu.VMEM((2, 2, ATT_TK, ATT_CHAIN), F32)],
    )
    return pl.pallas_call(
        _attn_kernel,
        grid_spec=grid_spec,
        out_shape=jax.ShapeDtypeStruct((B, S, ATTN_WIDTH), BF16),
        compiler_params=pltpu.CompilerParams(
            dimension_semantics=("arbitrary", "arbitrary", "arbitrary"),
            vmem_limit_bytes=VMEM_LIMIT),
        name="diffattn",
    )(qmin, kmax, consec, qt, k, vt, pos_row, pos_col, lut, lam_params, subln_g_col)


def _outproj_kernel(at_ref, cv_ref, x_ref, g1_ref, sc_ref, sh_ref, gn_ref, wo_ref, wr_ref, rb_ref,
                    x1_ref, hp_ref, ri_ref, rw_ref):
    tm = x_ref.shape[1]
    mix = (jnp.dot(at_ref[0], wo_ref[0:ATTN_WIDTH, :], preferred_element_type=F32)
           + jnp.dot(cv_ref[0], wo_ref[ATTN_WIDTH:, :], preferred_element_type=F32))
    x1 = x_ref[0] + g1_ref[0] * mix
    x1_ref[0] = x1
    ms = jnp.mean(x1 * x1, axis=-1, keepdims=True)
    h = x1 * lax.rsqrt(ms + NORM_EPS) * gn_ref[...]
    h = h * (1.0 + sc_ref[0]) + sh_ref[0]
    hb = h.astype(BF16)

    _store_packed_rows(hp_ref.at[0], h)

    lg_all = lax.dot_general(wr_ref[...], hb, (((1,), (1,)), ((), ())),
                             preferred_element_type=F32) + rb_ref[...]
    lg = lg_all[0:N_GROUPS, :]
    le = lg_all[N_GROUPS:N_GROUPS + N_EXPERTS, :]
    row4 = lax.broadcasted_iota(I32, (N_GROUPS, tm), 0)
    gmax = jnp.max(lg, axis=0, keepdims=True)
    pg_sel = 1.0 / jnp.sum(jnp.exp(lg - gmax), axis=0, keepdims=True)
    gsel = jnp.min(jnp.where(lg == gmax, row4, N_GROUPS), axis=0, keepdims=True)
    sel = jnp.zeros((EXPERTS_PER_GROUP, tm), F32)
    for g in range(N_GROUPS):
        sel = jnp.where(gsel == g, le[g * EXPERTS_PER_GROUP:(g + 1) * EXPERTS_PER_GROUP, :], sel)
    v1 = jnp.max(sel, axis=0, keepdims=True)
    i1 = jnp.min(jnp.where(sel == v1, row4, EXPERTS_PER_GROUP), axis=0, keepdims=True)
    rest = jnp.where(row4 == i1, -jnp.inf, sel)
    v2 = jnp.max(rest, axis=0, keepdims=True)
    i2 = jnp.min(jnp.where(rest == v2, row4, EXPERTS_PER_GROUP), axis=0, keepdims=True)
    e2 = jnp.exp(v2 - v1)
    w1 = pg_sel / (1.0 + e2)
    w2 = pg_sel * e2 / (1.0 + e2)
    row8 = lax.broadcasted_iota(I32, (8, tm), 0)
    eid1 = gsel * EXPERTS_PER_GROUP + i1
    eid2 = gsel * EXPERTS_PER_GROUP + i2
    ri_ref[0] = jnp.where(row8 == 0, eid1, jnp.where(row8 == 1, eid2, 0))
    rw_ref[0] = jnp.where(row8 == 0, w1, jnp.where(row8 == 1, w2, 0.0))


def _outproj(attn, conv, x, g1, sc2, sh2, g2n, w_out_bf, wr_t, rb):
    B, S, D = x.shape
    tm = ROW_TILE
    half_spec = pl.BlockSpec((1, tm, 512), lambda b, j: (b, j, 0))
    full_spec = pl.BlockSpec((1, tm, D), lambda b, j: (b, j, 0))
    mod_spec = pl.BlockSpec((1, 1, D), lambda b, j: (b, 0, 0))
    rt_spec = pl.BlockSpec((1, 8, tm), lambda b, j: (b, 0, j))
    return pl.pallas_call(
        _outproj_kernel,
        grid=(B, S // tm),
        in_specs=[half_spec, half_spec, full_spec, mod_spec, mod_spec, mod_spec,
                  pl.BlockSpec((1, D), lambda b, j: (0, 0)),
                  pl.BlockSpec((D, D), lambda b, j: (0, 0)),
                  pl.BlockSpec((ROUTER_ROWS, D), lambda b, j: (0, 0)),
                  pl.BlockSpec((ROUTER_ROWS, 1), lambda b, j: (0, 0))],
        out_specs=[full_spec, pl.BlockSpec((1, tm * ROW_SLAB, 128), lambda b, j: (b, j, 0)),
                   rt_spec, rt_spec],
        out_shape=[jax.ShapeDtypeStruct((B, S, D), F32),
                   jax.ShapeDtypeStruct((B, S * ROW_SLAB, 128), U32),
                   jax.ShapeDtypeStruct((B, 8, S), I32),
                   jax.ShapeDtypeStruct((B, 8, S), F32)],
        compiler_params=pltpu.CompilerParams(
            dimension_semantics=("arbitrary", "arbitrary"), vmem_limit_bytes=VMEM_LIMIT),
        name="outproj",
    )(attn, conv, x, g1, sc2, sh2, g2n, w_out_bf, wr_t, rb)


def _store_packed_rows(dst, x):
    half = D_MODEL // 2
    xb = x.astype(BF16).astype(F32)
    packed = (pltpu.bitcast(xb[:, :half], U32) >> 16) | (
        pltpu.bitcast(xb[:, half:], U32) & jnp.uint32(0xFFFF0000))
    rows = x.shape[0]
    for c in range(ROW_SLAB):
        dst[pl.ds(c, rows, stride=ROW_SLAB), :] = packed[:, c * 128:(c + 1) * 128]


def _load_packed_rows(src, row0, rows):
    packed = jnp.concatenate(
        [src[pl.ds(row0 * ROW_SLAB + c, rows, stride=ROW_SLAB), :] for c in range(ROW_SLAB)], axis=1)
    lo = pltpu.bitcast(packed << 16, F32)
    hi = pltpu.bitcast(packed & jnp.uint32(0xFFFF0000), F32)
    return lo, hi


def _row_gather_start(src_hbm, idx_ref, dst, sem, n_rows):
    for r in range(n_rows):
        off = pl.multiple_of(idx_ref[0, 0, r], ROW_SLAB)
        pltpu.make_async_copy(src_hbm.at[pl.ds(off, ROW_SLAB)],
                              dst.at[pl.ds(r * ROW_SLAB, ROW_SLAB)], sem).start(priority=r % 2)


def _row_gather_wait(src_hbm, dst, sem, n_rows):
    pltpu.make_async_copy(src_hbm.at[pl.ds(0, n_rows * ROW_SLAB)], dst, sem).wait()


def _dispatch_kernel(fill_ref, pos_ref, hp_ref, xs_hbm, zbuf, sem):
    i = pl.program_id(0)
    tm = COMB_TM
    tile_rows = MOE_TM * ROW_SLAB

    @pl.when(i == 0)
    def _():
        zbuf[...] = jnp.zeros_like(zbuf)

        def fill_copy(t):
            return pltpu.make_async_copy(
                zbuf, xs_hbm.at[pl.ds(pl.multiple_of(t * tile_rows, tile_rows), tile_rows)],
                sem.at[0])

        def start(t, c):
            @pl.when(fill_ref[t] == 1)
            def _():
                fill_copy(t).start()
            return c

        def wait(t, c):
            @pl.when(fill_ref[t] == 1)
            def _():
                fill_copy(t).wait()
            return c

        lax.fori_loop(0, fill_ref.shape[0], start, 0)
        lax.fori_loop(0, fill_ref.shape[0], wait, 0)

    par = lax.rem(i, 2)
    base = pl.multiple_of(i * (tm * ROW_SLAB), tm * ROW_SLAB)
    for r in range(2 * tm):
        off = pl.multiple_of(pos_ref[0, 0, r], ROW_SLAB)
        pltpu.make_async_copy(hp_ref.at[pl.ds(base + (r % tm) * ROW_SLAB, ROW_SLAB)],
                              xs_hbm.at[pl.ds(off, ROW_SLAB)],
                              sem.at[1 + par]).start(priority=r % 2)

    def drain(parity):
        for _ in range(2):
            pltpu.make_async_copy(hp_ref.at[pl.ds(0, tm * ROW_SLAB)],
                                  xs_hbm.at[pl.ds(0, tm * ROW_SLAB)], sem.at[1 + parity]).wait()

    @pl.when(i > 0)
    def _():
        drain(1 - par)

    @pl.when(i == pl.num_programs(0) - 1)
    def _():
        drain(par)


def _dispatch(tile_fill, pos_tiles, hp, n_tiles):
    T = hp.shape[0] // ROW_SLAB
    tm = COMB_TM
    grid_spec = pltpu.PrefetchScalarGridSpec(
        num_scalar_prefetch=1,
        grid=(T // tm,),
        in_specs=[pl.BlockSpec((1, 1, 2 * tm), lambda i, f: (i, 0, 0), memory_space=pltpu.SMEM),
                  pl.BlockSpec(memory_space=pltpu.VMEM)],
        out_specs=pl.BlockSpec(memory_space=pl.ANY),
        scratch_shapes=[pltpu.VMEM((MOE_TM * ROW_SLAB, 128), U32),
                        pltpu.SemaphoreType.DMA((3,))],
    )
    return pl.pallas_call(
        _dispatch_kernel,
        grid_spec=grid_spec,
        out_shape=jax.ShapeDtypeStruct((n_tiles * MOE_TM * ROW_SLAB, 128), U32),
        compiler_params=pltpu.CompilerParams(
            dimension_semantics=("arbitrary",), vmem_limit_bytes=VMEM_LIMIT),
        name="dispatch",
    )(tile_fill, pos_tiles, hp)


def _moe_kernel(te_ref, tv_ref, xs_ref, wg_ref, wu_ref, wd_ref, y_ref, wg_bf, wu_bf, wd_bf):
    i = pl.program_id(0)
    tm = MOE_TM

    @pl.when(jnp.logical_or(i == 0, te_ref[i] != te_ref[jnp.maximum(i - 1, 0)]))
    def _():
        wg_bf[...] = wg_ref[0].astype(BF16)
        wu_bf[...] = wu_ref[0].astype(BF16)
        wd_bf[...] = wd_ref[0].astype(BF16)

    @pl.when(tv_ref[i] == 1)
    def _():
        half = D_MODEL // 2
        lo, hi = _load_packed_rows(xs_ref, 0, tm)
        lo = lo.astype(BF16)
        hi = hi.astype(BF16)
        g = (jnp.dot(lo, wg_bf[0:half, :], preferred_element_type=F32)
             + jnp.dot(hi, wg_bf[half:, :], preferred_element_type=F32))
        u = (jnp.dot(lo, wu_bf[0:half, :], preferred_element_type=F32)
             + jnp.dot(hi, wu_bf[half:, :], preferred_element_type=F32))
        hid = (_silu(g) * u).astype(BF16)
        _store_packed_rows(y_ref, jnp.dot(hid, wd_bf[...], preferred_element_type=F32))

    @pl.when(tv_ref[i] == 0)
    def _():
        y_ref[...] = jnp.zeros_like(y_ref)


def _moe(tile_expert, tile_valid, xs, wg, wu, wd):
    nt = tile_expert.shape[0]
    tm = MOE_TM
    D = D_MODEL
    grid_spec = pltpu.PrefetchScalarGridSpec(
        num_scalar_prefetch=2,
        grid=(nt,),
        in_specs=[pl.BlockSpec((tm * ROW_SLAB, 128), lambda i, te, tv: (i, 0)),
                  pl.BlockSpec((1, D, D_EXPERT), lambda i, te, tv: (te[i], 0, 0)),
                  pl.BlockSpec((1, D, D_EXPERT), lambda i, te, tv: (te[i], 0, 0)),
                  pl.BlockSpec((1, D_EXPERT, D), lambda i, te, tv: (te[i], 0, 0))],
        out_specs=pl.BlockSpec((tm * ROW_SLAB, 128), lambda i, te, tv: (i, 0)),
        scratch_shapes=[pltpu.VMEM((D, D_EXPERT), BF16), pltpu.VMEM((D, D_EXPERT), BF16),
                        pltpu.VMEM((D_EXPERT, D), BF16)],
    )
    return pl.pallas_call(
        _moe_kernel,
        grid_spec=grid_spec,
        out_shape=jax.ShapeDtypeStruct((nt * tm * ROW_SLAB, 128), U32),
        compiler_params=pltpu.CompilerParams(
            dimension_semantics=("arbitrary",), vmem_limit_bytes=VMEM_LIMIT),
        name="moe",
    )(tile_expert, tile_valid, xs, wg, wu, wd)


def _combine_kernel(pos_ref, posn_ref, ys_hbm, x1_ref, g2_ref, w_ref, fg_ref, o_ref, rbuf, sem):
    i = pl.program_id(0)
    n = pl.num_programs(0)
    slot = lax.rem(i, 2)
    nslot = 1 - slot
    tm = COMB_TM

    @pl.when(i == 0)
    def _():
        _row_gather_start(ys_hbm, pos_ref, rbuf.at[0], sem.at[0], 2 * tm)

    @pl.when(i + 1 < n)
    def _():
        _row_gather_start(ys_hbm, posn_ref, rbuf.at[nslot], sem.at[nslot], 2 * tm)

    _row_gather_wait(ys_hbm, rbuf.at[slot], sem.at[slot], 2 * tm)
    w = w_ref[...]
    r1 = jnp.concatenate(_load_packed_rows(rbuf.at[slot], 0, tm), axis=1)
    r2 = jnp.concatenate(_load_packed_rows(rbuf.at[slot], tm, tm), axis=1)
    moe = w[:, 0:1] * r1 + w[:, 1:2] * r2
    y = x1_ref[...] + g2_ref[0] * moe
    ms = jnp.mean(y * y, axis=-1, keepdims=True)
    o_ref[...] = y * lax.rsqrt(ms + NORM_EPS) * fg_ref[...]


def _combine(pos_tiles, ys, x1, g2, w_tok, final_g, seq_len):
    T, D = x1.shape
    tm = COMB_TM
    nt = T // tm
    per_b = seq_len // tm
    return pl.pallas_call(
        _combine_kernel,
        grid=(nt,),
        in_specs=[pl.BlockSpec((1, 1, 2 * tm), lambda i: (i, 0, 0), memory_space=pltpu.SMEM),
                  pl.BlockSpec((1, 1, 2 * tm), lambda i: (jnp.minimum(i + 1, nt - 1), 0, 0),
                               memory_space=pltpu.SMEM),
                  pl.BlockSpec(memory_space=pl.ANY),
                  pl.BlockSpec((tm, D), lambda i: (i, 0)),
                  pl.BlockSpec((1, 1, D), lambda i: (i // per_b, 0, 0)),
                  pl.BlockSpec((tm, 2), lambda i: (i, 0)),
                  pl.BlockSpec((1, D), lambda i: (0, 0))],
        out_specs=pl.BlockSpec((tm, D), lambda i: (i, 0)),
        out_shape=jax.ShapeDtypeStruct((T, D), F32),
        scratch_shapes=[pltpu.VMEM((2, 2 * tm * ROW_SLAB, 128), U32),
                        pltpu.SemaphoreType.DMA((2,))],
        compiler_params=pltpu.CompilerParams(
            dimension_semantics=("arbitrary",), vmem_limit_bytes=VMEM_LIMIT),
        name="combine",
    )(pos_tiles, pos_tiles, ys, x1, g2, w_tok, final_g)


def _rel_bucket_table():
    n = jnp.arange(BIAS_LUT, dtype=I32)
    max_exact = N_BUCKETS // 2
    nf = jnp.maximum(n, 1).astype(F32)
    large = max_exact + (jnp.log(nf / max_exact) / math.log(MAX_DISTANCE / max_exact)
                         * (N_BUCKETS - max_exact)).astype(I32)
    large = jnp.minimum(large, N_BUCKETS - 1)
    return jnp.where(n < max_exact, n, large)


def _route_plan(eid, n_tiles, tm):
    two, T = eid.shape
    e_flat = eid.reshape(-1)
    onehot = (e_flat[:, None] == jnp.arange(N_EXPERTS, dtype=I32)[None, :]).astype(I32)
    csum = jnp.cumsum(onehot, axis=0)
    rank = jnp.sum((csum - onehot) * onehot, axis=1)
    counts = csum[-1]
    ptiles = (counts + tm - 1) // tm
    tend = jnp.cumsum(ptiles)
    tstart = tend - ptiles
    slot = jnp.sum(onehot * tstart[None, :], axis=1) * tm + rank
    total = tend[-1]
    tile_ids = jnp.arange(n_tiles, dtype=I32)
    tile_valid = (tile_ids < total).astype(I32)
    tile_expert = jnp.sum((tile_ids[:, None] >= tend[None, :]).astype(I32), axis=1)
    last_expert = jnp.sum((total - 1 >= tend).astype(I32))
    tile_expert = jnp.minimum(tile_expert, last_expert).astype(I32)
    partial_last = jnp.any((tile_ids[:, None] == tend[None, :] - 1)
                           & (ptiles[None, :] > 0) & (counts[None, :] % tm != 0), axis=1)
    tile_fill = (partial_last | (tile_ids >= total)).astype(I32)
    return slot.astype(I32), tile_expert, tile_valid, tile_fill


def kernel(x, c, positions, rel_bias, ada_w, ada_b, norm1_g, w_in, lambda_q1, lambda_k1, lambda_q2,
           lambda_k2, subln_g, conv_w, w_out, norm2_g, router_group_w, router_group_b,
           router_expert_w, router_expert_b, expert_w_gate, expert_w_up, expert_w_down, final_g):
    B, S, D = x.shape
    T = B * S
    l = 0

    c_pad = jnp.zeros((8, D), F32).at[:B].set(c)
    ada = _ada(c_pad, ada_w[l], ada_b[l].reshape(1, -1))[:B]
    sh1, sc1, g1, sh2, sc2, g2 = [a.reshape(B, 1, D) for a in jnp.split(ada, 6, axis=-1)]

    w_in_bf = w_in[l].astype(BF16)
    A = ATTN_WIDTH
    qt, k, vt, conv = _inproj(x, sc1, sh1, norm1_g[l].reshape(1, D), w_in_bf[:, 0:A].T,
                              w_in_bf[:, A:2 * A], w_in_bf[:, 2 * A:3 * A].T, w_in_bf[:, 3 * A:],
                              conv_w[l])
    lut = (rel_bias.astype(F32)[_rel_bucket_table(), :].T * LOG2E).reshape(N_DIFF_HEADS, 2, BIAS_LUT)
    lam_params = jnp.stack([lambda_q1[l], lambda_k1[l], lambda_q2[l], lambda_k2[l]]).astype(F32)
    attn = _attention(qt, k, vt, positions, lut, lam_params, subln_g[l].reshape(V_HEAD_DIM, 1))

    wr_t = jnp.zeros((ROUTER_ROWS, D), F32)
    wr_t = wr_t.at[0:N_GROUPS].set(router_group_w[l].T)
    wr_t = wr_t.at[N_GROUPS:N_GROUPS + N_EXPERTS].set(router_expert_w[l].T).astype(BF16)
    rb = jnp.zeros((ROUTER_ROWS, 1), F32)
    rb = rb.at[0:N_GROUPS, 0].set(router_group_b[l])
    rb = rb.at[N_GROUPS:N_GROUPS + N_EXPERTS, 0].set(router_expert_b[l])
    x1, hp, ri, rw = _outproj(attn, conv, x, g1, sc2, sh2, norm2_g[l].reshape(1, D),
                              w_out[l].astype(BF16), wr_t, rb)

    eid = ri[:, 0:2, :].transpose(1, 0, 2).reshape(2, T)
    w_tok = rw[:, 0:2, :].transpose(0, 2, 1).reshape(T, 2)
    n_tiles = 2 * T // MOE_TM + N_EXPERTS
    slot, tile_expert, tile_valid, tile_fill = _route_plan(eid, n_tiles, MOE_TM)
    nct = T // COMB_TM
    pos = (slot * ROW_SLAB).reshape(2, nct, 1, COMB_TM)
    pos_tiles = jnp.concatenate([pos[0], pos[1]], axis=2)

    xs = _dispatch(tile_fill, pos_tiles, hp.reshape(T * ROW_SLAB, 128), n_tiles)
    ys = _moe(tile_expert, tile_valid, xs, expert_w_gate[l], expert_w_up[l], expert_w_down[l])

    out = _combine(pos_tiles, ys, x1.reshape(T, D), g2, w_tok, final_g.reshape(1, D), S)
    return out.reshape(B, S, D)
```

```python
import functools
import math

import jax
import jax.numpy as jnp
from jax import lax
from jax.experimental import pallas as pl
from jax.experimental.pallas import tpu as pltpu

F32 = jnp.float32
BF16 = jnp.bfloat16
I32 = jnp.int32
U32 = jnp.uint32

D_MODEL = 1024
ATTN_WIDTH = 512
CONV_WIDTH = 512
N_DIFF_HEADS = 4
DIFF_HEAD_DIM = 64
V_HEAD_DIM = 128
IN_PROJ_WIDTH = 3 * ATTN_WIDTH + 3 * CONV_WIDTH
CONV_K = 3
N_BUCKETS = 32
MAX_DISTANCE = 128
N_GROUPS = 4
EXPERTS_PER_GROUP = 4
N_EXPERTS = 16
D_EXPERT = 512
NORM_EPS = 1e-6
SUBLN_EPS = 1e-5
NEG_INF = -1e30
LAMBDA_INIT = 0.8 - 0.6 * math.exp(-0.3 * 0)
QK_SCALE = DIFF_HEAD_DIM ** -0.5
LOG2E = math.log2(math.e)

BIAS_LUT = 128

ROW_TILE = 512
PROJ_SUBTILES = 2
ATT_TQ = 1024
ATT_CHAIN = 256
ATT_GROUP = 8
ATT_LOOKAHEAD = 8
ATT_TK = 256
MOE_TM = 256
COMB_TM = 256
ROUTER_ROWS = 32
ROW_SLAB = D_MODEL // 256
VMEM_LIMIT = 56 * 1024 * 1024


def _silu(x):
    return x * (1.0 / (1.0 + jnp.exp(-x)))


def _ada_kernel(c_ref, w_ref, b_ref, o_ref):
    s = _silu(c_ref[...])
    o_ref[...] = jnp.dot(s, w_ref[...], preferred_element_type=F32,
                         precision=lax.Precision.HIGHEST) + b_ref[...]


def _ada(c_pad, w, b):
    n = w.shape[1]
    bn = 1024
    return pl.pallas_call(
        _ada_kernel,
        grid=(n // bn,),
        in_specs=[pl.BlockSpec((8, D_MODEL), lambda j: (0, 0)),
                  pl.BlockSpec((D_MODEL, bn), lambda j: (0, j)),
                  pl.BlockSpec((1, bn), lambda j: (0, j))],
        out_specs=pl.BlockSpec((8, bn), lambda j: (0, j)),
        out_shape=jax.ShapeDtypeStruct((8, n), F32),
        name="ada",
    )(c_pad, w, b)


def _inproj_kernel(x_ref, sc_ref, sh_ref, g_ref, wqt_ref, wk_ref, wvt_ref, wc_ref, cw_ref,
                   qt_ref, k_ref, vt_ref, conv_ref, carry_ref):
    j = pl.program_id(1)
    tm = x_ref.shape[1]
    nt = (((1,), (1,)), ((), ()))

    @pl.when(j == 0)
    def _():
        carry_ref[...] = jnp.zeros_like(carry_ref)

    prev = carry_ref[...]
    sub = tm // PROJ_SUBTILES
    for s in range(PROJ_SUBTILES):
        rows = slice(s * sub, (s + 1) * sub)
        x = x_ref[0, rows, :]
        ms = jnp.mean(x * x, axis=-1, keepdims=True)
        h = x * lax.rsqrt(ms + NORM_EPS) * g_ref[...]
        h = h * (1.0 + sc_ref[0]) + sh_ref[0]
        hb = h.astype(BF16)

        def proj(c0):
            return jnp.dot(hb, wc_ref[:, c0:c0 + 512], preferred_element_type=F32)

        qt = lax.dot_general(wqt_ref[...], hb, nt, preferred_element_type=F32)
        qt_ref[0, :, rows] = (qt * (QK_SCALE * LOG2E)).astype(BF16)
        k_ref[0, rows, :] = jnp.dot(hb, wk_ref[...], preferred_element_type=F32).astype(BF16)
        vt_ref[0, :, rows] = lax.dot_general(wvt_ref[...], hb, nt,
                                             preferred_element_type=F32).astype(BF16)
        gate_b = proj(0)
        u = proj(512) * proj(1024)
        row = lax.broadcasted_iota(I32, u.shape, 0)
        u1 = pltpu.roll(u, 1, axis=0)
        u2 = pltpu.roll(u, 2, axis=0)
        u1 = jnp.where(row == 0, prev[7:8, :], u1)
        u2 = jnp.where(row == 0, prev[6:7, :], jnp.where(row == 1, prev[7:8, :], u2))
        conv = cw_ref[0:1, :] * u2 + cw_ref[1:2, :] * u1 + cw_ref[2:3, :] * u
        conv_ref[0, rows, :] = (gate_b * conv).astype(BF16)
        prev = u[sub - 8:sub, :]
    carry_ref[...] = prev


def _inproj(x, sc1, sh1, g1n, wq_t, wk, wv_t, wc, conv_w):
    B, S, D = x.shape
    tm = ROW_TILE
    row_out = jax.ShapeDtypeStruct((B, S, 512), BF16)
    col_out = jax.ShapeDtypeStruct((B, 512, S), BF16)
    row_spec = pl.BlockSpec((1, tm, 512), lambda b, j: (b, j, 0))
    col_spec = pl.BlockSpec((1, 512, tm), lambda b, j: (b, 0, j))
    mod_spec = pl.BlockSpec((1, 1, D), lambda b, j: (b, 0, 0))
    const2 = lambda b, j: (0, 0)
    return pl.pallas_call(
        _inproj_kernel,
        grid=(B, S // tm),
        in_specs=[pl.BlockSpec((1, tm, D), lambda b, j: (b, j, 0)),
                  mod_spec, mod_spec,
                  pl.BlockSpec((1, D), const2),
                  pl.BlockSpec((ATTN_WIDTH, D), const2),
                  pl.BlockSpec((D, ATTN_WIDTH), const2),
                  pl.BlockSpec((ATTN_WIDTH, D), const2),
                  pl.BlockSpec((D, 3 * CONV_WIDTH), const2),
                  pl.BlockSpec((CONV_K, CONV_WIDTH), const2)],
        out_specs=[col_spec, row_spec, col_spec, row_spec],
        out_shape=[col_out, row_out, col_out, row_out],
        scratch_shapes=[pltpu.VMEM((8, CONV_WIDTH), F32)],
        compiler_params=pltpu.CompilerParams(
            dimension_semantics=("arbitrary", "arbitrary"), vmem_limit_bytes=VMEM_LIMIT),
        name="inproj",
    )(x, sc1, sh1, g1n, wq_t, wk, wv_t, wc, conv_w)


def _attn_kernel(qmin_ref, kmax_ref, consec_ref, qt_ref, k_ref, vt_ref, pr_ref, pc_ref, lut_ref,
                 lam_ref, sg_ref, o_ref, acc_ref, tz_ref):
    b = pl.program_id(0)
    qi = pl.program_id(2)
    nq = pl.num_programs(2)
    tq, tk = ATT_TQ, ATT_TK
    hw = ATT_CHAIN
    n_half = tq // hw
    nk = nq * (tq // tk)
    qt = qt_ref[0]
    feat = lax.broadcasted_iota(I32, qt.shape, 0)
    zero = jnp.zeros_like(qt)
    qts = (jnp.where(feat < DIFF_HEAD_DIM, qt, zero), jnp.where(feat >= DIFF_HEAD_DIM, qt, zero))
    luts = (lut_ref[0, 0:1, :], lut_ref[0, 1:2, :])
    fars = tuple(t[:, BIAS_LUT - 1:BIAS_LUT] for t in luts)
    pq = pr_ref[0]
    qmin = qmin_ref[b * nq + qi]
    czero = jnp.zeros((1, 1), F32)
    ones_rows = jnp.ones((16, tk), BF16)
    chains = [(mi, hi) for mi in range(2) for hi in range(n_half)]

    def gather_bias(mi, dist):
        table = jnp.broadcast_to(luts[mi], (tk, BIAS_LUT))
        return jnp.concatenate([jnp.take_along_axis(table, dist[:, o:o + 128], axis=1)
                                for o in range(0, hw, 128)], axis=1)

    @pl.when(qi == 0)
    def _():
        delta = (lax.broadcasted_iota(I32, (tk, hw), 1) - lax.broadcasted_iota(I32, (tk, hw), 0))
        for mi in range(2):
            diag_bias = gather_bias(mi, jnp.clip(delta, 0, BIAS_LUT - 1))
            tz_ref[mi, 0] = jnp.where(delta >= 0, diag_bias, NEG_INF)
            tz_ref[mi, 1] = gather_bias(mi, jnp.clip(delta + tk, 0, BIAS_LUT - 1))

    def run_blocks(blocks, state):
        loaded = []
        for (j, kinds) in blocks:
            ks = pl.multiple_of(j * tk, tk)
            kb = k_ref[0, pl.ds(ks, tk), :]
            vtb = jnp.concatenate([vt_ref[0, :, pl.ds(ks, tk)], ones_rows], axis=0)
            dist = None
            if any(kd is not None and kd.startswith("near") for kd in kinds):
                pk = pc_ref[0, pl.ds(ks, tk), :]
                dist = jnp.clip(pq - pk, 0, BIAS_LUT - 1)
            loaded.append((kb, vtb, dist))
        items = [(bi, n) for bi, blk in enumerate(blocks) for n, (mi, hi) in enumerate(chains)
                 if blk[1][hi] is not None]
        scores = {}

        def issue_qk(t):
            bi, n = items[t]
            mi, hi = chains[n]
            scores[t] = jnp.dot(loaded[bi][0], qts[mi][:, hi * hw:(hi + 1) * hw],
                                preferred_element_type=F32)

        state = list(state)
        for t in range(min(ATT_LOOKAHEAD, len(items))):
            issue_qk(t)
        for t, (bi, n) in enumerate(items):
            if t + ATT_LOOKAHEAD < len(items):
                issue_qk(t + ATT_LOOKAHEAD)
            _, vtb, dist = loaded[bi]
            mi, hi = chains[n]
            kind = blocks[bi][1][hi]
            cols = slice(hi * hw, (hi + 1) * hw)
            m, l = state[n]
            s = scores.pop(t)
            c = czero
            if kind == "far":
                c = fars[mi]
            elif kind == "tz_diag":
                s = tz_ref[mi, 0] + s
            elif kind == "tz_sub":
                s = tz_ref[mi, 1] + s
            else:
                s = gather_bias(mi, dist[:, cols]) + s
                if kind == "near_masked":
                    keep = (lax.broadcasted_iota(I32, (tk, hw), 0)
                            <= lax.broadcasted_iota(I32, (tk, hw), 1))
                    s = jnp.where(keep, s, NEG_INF)
            mn = jnp.maximum(m, jnp.max(s, axis=0, keepdims=True) + c)
            alpha = jnp.exp2(m - mn)
            p = jnp.exp2(s - (mn - c))
            pv = jnp.dot(vtb, p.astype(BF16), preferred_element_type=F32)
            l = alpha * l + pv[V_HEAD_DIM:V_HEAD_DIM + 1, :]
            acc_ref[mi, :, cols] = alpha * acc_ref[mi, :, cols] + pv[:V_HEAD_DIM, :]
            state[n] = (mn, l)
        return tuple(state)

    def block_is_far(j):
        return qmin - kmax_ref[b * nk + j] >= BIAS_LUT - 1

    def one_block(j, state):
        return lax.cond(block_is_far(j), lambda st: run_blocks([(j, ("far",) * n_half)], st),
                        lambda st: run_blocks([(j, ("near",) * n_half)], st), state)

    def group_body(width):
        def body(g, carry):
            j0, state = carry
            all_far = block_is_far(j0)
            for u in range(1, width):
                all_far = jnp.logical_and(all_far, block_is_far(j0 + u))
            far_blocks = [(j0 + u, ("far",) * n_half) for u in range(width)]
            state = lax.cond(
                all_far, lambda st: run_blocks(far_blocks, st),
                lambda st: lax.fori_loop(0, width, lambda u, s2: one_block(j0 + u, s2), st), state)
            return j0 + width, state
        return body

    acc_ref[...] = jnp.zeros_like(acc_ref)
    m0 = jnp.full((1, hw), NEG_INF, F32)
    l0 = jnp.zeros((1, hw), F32)
    state = tuple((m0, l0) for _ in chains)
    assert hw == tk and n_half % 2 == 0
    n_full = n_half * qi
    n_main = jnp.maximum(n_full - 2, 0)
    j0 = jnp.int32(0)
    width = ATT_GROUP
    j0, state = lax.fori_loop(0, n_main // width, group_body(width), (j0, state))
    rem = n_main % width
    while width > 2:
        width //= 2
        j0, state = lax.fori_loop(0, (rem // width) % 2, group_body(width), (j0, state))

    consec = consec_ref[b * nq + qi] == 1
    def diag_kinds(d, on_diag, below, further):
        return tuple(None if hi < d else on_diag if hi == d else below if hi == d + 1 else further
                     for hi in range(n_half))

    diag_fast = [(n_full + d, diag_kinds(d, "tz_diag", "tz_sub", "far")) for d in range(n_half)]
    diag_any = [(n_full + d, diag_kinds(d, "near_masked", "near", "near")) for d in range(n_half)]
    below_fast = [(n_full - 2, ("far",) * n_half),
                  (n_full - 1, ("tz_sub",) + ("far",) * (n_half - 1))]
    below_any = [(n_full - 2, ("near",) * n_half), (n_full - 1, ("near",) * n_half)]

    def tail(below, diag):
        return lambda st: lax.cond(qi > 0, lambda s2: run_blocks(below + diag, s2),
                                   lambda s2: run_blocks(diag, s2), st)

    state = lax.cond(consec, tail(below_fast, diag_fast), tail(below_any, diag_any), state)

    l1 = jnp.concatenate([state[n][1] for n, (mi, hi) in enumerate(chains) if mi == 0], axis=1)
    l2 = jnp.concatenate([state[n][1] for n, (mi, hi) in enumerate(chains) if mi == 1], axis=1)
    lam = (jnp.exp(jnp.sum(lam_ref[0:1, :] * lam_ref[1:2, :], axis=-1, keepdims=True))
           - jnp.exp(jnp.sum(lam_ref[2:3, :] * lam_ref[3:4, :], axis=-1, keepdims=True))
           + LAMBDA_INIT)
    ot = acc_ref[0] * (1.0 / l1) - (lam * (1.0 / l2)) * acc_ref[1]
    ot = ot * lax.rsqrt(jnp.mean(ot * ot, axis=0, keepdims=True) + SUBLN_EPS)
    ot = ot * (sg_ref[...] * (1.0 - LAMBDA_INIT))
    o_ref[0] = ot.T.astype(BF16)


def _attention(qt, k, vt, positions, lut, lam_params, subln_g_col):
    B, S, _ = k.shape
    tq = ATT_TQ
    nq = S // tq
    pos_col = positions.reshape(B, S, 1)
    pos_row = positions.reshape(B, 1, S)
    qmin = jnp.min(positions.reshape(B * nq, tq), axis=1)
    kmax = jnp.max(positions.reshape(B * (S // ATT_TK), ATT_TK), axis=1)
    step_ok = jnp.concatenate([positions[:, 1:] - positions[:, :-1] == 1,
                               jnp.ones((B, 1), jnp.bool_)], axis=1).reshape(B, nq, tq)
    inner_ok = jnp.all(step_ok[:, :, :tq - 1], axis=2)
    link_ok = jnp.concatenate([jnp.ones((B, 1), jnp.bool_), step_ok[:, :-1, tq - 1]], axis=1)
    prev_ok = jnp.concatenate([jnp.ones((B, 1), jnp.bool_), inner_ok[:, :-1]], axis=1)
    consec = (inner_ok & link_ok & prev_ok).astype(I32).reshape(B * nq)
    grid_spec = pltpu.PrefetchScalarGridSpec(
        num_scalar_prefetch=3,
        grid=(B, N_DIFF_HEADS, nq),
        in_specs=[pl.BlockSpec((1, 128, tq), lambda b, h, i, *_: (b, h, i)),
                  pl.BlockSpec((1, S, 128), lambda b, h, i, *_: (b, 0, h)),
                  pl.BlockSpec((1, 128, S), lambda b, h, i, *_: (b, h, 0)),
                  pl.BlockSpec((1, 1, tq), lambda b, h, i, *_: (b, 0, i)),
                  pl.BlockSpec((1, S, 1), lambda b, h, i, *_: (b, 0, 0)),
                  pl.BlockSpec((1, 2, BIAS_LUT), lambda b, h, i, *_: (h, 0, 0)),
                  pl.BlockSpec((4, DIFF_HEAD_DIM), lambda b, h, i, *_: (0, 0)),
                  pl.BlockSpec((V_HEAD_DIM, 1), lambda b, h, i, *_: (0, 0))],
        out_specs=pl.BlockSpec((1, tq, 128), lambda b, h, i, *_: (b, i, h)),
        scratch_shapes=[pltpu.VMEM((2, V_HEAD_DIM, tq), F32),
                        pltpu.VMEM((2, 2, ATT_TK, ATT_CHAIN), F32)],
    )
    return pl.pallas_call(
        _attn_kernel,
        grid_spec=grid_spec,
        out_shape=jax.ShapeDtypeStruct((B, S, ATTN_WIDTH), BF16),
        compiler_params=pltpu.CompilerParams(
            dimension_semantics=("arbitrary", "arbitrary", "arbitrary"),
            vmem_limit_bytes=VMEM_LIMIT),
        name="diffattn",
    )(qmin, kmax, consec, qt, k, vt, pos_row, pos_col, lut, lam_params, subln_g_col)


def _outproj_kernel(at_ref, cv_ref, x_ref, g1_ref, sc_ref, sh_ref, gn_ref, wo_ref, wr_ref, rb_ref,
                    x1_ref, hp_ref, ri_ref, rw_ref):
    tm = x_ref.shape[1]
    mix = (jnp.dot(at_ref[0], wo_ref[0:ATTN_WIDTH, :], preferred_element_type=F32)
           + jnp.dot(cv_ref[0], wo_ref[ATTN_WIDTH:, :], preferred_element_type=F32))
    x1 = x_ref[0] + g1_ref[0] * mix
    x1_ref[0] = x1
    ms = jnp.mean(x1 * x1, axis=-1, keepdims=True)
    h = x1 * lax.rsqrt(ms + NORM_EPS) * gn_ref[...]
    h = h * (1.0 + sc_ref[0]) + sh_ref[0]
    hb = h.astype(BF16)

    _store_packed_rows(hp_ref.at[0], h)

    lg_all = lax.dot_general(wr_ref[...], hb, (((1,), (1,)), ((), ())),
                             preferred_element_type=F32) + rb_ref[...]
    lg = lg_all[0:N_GROUPS, :]
    le = lg_all[N_GROUPS:N_GROUPS + N_EXPERTS, :]
    row4 = lax.broadcasted_iota(I32, (N_GROUPS, tm), 0)
    gmax = jnp.max(lg, axis=0, keepdims=True)
    pg_sel = 1.0 / jnp.sum(jnp.exp(lg - gmax), axis=0, keepdims=True)
    gsel = jnp.min(jnp.where(lg == gmax, row4, N_GROUPS), axis=0, keepdims=True)
    sel = jnp.zeros((EXPERTS_PER_GROUP, tm), F32)
    for g in range(N_GROUPS):
        sel = jnp.where(gsel == g, le[g * EXPERTS_PER_GROUP:(g + 1) * EXPERTS_PER_GROUP, :], sel)
    v1 = jnp.max(sel, axis=0, keepdims=True)
    i1 = jnp.min(jnp.where(sel == v1, row4, EXPERTS_PER_GROUP), axis=0, keepdims=True)
    rest = jnp.where(row4 == i1, -jnp.inf, sel)
    v2 = jnp.max(rest, axis=0, keepdims=True)
    i2 = jnp.min(jnp.where(rest == v2, row4, EXPERTS_PER_GROUP), axis=0, keepdims=True)
    e2 = jnp.exp(v2 - v1)
    w1 = pg_sel / (1.0 + e2)
    w2 = pg_sel * e2 / (1.0 + e2)
    row8 = lax.broadcasted_iota(I32, (8, tm), 0)
    eid1 = gsel * EXPERTS_PER_GROUP + i1
    eid2 = gsel * EXPERTS_PER_GROUP + i2
    ri_ref[0] = jnp.where(row8 == 0, eid1, jnp.where(row8 == 1, eid2, 0))
    rw_ref[0] = jnp.where(row8 == 0, w1, jnp.where(row8 == 1, w2, 0.0))


def _outproj(attn, conv, x, g1, sc2, sh2, g2n, w_out_bf, wr_t, rb):
    B, S, D = x.shape
    tm = ROW_TILE
    half_spec = pl.BlockSpec((1, tm, 512), lambda b, j: (b, j, 0))
    full_spec = pl.BlockSpec((1, tm, D), lambda b, j: (b, j, 0))
    mod_spec = pl.BlockSpec((1, 1, D), lambda b, j: (b, 0, 0))
    rt_spec = pl.BlockSpec((1, 8, tm), lambda b, j: (b, 0, j))
    return pl.pallas_call(
        _outproj_kernel,
        grid=(B, S // tm),
        in_specs=[half_spec, half_spec, full_spec, mod_spec, mod_spec, mod_spec,
                  pl.BlockSpec((1, D), lambda b, j: (0, 0)),
                  pl.BlockSpec((D, D), lambda b, j: (0, 0)),
                  pl.BlockSpec((ROUTER_ROWS, D), lambda b, j: (0, 0)),
                  pl.BlockSpec((ROUTER_ROWS, 1), lambda b, j: (0, 0))],
        out_specs=[full_spec, pl.BlockSpec((1, tm * ROW_SLAB, 128), lambda b, j: (b, j, 0)),
                   rt_spec, rt_spec],
        out_shape=[jax.ShapeDtypeStruct((B, S, D), F32),
                   jax.ShapeDtypeStruct((B, S * ROW_SLAB, 128), U32),
                   jax.ShapeDtypeStruct((B, 8, S), I32),
                   jax.ShapeDtypeStruct((B, 8, S), F32)],
        compiler_params=pltpu.CompilerParams(
            dimension_semantics=("arbitrary", "arbitrary"), vmem_limit_bytes=VMEM_LIMIT),
        name="outproj",
    )(attn, conv, x, g1, sc2, sh2, g2n, w_out_bf, wr_t, rb)


def _store_packed_rows(dst, x):
    half = D_MODEL // 2
    xb = x.astype(BF16).astype(F32)
    packed = (pltpu.bitcast(xb[:, :half], U32) >> 16) | (
        pltpu.bitcast(xb[:, half:], U32) & jnp.uint32(0xFFFF0000))
    rows = x.shape[0]
    for c in range(ROW_SLAB):
        dst[pl.ds(c, rows, stride=ROW_SLAB), :] = packed[:, c * 128:(c + 1) * 128]


def _load_packed_rows(src, row0, rows):
    packed = jnp.concatenate(
        [src[pl.ds(row0 * ROW_SLAB + c, rows, stride=ROW_SLAB), :] for c in range(ROW_SLAB)], axis=1)
    lo = pltpu.bitcast(packed << 16, F32)
    hi = pltpu.bitcast(packed & jnp.uint32(0xFFFF0000), F32)
    return lo, hi


def _row_gather_start(src_hbm, idx_ref, dst, sem, n_rows):
    for r in range(n_rows):
        off = pl.multiple_of(idx_ref[0, 0, r], ROW_SLAB)
        pltpu.make_async_copy(src_hbm.at[pl.ds(off, ROW_SLAB)],
                              dst.at[pl.ds(r * ROW_SLAB, ROW_SLAB)], sem).start(priority=r % 2)


def _row_gather_wait(src_hbm, dst, sem, n_rows):
    pltpu.make_async_copy(src_hbm.at[pl.ds(0, n_rows * ROW_SLAB)], dst, sem).wait()


def _dispatch_kernel(fill_ref, pos_ref, hp_ref, xs_hbm, zbuf, sem):
    i = pl.program_id(0)
    tm = COMB_TM
    tile_rows = MOE_TM * ROW_SLAB

    @pl.when(i == 0)
    def _():
        zbuf[...] = jnp.zeros_like(zbuf)

        def fill_copy(t):
            return pltpu.make_async_copy(
                zbuf, xs_hbm.at[pl.ds(pl.multiple_of(t * tile_rows, tile_rows), tile_rows)],
                sem.at[0])

        def start(t, c):
            @pl.when(fill_ref[t] == 1)
            def _():
                fill_copy(t).start()
            return c

        def wait(t, c):
            @pl.when(fill_ref[t] == 1)
            def _():
                fill_copy(t).wait()
            return c

        lax.fori_loop(0, fill_ref.shape[0], start, 0)
        lax.fori_loop(0, fill_ref.shape[0], wait, 0)

    par = lax.rem(i, 2)
    base = pl.multiple_of(i * (tm * ROW_SLAB), tm * ROW_SLAB)
    for r in range(2 * tm):
        off = pl.multiple_of(pos_ref[0, 0, r], ROW_SLAB)
        pltpu.make_async_copy(hp_ref.at[pl.ds(base + (r % tm) * ROW_SLAB, ROW_SLAB)],
                              xs_hbm.at[pl.ds(off, ROW_SLAB)],
                              sem.at[1 + par]).start(priority=r % 2)

    def drain(parity):
        for _ in range(2):
            pltpu.make_async_copy(hp_ref.at[pl.ds(0, tm * ROW_SLAB)],
                                  xs_hbm.at[pl.ds(0, tm * ROW_SLAB)], sem.at[1 + parity]).wait()

    @pl.when(i > 0)
    def _():
        drain(1 - par)

    @pl.when(i == pl.num_programs(0) - 1)
    def _():
        drain(par)


def _dispatch(tile_fill, pos_tiles, hp, n_tiles):
    T = hp.shape[0] // ROW_SLAB
    tm = COMB_TM
    grid_spec = pltpu.PrefetchScalarGridSpec(
        num_scalar_prefetch=1,
        grid=(T // tm,),
        in_specs=[pl.BlockSpec((1, 1, 2 * tm), lambda i, f: (i, 0, 0), memory_space=pltpu.SMEM),
                  pl.BlockSpec(memory_space=pltpu.VMEM)],
        out_specs=pl.BlockSpec(memory_space=pl.ANY),
        scratch_shapes=[pltpu.VMEM((MOE_TM * ROW_SLAB, 128), U32),
                        pltpu.SemaphoreType.DMA((3,))],
    )
    return pl.pallas_call(
        _dispatch_kernel,
        grid_spec=grid_spec,
        out_shape=jax.ShapeDtypeStruct((n_tiles * MOE_TM * ROW_SLAB, 128), U32),
        compiler_params=pltpu.CompilerParams(
            dimension_semantics=("arbitrary",), vmem_limit_bytes=VMEM_LIMIT),
        name="dispatch",
    )(tile_fill, pos_tiles, hp)


def _moe_kernel(te_ref, tv_ref, ne_ref, xs_ref, wg_hbm, wu_hbm, wd_hbm, y_ref,
                wg_bf, wu_bf, wd_bf, wg_st, wu_st, wd_st, sem):
    i = pl.program_id(0)
    tm = MOE_TM

    def weight_copies(e):
        return (pltpu.make_async_copy(wg_hbm.at[e], wg_st, sem.at[0]),
                pltpu.make_async_copy(wu_hbm.at[e], wu_st, sem.at[1]),
                pltpu.make_async_copy(wd_hbm.at[e], wd_st, sem.at[2]))

    @pl.when(i == 0)
    def _():
        for cp in weight_copies(te_ref[0]):
            cp.start()

    @pl.when(jnp.logical_or(i == 0, te_ref[i] != te_ref[jnp.maximum(i - 1, 0)]))
    def _():
        for cp in weight_copies(te_ref[i]):
            cp.wait()
        wg_bf[...] = wg_st[...].astype(BF16)
        wu_bf[...] = wu_st[...].astype(BF16)
        wd_bf[...] = wd_st[...].astype(BF16)

        @pl.when(ne_ref[i] >= 0)
        def _():
            for cp in weight_copies(ne_ref[i]):
                cp.start()

    @pl.when(tv_ref[i] == 1)
    def _():
        half = D_MODEL // 2
        lo, hi = _load_packed_rows(xs_ref, 0, tm)
        lo = lo.astype(BF16)
        hi = hi.astype(BF16)
        g = (jnp.dot(lo, wg_bf[0:half, :], preferred_element_type=F32)
             + jnp.dot(hi, wg_bf[half:, :], preferred_element_type=F32))
        u = (jnp.dot(lo, wu_bf[0:half, :], preferred_element_type=F32)
             + jnp.dot(hi, wu_bf[half:, :], preferred_element_type=F32))
        hid = (_silu(g) * u).astype(BF16)
        _store_packed_rows(y_ref, jnp.dot(hid, wd_bf[...], preferred_element_type=F32))

    @pl.when(tv_ref[i] == 0)
    def _():
        y_ref[...] = jnp.zeros_like(y_ref)


def _moe(tile_expert, tile_valid, tile_next_expert, xs, wg, wu, wd):
    nt = tile_expert.shape[0]
    tm = MOE_TM
    D = D_MODEL
    any_spec = pl.BlockSpec(memory_space=pl.ANY)
    grid_spec = pltpu.PrefetchScalarGridSpec(
        num_scalar_prefetch=3,
        grid=(nt,),
        in_specs=[pl.BlockSpec((tm * ROW_SLAB, 128), lambda i, *_: (i, 0)),
                  any_spec, any_spec, any_spec],
        out_specs=pl.BlockSpec((tm * ROW_SLAB, 128), lambda i, *_: (i, 0)),
        scratch_shapes=[pltpu.VMEM((D, D_EXPERT), BF16), pltpu.VMEM((D, D_EXPERT), BF16),
                        pltpu.VMEM((D_EXPERT, D), BF16),
                        pltpu.VMEM((D, D_EXPERT), F32), pltpu.VMEM((D, D_EXPERT), F32),
                        pltpu.VMEM((D_EXPERT, D), F32),
                        pltpu.SemaphoreType.DMA((3,))],
    )
    return pl.pallas_call(
        _moe_kernel,
        grid_spec=grid_spec,
        out_shape=jax.ShapeDtypeStruct((nt * tm * ROW_SLAB, 128), U32),
        compiler_params=pltpu.CompilerParams(
            dimension_semantics=("arbitrary",), vmem_limit_bytes=VMEM_LIMIT),
        name="moe",
    )(tile_expert, tile_valid, tile_next_expert, xs, wg, wu, wd)


def _combine_kernel(pos_ref, posn_ref, ys_hbm, x1_ref, g2_ref, w_ref, fg_ref, o_ref, rbuf, sem):
    i = pl.program_id(0)
    n = pl.num_programs(0)
    slot = lax.rem(i, 2)
    nslot = 1 - slot
    tm = COMB_TM

    @pl.when(i == 0)
    def _():
        _row_gather_start(ys_hbm, pos_ref, rbuf.at[0], sem.at[0], 2 * tm)

    @pl.when(i + 1 < n)
    def _():
        _row_gather_start(ys_hbm, posn_ref, rbuf.at[nslot], sem.at[nslot], 2 * tm)

    _row_gather_wait(ys_hbm, rbuf.at[slot], sem.at[slot], 2 * tm)
    w = w_ref[...]
    r1 = jnp.concatenate(_load_packed_rows(rbuf.at[slot], 0, tm), axis=1)
    r2 = jnp.concatenate(_load_packed_rows(rbuf.at[slot], tm, tm), axis=1)
    moe = w[:, 0:1] * r1 + w[:, 1:2] * r2
    y = x1_ref[...] + g2_ref[0] * moe
    ms = jnp.mean(y * y, axis=-1, keepdims=True)
    o_ref[...] = y * lax.rsqrt(ms + NORM_EPS) * fg_ref[...]


def _combine(pos_tiles, ys, x1, g2, w_tok, final_g, seq_len):
    T, D = x1.shape
    tm = COMB_TM
    nt = T // tm
    per_b = seq_len // tm
    return pl.pallas_call(
        _combine_kernel,
        grid=(nt,),
        in_specs=[pl.BlockSpec((1, 1, 2 * tm), lambda i: (i, 0, 0), memory_space=pltpu.SMEM),
                  pl.BlockSpec((1, 1, 2 * tm), lambda i: (jnp.minimum(i + 1, nt - 1), 0, 0),
                               memory_space=pltpu.SMEM),
                  pl.BlockSpec(memory_space=pl.ANY),
                  pl.BlockSpec((tm, D), lambda i: (i, 0)),
                  pl.BlockSpec((1, 1, D), lambda i: (i // per_b, 0, 0)),
                  pl.BlockSpec((tm, 2), lambda i: (i, 0)),
                  pl.BlockSpec((1, D), lambda i: (0, 0))],
        out_specs=pl.BlockSpec((tm, D), lambda i: (i, 0)),
        out_shape=jax.ShapeDtypeStruct((T, D), F32),
        scratch_shapes=[pltpu.VMEM((2, 2 * tm * ROW_SLAB, 128), U32),
                        pltpu.SemaphoreType.DMA((2,))],
        compiler_params=pltpu.CompilerParams(
            dimension_semantics=("arbitrary",), vmem_limit_bytes=VMEM_LIMIT),
        name="combine",
    )(pos_tiles, pos_tiles, ys, x1, g2, w_tok, final_g)


def _rel_bucket_table():
    n = jnp.arange(BIAS_LUT, dtype=I32)
    max_exact = N_BUCKETS // 2
    nf = jnp.maximum(n, 1).astype(F32)
    large = max_exact + (jnp.log(nf / max_exact) / math.log(MAX_DISTANCE / max_exact)
                         * (N_BUCKETS - max_exact)).astype(I32)
    large = jnp.minimum(large, N_BUCKETS - 1)
    return jnp.where(n < max_exact, n, large)


def _route_plan(eid, n_tiles, tm):
    two, T = eid.shape
    e_flat = eid.reshape(-1)
    onehot = (e_flat[:, None] == jnp.arange(N_EXPERTS, dtype=I32)[None, :]).astype(I32)
    csum = jnp.cumsum(onehot, axis=0)
    rank = jnp.sum((csum - onehot) * onehot, axis=1)
    counts = csum[-1]
    ptiles = (counts + tm - 1) // tm
    tend = jnp.cumsum(ptiles)
    tstart = tend - ptiles
    slot = jnp.sum(onehot * tstart[None, :], axis=1) * tm + rank
    total = tend[-1]
    tile_ids = jnp.arange(n_tiles, dtype=I32)
    tile_valid = (tile_ids < total).astype(I32)
    tile_expert = jnp.sum((tile_ids[:, None] >= tend[None, :]).astype(I32), axis=1)
    last_expert = jnp.sum((total - 1 >= tend).astype(I32))
    tile_expert = jnp.minimum(tile_expert, last_expert).astype(I32)
    partial_last = jnp.any((tile_ids[:, None] == tend[None, :] - 1)
                           & (ptiles[None, :] > 0) & (counts[None, :] % tm != 0), axis=1)
    tile_fill = (partial_last | (tile_ids >= total)).astype(I32)
    experts = jnp.arange(N_EXPERTS, dtype=I32)
    later = (experts[None, :] > tile_expert[:, None]) & (ptiles[None, :] > 0)
    tile_next_expert = jnp.min(jnp.where(later, experts[None, :], N_EXPERTS), axis=1)
    tile_next_expert = jnp.where(tile_next_expert == N_EXPERTS, -1, tile_next_expert).astype(I32)
    return slot.astype(I32), tile_expert, tile_valid, tile_fill, tile_next_expert


def kernel(x, c, positions, rel_bias, ada_w, ada_b, norm1_g, w_in, lambda_q1, lambda_k1, lambda_q2,
           lambda_k2, subln_g, conv_w, w_out, norm2_g, router_group_w, router_group_b,
           router_expert_w, router_expert_b, expert_w_gate, expert_w_up, expert_w_down, final_g):
    B, S, D = x.shape
    T = B * S
    l = 0

    c_pad = jnp.zeros((8, D), F32).at[:B].set(c)
    ada = _ada(c_pad, ada_w[l], ada_b[l].reshape(1, -1))[:B]
    sh1, sc1, g1, sh2, sc2, g2 = [a.reshape(B, 1, D) for a in jnp.split(ada, 6, axis=-1)]

    w_in_bf = w_in[l].astype(BF16)
    A = ATTN_WIDTH
    qt, k, vt, conv = _inproj(x, sc1, sh1, norm1_g[l].reshape(1, D), w_in_bf[:, 0:A].T,
                              w_in_bf[:, A:2 * A], w_in_bf[:, 2 * A:3 * A].T, w_in_bf[:, 3 * A:],
                              conv_w[l])
    lut = (rel_bias.astype(F32)[_rel_bucket_table(), :].T * LOG2E).reshape(N_DIFF_HEADS, 2, BIAS_LUT)
    lam_params = jnp.stack([lambda_q1[l], lambda_k1[l], lambda_q2[l], lambda_k2[l]]).astype(F32)
    attn = _attention(qt, k, vt, positions, lut, lam_params, subln_g[l].reshape(V_HEAD_DIM, 1))

    wr_t = jnp.zeros((ROUTER_ROWS, D), F32)
    wr_t = wr_t.at[0:N_GROUPS].set(router_group_w[l].T)
    wr_t = wr_t.at[N_GROUPS:N_GROUPS + N_EXPERTS].set(router_expert_w[l].T).astype(BF16)
    rb = jnp.zeros((ROUTER_ROWS, 1), F32)
    rb = rb.at[0:N_GROUPS, 0].set(router_group_b[l])
    rb = rb.at[N_GROUPS:N_GROUPS + N_EXPERTS, 0].set(router_expert_b[l])
    x1, hp, ri, rw = _outproj(attn, conv, x, g1, sc2, sh2, norm2_g[l].reshape(1, D),
                              w_out[l].astype(BF16), wr_t, rb)

    eid = ri[:, 0:2, :].transpose(1, 0, 2).reshape(2, T)
    w_tok = rw[:, 0:2, :].transpose(0, 2, 1).reshape(T, 2)
    n_tiles = 2 * T // MOE_TM + N_EXPERTS
    slot, tile_expert, tile_valid, tile_fill, tile_next = _route_plan(eid, n_tiles, MOE_TM)
    nct = T // COMB_TM
    pos = (slot * ROW_SLAB).reshape(2, nct, 1, COMB_TM)
    pos_tiles = jnp.concatenate([pos[0], pos[1]], axis=2)

    xs = _dispatch(tile_fill, pos_tiles, hp.reshape(T * ROW_SLAB, 128), n_tiles)
    ys = _moe(tile_expert, tile_valid, tile_next, xs, expert_w_gate[l], expert_w_up[l],
              expert_w_down[l])

    out = _combine(pos_tiles, ys, x1.reshape(T, D), g2, w_tok, final_g.reshape(1, D), S)
    return out.reshape(B, S, D)
```

```python
import functools
import math

import jax
import jax.numpy as jnp
from jax import lax
from jax.experimental import pallas as pl
from jax.experimental.pallas import tpu as pltpu

F32 = jnp.float32
BF16 = jnp.bfloat16
I32 = jnp.int32
U32 = jnp.uint32

D_MODEL = 1024
ATTN_WIDTH = 512
CONV_WIDTH = 512
N_DIFF_HEADS = 4
DIFF_HEAD_DIM = 64
V_HEAD_DIM = 128
IN_PROJ_WIDTH = 3 * ATTN_WIDTH + 3 * CONV_WIDTH
CONV_K = 3
N_BUCKETS = 32
MAX_DISTANCE = 128
N_GROUPS = 4
EXPERTS_PER_GROUP = 4
N_EXPERTS = 16
D_EXPERT = 512
NORM_EPS = 1e-6
SUBLN_EPS = 1e-5
NEG_INF = -1e30
LAMBDA_INIT = 0.8 - 0.6 * math.exp(-0.3 * 0)
QK_SCALE = DIFF_HEAD_DIM ** -0.5
LOG2E = math.log2(math.e)

BIAS_LUT = 128

ROW_TILE = 512
PROJ_SUBTILES = 2
ATT_TQ = 1024
ATT_CHAIN = 256
ATT_GROUP = 8
ATT_LOOKAHEAD = 8
ATT_TK = 256
MOE_TM = 256
COMB_TM = 256
ROUTER_ROWS = 32
ROW_SLAB = D_MODEL // 256
VMEM_LIMIT = 56 * 1024 * 1024


def _silu(x):
    return x * (1.0 / (1.0 + jnp.exp(-x)))


def _ada_kernel(c_ref, w_ref, b_ref, o_ref):
    for bi in range(c_ref.shape[0]):
        s = _silu(c_ref[bi])
        o_ref[bi] = jnp.sum(s * w_ref[...], axis=0, keepdims=True) + b_ref[...]


def _ada(c_col, w, b):
    nb, n = c_col.shape[0], w.shape[1]
    bn = 1024
    return pl.pallas_call(
        _ada_kernel,
        grid=(n // bn,),
        in_specs=[pl.BlockSpec((nb, D_MODEL, 1), lambda j: (0, 0, 0)),
                  pl.BlockSpec((D_MODEL, bn), lambda j: (0, j)),
                  pl.BlockSpec((1, bn), lambda j: (0, j))],
        out_specs=pl.BlockSpec((nb, 1, bn), lambda j: (0, 0, j)),
        out_shape=jax.ShapeDtypeStruct((nb, 1, n), F32),
        name="ada",
    )(c_col, w, b)


def _inproj_kernel(x_ref, sc_ref, sh_ref, g_ref, w_hbm, cw_ref,
                   qt_ref, k_ref, vt_ref, conv_ref,
                   carry_ref, wqt_ref, wk_ref, wvt_ref, wc_ref, stage_ref, sem):
    j = pl.program_id(1)
    tm = x_ref.shape[1]
    nt = (((1,), (1,)), ((), ()))

    @pl.when(jnp.logical_and(pl.program_id(0) == 0, j == 0))
    def _():
        for c in range(IN_PROJ_WIDTH // 512):
            cp = pltpu.make_async_copy(w_hbm.at[:, pl.ds(c * 512, 512)], stage_ref, sem.at[0])
            cp.start()
            cp.wait()
            if c == 0:
                wqt_ref[...] = stage_ref[...].T.astype(BF16)
            elif c == 1:
                wk_ref[...] = stage_ref[...].astype(BF16)
            elif c == 2:
                wvt_ref[...] = stage_ref[...].T.astype(BF16)
            else:
                wc_ref[:, (c - 3) * 512:(c - 2) * 512] = stage_ref[...].astype(BF16)

    @pl.when(j == 0)
    def _():
        carry_ref[...] = jnp.zeros_like(carry_ref)

    prev = carry_ref[...]
    sub = tm // PROJ_SUBTILES
    for s in range(PROJ_SUBTILES):
        rows = slice(s * sub, (s + 1) * sub)
        x = x_ref[0, rows, :]
        ms = jnp.mean(x * x, axis=-1, keepdims=True)
        h = x * lax.rsqrt(ms + NORM_EPS) * g_ref[...]
        h = h * (1.0 + sc_ref[0]) + sh_ref[0]
        hb = h.astype(BF16)

        def proj(c0):
            return jnp.dot(hb, wc_ref[:, c0:c0 + 512], preferred_element_type=F32)

        qt = lax.dot_general(wqt_ref[...], hb, nt, preferred_element_type=F32)
        qt_ref[0, :, rows] = (qt * (QK_SCALE * LOG2E)).astype(BF16)
        k_ref[0, rows, :] = jnp.dot(hb, wk_ref[...], preferred_element_type=F32).astype(BF16)
        vt_ref[0, :, rows] = lax.dot_general(wvt_ref[...], hb, nt,
                                             preferred_element_type=F32).astype(BF16)
        gate_b = proj(0)
        u = proj(512) * proj(1024)
        row = lax.broadcasted_iota(I32, u.shape, 0)
        u1 = pltpu.roll(u, 1, axis=0)
        u2 = pltpu.roll(u, 2, axis=0)
        u1 = jnp.where(row == 0, prev[7:8, :], u1)
        u2 = jnp.where(row == 0, prev[6:7, :], jnp.where(row == 1, prev[7:8, :], u2))
        conv = cw_ref[0:1, :] * u2 + cw_ref[1:2, :] * u1 + cw_ref[2:3, :] * u
        conv_ref[0, rows, :] = (gate_b * conv).astype(BF16)
        prev = u[sub - 8:sub, :]
    carry_ref[...] = prev


def _inproj(x, sc1, sh1, g1n, w_in, conv_w):
    B, S, D = x.shape
    tm = ROW_TILE
    row_out = jax.ShapeDtypeStruct((B, S, 512), BF16)
    col_out = jax.ShapeDtypeStruct((B, 512, S), BF16)
    row_spec = pl.BlockSpec((1, tm, 512), lambda b, j: (b, j, 0))
    col_spec = pl.BlockSpec((1, 512, tm), lambda b, j: (b, 0, j))
    mod_spec = pl.BlockSpec((1, 1, D), lambda b, j: (b, 0, 0))
    const2 = lambda b, j: (0, 0)
    return pl.pallas_call(
        _inproj_kernel,
        grid=(B, S // tm),
        in_specs=[pl.BlockSpec((1, tm, D), lambda b, j: (b, j, 0)),
                  mod_spec, mod_spec,
                  pl.BlockSpec((1, D), const2),
                  pl.BlockSpec(memory_space=pl.ANY),
                  pl.BlockSpec((CONV_K, CONV_WIDTH), const2)],
        out_specs=[col_spec, row_spec, col_spec, row_spec],
        out_shape=[col_out, row_out, col_out, row_out],
        scratch_shapes=[pltpu.VMEM((8, CONV_WIDTH), F32),
                        pltpu.VMEM((ATTN_WIDTH, D), BF16),
                        pltpu.VMEM((D, ATTN_WIDTH), BF16),
                        pltpu.VMEM((ATTN_WIDTH, D), BF16),
                        pltpu.VMEM((D, 3 * CONV_WIDTH), BF16),
                        pltpu.VMEM((D, 512), F32),
                        pltpu.SemaphoreType.DMA((1,))],
        compiler_params=pltpu.CompilerParams(
            dimension_semantics=("arbitrary", "arbitrary"), vmem_limit_bytes=VMEM_LIMIT),
        name="inproj",
    )(x, sc1, sh1, g1n, w_in, conv_w)


def _attn_kernel(qmin_ref, kmax_ref, consec_ref, qt_ref, k_ref, vt_ref, pr_ref, pc_ref, lut_ref,
                 lam_ref, sg_ref, o_ref, acc_ref, tz_ref):
    b = pl.program_id(0)
    qi = pl.program_id(2)
    nq = pl.num_programs(2)
    tq, tk = ATT_TQ, ATT_TK
    hw = ATT_CHAIN
    n_half = tq // hw
    nk = nq * (tq // tk)
    qt = qt_ref[0]
    feat = lax.broadcasted_iota(I32, qt.shape, 0)
    zero = jnp.zeros_like(qt)
    qts = (jnp.where(feat < DIFF_HEAD_DIM, qt, zero), jnp.where(feat >= DIFF_HEAD_DIM, qt, zero))
    luts = (lut_ref[0, 0:1, :], lut_ref[0, 1:2, :])
    fars = tuple(t[:, BIAS_LUT - 1:BIAS_LUT] for t in luts)
    pq = pr_ref[0]
    qmin = qmin_ref[b * nq + qi]
    czero = jnp.zeros((1, 1), F32)
    ones_rows = jnp.ones((16, tk), BF16)
    chains = [(mi, hi) for mi in range(2) for hi in range(n_half)]

    def gather_bias(mi, dist):
        table = jnp.broadcast_to(luts[mi], (tk, BIAS_LUT))
        return jnp.concatenate([jnp.take_along_axis(table, dist[:, o:o + 128], axis=1)
                                for o in range(0, hw, 128)], axis=1)

    @pl.when(qi == 0)
    def _():
        delta = (lax.broadcasted_iota(I32, (tk, hw), 1) - lax.broadcasted_iota(I32, (tk, hw), 0))
        for mi in range(2):
            diag_bias = gather_bias(mi, jnp.clip(delta, 0, BIAS_LUT - 1))
            tz_ref[mi, 0] = jnp.where(delta >= 0, diag_bias, NEG_INF)
            tz_ref[mi, 1] = gather_bias(mi, jnp.clip(delta + tk, 0, BIAS_LUT - 1))

    def run_blocks(blocks, state):
        loaded = []
        for (j, kinds) in blocks:
            ks = pl.multiple_of(j * tk, tk)
            kb = k_ref[0, pl.ds(ks, tk), :]
            vtb = jnp.concatenate([vt_ref[0, :, pl.ds(ks, tk)], ones_rows], axis=0)
            dist = None
            if any(kd is not None and kd.startswith("near") for kd in kinds):
                pk = pc_ref[0, pl.ds(ks, tk), :]
                dist = jnp.clip(pq - pk, 0, BIAS_LUT - 1)
            loaded.append((kb, vtb, dist))
        items = [(bi, n) for bi, blk in enumerate(blocks) for n, (mi, hi) in enumerate(chains)
                 if blk[1][hi] is not None]
        scores = {}

        def issue_qk(t):
            bi, n = items[t]
            mi, hi = chains[n]
            scores[t] = jnp.dot(loaded[bi][0], qts[mi][:, hi * hw:(hi + 1) * hw],
                                preferred_element_type=F32)

        state = list(state)
        for t in range(min(ATT_LOOKAHEAD, len(items))):
            issue_qk(t)
        for t, (bi, n) in enumerate(items):
            if t + ATT_LOOKAHEAD < len(items):
                issue_qk(t + ATT_LOOKAHEAD)
            _, vtb, dist = loaded[bi]
            mi, hi = chains[n]
            kind = blocks[bi][1][hi]
            cols = slice(hi * hw, (hi + 1) * hw)
            m, l = state[n]
            s = scores.pop(t)
            c = czero
            if kind == "far":
                c = fars[mi]
            elif kind == "tz_diag":
                s = tz_ref[mi, 0] + s
            elif kind == "tz_sub":
                s = tz_ref[mi, 1] + s
            else:
                s = gather_bias(mi, dist[:, cols]) + s
                if kind == "near_masked":
                    keep = (lax.broadcasted_iota(I32, (tk, hw), 0)
                            <= lax.broadcasted_iota(I32, (tk, hw), 1))
                    s = jnp.where(keep, s, NEG_INF)
            mn = jnp.maximum(m, jnp.max(s, axis=0, keepdims=True) + c)
            alpha = jnp.exp2(m - mn)
            p = jnp.exp2(s - (mn - c))
            pv = jnp.dot(vtb, p.astype(BF16), preferred_element_type=F32)
            l = alpha * l + pv[V_HEAD_DIM:V_HEAD_DIM + 1, :]
            acc_ref[mi, :, cols] = alpha * acc_ref[mi, :, cols] + pv[:V_HEAD_DIM, :]
            state[n] = (mn, l)
        return tuple(state)

    def block_is_far(j):
        return qmin - kmax_ref[b * nk + j] >= BIAS_LUT - 1

    def one_block(j, state):
        return lax.cond(block_is_far(j), lambda st: run_blocks([(j, ("far",) * n_half)], st),
                        lambda st: run_blocks([(j, ("near",) * n_half)], st), state)

    def group_body(width):
        def body(g, carry):
            j0, state = carry
            all_far = block_is_far(j0)
            for u in range(1, width):
                all_far = jnp.logical_and(all_far, block_is_far(j0 + u))
            far_blocks = [(j0 + u, ("far",) * n_half) for u in range(width)]
            state = lax.cond(
                all_far, lambda st: run_blocks(far_blocks, st),
                lambda st: lax.fori_loop(0, width, lambda u, s2: one_block(j0 + u, s2), st), state)
            return j0 + width, state
        return body

    acc_ref[...] = jnp.zeros_like(acc_ref)
    m0 = jnp.full((1, hw), NEG_INF, F32)
    l0 = jnp.zeros((1, hw), F32)
    state = tuple((m0, l0) for _ in chains)
    assert hw == tk and n_half % 2 == 0
    n_full = n_half * qi
    n_main = jnp.maximum(n_full - 2, 0)
    j0 = jnp.int32(0)
    width = ATT_GROUP
    j0, state = lax.fori_loop(0, n_main // width, group_body(width), (j0, state))
    rem = n_main % width
    while width > 2:
        width //= 2
        j0, state = lax.fori_loop(0, (rem // width) % 2, group_body(width), (j0, state))

    consec = consec_ref[b * nq + qi] == 1
    def diag_kinds(d, on_diag, below, further):
        return tuple(None if hi < d else on_diag if hi == d else below if hi == d + 1 else further
                     for hi in range(n_half))

    diag_fast = [(n_full + d, diag_kinds(d, "tz_diag", "tz_sub", "far")) for d in range(n_half)]
    diag_any = [(n_full + d, diag_kinds(d, "near_masked", "near", "near")) for d in range(n_half)]
    below_fast = [(n_full - 2, ("far",) * n_half),
                  (n_full - 1, ("tz_sub",) + ("far",) * (n_half - 1))]
    below_any = [(n_full - 2, ("near",) * n_half), (n_full - 1, ("near",) * n_half)]

    def tail(below, diag):
        return lambda st: lax.cond(qi > 0, lambda s2: run_blocks(below + diag, s2),
                                   lambda s2: run_blocks(diag, s2), st)

    state = lax.cond(consec, tail(below_fast, diag_fast), tail(below_any, diag_any), state)

    l1 = jnp.concatenate([state[n][1] for n, (mi, hi) in enumerate(chains) if mi == 0], axis=1)
    l2 = jnp.concatenate([state[n][1] for n, (mi, hi) in enumerate(chains) if mi == 1], axis=1)
    lam = (jnp.exp(jnp.sum(lam_ref[0:1, :] * lam_ref[1:2, :], axis=-1, keepdims=True))
           - jnp.exp(jnp.sum(lam_ref[2:3, :] * lam_ref[3:4, :], axis=-1, keepdims=True))
           + LAMBDA_INIT)
    ot = acc_ref[0] * (1.0 / l1) - (lam * (1.0 / l2)) * acc_ref[1]
    ot = ot * lax.rsqrt(jnp.mean(ot * ot, axis=0, keepdims=True) + SUBLN_EPS)
    ot = ot * (sg_ref[...] * (1.0 - LAMBDA_INIT))
    o_ref[0] = ot.T.astype(BF16)


def _attention(qt, k, vt, positions, lut, lam_params, subln_g_col):
    B, S, _ = k.shape
    tq = ATT_TQ
    nq = S // tq
    pos_col = positions.reshape(B, S, 1)
    pos_row = positions.reshape(B, 1, S)
    qmin = jnp.min(positions.reshape(B * nq, tq), axis=1)
    kmax = jnp.max(positions.reshape(B * (S // ATT_TK), ATT_TK), axis=1)
    step_ok = jnp.concatenate([positions[:, 1:] - positions[:, :-1] == 1,
                               jnp.ones((B, 1), jnp.bool_)], axis=1).reshape(B, nq, tq)
    inner_ok = jnp.all(step_ok[:, :, :tq - 1], axis=2)
    link_ok = jnp.concatenate([jnp.ones((B, 1), jnp.bool_), step_ok[:, :-1, tq - 1]], axis=1)
    prev_ok = jnp.concatenate([jnp.ones((B, 1), jnp.bool_), inner_ok[:, :-1]], axis=1)
    consec = (inner_ok & link_ok & prev_ok).astype(I32).reshape(B * nq)
    grid_spec = pltpu.PrefetchScalarGridSpec(
        num_scalar_prefetch=3,
        grid=(B, N_DIFF_HEADS, nq),
        in_specs=[pl.BlockSpec((1, 128, tq), lambda b, h, i, *_: (b, h, i)),
                  pl.BlockSpec((1, S, 128), lambda b, h, i, *_: (b, 0, h)),
                  pl.BlockSpec((1, 128, S), lambda b, h, i, *_: (b, h, 0)),
                  pl.BlockSpec((1, 1, tq), lambda b, h, i, *_: (b, 0, i)),
                  pl.BlockSpec((1, S, 1), lambda b, h, i, *_: (b, 0, 0)),
                  pl.BlockSpec((1, 2, BIAS_LUT), lambda b, h, i, *_: (h, 0, 0)),
                  pl.BlockSpec((4, DIFF_HEAD_DIM), lambda b, h, i, *_: (0, 0)),
                  pl.BlockSpec((V_HEAD_DIM, 1), lambda b, h, i, *_: (0, 0))],
        out_specs=pl.BlockSpec((1, tq, 128), lambda b, h, i, *_: (b, i, h)),
        scratch_shapes=[pltpu.VMEM((2, V_HEAD_DIM, tq), F32),
                        pltpu.VMEM((2, 2, ATT_TK, ATT_CHAIN), F32)],
    )
    return pl.pallas_call(
        _attn_kernel,
        grid_spec=grid_spec,
        out_shape=jax.ShapeDtypeStruct((B, S, ATTN_WIDTH), BF16),
        compiler_params=pltpu.CompilerParams(
            dimension_semantics=("arbitrary", "arbitrary", "arbitrary"),
            vmem_limit_bytes=VMEM_LIMIT),
        name="diffattn",
    )(qmin, kmax, consec, qt, k, vt, pos_row, pos_col, lut, lam_params, subln_g_col)


def _outproj_kernel(at_ref, cv_ref, x_ref, g1_ref, sc_ref, sh_ref, gn_ref, wo_ref, wr_ref, rb_ref,
                    x1_ref, hp_ref, ri_ref, rw_ref):
    tm = x_ref.shape[1]
    mix = (jnp.dot(at_ref[0], wo_ref[0:ATTN_WIDTH, :], preferred_element_type=F32)
           + jnp.dot(cv_ref[0], wo_ref[ATTN_WIDTH:, :], preferred_element_type=F32))
    x1 = x_ref[0] + g1_ref[0] * mix
    x1_ref[0] = x1
    ms = jnp.mean(x1 * x1, axis=-1, keepdims=True)
    h = x1 * lax.rsqrt(ms + NORM_EPS) * gn_ref[...]
    h = h * (1.0 + sc_ref[0]) + sh_ref[0]
    hb = h.astype(BF16)

    _store_packed_rows(hp_ref.at[0], h)

    lg_all = lax.dot_general(wr_ref[...], hb, (((1,), (1,)), ((), ())),
                             preferred_element_type=F32) + rb_ref[...]
    lg = lg_all[0:N_GROUPS, :]
    le = lg_all[N_GROUPS:N_GROUPS + N_EXPERTS, :]
    row4 = lax.broadcasted_iota(I32, (N_GROUPS, tm), 0)
    gmax = jnp.max(lg, axis=0, keepdims=True)
    pg_sel = 1.0 / jnp.sum(jnp.exp(lg - gmax), axis=0, keepdims=True)
    gsel = jnp.min(jnp.where(lg == gmax, row4, N_GROUPS), axis=0, keepdims=True)
    sel = jnp.zeros((EXPERTS_PER_GROUP, tm), F32)
    for g in range(N_GROUPS):
        sel = jnp.where(gsel == g, le[g * EXPERTS_PER_GROUP:(g + 1) * EXPERTS_PER_GROUP, :], sel)
    v1 = jnp.max(sel, axis=0, keepdims=True)
    i1 = jnp.min(jnp.where(sel == v1, row4, EXPERTS_PER_GROUP), axis=0, keepdims=True)
    rest = jnp.where(row4 == i1, -jnp.inf, sel)
    v2 = jnp.max(rest, axis=0, keepdims=True)
    i2 = jnp.min(jnp.where(rest == v2, row4, EXPERTS_PER_GROUP), axis=0, keepdims=True)
    e2 = jnp.exp(v2 - v1)
    w1 = pg_sel / (1.0 + e2)
    w2 = pg_sel * e2 / (1.0 + e2)
    row8 = lax.broadcasted_iota(I32, (8, tm), 0)
    eid1 = gsel * EXPERTS_PER_GROUP + i1
    eid2 = gsel * EXPERTS_PER_GROUP + i2
    ri_ref[0] = jnp.where(row8 == 0, eid1, jnp.where(row8 == 1, eid2, 0))
    rw_ref[0] = jnp.where(row8 == 0, w1, jnp.where(row8 == 1, w2, 0.0))


def _outproj(attn, conv, x, g1, sc2, sh2, g2n, w_out_bf, wr_t, rb):
    B, S, D = x.shape
    tm = ROW_TILE
    half_spec = pl.BlockSpec((1, tm, 512), lambda b, j: (b, j, 0))
    full_spec = pl.BlockSpec((1, tm, D), lambda b, j: (b, j, 0))
    mod_spec = pl.BlockSpec((1, 1, D), lambda b, j: (b, 0, 0))
    rt_spec = pl.BlockSpec((1, 8, tm), lambda b, j: (b, 0, j))
    return pl.pallas_call(
        _outproj_kernel,
        grid=(B, S // tm),
        in_specs=[half_spec, half_spec, full_spec, mod_spec, mod_spec, mod_spec,
                  pl.BlockSpec((1, D), lambda b, j: (0, 0)),
                  pl.BlockSpec((D, D), lambda b, j: (0, 0)),
                  pl.BlockSpec((ROUTER_ROWS, D), lambda b, j: (0, 0)),
                  pl.BlockSpec((ROUTER_ROWS, 1), lambda b, j: (0, 0))],
        out_specs=[full_spec, pl.BlockSpec((1, tm * ROW_SLAB, 128), lambda b, j: (b, j, 0)),
                   rt_spec, rt_spec],
        out_shape=[jax.ShapeDtypeStruct((B, S, D), F32),
                   jax.ShapeDtypeStruct((B, S * ROW_SLAB, 128), U32),
                   jax.ShapeDtypeStruct((B, 8, S), I32),
                   jax.ShapeDtypeStruct((B, 8, S), F32)],
        compiler_params=pltpu.CompilerParams(
            dimension_semantics=("arbitrary", "arbitrary"), vmem_limit_bytes=VMEM_LIMIT),
        name="outproj",
    )(attn, conv, x, g1, sc2, sh2, g2n, w_out_bf, wr_t, rb)


def _store_packed_rows(dst, x):
    half = D_MODEL // 2
    xb = x.astype(BF16).astype(F32)
    packed = (pltpu.bitcast(xb[:, :half], U32) >> 16) | (
        pltpu.bitcast(xb[:, half:], U32) & jnp.uint32(0xFFFF0000))
    rows = x.shape[0]
    for c in range(ROW_SLAB):
        dst[pl.ds(c, rows, stride=ROW_SLAB), :] = packed[:, c * 128:(c + 1) * 128]


def _load_packed_rows(src, row0, rows):
    packed = jnp.concatenate(
        [src[pl.ds(row0 * ROW_SLAB + c, rows, stride=ROW_SLAB), :] for c in range(ROW_SLAB)], axis=1)
    lo = pltpu.bitcast(packed << 16, F32)
    hi = pltpu.bitcast(packed & jnp.uint32(0xFFFF0000), F32)
    return lo, hi


def _row_gather_start(src_hbm, idx_ref, dst, sem, n_rows):
    for r in range(n_rows):
        off = pl.multiple_of(idx_ref[0, 0, r], ROW_SLAB)
        pltpu.make_async_copy(src_hbm.at[pl.ds(off, ROW_SLAB)],
                              dst.at[pl.ds(r * ROW_SLAB, ROW_SLAB)], sem).start(priority=r % 2)


def _row_gather_wait(src_hbm, dst, sem, n_rows):
    pltpu.make_async_copy(src_hbm.at[pl.ds(0, n_rows * ROW_SLAB)], dst, sem).wait()


def _dispatch_kernel(fill_ref, pos_ref, hp_ref, xs_hbm, zbuf, sem):
    i = pl.program_id(0)
    tm = COMB_TM
    tile_rows = MOE_TM * ROW_SLAB

    @pl.when(i == 0)
    def _():
        zbuf[...] = jnp.zeros_like(zbuf)

        def fill_copy(t):
            return pltpu.make_async_copy(
                zbuf, xs_hbm.at[pl.ds(pl.multiple_of(t * tile_rows, tile_rows), tile_rows)],
                sem.at[0])

        def start(t, c):
            @pl.when(fill_ref[t] == 1)
            def _():
                fill_copy(t).start()
            return c

        def wait(t, c):
            @pl.when(fill_ref[t] == 1)
            def _():
                fill_copy(t).wait()
            return c

        lax.fori_loop(0, fill_ref.shape[0], start, 0)
        lax.fori_loop(0, fill_ref.shape[0], wait, 0)

    par = lax.rem(i, 2)
    base = pl.multiple_of(i * (tm * ROW_SLAB), tm * ROW_SLAB)
    for r in range(2 * tm):
        off = pl.multiple_of(pos_ref[0, 0, r], ROW_SLAB)
        pltpu.make_async_copy(hp_ref.at[pl.ds(base + (r % tm) * ROW_SLAB, ROW_SLAB)],
                              xs_hbm.at[pl.ds(off, ROW_SLAB)],
                              sem.at[1 + par]).start(priority=r % 2)

    def drain(parity):
        for _ in range(2):
            pltpu.make_async_copy(hp_ref.at[pl.ds(0, tm * ROW_SLAB)],
                                  xs_hbm.at[pl.ds(0, tm * ROW_SLAB)], sem.at[1 + parity]).wait()

    @pl.when(i > 0)
    def _():
        drain(1 - par)

    @pl.when(i == pl.num_programs(0) - 1)
    def _():
        drain(par)


def _dispatch(tile_fill, pos_tiles, hp, n_tiles):
    T = hp.shape[0] // ROW_SLAB
    tm = COMB_TM
    grid_spec = pltpu.PrefetchScalarGridSpec(
        num_scalar_prefetch=1,
        grid=(T // tm,),
        in_specs=[pl.BlockSpec((1, 1, 2 * tm), lambda i, f: (i, 0, 0), memory_space=pltpu.SMEM),
                  pl.BlockSpec(memory_space=pltpu.VMEM)],
        out_specs=pl.BlockSpec(memory_space=pl.ANY),
        scratch_shapes=[pltpu.VMEM((MOE_TM * ROW_SLAB, 128), U32),
                        pltpu.SemaphoreType.DMA((3,))],
    )
    return pl.pallas_call(
        _dispatch_kernel,
        grid_spec=grid_spec,
        out_shape=jax.ShapeDtypeStruct((n_tiles * MOE_TM * ROW_SLAB, 128), U32),
        compiler_params=pltpu.CompilerParams(
            dimension_semantics=("arbitrary",), vmem_limit_bytes=VMEM_LIMIT),
        name="dispatch",
    )(tile_fill, pos_tiles, hp)


def _moe_kernel(te_ref, tv_ref, ne_ref, xs_ref, wg_hbm, wu_hbm, wd_hbm, y_ref,
                wg_bf, wu_bf, wd_bf, wg_st, wu_st, wd_st, sem):
    i = pl.program_id(0)
    tm = MOE_TM

    def weight_copies(e):
        return (pltpu.make_async_copy(wg_hbm.at[e], wg_st, sem.at[0]),
                pltpu.make_async_copy(wu_hbm.at[e], wu_st, sem.at[1]),
                pltpu.make_async_copy(wd_hbm.at[e], wd_st, sem.at[2]))

    @pl.when(i == 0)
    def _():
        for cp in weight_copies(te_ref[0]):
            cp.start()

    @pl.when(jnp.logical_or(i == 0, te_ref[i] != te_ref[jnp.maximum(i - 1, 0)]))
    def _():
        for cp in weight_copies(te_ref[i]):
            cp.wait()
        wg_bf[...] = wg_st[...].astype(BF16)
        wu_bf[...] = wu_st[...].astype(BF16)
        wd_bf[...] = wd_st[...].astype(BF16)

        @pl.when(ne_ref[i] >= 0)
        def _():
            for cp in weight_copies(ne_ref[i]):
                cp.start()

    @pl.when(tv_ref[i] == 1)
    def _():
        half = D_MODEL // 2
        lo, hi = _load_packed_rows(xs_ref, 0, tm)
        lo = lo.astype(BF16)
        hi = hi.astype(BF16)
        g = (jnp.dot(lo, wg_bf[0:half, :], preferred_element_type=F32)
             + jnp.dot(hi, wg_bf[half:, :], preferred_element_type=F32))
        u = (jnp.dot(lo, wu_bf[0:half, :], preferred_element_type=F32)
             + jnp.dot(hi, wu_bf[half:, :], preferred_element_type=F32))
        hid = (_silu(g) * u).astype(BF16)
        _store_packed_rows(y_ref, jnp.dot(hid, wd_bf[...], preferred_element_type=F32))

    @pl.when(tv_ref[i] == 0)
    def _():
        y_ref[...] = jnp.zeros_like(y_ref)


def _moe(tile_expert, tile_valid, tile_next_expert, xs, wg, wu, wd):
    nt = tile_expert.shape[0]
    tm = MOE_TM
    D = D_MODEL
    any_spec = pl.BlockSpec(memory_space=pl.ANY)
    grid_spec = pltpu.PrefetchScalarGridSpec(
        num_scalar_prefetch=3,
        grid=(nt,),
        in_specs=[pl.BlockSpec((tm * ROW_SLAB, 128), lambda i, *_: (i, 0)),
                  any_spec, any_spec, any_spec],
        out_specs=pl.BlockSpec((tm * ROW_SLAB, 128), lambda i, *_: (i, 0)),
        scratch_shapes=[pltpu.VMEM((D, D_EXPERT), BF16), pltpu.VMEM((D, D_EXPERT), BF16),
                        pltpu.VMEM((D_EXPERT, D), BF16),
                        pltpu.VMEM((D, D_EXPERT), F32), pltpu.VMEM((D, D_EXPERT), F32),
                        pltpu.VMEM((D_EXPERT, D), F32),
                        pltpu.SemaphoreType.DMA((3,))],
    )
    return pl.pallas_call(
        _moe_kernel,
        grid_spec=grid_spec,
        out_shape=jax.ShapeDtypeStruct((nt * tm * ROW_SLAB, 128), U32),
        compiler_params=pltpu.CompilerParams(
            dimension_semantics=("arbitrary",), vmem_limit_bytes=VMEM_LIMIT),
        name="moe",
    )(tile_expert, tile_valid, tile_next_expert, xs, wg, wu, wd)


def _combine_kernel(pos_ref, posn_ref, ys_hbm, x1_ref, g2_ref, w_ref, fg_ref, o_ref, rbuf, sem):
    i = pl.program_id(0)
    n = pl.num_programs(0)
    slot = lax.rem(i, 2)
    nslot = 1 - slot
    tm = COMB_TM

    @pl.when(i == 0)
    def _():
        _row_gather_start(ys_hbm, pos_ref, rbuf.at[0], sem.at[0], 2 * tm)

    @pl.when(i + 1 < n)
    def _():
        _row_gather_start(ys_hbm, posn_ref, rbuf.at[nslot], sem.at[nslot], 2 * tm)

    _row_gather_wait(ys_hbm, rbuf.at[slot], sem.at[slot], 2 * tm)
    w = w_ref[...]
    r1 = jnp.concatenate(_load_packed_rows(rbuf.at[slot], 0, tm), axis=1)
    r2 = jnp.concatenate(_load_packed_rows(rbuf.at[slot], tm, tm), axis=1)
    moe = w[:, 0:1] * r1 + w[:, 1:2] * r2
    y = x1_ref[...] + g2_ref[0] * moe
    ms = jnp.mean(y * y, axis=-1, keepdims=True)
    o_ref[...] = y * lax.rsqrt(ms + NORM_EPS) * fg_ref[...]


def _combine(pos_tiles, ys, x1, g2, w_tok, final_g, seq_len):
    T, D = x1.shape
    tm = COMB_TM
    nt = T // tm
    per_b = seq_len // tm
    return pl.pallas_call(
        _combine_kernel,
        grid=(nt,),
        in_specs=[pl.BlockSpec((1, 1, 2 * tm), lambda i: (i, 0, 0), memory_space=pltpu.SMEM),
                  pl.BlockSpec((1, 1, 2 * tm), lambda i: (jnp.minimum(i + 1, nt - 1), 0, 0),
                               memory_space=pltpu.SMEM),
                  pl.BlockSpec(memory_space=pl.ANY),
                  pl.BlockSpec((tm, D), lambda i: (i, 0)),
                  pl.BlockSpec((1, 1, D), lambda i: (i // per_b, 0, 0)),
                  pl.BlockSpec((tm, 2), lambda i: (i, 0)),
                  pl.BlockSpec((1, D), lambda i: (0, 0))],
        out_specs=pl.BlockSpec((tm, D), lambda i: (i, 0)),
        out_shape=jax.ShapeDtypeStruct((T, D), F32),
        scratch_shapes=[pltpu.VMEM((2, 2 * tm * ROW_SLAB, 128), U32),
                        pltpu.SemaphoreType.DMA((2,))],
        compiler_params=pltpu.CompilerParams(
            dimension_semantics=("arbitrary",), vmem_limit_bytes=VMEM_LIMIT),
        name="combine",
    )(pos_tiles, pos_tiles, ys, x1, g2, w_tok, final_g)


def _rel_bucket_table():
    n = jnp.arange(BIAS_LUT, dtype=I32)
    max_exact = N_BUCKETS // 2
    nf = jnp.maximum(n, 1).astype(F32)
    large = max_exact + (jnp.log(nf / max_exact) / math.log(MAX_DISTANCE / max_exact)
                         * (N_BUCKETS - max_exact)).astype(I32)
    large = jnp.minimum(large, N_BUCKETS - 1)
    return jnp.where(n < max_exact, n, large)


def _route_plan(eid, n_tiles, tm):
    two, T = eid.shape
    e_flat = eid.reshape(-1)
    onehot = (e_flat[:, None] == jnp.arange(N_EXPERTS, dtype=I32)[None, :]).astype(I32)
    csum = jnp.cumsum(onehot, axis=0)
    rank = jnp.sum((csum - onehot) * onehot, axis=1)
    counts = csum[-1]
    ptiles = (counts + tm - 1) // tm
    tend = jnp.cumsum(ptiles)
    tstart = tend - ptiles
    slot = jnp.sum(onehot * tstart[None, :], axis=1) * tm + rank
    total = tend[-1]
    tile_ids = jnp.arange(n_tiles, dtype=I32)
    tile_valid = (tile_ids < total).astype(I32)
    tile_expert = jnp.sum((tile_ids[:, None] >= tend[None, :]).astype(I32), axis=1)
    last_expert = jnp.sum((total - 1 >= tend).astype(I32))
    tile_expert = jnp.minimum(tile_expert, last_expert).astype(I32)
    partial_last = jnp.any((tile_ids[:, None] == tend[None, :] - 1)
                           & (ptiles[None, :] > 0) & (counts[None, :] % tm != 0), axis=1)
    tile_fill = (partial_last | (tile_ids >= total)).astype(I32)
    experts = jnp.arange(N_EXPERTS, dtype=I32)
    later = (experts[None, :] > tile_expert[:, None]) & (ptiles[None, :] > 0)
    tile_next_expert = jnp.min(jnp.where(later, experts[None, :], N_EXPERTS), axis=1)
    tile_next_expert = jnp.where(tile_next_expert == N_EXPERTS, -1, tile_next_expert).astype(I32)
    return slot.astype(I32), tile_expert, tile_valid, tile_fill, tile_next_expert


def kernel(x, c, positions, rel_bias, ada_w, ada_b, norm1_g, w_in, lambda_q1, lambda_k1, lambda_q2,
           lambda_k2, subln_g, conv_w, w_out, norm2_g, router_group_w, router_group_b,
           router_expert_w, router_expert_b, expert_w_gate, expert_w_up, expert_w_down, final_g):
    B, S, D = x.shape
    T = B * S
    l = 0

    ada = _ada(c.reshape(B, D, 1), ada_w[l], ada_b[l].reshape(1, -1))
    sh1, sc1, g1, sh2, sc2, g2 = jnp.split(ada, 6, axis=-1)

    qt, k, vt, conv = _inproj(x, sc1, sh1, norm1_g[l].reshape(1, D), w_in[l], conv_w[l])
    lut = (rel_bias.astype(F32)[_rel_bucket_table(), :].T * LOG2E).reshape(N_DIFF_HEADS, 2, BIAS_LUT)
    lam_params = jnp.stack([lambda_q1[l], lambda_k1[l], lambda_q2[l], lambda_k2[l]]).astype(F32)
    attn = _attention(qt, k, vt, positions, lut, lam_params, subln_g[l].reshape(V_HEAD_DIM, 1))

    wr_t = jnp.zeros((ROUTER_ROWS, D), F32)
    wr_t = wr_t.at[0:N_GROUPS].set(router_group_w[l].T)
    wr_t = wr_t.at[N_GROUPS:N_GROUPS + N_EXPERTS].set(router_expert_w[l].T).astype(BF16)
    rb = jnp.zeros((ROUTER_ROWS, 1), F32)
    rb = rb.at[0:N_GROUPS, 0].set(router_group_b[l])
    rb = rb.at[N_GROUPS:N_GROUPS + N_EXPERTS, 0].set(router_expert_b[l])
    x1, hp, ri, rw = _outproj(attn, conv, x, g1, sc2, sh2, norm2_g[l].reshape(1, D),
                              w_out[l].astype(BF16), wr_t, rb)

    eid = ri[:, 0:2, :].transpose(1, 0, 2).reshape(2, T)
    w_tok = rw[:, 0:2, :].transpose(0, 2, 1).reshape(T, 2)
    n_tiles = 2 * T // MOE_TM + N_EXPERTS
    slot, tile_expert, tile_valid, tile_fill, tile_next = _route_plan(eid, n_tiles, MOE_TM)
    nct = T // COMB_TM
    pos = (slot * ROW_SLAB).reshape(2, nct, 1, COMB_TM)
    pos_tiles = jnp.concatenate([pos[0], pos[1]], axis=2)

    xs = _dispatch(tile_fill, pos_tiles, hp.reshape(T * ROW_SLAB, 128), n_tiles)
    ys = _moe(tile_expert, tile_valid, tile_next, xs, expert_w_gate[l], expert_w_up[l],
              expert_w_down[l])

    out = _combine(pos_tiles, ys, x1.reshape(T, D), g2, w_tok, final_g.reshape(1, D), S)
    return out.reshape(B, S, D)
```

```python
import functools
import math

import jax
import jax.numpy as jnp
from jax import lax
from jax.experimental import pallas as pl
from jax.experimental.pallas import tpu as pltpu

F32 = jnp.float32
BF16 = jnp.bfloat16
I32 = jnp.int32
U32 = jnp.uint32

D_MODEL = 1024
ATTN_WIDTH = 512
CONV_WIDTH = 512
N_DIFF_HEADS = 4
DIFF_HEAD_DIM = 64
V_HEAD_DIM = 128
IN_PROJ_WIDTH = 3 * ATTN_WIDTH + 3 * CONV_WIDTH
CONV_K = 3
N_BUCKETS = 32
MAX_DISTANCE = 128
N_GROUPS = 4
EXPERTS_PER_GROUP = 4
N_EXPERTS = 16
D_EXPERT = 512
NORM_EPS = 1e-6
SUBLN_EPS = 1e-5
NEG_INF = -1e30
LAMBDA_INIT = 0.8 - 0.6 * math.exp(-0.3 * 0)
QK_SCALE = DIFF_HEAD_DIM ** -0.5
LOG2E = math.log2(math.e)

BIAS_LUT = 128

ROW_TILE = 512
PROJ_SUBTILES = 2
OUT_SUBTILES = 4
ATT_TQ = 1024
ATT_CHAIN = 256
ATT_GROUP = 8
ATT_LOOKAHEAD = 8
ATT_TK = 256
MOE_TM = 256
COMB_TM = 256
ROUTER_ROWS = 32
ROW_SLAB = D_MODEL // 256
VMEM_LIMIT = 56 * 1024 * 1024


def _silu(x):
    return x * (1.0 / (1.0 + jnp.exp(-x)))


def _ada_kernel(c_ref, w_ref, b_ref, o_ref):
    for bi in range(c_ref.shape[0]):
        s = _silu(c_ref[bi])
        o_ref[bi] = jnp.sum(s * w_ref[...], axis=0, keepdims=True) + b_ref[...]


def _ada(c_col, w, b):
    nb, n = c_col.shape[0], w.shape[1]
    bn = 1024
    return pl.pallas_call(
        _ada_kernel,
        grid=(n // bn,),
        in_specs=[pl.BlockSpec((nb, D_MODEL, 1), lambda j: (0, 0, 0)),
                  pl.BlockSpec((D_MODEL, bn), lambda j: (0, j)),
                  pl.BlockSpec((1, bn), lambda j: (0, j))],
        out_specs=pl.BlockSpec((nb, 1, bn), lambda j: (0, 0, j)),
        out_shape=jax.ShapeDtypeStruct((nb, 1, n), F32),
        name="ada",
    )(c_col, w, b)


def _inproj_kernel(x_ref, sc_ref, sh_ref, g_ref, w_hbm, cw_ref,
                   qt_ref, k_ref, vt_ref, conv_ref,
                   carry_ref, wqt_ref, wk_ref, wvt_ref, wc_ref, stage_ref, sem):
    j = pl.program_id(1)
    tm = x_ref.shape[1]
    nt = (((1,), (1,)), ((), ()))

    @pl.when(jnp.logical_and(pl.program_id(0) == 0, j == 0))
    def _():
        for c in range(IN_PROJ_WIDTH // 512):
            cp = pltpu.make_async_copy(w_hbm.at[:, pl.ds(c * 512, 512)], stage_ref, sem.at[0])
            cp.start()
            cp.wait()
            if c == 0:
                wqt_ref[...] = stage_ref[...].T.astype(BF16)
            elif c == 1:
                wk_ref[...] = stage_ref[...].astype(BF16)
            elif c == 2:
                wvt_ref[...] = stage_ref[...].T.astype(BF16)
            else:
                wc_ref[:, (c - 3) * 512:(c - 2) * 512] = stage_ref[...].astype(BF16)

    @pl.when(j == 0)
    def _():
        carry_ref[...] = jnp.zeros_like(carry_ref)

    prev = carry_ref[...]
    sub = tm // PROJ_SUBTILES
    for s in range(PROJ_SUBTILES):
        rows = slice(s * sub, (s + 1) * sub)
        x = x_ref[0, rows, :]
        ms = jnp.mean(x * x, axis=-1, keepdims=True)
        h = x * lax.rsqrt(ms + NORM_EPS) * g_ref[...]
        h = h * (1.0 + sc_ref[0]) + sh_ref[0]
        hb = h.astype(BF16)

        def proj(c0):
            return jnp.dot(hb, wc_ref[:, c0:c0 + 512], preferred_element_type=F32)

        qt = lax.dot_general(wqt_ref[...], hb, nt, preferred_element_type=F32)
        qt_ref[0, :, rows] = (qt * (QK_SCALE * LOG2E)).astype(BF16)
        k_ref[0, rows, :] = jnp.dot(hb, wk_ref[...], preferred_element_type=F32).astype(BF16)
        vt_ref[0, :, rows] = lax.dot_general(wvt_ref[...], hb, nt,
                                             preferred_element_type=F32).astype(BF16)
        gate_b = proj(0)
        u = proj(512) * proj(1024)
        row = lax.broadcasted_iota(I32, u.shape, 0)
        u1 = pltpu.roll(u, 1, axis=0)
        u2 = pltpu.roll(u, 2, axis=0)
        u1 = jnp.where(row == 0, prev[7:8, :], u1)
        u2 = jnp.where(row == 0, prev[6:7, :], jnp.where(row == 1, prev[7:8, :], u2))
        conv = cw_ref[0:1, :] * u2 + cw_ref[1:2, :] * u1 + cw_ref[2:3, :] * u
        conv_ref[0, rows, :] = (gate_b * conv).astype(BF16)
        prev = u[sub - 8:sub, :]
    carry_ref[...] = prev


def _inproj(x, sc1, sh1, g1n, w_in, conv_w):
    B, S, D = x.shape
    tm = ROW_TILE
    row_out = jax.ShapeDtypeStruct((B, S, 512), BF16)
    col_out = jax.ShapeDtypeStruct((B, 512, S), BF16)
    row_spec = pl.BlockSpec((1, tm, 512), lambda b, j: (b, j, 0))
    col_spec = pl.BlockSpec((1, 512, tm), lambda b, j: (b, 0, j))
    mod_spec = pl.BlockSpec((1, 1, D), lambda b, j: (b, 0, 0))
    const2 = lambda b, j: (0, 0)
    return pl.pallas_call(
        _inproj_kernel,
        grid=(B, S // tm),
        in_specs=[pl.BlockSpec((1, tm, D), lambda b, j: (b, j, 0)),
                  mod_spec, mod_spec,
                  pl.BlockSpec((1, D), const2),
                  pl.BlockSpec(memory_space=pl.ANY),
                  pl.BlockSpec((CONV_K, CONV_WIDTH), const2)],
        out_specs=[col_spec, row_spec, col_spec, row_spec],
        out_shape=[col_out, row_out, col_out, row_out],
        scratch_shapes=[pltpu.VMEM((8, CONV_WIDTH), F32),
                        pltpu.VMEM((ATTN_WIDTH, D), BF16),
                        pltpu.VMEM((D, ATTN_WIDTH), BF16),
                        pltpu.VMEM((ATTN_WIDTH, D), BF16),
                        pltpu.VMEM((D, 3 * CONV_WIDTH), BF16),
                        pltpu.VMEM((D, 512), F32),
                        pltpu.SemaphoreType.DMA((1,))],
        compiler_params=pltpu.CompilerParams(
            dimension_semantics=("arbitrary", "arbitrary"), vmem_limit_bytes=VMEM_LIMIT),
        name="inproj",
    )(x, sc1, sh1, g1n, w_in, conv_w)


def _attn_kernel(qmin_ref, kmax_ref, consec_ref, qt_ref, k_ref, vt_ref, pr_ref, pr_all_ref, lut_ref,
                 lam_ref, sg_ref, o_ref, acc_ref, tz_ref):
    b = pl.program_id(0)
    qi = pl.program_id(2)
    nq = pl.num_programs(2)
    tq, tk = ATT_TQ, ATT_TK
    hw = ATT_CHAIN
    n_half = tq // hw
    nk = nq * (tq // tk)
    qt = qt_ref[0]
    feat = lax.broadcasted_iota(I32, qt.shape, 0)
    zero = jnp.zeros_like(qt)
    qts = (jnp.where(feat < DIFF_HEAD_DIM, qt, zero), jnp.where(feat >= DIFF_HEAD_DIM, qt, zero))
    luts = (lut_ref[0, 0:1, :], lut_ref[0, 1:2, :])
    fars = tuple(t[:, BIAS_LUT - 1:BIAS_LUT] for t in luts)
    pq = pr_ref[0]
    qmin = qmin_ref[b * nq + qi]
    czero = jnp.zeros((1, 1), F32)
    ones_rows = jnp.ones((16, tk), BF16)
    chains = [(mi, hi) for mi in range(2) for hi in range(n_half)]

    def gather_bias(mi, dist):
        table = jnp.broadcast_to(luts[mi], (tk, BIAS_LUT))
        return jnp.concatenate([jnp.take_along_axis(table, dist[:, o:o + 128], axis=1)
                                for o in range(0, hw, 128)], axis=1)

    @pl.when(qi == 0)
    def _():
        delta = (lax.broadcasted_iota(I32, (tk, hw), 1) - lax.broadcasted_iota(I32, (tk, hw), 0))
        for mi in range(2):
            diag_bias = gather_bias(mi, jnp.clip(delta, 0, BIAS_LUT - 1))
            tz_ref[mi, 0] = jnp.where(delta >= 0, diag_bias, NEG_INF)
            tz_ref[mi, 1] = gather_bias(mi, jnp.clip(delta + tk, 0, BIAS_LUT - 1))

    def run_blocks(blocks, state):
        loaded = []
        for (j, kinds) in blocks:
            ks = pl.multiple_of(j * tk, tk)
            kb = k_ref[0, pl.ds(ks, tk), :]
            vtb = jnp.concatenate([vt_ref[0, :, pl.ds(ks, tk)], ones_rows], axis=0)
            dist = None
            if any(kd is not None and kd.startswith("near") for kd in kinds):
                pk_rows = jnp.broadcast_to(pr_all_ref[0, :, pl.ds(ks, tk)], (8, tk))
                pk = pk_rows.T[:, 0:1]
                dist = jnp.clip(pq - pk, 0, BIAS_LUT - 1)
            loaded.append((kb, vtb, dist))
        items = [(bi, n) for bi, blk in enumerate(blocks) for n, (mi, hi) in enumerate(chains)
                 if blk[1][hi] is not None]
        scores = {}

        def issue_qk(t):
            bi, n = items[t]
            mi, hi = chains[n]
            scores[t] = jnp.dot(loaded[bi][0], qts[mi][:, hi * hw:(hi + 1) * hw],
                                preferred_element_type=F32)

        state = list(state)
        for t in range(min(ATT_LOOKAHEAD, len(items))):
            issue_qk(t)
        for t, (bi, n) in enumerate(items):
            if t + ATT_LOOKAHEAD < len(items):
                issue_qk(t + ATT_LOOKAHEAD)
            _, vtb, dist = loaded[bi]
            mi, hi = chains[n]
            kind = blocks[bi][1][hi]
            cols = slice(hi * hw, (hi + 1) * hw)
            m, l = state[n]
            s = scores.pop(t)
            c = czero
            if kind == "far":
                c = fars[mi]
            elif kind == "tz_diag":
                s = tz_ref[mi, 0] + s
            elif kind == "tz_sub":
                s = tz_ref[mi, 1] + s
            else:
                s = gather_bias(mi, dist[:, cols]) + s
                if kind == "near_masked":
                    keep = (lax.broadcasted_iota(I32, (tk, hw), 0)
                            <= lax.broadcasted_iota(I32, (tk, hw), 1))
                    s = jnp.where(keep, s, NEG_INF)
            mn = jnp.maximum(m, jnp.max(s, axis=0, keepdims=True) + c)
            alpha = jnp.exp2(m - mn)
            p = jnp.exp2(s - (mn - c))
            pv = jnp.dot(vtb, p.astype(BF16), preferred_element_type=F32)
            l = alpha * l + pv[V_HEAD_DIM:V_HEAD_DIM + 1, :]
            acc_ref[mi, :, cols] = alpha * acc_ref[mi, :, cols] + pv[:V_HEAD_DIM, :]
            state[n] = (mn, l)
        return tuple(state)

    def block_is_far(j):
        return qmin - kmax_ref[b * nk + j] >= BIAS_LUT - 1

    def one_block(j, state):
        return lax.cond(block_is_far(j), lambda st: run_blocks([(j, ("far",) * n_half)], st),
                        lambda st: run_blocks([(j, ("near",) * n_half)], st), state)

    def group_body(width):
        def body(g, carry):
            j0, state = carry
            all_far = block_is_far(j0)
            for u in range(1, width):
                all_far = jnp.logical_and(all_far, block_is_far(j0 + u))
            far_blocks = [(j0 + u, ("far",) * n_half) for u in range(width)]
            state = lax.cond(
                all_far, lambda st: run_blocks(far_blocks, st),
                lambda st: lax.fori_loop(0, width, lambda u, s2: one_block(j0 + u, s2), st), state)
            return j0 + width, state
        return body

    acc_ref[...] = jnp.zeros_like(acc_ref)
    m0 = jnp.full((1, hw), NEG_INF, F32)
    l0 = jnp.zeros((1, hw), F32)
    state = tuple((m0, l0) for _ in chains)
    assert hw == tk and n_half % 2 == 0
    n_full = n_half * qi
    n_main = jnp.maximum(n_full - 2, 0)
    j0 = jnp.int32(0)
    width = ATT_GROUP
    j0, state = lax.fori_loop(0, n_main // width, group_body(width), (j0, state))
    rem = n_main % width
    while width > 2:
        width //= 2
        j0, state = lax.fori_loop(0, (rem // width) % 2, group_body(width), (j0, state))

    consec = consec_ref[b * nq + qi] == 1
    def diag_kinds(d, on_diag, below, further):
        return tuple(None if hi < d else on_diag if hi == d else below if hi == d + 1 else further
                     for hi in range(n_half))

    diag_fast = [(n_full + d, diag_kinds(d, "tz_diag", "tz_sub", "far")) for d in range(n_half)]
    diag_any = [(n_full + d, diag_kinds(d, "near_masked", "near", "near")) for d in range(n_half)]
    below_fast = [(n_full - 2, ("far",) * n_half),
                  (n_full - 1, ("tz_sub",) + ("far",) * (n_half - 1))]
    below_any = [(n_full - 2, ("near",) * n_half), (n_full - 1, ("near",) * n_half)]

    def tail(below, diag):
        return lambda st: lax.cond(qi > 0, lambda s2: run_blocks(below + diag, s2),
                                   lambda s2: run_blocks(diag, s2), st)

    state = lax.cond(consec, tail(below_fast, diag_fast), tail(below_any, diag_any), state)

    l1 = jnp.concatenate([state[n][1] for n, (mi, hi) in enumerate(chains) if mi == 0], axis=1)
    l2 = jnp.concatenate([state[n][1] for n, (mi, hi) in enumerate(chains) if mi == 1], axis=1)
    lam = (jnp.exp(jnp.sum(lam_ref[0:1, :] * lam_ref[1:2, :], axis=-1, keepdims=True))
           - jnp.exp(jnp.sum(lam_ref[2:3, :] * lam_ref[3:4, :], axis=-1, keepdims=True))
           + LAMBDA_INIT)
    ot = acc_ref[0] * (1.0 / l1) - (lam * (1.0 / l2)) * acc_ref[1]
    ot = ot * lax.rsqrt(jnp.mean(ot * ot, axis=0, keepdims=True) + SUBLN_EPS)
    ot = ot * (sg_ref[...] * (1.0 - LAMBDA_INIT))
    o_ref[0] = ot.T.astype(BF16)


def _attention(qt, k, vt, positions, lut, lam_params, subln_g_col):
    B, S, _ = k.shape
    tq = ATT_TQ
    nq = S // tq
    pos_row = positions.reshape(B, 1, S)
    qmin = jnp.min(positions.reshape(B * nq, tq), axis=1)
    kmax = jnp.max(positions.reshape(B * (S // ATT_TK), ATT_TK), axis=1)
    step_ok = jnp.concatenate([positions[:, 1:] - positions[:, :-1] == 1,
                               jnp.ones((B, 1), jnp.bool_)], axis=1).reshape(B, nq, tq)
    inner_ok = jnp.all(step_ok[:, :, :tq - 1], axis=2)
    link_ok = jnp.concatenate([jnp.ones((B, 1), jnp.bool_), step_ok[:, :-1, tq - 1]], axis=1)
    prev_ok = jnp.concatenate([jnp.ones((B, 1), jnp.bool_), inner_ok[:, :-1]], axis=1)
    consec = (inner_ok & link_ok & prev_ok).astype(I32).reshape(B * nq)
    grid_spec = pltpu.PrefetchScalarGridSpec(
        num_scalar_prefetch=3,
        grid=(B, N_DIFF_HEADS, nq),
        in_specs=[pl.BlockSpec((1, 128, tq), lambda b, h, i, *_: (b, h, i)),
                  pl.BlockSpec((1, S, 128), lambda b, h, i, *_: (b, 0, h)),
                  pl.BlockSpec((1, 128, S), lambda b, h, i, *_: (b, h, 0)),
                  pl.BlockSpec((1, 1, tq), lambda b, h, i, *_: (b, 0, i)),
                  pl.BlockSpec((1, 1, S), lambda b, h, i, *_: (b, 0, 0)),
                  pl.BlockSpec((1, 2, BIAS_LUT), lambda b, h, i, *_: (h, 0, 0)),
                  pl.BlockSpec((4, DIFF_HEAD_DIM), lambda b, h, i, *_: (0, 0)),
                  pl.BlockSpec((V_HEAD_DIM, 1), lambda b, h, i, *_: (0, 0))],
        out_specs=pl.BlockSpec((1, tq, 128), lambda b, h, i, *_: (b, i, h)),
        scratch_shapes=[pltpu.VMEM((2, V_HEAD_DIM, tq), F32),
                        pltpu.VMEM((2, 2, ATT_TK, ATT_CHAIN), F32)],
    )
    return pl.pallas_call(
        _attn_kernel,
        grid_spec=grid_spec,
        out_shape=jax.ShapeDtypeStruct((B, S, ATTN_WIDTH), BF16),
        compiler_params=pltpu.CompilerParams(
            dimension_semantics=("arbitrary", "arbitrary", "arbitrary"),
            vmem_limit_bytes=VMEM_LIMIT),
        name="diffattn",
    )(qmin, kmax, consec, qt, k, vt, pos_row, pos_row, lut, lam_params, subln_g_col)


def _outproj_kernel(at_ref, cv_ref, x_ref, g1_ref, sc_ref, sh_ref, gn_ref, wo_ref, wr_ref, rb_ref,
                    x1_ref, hp_ref, ri_ref, rw_ref):
    sub = x_ref.shape[1] // OUT_SUBTILES
    groups = [slice(s * sub, (s + 1) * sub) for s in range(OUT_SUBTILES)]
    mixes = [jnp.dot(at_ref[0, rows, :], wo_ref[0:ATTN_WIDTH, :], preferred_element_type=F32)
             + jnp.dot(cv_ref[0, rows, :], wo_ref[ATTN_WIDTH:, :], preferred_element_type=F32)
             for rows in groups]
    for rows, mix in zip(groups, mixes):
        _outproj_rows(rows, mix, x_ref, g1_ref, sc_ref, sh_ref, gn_ref, wr_ref, rb_ref,
                      x1_ref, hp_ref, ri_ref, rw_ref)


def _outproj_rows(rows, mix, x_ref, g1_ref, sc_ref, sh_ref, gn_ref, wr_ref, rb_ref,
                  x1_ref, hp_ref, ri_ref, rw_ref):
    tm = rows.stop - rows.start
    x1 = x_ref[0, rows, :] + g1_ref[0] * mix
    x1_ref[0, rows, :] = x1
    ms = jnp.mean(x1 * x1, axis=-1, keepdims=True)
    h = x1 * lax.rsqrt(ms + NORM_EPS) * gn_ref[...]
    h = h * (1.0 + sc_ref[0]) + sh_ref[0]
    hb = h.astype(BF16)

    _store_packed_rows(hp_ref.at[0, pl.ds(rows.start * ROW_SLAB, tm * ROW_SLAB)], h)

    lg_all = lax.dot_general(wr_ref[...], hb, (((1,), (1,)), ((), ())),
                             preferred_element_type=F32) + rb_ref[...]
    lg = lg_all[0:N_GROUPS, :]
    le = lg_all[N_GROUPS:N_GROUPS + N_EXPERTS, :]
    row4 = lax.broadcasted_iota(I32, (N_GROUPS, tm), 0)
    gmax = jnp.max(lg, axis=0, keepdims=True)
    pg_sel = 1.0 / jnp.sum(jnp.exp(lg - gmax), axis=0, keepdims=True)
    gsel = jnp.min(jnp.where(lg == gmax, row4, N_GROUPS), axis=0, keepdims=True)
    sel = jnp.zeros((EXPERTS_PER_GROUP, tm), F32)
    for g in range(N_GROUPS):
        sel = jnp.where(gsel == g, le[g * EXPERTS_PER_GROUP:(g + 1) * EXPERTS_PER_GROUP, :], sel)
    v1 = jnp.max(sel, axis=0, keepdims=True)
    i1 = jnp.min(jnp.where(sel == v1, row4, EXPERTS_PER_GROUP), axis=0, keepdims=True)
    rest = jnp.where(row4 == i1, -jnp.inf, sel)
    v2 = jnp.max(rest, axis=0, keepdims=True)
    i2 = jnp.min(jnp.where(rest == v2, row4, EXPERTS_PER_GROUP), axis=0, keepdims=True)
    e2 = jnp.exp(v2 - v1)
    w1 = pg_sel / (1.0 + e2)
    w2 = pg_sel * e2 / (1.0 + e2)
    row8 = lax.broadcasted_iota(I32, (8, tm), 0)
    eid1 = gsel * EXPERTS_PER_GROUP + i1
    eid2 = gsel * EXPERTS_PER_GROUP + i2
    ri_ref[0, :, rows] = jnp.where(row8 == 0, eid1, jnp.where(row8 == 1, eid2, 0))
    rw_ref[0, :, rows] = jnp.where(row8 == 0, w1, jnp.where(row8 == 1, w2, 0.0))


def _outproj(attn, conv, x, g1, sc2, sh2, g2n, w_out_bf, wr_t, rb):
    B, S, D = x.shape
    tm = ROW_TILE
    half_spec = pl.BlockSpec((1, tm, 512), lambda b, j: (b, j, 0))
    full_spec = pl.BlockSpec((1, tm, D), lambda b, j: (b, j, 0))
    mod_spec = pl.BlockSpec((1, 1, D), lambda b, j: (b, 0, 0))
    rt_spec = pl.BlockSpec((1, 8, tm), lambda b, j: (b, 0, j))
    return pl.pallas_call(
        _outproj_kernel,
        grid=(B, S // tm),
        in_specs=[half_spec, half_spec, full_spec, mod_spec, mod_spec, mod_spec,
                  pl.BlockSpec((1, D), lambda b, j: (0, 0)),
                  pl.BlockSpec((D, D), lambda b, j: (0, 0)),
                  pl.BlockSpec((ROUTER_ROWS, D), lambda b, j: (0, 0)),
                  pl.BlockSpec((ROUTER_ROWS, 1), lambda b, j: (0, 0))],
        out_specs=[full_spec, pl.BlockSpec((1, tm * ROW_SLAB, 128), lambda b, j: (b, j, 0)),
                   rt_spec, rt_spec],
        out_shape=[jax.ShapeDtypeStruct((B, S, D), F32),
                   jax.ShapeDtypeStruct((B, S * ROW_SLAB, 128), U32),
                   jax.ShapeDtypeStruct((B, 8, S), I32),
                   jax.ShapeDtypeStruct((B, 8, S), F32)],
        compiler_params=pltpu.CompilerParams(
            dimension_semantics=("arbitrary", "arbitrary"), vmem_limit_bytes=VMEM_LIMIT),
        name="outproj",
    )(attn, conv, x, g1, sc2, sh2, g2n, w_out_bf, wr_t, rb)


def _store_packed_rows(dst, x):
    half = D_MODEL // 2
    xb = x.astype(BF16).astype(F32)
    packed = (pltpu.bitcast(xb[:, :half], U32) >> 16) | (
        pltpu.bitcast(xb[:, half:], U32) & jnp.uint32(0xFFFF0000))
    rows = x.shape[0]
    for c in range(ROW_SLAB):
        dst[pl.ds(c, rows, stride=ROW_SLAB), :] = packed[:, c * 128:(c + 1) * 128]


def _load_packed_rows(src, row0, rows):
    packed = jnp.concatenate(
        [src[pl.ds(row0 * ROW_SLAB + c, rows, stride=ROW_SLAB), :] for c in range(ROW_SLAB)], axis=1)
    lo = pltpu.bitcast(packed << 16, F32)
    hi = pltpu.bitcast(packed & jnp.uint32(0xFFFF0000), F32)
    return lo, hi


def _row_gather_start(src_hbm, idx_ref, dst, sem, n_rows):
    for r in range(n_rows):
        off = pl.multiple_of(idx_ref[0, 0, r], ROW_SLAB)
        pltpu.make_async_copy(src_hbm.at[pl.ds(off, ROW_SLAB)],
                              dst.at[pl.ds(r * ROW_SLAB, ROW_SLAB)], sem).start(priority=r % 2)


def _row_gather_wait(src_hbm, dst, sem, n_rows):
    pltpu.make_async_copy(src_hbm.at[pl.ds(0, n_rows * ROW_SLAB)], dst, sem).wait()


def _dispatch_kernel(fill_ref, pos_ref, hp_ref, xs_hbm, zbuf, sem):
    i = pl.program_id(0)
    tm = COMB_TM
    tile_rows = MOE_TM * ROW_SLAB

    @pl.when(i == 0)
    def _():
        zbuf[...] = jnp.zeros_like(zbuf)

        def fill_copy(t):
            return pltpu.make_async_copy(
                zbuf, xs_hbm.at[pl.ds(pl.multiple_of(t * tile_rows, tile_rows), tile_rows)],
                sem.at[0])

        def start(t, c):
            @pl.when(fill_ref[t] == 1)
            def _():
                fill_copy(t).start()
            return c

        def wait(t, c):
            @pl.when(fill_ref[t] == 1)
            def _():
                fill_copy(t).wait()
            return c

        lax.fori_loop(0, fill_ref.shape[0], start, 0)
        lax.fori_loop(0, fill_ref.shape[0], wait, 0)

    par = lax.rem(i, 2)
    base = pl.multiple_of(i * (tm * ROW_SLAB), tm * ROW_SLAB)
    for r in range(2 * tm):
        off = pl.multiple_of(pos_ref[0, 0, r], ROW_SLAB)
        pltpu.make_async_copy(hp_ref.at[pl.ds(base + (r % tm) * ROW_SLAB, ROW_SLAB)],
                              xs_hbm.at[pl.ds(off, ROW_SLAB)],
                              sem.at[1 + par]).start(priority=r % 2)

    def drain(parity):
        for _ in range(2):
            pltpu.make_async_copy(hp_ref.at[pl.ds(0, tm * ROW_SLAB)],
                                  xs_hbm.at[pl.ds(0, tm * ROW_SLAB)], sem.at[1 + parity]).wait()

    @pl.when(i > 0)
    def _():
        drain(1 - par)

    @pl.when(i == pl.num_programs(0) - 1)
    def _():
        drain(par)


def _dispatch(tile_fill, pos_tiles, hp, n_tiles):
    T = hp.shape[0] // ROW_SLAB
    tm = COMB_TM
    grid_spec = pltpu.PrefetchScalarGridSpec(
        num_scalar_prefetch=1,
        grid=(T // tm,),
        in_specs=[pl.BlockSpec((1, 1, 2 * tm), lambda i, f: (i, 0, 0), memory_space=pltpu.SMEM),
                  pl.BlockSpec(memory_space=pltpu.VMEM)],
        out_specs=pl.BlockSpec(memory_space=pl.ANY),
        scratch_shapes=[pltpu.VMEM((MOE_TM * ROW_SLAB, 128), U32),
                        pltpu.SemaphoreType.DMA((3,))],
    )
    return pl.pallas_call(
        _dispatch_kernel,
        grid_spec=grid_spec,
        out_shape=jax.ShapeDtypeStruct((n_tiles * MOE_TM * ROW_SLAB, 128), U32),
        compiler_params=pltpu.CompilerParams(
            dimension_semantics=("arbitrary",), vmem_limit_bytes=VMEM_LIMIT),
        name="dispatch",
    )(tile_fill, pos_tiles, hp)


def _moe_kernel(te_ref, tv_ref, ne_ref, xs_ref, wg_hbm, wu_hbm, wd_hbm, y_ref,
                wg_bf, wu_bf, wd_bf, wg_st, wu_st, wd_st, sem):
    i = pl.program_id(0)
    tm = MOE_TM

    def weight_copies(e):
        return (pltpu.make_async_copy(wg_hbm.at[e], wg_st, sem.at[0]),
                pltpu.make_async_copy(wu_hbm.at[e], wu_st, sem.at[1]),
                pltpu.make_async_copy(wd_hbm.at[e], wd_st, sem.at[2]))

    @pl.when(i == 0)
    def _():
        for cp in weight_copies(te_ref[0]):
            cp.start()

    @pl.when(jnp.logical_or(i == 0, te_ref[i] != te_ref[jnp.maximum(i - 1, 0)]))
    def _():
        for cp in weight_copies(te_ref[i]):
            cp.wait()
        wg_bf[...] = wg_st[...].astype(BF16)
        wu_bf[...] = wu_st[...].astype(BF16)
        wd_bf[...] = wd_st[...].astype(BF16)

        @pl.when(ne_ref[i] >= 0)
        def _():
            for cp in weight_copies(ne_ref[i]):
                cp.start()

    @pl.when(tv_ref[i] == 1)
    def _():
        half = D_MODEL // 2
        lo, hi = _load_packed_rows(xs_ref, 0, tm)
        lo = lo.astype(BF16)
        hi = hi.astype(BF16)
        g = (jnp.dot(lo, wg_bf[0:half, :], preferred_element_type=F32)
             + jnp.dot(hi, wg_bf[half:, :], preferred_element_type=F32))
        u = (jnp.dot(lo, wu_bf[0:half, :], preferred_element_type=F32)
             + jnp.dot(hi, wu_bf[half:, :], preferred_element_type=F32))
        hid = (_silu(g) * u).astype(BF16)
        _store_packed_rows(y_ref, jnp.dot(hid, wd_bf[...], preferred_element_type=F32))

    @pl.when(tv_ref[i] == 0)
    def _():
        y_ref[...] = jnp.zeros_like(y_ref)


def _moe(tile_expert, tile_valid, tile_next_expert, xs, wg, wu, wd):
    nt = tile_expert.shape[0]
    tm = MOE_TM
    D = D_MODEL
    any_spec = pl.BlockSpec(memory_space=pl.ANY)
    grid_spec = pltpu.PrefetchScalarGridSpec(
        num_scalar_prefetch=3,
        grid=(nt,),
        in_specs=[pl.BlockSpec((tm * ROW_SLAB, 128), lambda i, *_: (i, 0)),
                  any_spec, any_spec, any_spec],
        out_specs=pl.BlockSpec((tm * ROW_SLAB, 128), lambda i, *_: (i, 0)),
        scratch_shapes=[pltpu.VMEM((D, D_EXPERT), BF16), pltpu.VMEM((D, D_EXPERT), BF16),
                        pltpu.VMEM((D_EXPERT, D), BF16),
                        pltpu.VMEM((D, D_EXPERT), F32), pltpu.VMEM((D, D_EXPERT), F32),
                        pltpu.VMEM((D_EXPERT, D), F32),
                        pltpu.SemaphoreType.DMA((3,))],
    )
    return pl.pallas_call(
        _moe_kernel,
        grid_spec=grid_spec,
        out_shape=jax.ShapeDtypeStruct((nt * tm * ROW_SLAB, 128), U32),
        compiler_params=pltpu.CompilerParams(
            dimension_semantics=("arbitrary",), vmem_limit_bytes=VMEM_LIMIT),
        name="moe",
    )(tile_expert, tile_valid, tile_next_expert, xs, wg, wu, wd)


def _combine_kernel(pos_ref, posn_ref, ys_hbm, x1_ref, g2_ref, w_ref, fg_ref, o_ref, rbuf, sem):
    i = pl.program_id(0)
    n = pl.num_programs(0)
    slot = lax.rem(i, 2)
    nslot = 1 - slot
    tm = COMB_TM

    @pl.when(i == 0)
    def _():
        _row_gather_start(ys_hbm, pos_ref, rbuf.at[0], sem.at[0], 2 * tm)

    @pl.when(i + 1 < n)
    def _():
        _row_gather_start(ys_hbm, posn_ref, rbuf.at[nslot], sem.at[nslot], 2 * tm)

    _row_gather_wait(ys_hbm, rbuf.at[slot], sem.at[slot], 2 * tm)
    w = w_ref[...]
    r1 = jnp.concatenate(_load_packed_rows(rbuf.at[slot], 0, tm), axis=1)
    r2 = jnp.concatenate(_load_packed_rows(rbuf.at[slot], tm, tm), axis=1)
    moe = w[:, 0:1] * r1 + w[:, 1:2] * r2
    y = x1_ref[...] + g2_ref[0] * moe
    ms = jnp.mean(y * y, axis=-1, keepdims=True)
    o_ref[...] = y * lax.rsqrt(ms + NORM_EPS) * fg_ref[...]


def _combine(pos_tiles, ys, x1, g2, w_tok, final_g, seq_len):
    T, D = x1.shape
    tm = COMB_TM
    nt = T // tm
    per_b = seq_len // tm
    return pl.pallas_call(
        _combine_kernel,
        grid=(nt,),
        in_specs=[pl.BlockSpec((1, 1, 2 * tm), lambda i: (i, 0, 0), memory_space=pltpu.SMEM),
                  pl.BlockSpec((1, 1, 2 * tm), lambda i: (jnp.minimum(i + 1, nt - 1), 0, 0),
                               memory_space=pltpu.SMEM),
                  pl.BlockSpec(memory_space=pl.ANY),
                  pl.BlockSpec((tm, D), lambda i: (i, 0)),
                  pl.BlockSpec((1, 1, D), lambda i: (i // per_b, 0, 0)),
                  pl.BlockSpec((tm, 2), lambda i: (i, 0)),
                  pl.BlockSpec((1, D), lambda i: (0, 0))],
        out_specs=pl.BlockSpec((tm, D), lambda i: (i, 0)),
        out_shape=jax.ShapeDtypeStruct((T, D), F32),
        scratch_shapes=[pltpu.VMEM((2, 2 * tm * ROW_SLAB, 128), U32),
                        pltpu.SemaphoreType.DMA((2,))],
        compiler_params=pltpu.CompilerParams(
            dimension_semantics=("arbitrary",), vmem_limit_bytes=VMEM_LIMIT),
        name="combine",
    )(pos_tiles, pos_tiles, ys, x1, g2, w_tok, final_g)


def _rel_bucket_table():
    n = jnp.arange(BIAS_LUT, dtype=I32)
    max_exact = N_BUCKETS // 2
    nf = jnp.maximum(n, 1).astype(F32)
    large = max_exact + (jnp.log(nf / max_exact) / math.log(MAX_DISTANCE / max_exact)
                         * (N_BUCKETS - max_exact)).astype(I32)
    large = jnp.minimum(large, N_BUCKETS - 1)
    return jnp.where(n < max_exact, n, large)


def _route_plan(eid, n_tiles, tm):
    two, T = eid.shape
    e_flat = eid.reshape(-1)
    onehot = (e_flat[:, None] == jnp.arange(N_EXPERTS, dtype=I32)[None, :]).astype(I32)
    csum = jnp.cumsum(onehot, axis=0)
    rank = jnp.sum((csum - onehot) * onehot, axis=1)
    counts = csum[-1]
    ptiles = (counts + tm - 1) // tm
    tend = jnp.cumsum(ptiles)
    tstart = tend - ptiles
    slot = jnp.sum(onehot * tstart[None, :], axis=1) * tm + rank
    total = tend[-1]
    tile_ids = jnp.arange(n_tiles, dtype=I32)
    tile_valid = (tile_ids < total).astype(I32)
    tile_expert = jnp.sum((tile_ids[:, None] >= tend[None, :]).astype(I32), axis=1)
    last_expert = jnp.sum((total - 1 >= tend).astype(I32))
    tile_expert = jnp.minimum(tile_expert, last_expert).astype(I32)
    partial_last = jnp.any((tile_ids[:, None] == tend[None, :] - 1)
                           & (ptiles[None, :] > 0) & (counts[None, :] % tm != 0), axis=1)
    tile_fill = (partial_last | (tile_ids >= total)).astype(I32)
    experts = jnp.arange(N_EXPERTS, dtype=I32)
    later = (experts[None, :] > tile_expert[:, None]) & (ptiles[None, :] > 0)
    tile_next_expert = jnp.min(jnp.where(later, experts[None, :], N_EXPERTS), axis=1)
    tile_next_expert = jnp.where(tile_next_expert == N_EXPERTS, -1, tile_next_expert).astype(I32)
    return slot.astype(I32), tile_expert, tile_valid, tile_fill, tile_next_expert


def kernel(x, c, positions, rel_bias, ada_w, ada_b, norm1_g, w_in, lambda_q1, lambda_k1, lambda_q2,
           lambda_k2, subln_g, conv_w, w_out, norm2_g, router_group_w, router_group_b,
           router_expert_w, router_expert_b, expert_w_gate, expert_w_up, expert_w_down, final_g):
    B, S, D = x.shape
    T = B * S
    l = 0

    ada = _ada(c.reshape(B, D, 1), ada_w[l], ada_b[l].reshape(1, -1))
    sh1, sc1, g1, sh2, sc2, g2 = jnp.split(ada, 6, axis=-1)

    qt, k, vt, conv = _inproj(x, sc1, sh1, norm1_g[l].reshape(1, D), w_in[l], conv_w[l])
    lut = (rel_bias.astype(F32)[_rel_bucket_table(), :].T * LOG2E).reshape(N_DIFF_HEADS, 2, BIAS_LUT)
    lam_params = jnp.stack([lambda_q1[l], lambda_k1[l], lambda_q2[l], lambda_k2[l]]).astype(F32)
    attn = _attention(qt, k, vt, positions, lut, lam_params, subln_g[l].reshape(V_HEAD_DIM, 1))

    wr_t = jnp.zeros((ROUTER_ROWS, D), F32)
    wr_t = wr_t.at[0:N_GROUPS].set(router_group_w[l].T)
    wr_t = wr_t.at[N_GROUPS:N_GROUPS + N_EXPERTS].set(router_expert_w[l].T).astype(BF16)
    rb = jnp.zeros((ROUTER_ROWS, 1), F32)
    rb = rb.at[0:N_GROUPS, 0].set(router_group_b[l])
    rb = rb.at[N_GROUPS:N_GROUPS + N_EXPERTS, 0].set(router_expert_b[l])
    x1, hp, ri, rw = _outproj(attn, conv, x, g1, sc2, sh2, norm2_g[l].reshape(1, D),
                              w_out[l].astype(BF16), wr_t, rb)

    eid = ri[:, 0:2, :].transpose(1, 0, 2).reshape(2, T)
    n_tiles = 2 * T // MOE_TM + N_EXPERTS
    slot, tile_expert, tile_valid, tile_fill, tile_next = _route_plan(eid, n_tiles, MOE_TM)
    nct = T // COMB_TM
    pos = (slot * ROW_SLAB).reshape(2, nct, 1, COMB_TM)
    pos_tiles = jnp.concatenate([pos[0], pos[1]], axis=2)

    xs = _dispatch(tile_fill, pos_tiles, hp.reshape(T * ROW_SLAB, 128), n_tiles)
    ys = _moe(tile_expert, tile_valid, tile_next, xs, expert_w_gate[l], expert_w_up[l],
              expert_w_down[l])

    w_tok = rw[:, 0:2, :].transpose(0, 2, 1).reshape(T, 2)
    out = _combine(pos_tiles, ys, x1.reshape(T, D), g2, w_tok, final_g.reshape(1, D), S)
    return out.reshape(B, S, D)
```

```python
import functools
import math

import jax
import jax.numpy as jnp
from jax import lax
from jax.experimental import pallas as pl
from jax.experimental.pallas import tpu as pltpu

F32 = jnp.float32
BF16 = jnp.bfloat16
I32 = jnp.int32
U32 = jnp.uint32

D_MODEL = 1024
ATTN_WIDTH = 512
CONV_WIDTH = 512
N_DIFF_HEADS = 4
DIFF_HEAD_DIM = 64
V_HEAD_DIM = 128
IN_PROJ_WIDTH = 3 * ATTN_WIDTH + 3 * CONV_WIDTH
CONV_K = 3
N_BUCKETS = 32
MAX_DISTANCE = 128
N_GROUPS = 4
EXPERTS_PER_GROUP = 4
N_EXPERTS = 16
D_EXPERT = 512
NORM_EPS = 1e-6
SUBLN_EPS = 1e-5
NEG_INF = -1e30
LAMBDA_INIT = 0.8 - 0.6 * math.exp(-0.3 * 0)
QK_SCALE = DIFF_HEAD_DIM ** -0.5
LOG2E = math.log2(math.e)

BIAS_LUT = 128

ROW_TILE = 1024
PROJ_SUBTILES = 2
OUT_TILE = 1024
OUT_SUBTILES = 8
ATT_TQ = 1024
ATT_CHAIN = 256
ATT_GROUP = 8
ATT_LOOKAHEAD = 8
ATT_TK = 256
MOE_TM = 256
COMB_TM = 512
ROUTER_ROWS = 32
ROW_SLAB = D_MODEL // 256
VMEM_LIMIT = 56 * 1024 * 1024


def _silu(x):
    return x * (1.0 / (1.0 + jnp.exp(-x)))


def _ada_kernel(c_ref, w_ref, b_ref, o_ref):
    for bi in range(c_ref.shape[0]):
        s = _silu(c_ref[bi])
        o_ref[bi] = jnp.sum(s * w_ref[...], axis=0, keepdims=True) + b_ref[...]


def _ada(c_col, w, b):
    nb, n = c_col.shape[0], w.shape[1]
    bn = 1024
    return pl.pallas_call(
        _ada_kernel,
        grid=(n // bn,),
        in_specs=[pl.BlockSpec((nb, D_MODEL, 1), lambda j: (0, 0, 0)),
                  pl.BlockSpec((D_MODEL, bn), lambda j: (0, j)),
                  pl.BlockSpec((1, bn), lambda j: (0, j))],
        out_specs=pl.BlockSpec((nb, 1, bn), lambda j: (0, 0, j)),
        out_shape=jax.ShapeDtypeStruct((nb, 1, n), F32),
        name="ada",
    )(c_col, w, b)


def _inproj_kernel(x_ref, sc_ref, sh_ref, g_ref, w_hbm, cw_ref,
                   qt_ref, k_ref, vt_ref, conv_ref,
                   carry_ref, wqt_ref, wk_ref, wvt_ref, wc_ref, stage_ref, sem):
    j = pl.program_id(1)
    tm = x_ref.shape[1]
    nt = (((1,), (1,)), ((), ()))

    @pl.when(jnp.logical_and(pl.program_id(0) == 0, j == 0))
    def _():
        for c in range(IN_PROJ_WIDTH // 512):
            cp = pltpu.make_async_copy(w_hbm.at[:, pl.ds(c * 512, 512)], stage_ref, sem.at[0])
            cp.start()
            cp.wait()
            if c == 0:
                wqt_ref[...] = stage_ref[...].T.astype(BF16)
            elif c == 1:
                wk_ref[...] = stage_ref[...].astype(BF16)
            elif c == 2:
                wvt_ref[...] = stage_ref[...].T.astype(BF16)
            else:
                wc_ref[:, (c - 3) * 512:(c - 2) * 512] = stage_ref[...].astype(BF16)

    @pl.when(j == 0)
    def _():
        carry_ref[...] = jnp.zeros_like(carry_ref)

    prev = carry_ref[...]
    sub = tm // PROJ_SUBTILES
    for s in range(PROJ_SUBTILES):
        rows = slice(s * sub, (s + 1) * sub)
        x = x_ref[0, rows, :]
        ms = jnp.mean(x * x, axis=-1, keepdims=True)
        h = x * lax.rsqrt(ms + NORM_EPS) * g_ref[...]
        h = h * (1.0 + sc_ref[0]) + sh_ref[0]
        hb = h.astype(BF16)

        def proj(c0):
            return jnp.dot(hb, wc_ref[:, c0:c0 + 512], preferred_element_type=F32)

        qt = lax.dot_general(wqt_ref[...], hb, nt, preferred_element_type=F32)
        qt_ref[0, :, rows] = (qt * (QK_SCALE * LOG2E)).astype(BF16)
        k_ref[0, rows, :] = jnp.dot(hb, wk_ref[...], preferred_element_type=F32).astype(BF16)
        vt_ref[0, :, rows] = lax.dot_general(wvt_ref[...], hb, nt,
                                             preferred_element_type=F32).astype(BF16)
        gate_b = proj(0)
        u = proj(512) * proj(1024)
        row = lax.broadcasted_iota(I32, u.shape, 0)
        u1 = pltpu.roll(u, 1, axis=0)
        u2 = pltpu.roll(u, 2, axis=0)
        u1 = jnp.where(row == 0, prev[7:8, :], u1)
        u2 = jnp.where(row == 0, prev[6:7, :], jnp.where(row == 1, prev[7:8, :], u2))
        conv = cw_ref[0:1, :] * u2 + cw_ref[1:2, :] * u1 + cw_ref[2:3, :] * u
        conv_ref[0, rows, :] = (gate_b * conv).astype(BF16)
        prev = u[sub - 8:sub, :]
    carry_ref[...] = prev


def _inproj(x, sc1, sh1, g1n, w_in, conv_w):
    B, S, D = x.shape
    tm = ROW_TILE
    row_out = jax.ShapeDtypeStruct((B, S, 512), BF16)
    col_out = jax.ShapeDtypeStruct((B, 512, S), BF16)
    row_spec = pl.BlockSpec((1, tm, 512), lambda b, j: (b, j, 0))
    col_spec = pl.BlockSpec((1, 512, tm), lambda b, j: (b, 0, j))
    mod_spec = pl.BlockSpec((1, 1, D), lambda b, j: (b, 0, 0))
    const2 = lambda b, j: (0, 0)
    return pl.pallas_call(
        _inproj_kernel,
        grid=(B, S // tm),
        in_specs=[pl.BlockSpec((1, tm, D), lambda b, j: (b, j, 0)),
                  mod_spec, mod_spec,
                  pl.BlockSpec((1, D), const2),
                  pl.BlockSpec(memory_space=pl.ANY),
                  pl.BlockSpec((CONV_K, CONV_WIDTH), const2)],
        out_specs=[col_spec, row_spec, col_spec, row_spec],
        out_shape=[col_out, row_out, col_out, row_out],
        scratch_shapes=[pltpu.VMEM((8, CONV_WIDTH), F32),
                        pltpu.VMEM((ATTN_WIDTH, D), BF16),
                        pltpu.VMEM((D, ATTN_WIDTH), BF16),
                        pltpu.VMEM((ATTN_WIDTH, D), BF16),
                        pltpu.VMEM((D, 3 * CONV_WIDTH), BF16),
                        pltpu.VMEM((D, 512), F32),
                        pltpu.SemaphoreType.DMA((1,))],
        compiler_params=pltpu.CompilerParams(
            dimension_semantics=("arbitrary", "arbitrary"), vmem_limit_bytes=VMEM_LIMIT),
        name="inproj",
    )(x, sc1, sh1, g1n, w_in, conv_w)


def _attn_kernel(qmin_ref, kmax_ref, consec_ref, qt_ref, k_ref, vt_ref, pr_ref, pr_all_ref, lut_ref,
                 lam_ref, sg_ref, o_ref, acc_ref, tz_ref, *, n_q_tiles):
    b = pl.program_id(0)
    qi = pl.program_id(2)
    nq = pl.num_programs(2)
    tq, tk = ATT_TQ, ATT_TK
    hw = ATT_CHAIN
    n_half = tq // hw
    nk = nq * (tq // tk)
    qt = qt_ref[0]
    feat = lax.broadcasted_iota(I32, qt.shape, 0)
    zero = jnp.zeros_like(qt)
    qts = (jnp.where(feat < DIFF_HEAD_DIM, qt, zero), jnp.where(feat >= DIFF_HEAD_DIM, qt, zero))
    luts = (lut_ref[0, 0:1, :], lut_ref[0, 1:2, :])
    fars = tuple(t[:, BIAS_LUT - 1:BIAS_LUT] for t in luts)
    pq = pr_ref[0]
    qmin = qmin_ref[b * nq + qi]
    czero = jnp.zeros((1, 1), F32)
    ones_rows = jnp.ones((16, tk), BF16)
    chains = [(mi, hi) for mi in range(2) for hi in range(n_half)]

    def gather_bias(mi, dist):
        table = jnp.broadcast_to(luts[mi], (tk, BIAS_LUT))
        return jnp.concatenate([jnp.take_along_axis(table, dist[:, o:o + 128], axis=1)
                                for o in range(0, hw, 128)], axis=1)

    @pl.when(qi == 0)
    def _():
        delta = (lax.broadcasted_iota(I32, (tk, hw), 1) - lax.broadcasted_iota(I32, (tk, hw), 0))
        for mi in range(2):
            diag_bias = gather_bias(mi, jnp.clip(delta, 0, BIAS_LUT - 1))
            tz_ref[mi, 0] = jnp.where(delta >= 0, diag_bias, NEG_INF)
            tz_ref[mi, 1] = gather_bias(mi, jnp.clip(delta + tk, 0, BIAS_LUT - 1))

    def run_blocks(blocks, state):
        loaded = []
        for (j, kinds) in blocks:
            ks = pl.multiple_of(j * tk, tk)
            kb = k_ref[0, pl.ds(ks, tk), :]
            vtb = jnp.concatenate([vt_ref[0, :, pl.ds(ks, tk)], ones_rows], axis=0)
            dist = None
            if any(kd is not None and kd.startswith("near") for kd in kinds):
                pk_rows = jnp.broadcast_to(pr_all_ref[0, :, pl.ds(ks, tk)], (8, tk))
                pk = pk_rows.T[:, 0:1]
                dist = jnp.clip(pq - pk, 0, BIAS_LUT - 1)
            loaded.append((kb, vtb, dist))
        items = [(bi, n) for bi, blk in enumerate(blocks) for n, (mi, hi) in enumerate(chains)
                 if blk[1][hi] is not None]
        scores = {}

        def issue_qk(t):
            bi, n = items[t]
            mi, hi = chains[n]
            scores[t] = jnp.dot(loaded[bi][0], qts[mi][:, hi * hw:(hi + 1) * hw],
                                preferred_element_type=F32)

        state = list(state)
        for t in range(min(ATT_LOOKAHEAD, len(items))):
            issue_qk(t)
        for t, (bi, n) in enumerate(items):
            if t + ATT_LOOKAHEAD < len(items):
                issue_qk(t + ATT_LOOKAHEAD)
            _, vtb, dist = loaded[bi]
            mi, hi = chains[n]
            kind = blocks[bi][1][hi]
            cols = slice(hi * hw, (hi + 1) * hw)
            m, l = state[n]
            s = scores.pop(t)
            c = czero
            if kind == "far":
                c = fars[mi]
            elif kind == "tz_diag":
                s = tz_ref[mi, 0] + s
            elif kind == "tz_sub":
                s = tz_ref[mi, 1] + s
            else:
                s = gather_bias(mi, dist[:, cols]) + s
                if kind == "near_masked":
                    keep = (lax.broadcasted_iota(I32, (tk, hw), 0)
                            <= lax.broadcasted_iota(I32, (tk, hw), 1))
                    s = jnp.where(keep, s, NEG_INF)
            mn = jnp.maximum(m, jnp.max(s, axis=0, keepdims=True) + c)
            alpha = jnp.exp2(m - mn)
            p = jnp.exp2(s - (mn - c))
            pv = jnp.dot(vtb, p.astype(BF16), preferred_element_type=F32)
            l = alpha * l + pv[V_HEAD_DIM:V_HEAD_DIM + 1, :]
            acc_ref[mi, :, cols] = alpha * acc_ref[mi, :, cols] + pv[:V_HEAD_DIM, :]
            state[n] = (mn, l)
        return tuple(state)

    def block_is_far(j):
        return qmin - kmax_ref[b * nk + j] >= BIAS_LUT - 1

    def one_block(j, state):
        return lax.cond(block_is_far(j), lambda st: run_blocks([(j, ("far",) * n_half)], st),
                        lambda st: run_blocks([(j, ("near",) * n_half)], st), state)

    def group_body(width):
        def body(g, carry):
            j0, state = carry
            all_far = block_is_far(j0)
            for u in range(1, width):
                all_far = jnp.logical_and(all_far, block_is_far(j0 + u))
            far_blocks = [(j0 + u, ("far",) * n_half) for u in range(width)]
            state = lax.cond(
                all_far, lambda st: run_blocks(far_blocks, st),
                lambda st: lax.fori_loop(0, width, lambda u, s2: one_block(j0 + u, s2), st), state)
            return j0 + width, state
        return body

    acc_ref[...] = jnp.zeros_like(acc_ref)
    m0 = jnp.full((1, hw), NEG_INF, F32)
    l0 = jnp.zeros((1, hw), F32)
    state = tuple((m0, l0) for _ in chains)
    assert hw == tk and n_half % 2 == 0
    n_full = n_half * qi
    n_main = jnp.maximum(n_full - 2, 0)
    n_groups = n_main // ATT_GROUP
    j0, state = lax.fori_loop(0, n_groups, group_body(ATT_GROUP), (jnp.int32(0), state))
    rem = n_main - n_groups * ATT_GROUP

    consec = consec_ref[b * nq + qi] == 1

    def diag_kinds(d, on_diag, below, further):
        return tuple(None if hi < d else on_diag if hi == d else below if hi == d + 1 else further
                     for hi in range(n_half))

    diag_fast = [(n_full + d, diag_kinds(d, "tz_diag", "tz_sub", "far")) for d in range(n_half)]
    diag_any = [(n_full + d, diag_kinds(d, "near_masked", "near", "near")) for d in range(n_half)]
    below_fast = [(n_full - 2, ("far",) * n_half),
                  (n_full - 1, ("tz_sub",) + ("far",) * (n_half - 1))]
    below_any = [(n_full - 2, ("near",) * n_half), (n_full - 1, ("near",) * n_half)]

    def first_tile(st):
        return lax.cond(consec, lambda s2: run_blocks(diag_fast, s2),
                        lambda s2: run_blocks(diag_any, s2), st)

    def slow_rest(st):
        st = lax.fori_loop(0, rem, lambda u, s2: one_block(j0 + u, s2), st)
        return lax.cond(consec, lambda s2: run_blocks(below_fast + diag_fast, s2),
                        lambda s2: run_blocks(below_any + diag_any, s2), st)

    def fast_rest(r):
        left = [(j0 + u, ("far",) * n_half) for u in range(r)]
        return lambda st: run_blocks(left + below_fast + diag_fast, st)

    left_counts = sorted({(n_half * q - 2) % ATT_GROUP for q in range(1, n_q_tiles)})
    left_far = consec
    for u in range(max(left_counts, default=0)):
        far_u = block_is_far(jnp.minimum(j0 + u, nk - 1))
        left_far = jnp.logical_and(left_far, jnp.logical_or(u >= rem, far_u))

    def later_tile(st):
        out = slow_rest
        for r in left_counts:
            out = (lambda r, nxt: lambda s2: lax.cond(
                jnp.logical_and(left_far, rem == r), fast_rest(r), nxt, s2))(r, out)
        return out(st)

    state = lax.cond(qi > 0, later_tile, first_tile, state)

    l1 = jnp.concatenate([state[n][1] for n, (mi, hi) in enumerate(chains) if mi == 0], axis=1)
    l2 = jnp.concatenate([state[n][1] for n, (mi, hi) in enumerate(chains) if mi == 1], axis=1)
    lam = (jnp.exp(jnp.sum(lam_ref[0:1, :] * lam_ref[1:2, :], axis=-1, keepdims=True))
           - jnp.exp(jnp.sum(lam_ref[2:3, :] * lam_ref[3:4, :], axis=-1, keepdims=True))
           + LAMBDA_INIT)
    ot = acc_ref[0] * (1.0 / l1) - (lam * (1.0 / l2)) * acc_ref[1]
    ot = ot * lax.rsqrt(jnp.mean(ot * ot, axis=0, keepdims=True) + SUBLN_EPS)
    ot = ot * (sg_ref[...] * (1.0 - LAMBDA_INIT))
    o_ref[0] = ot.T.astype(BF16)


def _attention(qt, k, vt, positions, lut, lam_params, subln_g_col):
    B, S, _ = k.shape
    tq = ATT_TQ
    nq = S // tq
    pos_row = positions.reshape(B, 1, S)
    qmin = jnp.min(positions.reshape(B * nq, tq), axis=1)
    kmax = jnp.max(positions.reshape(B * (S // ATT_TK), ATT_TK), axis=1)
    step_ok = jnp.concatenate([positions[:, 1:] - positions[:, :-1] == 1,
                               jnp.ones((B, 1), jnp.bool_)], axis=1).reshape(B, nq, tq)
    inner_ok = jnp.all(step_ok[:, :, :tq - 1], axis=2)
    link_ok = jnp.concatenate([jnp.ones((B, 1), jnp.bool_), step_ok[:, :-1, tq - 1]], axis=1)
    prev_ok = jnp.concatenate([jnp.ones((B, 1), jnp.bool_), inner_ok[:, :-1]], axis=1)
    consec = (inner_ok & link_ok & prev_ok).astype(I32).reshape(B * nq)
    grid_spec = pltpu.PrefetchScalarGridSpec(
        num_scalar_prefetch=3,
        grid=(B, N_DIFF_HEADS, nq),
        in_specs=[pl.BlockSpec((1, 128, tq), lambda b, h, i, *_: (b, h, i)),
                  pl.BlockSpec((1, S, 128), lambda b, h, i, *_: (b, 0, h)),
                  pl.BlockSpec((1, 128, S), lambda b, h, i, *_: (b, h, 0)),
                  pl.BlockSpec((1, 1, tq), lambda b, h, i, *_: (b, 0, i)),
                  pl.BlockSpec((1, 1, S), lambda b, h, i, *_: (b, 0, 0)),
                  pl.BlockSpec((1, 2, BIAS_LUT), lambda b, h, i, *_: (h, 0, 0)),
                  pl.BlockSpec((4, DIFF_HEAD_DIM), lambda b, h, i, *_: (0, 0)),
                  pl.BlockSpec((V_HEAD_DIM, 1), lambda b, h, i, *_: (0, 0))],
        out_specs=pl.BlockSpec((1, tq, 128), lambda b, h, i, *_: (b, i, h)),
        scratch_shapes=[pltpu.VMEM((2, V_HEAD_DIM, tq), F32),
                        pltpu.VMEM((2, 2, ATT_TK, ATT_CHAIN), F32)],
    )
    return pl.pallas_call(
        functools.partial(_attn_kernel, n_q_tiles=nq),
        grid_spec=grid_spec,
        out_shape=jax.ShapeDtypeStruct((B, S, ATTN_WIDTH), BF16),
        compiler_params=pltpu.CompilerParams(
            dimension_semantics=("arbitrary", "arbitrary", "arbitrary"),
            vmem_limit_bytes=VMEM_LIMIT),
        name="diffattn",
    )(qmin, kmax, consec, qt, k, vt, pos_row, pos_row, lut, lam_params, subln_g_col)


def _outproj_kernel(at_ref, cv_ref, x_ref, g1_ref, sc_ref, sh_ref, gn_ref, wo_ref, wr_ref, rb_ref,
                    x1_ref, hp_ref, ri_ref, rw_ref):
    sub = x_ref.shape[1] // OUT_SUBTILES
    groups = [slice(s * sub, (s + 1) * sub) for s in range(OUT_SUBTILES)]
    mixes = [jnp.dot(at_ref[0, rows, :], wo_ref[0:ATTN_WIDTH, :], preferred_element_type=F32)
             + jnp.dot(cv_ref[0, rows, :], wo_ref[ATTN_WIDTH:, :], preferred_element_type=F32)
             for rows in groups]
    for rows, mix in zip(groups, mixes):
        _outproj_rows(rows, mix, x_ref, g1_ref, sc_ref, sh_ref, gn_ref, wr_ref, rb_ref,
                      x1_ref, hp_ref, ri_ref, rw_ref)


def _outproj_rows(rows, mix, x_ref, g1_ref, sc_ref, sh_ref, gn_ref, wr_ref, rb_ref,
                  x1_ref, hp_ref, ri_ref, rw_ref):
    tm = rows.stop - rows.start
    x1 = x_ref[0, rows, :] + g1_ref[0] * mix
    x1_ref[0, rows, :] = x1
    ms = jnp.mean(x1 * x1, axis=-1, keepdims=True)
    h = x1 * lax.rsqrt(ms + NORM_EPS) * gn_ref[...]
    h = h * (1.0 + sc_ref[0]) + sh_ref[0]
    hb = h.astype(BF16)

    _store_packed_rows(hp_ref.at[0, pl.ds(rows.start * ROW_SLAB, tm * ROW_SLAB)], h)

    lg_all = lax.dot_general(wr_ref[...], hb, (((1,), (1,)), ((), ())),
                             preferred_element_type=F32) + rb_ref[...]
    lg = lg_all[0:N_GROUPS, :]
    le = lg_all[N_GROUPS:N_GROUPS + N_EXPERTS, :]
    row4 = lax.broadcasted_iota(I32, (N_GROUPS, tm), 0)
    gmax = jnp.max(lg, axis=0, keepdims=True)
    pg_sel = 1.0 / jnp.sum(jnp.exp(lg - gmax), axis=0, keepdims=True)
    gsel = jnp.min(jnp.where(lg == gmax, row4, N_GROUPS), axis=0, keepdims=True)
    sel = jnp.zeros((EXPERTS_PER_GROUP, tm), F32)
    for g in range(N_GROUPS):
        sel = jnp.where(gsel == g, le[g * EXPERTS_PER_GROUP:(g + 1) * EXPERTS_PER_GROUP, :], sel)
    v1 = jnp.max(sel, axis=0, keepdims=True)
    i1 = jnp.min(jnp.where(sel == v1, row4, EXPERTS_PER_GROUP), axis=0, keepdims=True)
    rest = jnp.where(row4 == i1, -jnp.inf, sel)
    v2 = jnp.max(rest, axis=0, keepdims=True)
    i2 = jnp.min(jnp.where(rest == v2, row4, EXPERTS_PER_GROUP), axis=0, keepdims=True)
    e2 = jnp.exp(v2 - v1)
    w1 = pg_sel / (1.0 + e2)
    w2 = pg_sel * e2 / (1.0 + e2)
    row8 = lax.broadcasted_iota(I32, (8, tm), 0)
    eid1 = gsel * EXPERTS_PER_GROUP + i1
    eid2 = gsel * EXPERTS_PER_GROUP + i2
    ri_ref[0, :, rows] = jnp.where(row8 == 0, eid1, jnp.where(row8 == 1, eid2, 0))
    rw_ref[0, :, rows] = jnp.where(row8 == 0, w1, jnp.where(row8 == 1, w2, 0.0))


def _outproj(attn, conv, x, g1, sc2, sh2, g2n, w_out_bf, wr_t, rb):
    B, S, D = x.shape
    tm = OUT_TILE
    half_spec = pl.BlockSpec((1, tm, 512), lambda b, j: (b, j, 0))
    full_spec = pl.BlockSpec((1, tm, D), lambda b, j: (b, j, 0))
    mod_spec = pl.BlockSpec((1, 1, D), lambda b, j: (b, 0, 0))
    rt_spec = pl.BlockSpec((1, 8, tm), lambda b, j: (b, 0, j))
    return pl.pallas_call(
        _outproj_kernel,
        grid=(B, S // tm),
        in_specs=[half_spec, half_spec, full_spec, mod_spec, mod_spec, mod_spec,
                  pl.BlockSpec((1, D), lambda b, j: (0, 0)),
                  pl.BlockSpec((D, D), lambda b, j: (0, 0)),
                  pl.BlockSpec((ROUTER_ROWS, D), lambda b, j: (0, 0)),
                  pl.BlockSpec((ROUTER_ROWS, 1), lambda b, j: (0, 0))],
        out_specs=[full_spec, pl.BlockSpec((1, tm * ROW_SLAB, 128), lambda b, j: (b, j, 0)),
                   rt_spec, rt_spec],
        out_shape=[jax.ShapeDtypeStruct((B, S, D), F32),
                   jax.ShapeDtypeStruct((B, S * ROW_SLAB, 128), U32),
                   jax.ShapeDtypeStruct((B, 8, S), I32),
                   jax.ShapeDtypeStruct((B, 8, S), F32)],
        compiler_params=pltpu.CompilerParams(
            dimension_semantics=("arbitrary", "arbitrary"), vmem_limit_bytes=VMEM_LIMIT),
        name="outproj",
    )(attn, conv, x, g1, sc2, sh2, g2n, w_out_bf, wr_t, rb)


def _store_packed_rows(dst, x):
    half = D_MODEL // 2
    xb = x.astype(BF16).astype(F32)
    packed = (pltpu.bitcast(xb[:, :half], U32) >> 16) | (
        pltpu.bitcast(xb[:, half:], U32) & jnp.uint32(0xFFFF0000))
    rows = x.shape[0]
    for c in range(ROW_SLAB):
        dst[pl.ds(c, rows, stride=ROW_SLAB), :] = packed[:, c * 128:(c + 1) * 128]


def _load_packed_rows(src, row0, rows):
    packed = jnp.concatenate(
        [src[pl.ds(row0 * ROW_SLAB + c, rows, stride=ROW_SLAB), :] for c in range(ROW_SLAB)], axis=1)
    lo = pltpu.bitcast(packed << 16, F32)
    hi = pltpu.bitcast(packed & jnp.uint32(0xFFFF0000), F32)
    return lo, hi


def _row_gather_start(src_hbm, idx_ref, dst, sem, n_rows):
    for r in range(n_rows):
        off = pl.multiple_of(idx_ref[0, 0, r], ROW_SLAB)
        pltpu.make_async_copy(src_hbm.at[pl.ds(off, ROW_SLAB)],
                              dst.at[pl.ds(r * ROW_SLAB, ROW_SLAB)], sem).start(priority=r % 2)


def _row_gather_wait(src_hbm, dst, sem, n_rows):
    pltpu.make_async_copy(src_hbm.at[pl.ds(0, n_rows * ROW_SLAB)], dst, sem).wait()


def _dispatch_kernel(fill_ref, pos_ref, hp_ref, xs_hbm, zbuf, sem):
    i = pl.program_id(0)
    tm = COMB_TM
    tile_rows = MOE_TM * ROW_SLAB

    @pl.when(i == 0)
    def _():
        zbuf[...] = jnp.zeros_like(zbuf)

        def fill_copy(t):
            return pltpu.make_async_copy(
                zbuf, xs_hbm.at[pl.ds(pl.multiple_of(t * tile_rows, tile_rows), tile_rows)],
                sem.at[0])

        def start(t, c):
            @pl.when(fill_ref[t] == 1)
            def _():
                fill_copy(t).start()
            return c

        def wait(t, c):
            @pl.when(fill_ref[t] == 1)
            def _():
                fill_copy(t).wait()
            return c

        lax.fori_loop(0, fill_ref.shape[0], start, 0)
        lax.fori_loop(0, fill_ref.shape[0], wait, 0)

    par = lax.rem(i, 2)
    base = pl.multiple_of(i * (tm * ROW_SLAB), tm * ROW_SLAB)
    for r in range(2 * tm):
        off = pl.multiple_of(pos_ref[0, 0, r], ROW_SLAB)
        pltpu.make_async_copy(hp_ref.at[pl.ds(base + (r % tm) * ROW_SLAB, ROW_SLAB)],
                              xs_hbm.at[pl.ds(off, ROW_SLAB)],
                              sem.at[1 + par]).start(priority=r % 2)

    def drain(parity):
        for _ in range(2):
            pltpu.make_async_copy(hp_ref.at[pl.ds(0, tm * ROW_SLAB)],
                                  xs_hbm.at[pl.ds(0, tm * ROW_SLAB)], sem.at[1 + parity]).wait()

    @pl.when(i > 0)
    def _():
        drain(1 - par)

    @pl.when(i == pl.num_programs(0) - 1)
    def _():
        drain(par)


def _dispatch(tile_fill, pos_tiles, hp, n_tiles):
    T = hp.shape[0] // ROW_SLAB
    tm = COMB_TM
    grid_spec = pltpu.PrefetchScalarGridSpec(
        num_scalar_prefetch=1,
        grid=(T // tm,),
        in_specs=[pl.BlockSpec((1, 1, 2 * tm), lambda i, f: (i, 0, 0), memory_space=pltpu.SMEM),
                  pl.BlockSpec(memory_space=pltpu.VMEM)],
        out_specs=pl.BlockSpec(memory_space=pl.ANY),
        scratch_shapes=[pltpu.VMEM((MOE_TM * ROW_SLAB, 128), U32),
                        pltpu.SemaphoreType.DMA((3,))],
    )
    return pl.pallas_call(
        _dispatch_kernel,
        grid_spec=grid_spec,
        out_shape=jax.ShapeDtypeStruct((n_tiles * MOE_TM * ROW_SLAB, 128), U32),
        compiler_params=pltpu.CompilerParams(
            dimension_semantics=("arbitrary",), vmem_limit_bytes=VMEM_LIMIT),
        name="dispatch",
    )(tile_fill, pos_tiles, hp)


def _moe_kernel(te_ref, tv_ref, ne_ref, xs_ref, wg_hbm, wu_hbm, wd_hbm, y_ref,
                wg_bf, wu_bf, wd_bf, wg_st, wu_st, wd_st, sem):
    i = pl.program_id(0)
    tm = MOE_TM

    def weight_copies(e):
        return (pltpu.make_async_copy(wg_hbm.at[e], wg_st, sem.at[0]),
                pltpu.make_async_copy(wu_hbm.at[e], wu_st, sem.at[1]),
                pltpu.make_async_copy(wd_hbm.at[e], wd_st, sem.at[2]))

    @pl.when(i == 0)
    def _():
        for cp in weight_copies(te_ref[0]):
            cp.start()

    @pl.when(jnp.logical_or(i == 0, te_ref[i] != te_ref[jnp.maximum(i - 1, 0)]))
    def _():
        for cp in weight_copies(te_ref[i]):
            cp.wait()
        wg_bf[...] = wg_st[...].astype(BF16)
        wu_bf[...] = wu_st[...].astype(BF16)
        wd_bf[...] = wd_st[...].astype(BF16)

        @pl.when(ne_ref[i] >= 0)
        def _():
            for cp in weight_copies(ne_ref[i]):
                cp.start()

    @pl.when(tv_ref[i] == 1)
    def _():
        half = D_MODEL // 2
        lo, hi = _load_packed_rows(xs_ref, 0, tm)
        lo = lo.astype(BF16)
        hi = hi.astype(BF16)
        g = (jnp.dot(lo, wg_bf[0:half, :], preferred_element_type=F32)
             + jnp.dot(hi, wg_bf[half:, :], preferred_element_type=F32))
        u = (jnp.dot(lo, wu_bf[0:half, :], preferred_element_type=F32)
             + jnp.dot(hi, wu_bf[half:, :], preferred_element_type=F32))
        hid = (_silu(g) * u).astype(BF16)
        _store_packed_rows(y_ref, jnp.dot(hid, wd_bf[...], preferred_element_type=F32))

    @pl.when(tv_ref[i] == 0)
    def _():
        y_ref[...] = jnp.zeros_like(y_ref)


def _moe(tile_expert, tile_valid, tile_next_expert, xs, wg, wu, wd):
    nt = tile_expert.shape[0]
    tm = MOE_TM
    D = D_MODEL
    any_spec = pl.BlockSpec(memory_space=pl.ANY)
    grid_spec = pltpu.PrefetchScalarGridSpec(
        num_scalar_prefetch=3,
        grid=(nt,),
        in_specs=[pl.BlockSpec((tm * ROW_SLAB, 128), lambda i, *_: (i, 0)),
                  any_spec, any_spec, any_spec],
        out_specs=pl.BlockSpec((tm * ROW_SLAB, 128), lambda i, *_: (i, 0)),
        scratch_shapes=[pltpu.VMEM((D, D_EXPERT), BF16), pltpu.VMEM((D, D_EXPERT), BF16),
                        pltpu.VMEM((D_EXPERT, D), BF16),
                        pltpu.VMEM((D, D_EXPERT), F32), pltpu.VMEM((D, D_EXPERT), F32),
                        pltpu.VMEM((D_EXPERT, D), F32),
                        pltpu.SemaphoreType.DMA((3,))],
    )
    return pl.pallas_call(
        _moe_kernel,
        grid_spec=grid_spec,
        out_shape=jax.ShapeDtypeStruct((nt * tm * ROW_SLAB, 128), U32),
        compiler_params=pltpu.CompilerParams(
            dimension_semantics=("arbitrary",), vmem_limit_bytes=VMEM_LIMIT),
        name="moe",
    )(tile_expert, tile_valid, tile_next_expert, xs, wg, wu, wd)


def _combine_kernel(pos_ref, posn_ref, ys_hbm, x1_ref, g2_ref, w_ref, fg_ref, o_ref, rbuf, sem):
    i = pl.program_id(0)
    n = pl.num_programs(0)
    slot = lax.rem(i, 2)
    nslot = 1 - slot
    tm = COMB_TM

    @pl.when(i == 0)
    def _():
        _row_gather_start(ys_hbm, pos_ref, rbuf.at[0], sem.at[0], 2 * tm)

    @pl.when(i + 1 < n)
    def _():
        _row_gather_start(ys_hbm, posn_ref, rbuf.at[nslot], sem.at[nslot], 2 * tm)

    _row_gather_wait(ys_hbm, rbuf.at[slot], sem.at[slot], 2 * tm)
    w = w_ref[...]
    r1 = jnp.concatenate(_load_packed_rows(rbuf.at[slot], 0, tm), axis=1)
    r2 = jnp.concatenate(_load_packed_rows(rbuf.at[slot], tm, tm), axis=1)
    moe = w[:, 0:1] * r1 + w[:, 1:2] * r2
    y = x1_ref[...] + g2_ref[0] * moe
    ms = jnp.mean(y * y, axis=-1, keepdims=True)
    o_ref[...] = y * lax.rsqrt(ms + NORM_EPS) * fg_ref[...]


def _combine(pos_tiles, ys, x1, g2, w_tok, final_g, seq_len):
    T, D = x1.shape
    tm = COMB_TM
    nt = T // tm
    per_b = seq_len // tm
    return pl.pallas_call(
        _combine_kernel,
        grid=(nt,),
        in_specs=[pl.BlockSpec((1, 1, 2 * tm), lambda i: (i, 0, 0), memory_space=pltpu.SMEM),
                  pl.BlockSpec((1, 1, 2 * tm), lambda i: (jnp.minimum(i + 1, nt - 1), 0, 0),
                               memory_space=pltpu.SMEM),
                  pl.BlockSpec(memory_space=pl.ANY),
                  pl.BlockSpec((tm, D), lambda i: (i, 0)),
                  pl.BlockSpec((1, 1, D), lambda i: (i // per_b, 0, 0)),
                  pl.BlockSpec((tm, 2), lambda i: (i, 0)),
                  pl.BlockSpec((1, D), lambda i: (0, 0))],
        out_specs=pl.BlockSpec((tm, D), lambda i: (i, 0)),
        out_shape=jax.ShapeDtypeStruct((T, D), F32),
        scratch_shapes=[pltpu.VMEM((2, 2 * tm * ROW_SLAB, 128), U32),
                        pltpu.SemaphoreType.DMA((2,))],
        compiler_params=pltpu.CompilerParams(
            dimension_semantics=("arbitrary",), vmem_limit_bytes=VMEM_LIMIT),
        name="combine",
    )(pos_tiles, pos_tiles, ys, x1, g2, w_tok, final_g)


def _rel_bucket_table():
    n = jnp.arange(BIAS_LUT, dtype=I32)
    max_exact = N_BUCKETS // 2
    nf = jnp.maximum(n, 1).astype(F32)
    large = max_exact + (jnp.log(nf / max_exact) / math.log(MAX_DISTANCE / max_exact)
                         * (N_BUCKETS - max_exact)).astype(I32)
    large = jnp.minimum(large, N_BUCKETS - 1)
    return jnp.where(n < max_exact, n, large)


def _route_plan(eid, n_tiles, tm):
    two, T = eid.shape
    e_flat = eid.reshape(-1)
    onehot = (e_flat[:, None] == jnp.arange(N_EXPERTS, dtype=I32)[None, :]).astype(I32)
    csum = jnp.cumsum(onehot, axis=0)
    rank = jnp.sum((csum - onehot) * onehot, axis=1)
    counts = csum[-1]
    ptiles = (counts + tm - 1) // tm
    tend = jnp.cumsum(ptiles)
    tstart = tend - ptiles
    slot = jnp.sum(onehot * tstart[None, :], axis=1) * tm + rank
    total = tend[-1]
    tile_ids = jnp.arange(n_tiles, dtype=I32)
    tile_valid = (tile_ids < total).astype(I32)
    tile_expert = jnp.sum((tile_ids[:, None] >= tend[None, :]).astype(I32), axis=1)
    last_expert = jnp.sum((total - 1 >= tend).astype(I32))
    tile_expert = jnp.minimum(tile_expert, last_expert).astype(I32)
    partial_last = jnp.any((tile_ids[:, None] == tend[None, :] - 1)
                           & (ptiles[None, :] > 0) & (counts[None, :] % tm != 0), axis=1)
    tile_fill = (partial_last | (tile_ids >= total)).astype(I32)
    experts = jnp.arange(N_EXPERTS, dtype=I32)
    later = (experts[None, :] > tile_expert[:, None]) & (ptiles[None, :] > 0)
    tile_next_expert = jnp.min(jnp.where(later, experts[None, :], N_EXPERTS), axis=1)
    tile_next_expert = jnp.where(tile_next_expert == N_EXPERTS, -1, tile_next_expert).astype(I32)
    return slot.astype(I32), tile_expert, tile_valid, tile_fill, tile_next_expert


def kernel(x, c, positions, rel_bias, ada_w, ada_b, norm1_g, w_in, lambda_q1, lambda_k1, lambda_q2,
           lambda_k2, subln_g, conv_w, w_out, norm2_g, router_group_w, router_group_b,
           router_expert_w, router_expert_b, expert_w_gate, expert_w_up, expert_w_down, final_g):
    B, S, D = x.shape
    T = B * S
    l = 0

    ada = _ada(c.reshape(B, D, 1), ada_w[l], ada_b[l].reshape(1, -1))
    sh1, sc1, g1, sh2, sc2, g2 = jnp.split(ada, 6, axis=-1)

    qt, k, vt, conv = _inproj(x, sc1, sh1, norm1_g[l].reshape(1, D), w_in[l], conv_w[l])
    lut = (rel_bias.astype(F32)[_rel_bucket_table(), :].T * LOG2E).reshape(N_DIFF_HEADS, 2, BIAS_LUT)
    lam_params = jnp.stack([lambda_q1[l], lambda_k1[l], lambda_q2[l], lambda_k2[l]]).astype(F32)
    attn = _attention(qt, k, vt, positions, lut, lam_params, subln_g[l].reshape(V_HEAD_DIM, 1))

    wr_t = jnp.zeros((ROUTER_ROWS, D), F32)
    wr_t = wr_t.at[0:N_GROUPS].set(router_group_w[l].T)
    wr_t = wr_t.at[N_GROUPS:N_GROUPS + N_EXPERTS].set(router_expert_w[l].T).astype(BF16)
    rb = jnp.zeros((ROUTER_ROWS, 1), F32)
    rb = rb.at[0:N_GROUPS, 0].set(router_group_b[l])
    rb = rb.at[N_GROUPS:N_GROUPS + N_EXPERTS, 0].set(router_expert_b[l])
    x1, hp, ri, rw = _outproj(attn, conv, x, g1, sc2, sh2, norm2_g[l].reshape(1, D),
                              w_out[l].astype(BF16), wr_t, rb)

    eid = ri[:, 0:2, :].transpose(1, 0, 2).reshape(2, T)
    n_tiles = 2 * T // MOE_TM + N_EXPERTS
    slot, tile_expert, tile_valid, tile_fill, tile_next = _route_plan(eid, n_tiles, MOE_TM)
    nct = T // COMB_TM
    pos = (slot * ROW_SLAB).reshape(2, nct, 1, COMB_TM)
    pos_tiles = jnp.concatenate([pos[0], pos[1]], axis=2)

    xs = _dispatch(tile_fill, pos_tiles, hp.reshape(T * ROW_SLAB, 128), n_tiles)
    ys = _moe(tile_expert, tile_valid, tile_next, xs, expert_w_gate[l], expert_w_up[l],
              expert_w_down[l])

    w_tok = rw[:, 0:2, :].transpose(0, 2, 1).reshape(T, 2)
    out = _combine(pos_tiles, ys, x1.reshape(T, D), g2, w_tok, final_g.reshape(1, D), S)
    return out.reshape(B, S, D)
```

```python
import functools
import math

import jax
import jax.numpy as jnp
from jax import lax
from jax.experimental import pallas as pl
from jax.experimental.pallas import tpu as pltpu

F32 = jnp.float32
BF16 = jnp.bfloat16
I32 = jnp.int32
U32 = jnp.uint32

D_MODEL = 1024
ATTN_WIDTH = 512
CONV_WIDTH = 512
N_DIFF_HEADS = 4
DIFF_HEAD_DIM = 64
V_HEAD_DIM = 128
IN_PROJ_WIDTH = 3 * ATTN_WIDTH + 3 * CONV_WIDTH
CONV_K = 3
N_BUCKETS = 32
MAX_DISTANCE = 128
N_GROUPS = 4
EXPERTS_PER_GROUP = 4
N_EXPERTS = 16
D_EXPERT = 512
NORM_EPS = 1e-6
SUBLN_EPS = 1e-5
NEG_INF = -1e30
LAMBDA_INIT = 0.8 - 0.6 * math.exp(-0.3 * 0)
QK_SCALE = DIFF_HEAD_DIM ** -0.5
LOG2E = math.log2(math.e)

BIAS_LUT = 128

ROW_TILE = 512
PROJ_SUBTILES = 2
OUT_TILE = 1024
OUT_SUBTILES = 8
ATT_TQ = 1024
ATT_CHAIN = 256
ATT_GROUP = 8
ATT_LOOKAHEAD = 8
ATT_TK = 256
MOE_TM = 256
COMB_TM = 256
ROUTER_ROWS = 32
ROW_SLAB = D_MODEL // 256
VMEM_LIMIT = 56 * 1024 * 1024


def _silu(x):
    return x * (1.0 / (1.0 + jnp.exp(-x)))


def _ada_kernel(c_ref, w_ref, b_ref, o_ref):
    for bi in range(c_ref.shape[0]):
        s = _silu(c_ref[bi])
        o_ref[bi] = jnp.sum(s * w_ref[...], axis=0, keepdims=True) + b_ref[...]


def _ada(c_col, w, b):
    nb, n = c_col.shape[0], w.shape[1]
    bn = 1024
    return pl.pallas_call(
        _ada_kernel,
        grid=(n // bn,),
        in_specs=[pl.BlockSpec((nb, D_MODEL, 1), lambda j: (0, 0, 0)),
                  pl.BlockSpec((D_MODEL, bn), lambda j: (0, j)),
                  pl.BlockSpec((1, bn), lambda j: (0, j))],
        out_specs=pl.BlockSpec((nb, 1, bn), lambda j: (0, 0, j)),
        out_shape=jax.ShapeDtypeStruct((nb, 1, n), F32),
        name="ada",
    )(c_col, w, b)


def _inproj_kernel(x_ref, sc_ref, sh_ref, g_ref, w_hbm, cw_ref,
                   qt_ref, k_ref, vt_ref, conv_ref,
                   carry_ref, wqt_ref, wk_ref, wvt_ref, wc_ref, stage_ref, sem):
    j = pl.program_id(1)
    tm = x_ref.shape[1]
    nt = (((1,), (1,)), ((), ()))

    @pl.when(jnp.logical_and(pl.program_id(0) == 0, j == 0))
    def _():
        for c in range(IN_PROJ_WIDTH // 512):
            cp = pltpu.make_async_copy(w_hbm.at[:, pl.ds(c * 512, 512)], stage_ref, sem.at[0])
            cp.start()
            cp.wait()
            if c == 0:
                wqt_ref[...] = stage_ref[...].T.astype(BF16)
            elif c == 1:
                wk_ref[...] = stage_ref[...].astype(BF16)
            elif c == 2:
                wvt_ref[...] = stage_ref[...].T.astype(BF16)
            else:
                wc_ref[:, (c - 3) * 512:(c - 2) * 512] = stage_ref[...].astype(BF16)

    @pl.when(j == 0)
    def _():
        carry_ref[...] = jnp.zeros_like(carry_ref)

    prev = carry_ref[...]
    sub = tm // PROJ_SUBTILES
    for s in range(PROJ_SUBTILES):
        rows = slice(s * sub, (s + 1) * sub)
        x = x_ref[0, rows, :]
        ms = jnp.mean(x * x, axis=-1, keepdims=True)
        h = x * lax.rsqrt(ms + NORM_EPS) * g_ref[...]
        h = h * (1.0 + sc_ref[0]) + sh_ref[0]
        hb = h.astype(BF16)

        def proj(c0):
            return jnp.dot(hb, wc_ref[:, c0:c0 + 512], preferred_element_type=F32)

        qt = lax.dot_general(wqt_ref[...], hb, nt, preferred_element_type=F32)
        qt_ref[0, :, rows] = (qt * (QK_SCALE * LOG2E)).astype(BF16)
        k_ref[0, rows, :] = jnp.dot(hb, wk_ref[...], preferred_element_type=F32).astype(BF16)
        vt_ref[0, :, rows] = lax.dot_general(wvt_ref[...], hb, nt,
                                             preferred_element_type=F32).astype(BF16)
        gate_b = proj(0)
        u = proj(512) * proj(1024)
        row = lax.broadcasted_iota(I32, u.shape, 0)
        u1 = pltpu.roll(u, 1, axis=0)
        u2 = pltpu.roll(u, 2, axis=0)
        u1 = jnp.where(row == 0, prev[7:8, :], u1)
        u2 = jnp.where(row == 0, prev[6:7, :], jnp.where(row == 1, prev[7:8, :], u2))
        conv = cw_ref[0:1, :] * u2 + cw_ref[1:2, :] * u1 + cw_ref[2:3, :] * u
        conv_ref[0, rows, :] = (gate_b * conv).astype(BF16)
        prev = u[sub - 8:sub, :]
    carry_ref[...] = prev


def _inproj(x, sc1, sh1, g1n, w_in, conv_w):
    B, S, D = x.shape
    tm = ROW_TILE
    row_out = jax.ShapeDtypeStruct((B, S, 512), BF16)
    col_out = jax.ShapeDtypeStruct((B, 512, S), BF16)
    row_spec = pl.BlockSpec((1, tm, 512), lambda b, j: (b, j, 0))
    col_spec = pl.BlockSpec((1, 512, tm), lambda b, j: (b, 0, j))
    mod_spec = pl.BlockSpec((1, 1, D), lambda b, j: (b, 0, 0))
    const2 = lambda b, j: (0, 0)
    return pl.pallas_call(
        _inproj_kernel,
        grid=(B, S // tm),
        in_specs=[pl.BlockSpec((1, tm, D), lambda b, j: (b, j, 0)),
                  mod_spec, mod_spec,
                  pl.BlockSpec((1, D), const2),
                  pl.BlockSpec(memory_space=pl.ANY),
                  pl.BlockSpec((CONV_K, CONV_WIDTH), const2)],
        out_specs=[col_spec, row_spec, col_spec, row_spec],
        out_shape=[col_out, row_out, col_out, row_out],
        scratch_shapes=[pltpu.VMEM((8, CONV_WIDTH), F32),
                        pltpu.VMEM((ATTN_WIDTH, D), BF16),
                        pltpu.VMEM((D, ATTN_WIDTH), BF16),
                        pltpu.VMEM((ATTN_WIDTH, D), BF16),
                        pltpu.VMEM((D, 3 * CONV_WIDTH), BF16),
                        pltpu.VMEM((D, 512), F32),
                        pltpu.SemaphoreType.DMA((1,))],
        compiler_params=pltpu.CompilerParams(
            dimension_semantics=("arbitrary", "arbitrary"), vmem_limit_bytes=VMEM_LIMIT),
        name="inproj",
    )(x, sc1, sh1, g1n, w_in, conv_w)


def _attn_kernel(qmin_ref, kmax_ref, consec_ref, qt_ref, k_ref, vt_ref, pr_ref, pr_all_ref, lut_ref,
                 lam_ref, sg_ref, o_ref, acc_ref, tz_ref, *, n_q_tiles):
    b = pl.program_id(0)
    qi = pl.program_id(2)
    nq = pl.num_programs(2)
    tq, tk = ATT_TQ, ATT_TK
    hw = ATT_CHAIN
    n_half = tq // hw
    nk = nq * (tq // tk)
    qt = qt_ref[0]
    feat = lax.broadcasted_iota(I32, qt.shape, 0)
    zero = jnp.zeros_like(qt)
    qts = (jnp.where(feat < DIFF_HEAD_DIM, qt, zero), jnp.where(feat >= DIFF_HEAD_DIM, qt, zero))
    luts = (lut_ref[0, 0:1, :], lut_ref[0, 1:2, :])
    fars = tuple(t[:, BIAS_LUT - 1:BIAS_LUT] for t in luts)
    pq = pr_ref[0]
    qmin = qmin_ref[b * nq + qi]
    czero = jnp.zeros((1, 1), F32)
    ones_rows = jnp.ones((16, tk), BF16)
    chains = [(mi, hi) for mi in range(2) for hi in range(n_half)]

    def gather_bias(mi, dist):
        table = jnp.broadcast_to(luts[mi], (tk, BIAS_LUT))
        return jnp.concatenate([jnp.take_along_axis(table, dist[:, o:o + 128], axis=1)
                                for o in range(0, hw, 128)], axis=1)

    @pl.when(qi == 0)
    def _():
        delta = (lax.broadcasted_iota(I32, (tk, hw), 1) - lax.broadcasted_iota(I32, (tk, hw), 0))
        for mi in range(2):
            diag_bias = gather_bias(mi, jnp.clip(delta, 0, BIAS_LUT - 1))
            tz_ref[mi, 0] = jnp.where(delta >= 0, diag_bias, NEG_INF)
            tz_ref[mi, 1] = gather_bias(mi, jnp.clip(delta + tk, 0, BIAS_LUT - 1))

    def run_blocks(blocks, state):
        loaded = []
        for (j, kinds) in blocks:
            ks = pl.multiple_of(j * tk, tk)
            kb = k_ref[0, pl.ds(ks, tk), :]
            vtb = jnp.concatenate([vt_ref[0, :, pl.ds(ks, tk)], ones_rows], axis=0)
            dist = None
            if any(kd is not None and kd.startswith("near") for kd in kinds):
                pk_rows = jnp.broadcast_to(pr_all_ref[0, :, pl.ds(ks, tk)], (8, tk))
                pk = pk_rows.T[:, 0:1]
                dist = jnp.clip(pq - pk, 0, BIAS_LUT - 1)
            loaded.append((kb, vtb, dist))
        items = [(bi, n) for bi, blk in enumerate(blocks) for n, (mi, hi) in enumerate(chains)
                 if blk[1][hi] is not None]
        scores = {}

        def issue_qk(t):
            bi, n = items[t]
            mi, hi = chains[n]
            scores[t] = jnp.dot(loaded[bi][0], qts[mi][:, hi * hw:(hi + 1) * hw],
                                preferred_element_type=F32)

        state = list(state)
        for t in range(min(ATT_LOOKAHEAD, len(items))):
            issue_qk(t)
        for t, (bi, n) in enumerate(items):
            if t + ATT_LOOKAHEAD < len(items):
                issue_qk(t + ATT_LOOKAHEAD)
            _, vtb, dist = loaded[bi]
            mi, hi = chains[n]
            kind = blocks[bi][1][hi]
            cols = slice(hi * hw, (hi + 1) * hw)
            m, l = state[n]
            s = scores.pop(t)
            c = czero
            if kind == "far":
                c = fars[mi]
            elif kind == "tz_diag":
                s = tz_ref[mi, 0] + s
            elif kind == "tz_sub":
                s = tz_ref[mi, 1] + s
            else:
                s = gather_bias(mi, dist[:, cols]) + s
                if kind == "near_masked":
                    keep = (lax.broadcasted_iota(I32, (tk, hw), 0)
                            <= lax.broadcasted_iota(I32, (tk, hw), 1))
                    s = jnp.where(keep, s, NEG_INF)
            mn = jnp.maximum(m, jnp.max(s, axis=0, keepdims=True) + c)
            alpha = jnp.exp2(m - mn)
            p = jnp.exp2(s - (mn - c))
            pv = jnp.dot(vtb, p.astype(BF16), preferred_element_type=F32)
            l = alpha * l + pv[V_HEAD_DIM:V_HEAD_DIM + 1, :]
            acc_ref[mi, :, cols] = alpha * acc_ref[mi, :, cols] + pv[:V_HEAD_DIM, :]
            state[n] = (mn, l)
        return tuple(state)

    def block_is_far(j):
        return qmin - kmax_ref[b * nk + j] >= BIAS_LUT - 1

    def one_block(j, state):
        return lax.cond(block_is_far(j), lambda st: run_blocks([(j, ("far",) * n_half)], st),
                        lambda st: run_blocks([(j, ("near",) * n_half)], st), state)

    def group_body(width):
        def body(g, carry):
            j0, state = carry
            all_far = block_is_far(j0)
            for u in range(1, width):
                all_far = jnp.logical_and(all_far, block_is_far(j0 + u))
            far_blocks = [(j0 + u, ("far",) * n_half) for u in range(width)]
            state = lax.cond(
                all_far, lambda st: run_blocks(far_blocks, st),
                lambda st: lax.fori_loop(0, width, lambda u, s2: one_block(j0 + u, s2), st), state)
            return j0 + width, state
        return body

    acc_ref[...] = jnp.zeros_like(acc_ref)
    m0 = jnp.full((1, hw), NEG_INF, F32)
    l0 = jnp.zeros((1, hw), F32)
    state = tuple((m0, l0) for _ in chains)
    assert hw == tk and n_half % 2 == 0
    n_full = n_half * qi
    n_main = jnp.maximum(n_full - 2, 0)
    n_groups = n_main // ATT_GROUP
    j0, state = lax.fori_loop(0, n_groups, group_body(ATT_GROUP), (jnp.int32(0), state))
    rem = n_main - n_groups * ATT_GROUP

    consec = consec_ref[b * nq + qi] == 1

    def diag_kinds(d, on_diag, below, further):
        return tuple(None if hi < d else on_diag if hi == d else below if hi == d + 1 else further
                     for hi in range(n_half))

    diag_fast = [(n_full + d, diag_kinds(d, "tz_diag", "tz_sub", "far")) for d in range(n_half)]
    diag_any = [(n_full + d, diag_kinds(d, "near_masked", "near", "near")) for d in range(n_half)]
    below_fast = [(n_full - 2, ("far",) * n_half),
                  (n_full - 1, ("tz_sub",) + ("far",) * (n_half - 1))]
    below_any = [(n_full - 2, ("near",) * n_half), (n_full - 1, ("near",) * n_half)]

    def first_tile(st):
        return lax.cond(consec, lambda s2: run_blocks(diag_fast, s2),
                        lambda s2: run_blocks(diag_any, s2), st)

    def slow_rest(st):
        st = lax.fori_loop(0, rem, lambda u, s2: one_block(j0 + u, s2), st)
        return lax.cond(consec, lambda s2: run_blocks(below_fast + diag_fast, s2),
                        lambda s2: run_blocks(below_any + diag_any, s2), st)

    def fast_rest(r):
        left = [(j0 + u, ("far",) * n_half) for u in range(r)]
        return lambda st: run_blocks(left + below_fast + diag_fast, st)

    left_counts = sorted({(n_half * q - 2) % ATT_GROUP for q in range(1, n_q_tiles)})
    left_far = consec
    for u in range(max(left_counts, default=0)):
        far_u = block_is_far(jnp.minimum(j0 + u, nk - 1))
        left_far = jnp.logical_and(left_far, jnp.logical_or(u >= rem, far_u))

    def later_tile(st):
        out = slow_rest
        for r in left_counts:
            out = (lambda r, nxt: lambda s2: lax.cond(
                jnp.logical_and(left_far, rem == r), fast_rest(r), nxt, s2))(r, out)
        return out(st)

    state = lax.cond(qi > 0, later_tile, first_tile, state)

    l1 = jnp.concatenate([state[n][1] for n, (mi, hi) in enumerate(chains) if mi == 0], axis=1)
    l2 = jnp.concatenate([state[n][1] for n, (mi, hi) in enumerate(chains) if mi == 1], axis=1)
    lam = (jnp.exp(jnp.sum(lam_ref[0:1, :] * lam_ref[1:2, :], axis=-1, keepdims=True))
           - jnp.exp(jnp.sum(lam_ref[2:3, :] * lam_ref[3:4, :], axis=-1, keepdims=True))
           + LAMBDA_INIT)
    ot = acc_ref[0] * (1.0 / l1) - (lam * (1.0 / l2)) * acc_ref[1]
    ot = ot * lax.rsqrt(jnp.mean(ot * ot, axis=0, keepdims=True) + SUBLN_EPS)
    ot = ot * (sg_ref[...] * (1.0 - LAMBDA_INIT))
    o_ref[0] = ot.T.astype(BF16)


def _attention(qt, k, vt, positions, lut, lam_params, subln_g_col):
    B, S, _ = k.shape
    tq = ATT_TQ
    nq = S // tq
    pos_row = positions.reshape(B, 1, S)
    qmin = jnp.min(positions.reshape(B * nq, tq), axis=1)
    kmax = jnp.max(positions.reshape(B * (S // ATT_TK), ATT_TK), axis=1)
    step_ok = jnp.concatenate([positions[:, 1:] - positions[:, :-1] == 1,
                               jnp.ones((B, 1), jnp.bool_)], axis=1).reshape(B, nq, tq)
    inner_ok = jnp.all(step_ok[:, :, :tq - 1], axis=2)
    link_ok = jnp.concatenate([jnp.ones((B, 1), jnp.bool_), step_ok[:, :-1, tq - 1]], axis=1)
    prev_ok = jnp.concatenate([jnp.ones((B, 1), jnp.bool_), inner_ok[:, :-1]], axis=1)
    consec = (inner_ok & link_ok & prev_ok).astype(I32).reshape(B * nq)
    grid_spec = pltpu.PrefetchScalarGridSpec(
        num_scalar_prefetch=3,
        grid=(B, N_DIFF_HEADS, nq),
        in_specs=[pl.BlockSpec((1, 128, tq), lambda b, h, i, *_: (b, h, i)),
                  pl.BlockSpec((1, S, 128), lambda b, h, i, *_: (b, 0, h)),
                  pl.BlockSpec((1, 128, S), lambda b, h, i, *_: (b, h, 0)),
                  pl.BlockSpec((1, 1, tq), lambda b, h, i, *_: (b, 0, i)),
                  pl.BlockSpec((1, 1, S), lambda b, h, i, *_: (b, 0, 0)),
                  pl.BlockSpec((1, 2, BIAS_LUT), lambda b, h, i, *_: (h, 0, 0)),
                  pl.BlockSpec((4, DIFF_HEAD_DIM), lambda b, h, i, *_: (0, 0)),
                  pl.BlockSpec((V_HEAD_DIM, 1), lambda b, h, i, *_: (0, 0))],
        out_specs=pl.BlockSpec((1, tq, 128), lambda b, h, i, *_: (b, i, h)),
        scratch_shapes=[pltpu.VMEM((2, V_HEAD_DIM, tq), F32),
                        pltpu.VMEM((2, 2, ATT_TK, ATT_CHAIN), F32)],
    )
    return pl.pallas_call(
        functools.partial(_attn_kernel, n_q_tiles=nq),
        grid_spec=grid_spec,
        out_shape=jax.ShapeDtypeStruct((B, S, ATTN_WIDTH), BF16),
        compiler_params=pltpu.CompilerParams(
            dimension_semantics=("arbitrary", "arbitrary", "arbitrary"),
            vmem_limit_bytes=VMEM_LIMIT),
        name="diffattn",
    )(qmin, kmax, consec, qt, k, vt, pos_row, pos_row, lut, lam_params, subln_g_col)


def _outproj_kernel(at_ref, cv_ref, x_ref, g1_ref, sc_ref, sh_ref, gn_ref, wo_ref, wr_ref, rb_ref,
                    x1_ref, hp_ref, ri_ref, rw_ref):
    sub = x_ref.shape[1] // OUT_SUBTILES
    groups = [slice(s * sub, (s + 1) * sub) for s in range(OUT_SUBTILES)]
    mixes = [jnp.dot(at_ref[0, rows, :], wo_ref[0:ATTN_WIDTH, :], preferred_element_type=F32)
             + jnp.dot(cv_ref[0, rows, :], wo_ref[ATTN_WIDTH:, :], preferred_element_type=F32)
             for rows in groups]
    for rows, mix in zip(groups, mixes):
        _outproj_rows(rows, mix, x_ref, g1_ref, sc_ref, sh_ref, gn_ref, wr_ref, rb_ref,
                      x1_ref, hp_ref, ri_ref, rw_ref)


def _outproj_rows(rows, mix, x_ref, g1_ref, sc_ref, sh_ref, gn_ref, wr_ref, rb_ref,
                  x1_ref, hp_ref, ri_ref, rw_ref):
    tm = rows.stop - rows.start
    x1 = x_ref[0, rows, :] + g1_ref[0] * mix
    x1_ref[0, rows, :] = x1
    ms = jnp.mean(x1 * x1, axis=-1, keepdims=True)
    h = x1 * lax.rsqrt(ms + NORM_EPS) * gn_ref[...]
    h = h * (1.0 + sc_ref[0]) + sh_ref[0]
    hb = h.astype(BF16)

    _store_packed_rows(hp_ref.at[0, pl.ds(rows.start * ROW_SLAB, tm * ROW_SLAB)], h)

    lg_all = lax.dot_general(wr_ref[...], hb, (((1,), (1,)), ((), ())),
                             preferred_element_type=F32) + rb_ref[...]
    lg = lg_all[0:N_GROUPS, :]
    le = lg_all[N_GROUPS:N_GROUPS + N_EXPERTS, :]
    row4 = lax.broadcasted_iota(I32, (N_GROUPS, tm), 0)
    gmax = jnp.max(lg, axis=0, keepdims=True)
    pg_sel = 1.0 / jnp.sum(jnp.exp(lg - gmax), axis=0, keepdims=True)
    gsel = jnp.min(jnp.where(lg == gmax, row4, N_GROUPS), axis=0, keepdims=True)
    sel = jnp.zeros((EXPERTS_PER_GROUP, tm), F32)
    for g in range(N_GROUPS):
        sel = jnp.where(gsel == g, le[g * EXPERTS_PER_GROUP:(g + 1) * EXPERTS_PER_GROUP, :], sel)
    v1 = jnp.max(sel, axis=0, keepdims=True)
    i1 = jnp.min(jnp.where(sel == v1, row4, EXPERTS_PER_GROUP), axis=0, keepdims=True)
    rest = jnp.where(row4 == i1, -jnp.inf, sel)
    v2 = jnp.max(rest, axis=0, keepdims=True)
    i2 = jnp.min(jnp.where(rest == v2, row4, EXPERTS_PER_GROUP), axis=0, keepdims=True)
    e2 = jnp.exp(v2 - v1)
    w1 = pg_sel / (1.0 + e2)
    w2 = pg_sel * e2 / (1.0 + e2)
    row8 = lax.broadcasted_iota(I32, (8, tm), 0)
    eid1 = gsel * EXPERTS_PER_GROUP + i1
    eid2 = gsel * EXPERTS_PER_GROUP + i2
    ri_ref[0, :, rows] = jnp.where(row8 == 0, eid1, jnp.where(row8 == 1, eid2, 0))
    rw_ref[0, :, rows] = jnp.where(row8 == 0, w1, jnp.where(row8 == 1, w2, 0.0))


def _outproj(attn, conv, x, g1, sc2, sh2, g2n, w_out_bf, wr_t, rb):
    B, S, D = x.shape
    tm = OUT_TILE
    half_spec = pl.BlockSpec((1, tm, 512), lambda b, j: (b, j, 0))
    full_spec = pl.BlockSpec((1, tm, D), lambda b, j: (b, j, 0))
    mod_spec = pl.BlockSpec((1, 1, D), lambda b, j: (b, 0, 0))
    rt_spec = pl.BlockSpec((1, 8, tm), lambda b, j: (b, 0, j))
    return pl.pallas_call(
        _outproj_kernel,
        grid=(B, S // tm),
        in_specs=[half_spec, half_spec, full_spec, mod_spec, mod_spec, mod_spec,
                  pl.BlockSpec((1, D), lambda b, j: (0, 0)),
                  pl.BlockSpec((D, D), lambda b, j: (0, 0)),
                  pl.BlockSpec((ROUTER_ROWS, D), lambda b, j: (0, 0)),
                  pl.BlockSpec((ROUTER_ROWS, 1), lambda b, j: (0, 0))],
        out_specs=[full_spec, pl.BlockSpec((1, tm * ROW_SLAB, 128), lambda b, j: (b, j, 0)),
                   rt_spec, rt_spec],
        out_shape=[jax.ShapeDtypeStruct((B, S, D), F32),
                   jax.ShapeDtypeStruct((B, S * ROW_SLAB, 128), U32),
                   jax.ShapeDtypeStruct((B, 8, S), I32),
                   jax.ShapeDtypeStruct((B, 8, S), F32)],
        compiler_params=pltpu.CompilerParams(
            dimension_semantics=("arbitrary", "arbitrary"), vmem_limit_bytes=VMEM_LIMIT),
        name="outproj",
    )(attn, conv, x, g1, sc2, sh2, g2n, w_out_bf, wr_t, rb)


def _store_packed_rows(dst, x):
    half = D_MODEL // 2
    xb = x.astype(BF16).astype(F32)
    packed = (pltpu.bitcast(xb[:, :half], U32) >> 16) | (
        pltpu.bitcast(xb[:, half:], U32) & jnp.uint32(0xFFFF0000))
    rows = x.shape[0]
    for c in range(ROW_SLAB):
        dst[pl.ds(c, rows, stride=ROW_SLAB), :] = packed[:, c * 128:(c + 1) * 128]


def _load_packed_rows(src, row0, rows):
    packed = jnp.concatenate(
        [src[pl.ds(row0 * ROW_SLAB + c, rows, stride=ROW_SLAB), :] for c in range(ROW_SLAB)], axis=1)
    lo = pltpu.bitcast(packed << 16, F32)
    hi = pltpu.bitcast(packed & jnp.uint32(0xFFFF0000), F32)
    return lo, hi


def _row_gather_start(src_hbm, idx_ref, dst, sem, n_rows):
    for r in range(n_rows):
        off = pl.multiple_of(idx_ref[0, 0, r], ROW_SLAB)
        pltpu.make_async_copy(src_hbm.at[pl.ds(off, ROW_SLAB)],
                              dst.at[pl.ds(r * ROW_SLAB, ROW_SLAB)], sem).start(priority=r % 2)


def _row_gather_wait(src_hbm, dst, sem, n_rows):
    pltpu.make_async_copy(src_hbm.at[pl.ds(0, n_rows * ROW_SLAB)], dst, sem).wait()


def _dispatch_kernel(fill_ref, pos_ref, hp_ref, hp_hbm, xs_hbm, zbuf, sem):
    i = pl.program_id(0)
    tm = COMB_TM
    tile_rows = MOE_TM * ROW_SLAB

    @pl.when(i == 0)
    def _():
        zbuf[...] = jnp.zeros_like(zbuf)

        def fill_copy(t):
            return pltpu.make_async_copy(
                zbuf, xs_hbm.at[pl.ds(pl.multiple_of(t * tile_rows, tile_rows), tile_rows)],
                sem.at[0])

        def start(t, c):
            @pl.when(fill_ref[t] == 1)
            def _():
                fill_copy(t).start()
            return c

        def wait(t, c):
            @pl.when(fill_ref[t] == 1)
            def _():
                fill_copy(t).wait()
            return c

        lax.fori_loop(0, fill_ref.shape[0], start, 0)
        lax.fori_loop(0, fill_ref.shape[0], wait, 0)

    par = lax.rem(i, 2)
    base = pl.multiple_of(i * (tm * ROW_SLAB), tm * ROW_SLAB)
    for r in range(2 * tm):
        off = pl.multiple_of(pos_ref[0, 0, r], ROW_SLAB)
        src = hp_ref if r < tm else hp_hbm
        pltpu.make_async_copy(src.at[pl.ds(base + (r % tm) * ROW_SLAB, ROW_SLAB)],
                              xs_hbm.at[pl.ds(off, ROW_SLAB)],
                              sem.at[1 + par]).start(priority=r % 2)

    def drain(parity):
        for _ in range(2):
            pltpu.make_async_copy(hp_ref.at[pl.ds(0, tm * ROW_SLAB)],
                                  xs_hbm.at[pl.ds(0, tm * ROW_SLAB)], sem.at[1 + parity]).wait()

    @pl.when(i > 0)
    def _():
        drain(1 - par)

    @pl.when(i == pl.num_programs(0) - 1)
    def _():
        drain(par)


def _dispatch(tile_fill, pos_tiles, hp, n_tiles):
    T = hp.shape[0] // ROW_SLAB
    tm = COMB_TM
    grid_spec = pltpu.PrefetchScalarGridSpec(
        num_scalar_prefetch=1,
        grid=(T // tm,),
        in_specs=[pl.BlockSpec((1, 1, 2 * tm), lambda i, f: (i, 0, 0), memory_space=pltpu.SMEM),
                  pl.BlockSpec(memory_space=pltpu.VMEM),
                  pl.BlockSpec(memory_space=pl.ANY)],
        out_specs=pl.BlockSpec(memory_space=pl.ANY),
        scratch_shapes=[pltpu.VMEM((MOE_TM * ROW_SLAB, 128), U32),
                        pltpu.SemaphoreType.DMA((3,))],
    )
    return pl.pallas_call(
        _dispatch_kernel,
        grid_spec=grid_spec,
        out_shape=jax.ShapeDtypeStruct((n_tiles * MOE_TM * ROW_SLAB, 128), U32),
        compiler_params=pltpu.CompilerParams(
            dimension_semantics=("arbitrary",), vmem_limit_bytes=VMEM_LIMIT),
        name="dispatch",
    )(tile_fill, pos_tiles, hp, hp)


def _moe_kernel(te_ref, tv_ref, ne_ref, xs_ref, wg_hbm, wu_hbm, wd_hbm, y_ref,
                wg_bf, wu_bf, wd_bf, wg_st, wu_st, wd_st, sem):
    i = pl.program_id(0)
    tm = MOE_TM

    def weight_copies(e):
        return (pltpu.make_async_copy(wg_hbm.at[e], wg_st, sem.at[0]),
                pltpu.make_async_copy(wu_hbm.at[e], wu_st, sem.at[1]),
                pltpu.make_async_copy(wd_hbm.at[e], wd_st, sem.at[2]))

    @pl.when(i == 0)
    def _():
        for cp in weight_copies(te_ref[0]):
            cp.start()

    @pl.when(jnp.logical_or(i == 0, te_ref[i] != te_ref[jnp.maximum(i - 1, 0)]))
    def _():
        for cp in weight_copies(te_ref[i]):
            cp.wait()
        wg_bf[...] = wg_st[...].astype(BF16)
        wu_bf[...] = wu_st[...].astype(BF16)
        wd_bf[...] = wd_st[...].astype(BF16)

        @pl.when(ne_ref[i] >= 0)
        def _():
            for cp in weight_copies(ne_ref[i]):
                cp.start()

    @pl.when(tv_ref[i] == 1)
    def _():
        half = D_MODEL // 2
        lo, hi = _load_packed_rows(xs_ref, 0, tm)
        lo = lo.astype(BF16)
        hi = hi.astype(BF16)
        g = (jnp.dot(lo, wg_bf[0:half, :], preferred_element_type=F32)
             + jnp.dot(hi, wg_bf[half:, :], preferred_element_type=F32))
        u = (jnp.dot(lo, wu_bf[0:half, :], preferred_element_type=F32)
             + jnp.dot(hi, wu_bf[half:, :], preferred_element_type=F32))
        hid = (_silu(g) * u).astype(BF16)
        _store_packed_rows(y_ref, jnp.dot(hid, wd_bf[...], preferred_element_type=F32))

    @pl.when(tv_ref[i] == 0)
    def _():
        y_ref[...] = jnp.zeros_like(y_ref)


def _moe(tile_expert, tile_valid, tile_next_expert, xs, wg, wu, wd):
    nt = tile_expert.shape[0]
    tm = MOE_TM
    D = D_MODEL
    any_spec = pl.BlockSpec(memory_space=pl.ANY)
    grid_spec = pltpu.PrefetchScalarGridSpec(
        num_scalar_prefetch=3,
        grid=(nt,),
        in_specs=[pl.BlockSpec((tm * ROW_SLAB, 128), lambda i, *_: (i, 0)),
                  any_spec, any_spec, any_spec],
        out_specs=pl.BlockSpec((tm * ROW_SLAB, 128), lambda i, *_: (i, 0)),
        scratch_shapes=[pltpu.VMEM((D, D_EXPERT), BF16), pltpu.VMEM((D, D_EXPERT), BF16),
                        pltpu.VMEM((D_EXPERT, D), BF16),
                        pltpu.VMEM((D, D_EXPERT), F32), pltpu.VMEM((D, D_EXPERT), F32),
                        pltpu.VMEM((D_EXPERT, D), F32),
                        pltpu.SemaphoreType.DMA((3,))],
    )
    return pl.pallas_call(
        _moe_kernel,
        grid_spec=grid_spec,
        out_shape=jax.ShapeDtypeStruct((nt * tm * ROW_SLAB, 128), U32),
        compiler_params=pltpu.CompilerParams(
            dimension_semantics=("arbitrary",), vmem_limit_bytes=VMEM_LIMIT),
        name="moe",
    )(tile_expert, tile_valid, tile_next_expert, xs, wg, wu, wd)


def _combine_kernel(pos_ref, posn_ref, ys_hbm, x1_ref, g2_ref, w_ref, fg_ref, o_ref, rbuf, sem):
    i = pl.program_id(0)
    n = pl.num_programs(0)
    slot = lax.rem(i, 2)
    nslot = 1 - slot
    tm = COMB_TM

    @pl.when(i == 0)
    def _():
        _row_gather_start(ys_hbm, pos_ref, rbuf.at[0], sem.at[0], 2 * tm)

    @pl.when(i + 1 < n)
    def _():
        _row_gather_start(ys_hbm, posn_ref, rbuf.at[nslot], sem.at[nslot], 2 * tm)

    _row_gather_wait(ys_hbm, rbuf.at[slot], sem.at[slot], 2 * tm)
    w = w_ref[...]
    r1 = jnp.concatenate(_load_packed_rows(rbuf.at[slot], 0, tm), axis=1)
    r2 = jnp.concatenate(_load_packed_rows(rbuf.at[slot], tm, tm), axis=1)
    moe = w[:, 0:1] * r1 + w[:, 1:2] * r2
    y = x1_ref[...] + g2_ref[0] * moe
    ms = jnp.mean(y * y, axis=-1, keepdims=True)
    o_ref[...] = y * lax.rsqrt(ms + NORM_EPS) * fg_ref[...]


def _combine(pos_tiles, ys, x1, g2, w_tok, final_g, seq_len):
    T, D = x1.shape
    tm = COMB_TM
    nt = T // tm
    per_b = seq_len // tm
    return pl.pallas_call(
        _combine_kernel,
        grid=(nt,),
        in_specs=[pl.BlockSpec((1, 1, 2 * tm), lambda i: (i, 0, 0), memory_space=pltpu.SMEM),
                  pl.BlockSpec((1, 1, 2 * tm), lambda i: (jnp.minimum(i + 1, nt - 1), 0, 0),
                               memory_space=pltpu.SMEM),
                  pl.BlockSpec(memory_space=pl.ANY),
                  pl.BlockSpec((tm, D), lambda i: (i, 0)),
                  pl.BlockSpec((1, 1, D), lambda i: (i // per_b, 0, 0)),
                  pl.BlockSpec((tm, 2), lambda i: (i, 0)),
                  pl.BlockSpec((1, D), lambda i: (0, 0))],
        out_specs=pl.BlockSpec((tm, D), lambda i: (i, 0)),
        out_shape=jax.ShapeDtypeStruct((T, D), F32),
        scratch_shapes=[pltpu.VMEM((2, 2 * tm * ROW_SLAB, 128), U32),
                        pltpu.SemaphoreType.DMA((2,))],
        compiler_params=pltpu.CompilerParams(
            dimension_semantics=("arbitrary",), vmem_limit_bytes=VMEM_LIMIT),
        name="combine",
    )(pos_tiles, pos_tiles, ys, x1, g2, w_tok, final_g)


def _rel_bucket_table():
    n = jnp.arange(BIAS_LUT, dtype=I32)
    max_exact = N_BUCKETS // 2
    nf = jnp.maximum(n, 1).astype(F32)
    large = max_exact + (jnp.log(nf / max_exact) / math.log(MAX_DISTANCE / max_exact)
                         * (N_BUCKETS - max_exact)).astype(I32)
    large = jnp.minimum(large, N_BUCKETS - 1)
    return jnp.where(n < max_exact, n, large)


def _route_plan(eid, n_tiles, tm):
    two, T = eid.shape
    e_flat = eid.reshape(-1)
    onehot = (e_flat[:, None] == jnp.arange(N_EXPERTS, dtype=I32)[None, :]).astype(I32)
    csum = jnp.cumsum(onehot, axis=0)
    rank = jnp.sum((csum - onehot) * onehot, axis=1)
    counts = csum[-1]
    ptiles = (counts + tm - 1) // tm
    tend = jnp.cumsum(ptiles)
    tstart = tend - ptiles
    slot = jnp.sum(onehot * tstart[None, :], axis=1) * tm + rank
    total = tend[-1]
    tile_ids = jnp.arange(n_tiles, dtype=I32)
    tile_valid = (tile_ids < total).astype(I32)
    tile_expert = jnp.sum((tile_ids[:, None] >= tend[None, :]).astype(I32), axis=1)
    last_expert = jnp.sum((total - 1 >= tend).astype(I32))
    tile_expert = jnp.minimum(tile_expert, last_expert).astype(I32)
    partial_last = jnp.any((tile_ids[:, None] == tend[None, :] - 1)
                           & (ptiles[None, :] > 0) & (counts[None, :] % tm != 0), axis=1)
    tile_fill = (partial_last | (tile_ids >= total)).astype(I32)
    experts = jnp.arange(N_EXPERTS, dtype=I32)
    later = (experts[None, :] > tile_expert[:, None]) & (ptiles[None, :] > 0)
    tile_next_expert = jnp.min(jnp.where(later, experts[None, :], N_EXPERTS), axis=1)
    tile_next_expert = jnp.where(tile_next_expert == N_EXPERTS, -1, tile_next_expert).astype(I32)
    return slot.astype(I32), tile_expert, tile_valid, tile_fill, tile_next_expert


def kernel(x, c, positions, rel_bias, ada_w, ada_b, norm1_g, w_in, lambda_q1, lambda_k1, lambda_q2,
           lambda_k2, subln_g, conv_w, w_out, norm2_g, router_group_w, router_group_b,
           router_expert_w, router_expert_b, expert_w_gate, expert_w_up, expert_w_down, final_g):
    B, S, D = x.shape
    T = B * S
    l = 0

    ada = _ada(c.reshape(B, D, 1), ada_w[l], ada_b[l].reshape(1, -1))
    sh1, sc1, g1, sh2, sc2, g2 = jnp.split(ada, 6, axis=-1)

    qt, k, vt, conv = _inproj(x, sc1, sh1, norm1_g[l].reshape(1, D), w_in[l], conv_w[l])
    lut = (rel_bias.astype(F32)[_rel_bucket_table(), :].T * LOG2E).reshape(N_DIFF_HEADS, 2, BIAS_LUT)
    lam_params = jnp.stack([lambda_q1[l], lambda_k1[l], lambda_q2[l], lambda_k2[l]]).astype(F32)
    attn = _attention(qt, k, vt, positions, lut, lam_params, subln_g[l].reshape(V_HEAD_DIM, 1))

    wr_t = jnp.zeros((ROUTER_ROWS, D), F32)
    wr_t = wr_t.at[0:N_GROUPS].set(router_group_w[l].T)
    wr_t = wr_t.at[N_GROUPS:N_GROUPS + N_EXPERTS].set(router_expert_w[l].T).astype(BF16)
    rb = jnp.zeros((ROUTER_ROWS, 1), F32)
    rb = rb.at[0:N_GROUPS, 0].set(router_group_b[l])
    rb = rb.at[N_GROUPS:N_GROUPS + N_EXPERTS, 0].set(router_expert_b[l])
    x1, hp, ri, rw = _outproj(attn, conv, x, g1, sc2, sh2, norm2_g[l].reshape(1, D),
                              w_out[l].astype(BF16), wr_t, rb)

    eid = ri[:, 0:2, :].transpose(1, 0, 2).reshape(2, T)
    n_tiles = 2 * T // MOE_TM + N_EXPERTS
    slot, tile_expert, tile_valid, tile_fill, tile_next = _route_plan(eid, n_tiles, MOE_TM)
    nct = T // COMB_TM
    pos = (slot * ROW_SLAB).reshape(2, nct, 1, COMB_TM)
    pos_tiles = jnp.concatenate([pos[0], pos[1]], axis=2)

    xs = _dispatch(tile_fill, pos_tiles, hp.reshape(T * ROW_SLAB, 128), n_tiles)
    ys = _moe(tile_expert, tile_valid, tile_next, xs, expert_w_gate[l], expert_w_up[l],
              expert_w_down[l])

    w_tok = rw[:, 0:2, :].transpose(0, 2, 1).reshape(T, 2)
    out = _combine(pos_tiles, ys, x1.reshape(T, D), g2, w_tok, final_g.reshape(1, D), S)
    return out.reshape(B, S, D)
```

```python
import functools
import math

import jax
import jax.numpy as jnp
from jax import lax
from jax.experimental import pallas as pl
from jax.experimental.pallas import tpu as pltpu

F32 = jnp.float32
BF16 = jnp.bfloat16
I32 = jnp.int32
U32 = jnp.uint32

D_MODEL = 1024
ATTN_WIDTH = 512
CONV_WIDTH = 512
N_DIFF_HEADS = 4
DIFF_HEAD_DIM = 64
V_HEAD_DIM = 128
IN_PROJ_WIDTH = 3 * ATTN_WIDTH + 3 * CONV_WIDTH
CONV_K = 3
N_BUCKETS = 32
MAX_DISTANCE = 128
N_GROUPS = 4
EXPERTS_PER_GROUP = 4
N_EXPERTS = 16
D_EXPERT = 512
NORM_EPS = 1e-6
SUBLN_EPS = 1e-5
NEG_INF = -1e30
LAMBDA_INIT = 0.8 - 0.6 * math.exp(-0.3 * 0)
QK_SCALE = DIFF_HEAD_DIM ** -0.5
LOG2E = math.log2(math.e)

BIAS_LUT = 128

ROW_TILE = 512
PROJ_SUBTILES = 2
OUT_TILE = 1024
OUT_SUBTILES = 8
ATT_TQ = 1024
ATT_CHAIN = 256
ATT_GROUP = 8
ATT_LOOKAHEAD = 8
ATT_TK = 256
MOE_TM = 256
COMB_TM = 256
ROUTER_ROWS = 32
ROW_SLAB = D_MODEL // 256
VMEM_LIMIT = 56 * 1024 * 1024


def _silu(x):
    return x * (1.0 / (1.0 + jnp.exp(-x)))


def _ada_kernel(c_ref, w_ref, b_ref, o_ref):
    for bi in range(c_ref.shape[0]):
        s = _silu(c_ref[bi])
        o_ref[bi] = jnp.sum(s * w_ref[...], axis=0, keepdims=True) + b_ref[...]


def _ada(c_col, w, b):
    nb, n = c_col.shape[0], w.shape[1]
    bn = 1024
    return pl.pallas_call(
        _ada_kernel,
        grid=(n // bn,),
        in_specs=[pl.BlockSpec((nb, D_MODEL, 1), lambda j: (0, 0, 0)),
                  pl.BlockSpec((D_MODEL, bn), lambda j: (0, j)),
                  pl.BlockSpec((1, bn), lambda j: (0, j))],
        out_specs=pl.BlockSpec((nb, 1, bn), lambda j: (0, 0, j)),
        out_shape=jax.ShapeDtypeStruct((nb, 1, n), F32),
        name="ada",
    )(c_col, w, b)


def _inproj_kernel(x_ref, sc_ref, sh_ref, g_ref, w_hbm, cw_ref,
                   qt_ref, k_ref, vt_ref, conv_ref,
                   carry_ref, wqt_ref, wk_ref, wvt_ref, wc_ref, stage_ref, sem):
    j = pl.program_id(1)
    tm = x_ref.shape[1]
    nt = (((1,), (1,)), ((), ()))

    @pl.when(jnp.logical_and(pl.program_id(0) == 0, j == 0))
    def _():
        for c in range(IN_PROJ_WIDTH // 512):
            cp = pltpu.make_async_copy(w_hbm.at[:, pl.ds(c * 512, 512)], stage_ref, sem.at[0])
            cp.start()
            cp.wait()
            if c == 0:
                wqt_ref[...] = stage_ref[...].T.astype(BF16)
            elif c == 1:
                wk_ref[...] = stage_ref[...].astype(BF16)
            elif c == 2:
                wvt_ref[...] = stage_ref[...].T.astype(BF16)
            else:
                wc_ref[:, (c - 3) * 512:(c - 2) * 512] = stage_ref[...].astype(BF16)

    @pl.when(j == 0)
    def _():
        carry_ref[...] = jnp.zeros_like(carry_ref)

    prev = carry_ref[...]
    sub = tm // PROJ_SUBTILES
    for s in range(PROJ_SUBTILES):
        rows = slice(s * sub, (s + 1) * sub)
        x = x_ref[0, rows, :]
        ms = jnp.mean(x * x, axis=-1, keepdims=True)
        h = x * lax.rsqrt(ms + NORM_EPS) * g_ref[...]
        h = h * (1.0 + sc_ref[0]) + sh_ref[0]
        hb = h.astype(BF16)

        def proj(c0):
            return jnp.dot(hb, wc_ref[:, c0:c0 + 512], preferred_element_type=F32)

        qt = lax.dot_general(wqt_ref[...], hb, nt, preferred_element_type=F32)
        qt_ref[0, :, rows] = (qt * (QK_SCALE * LOG2E)).astype(BF16)
        k_ref[0, rows, :] = jnp.dot(hb, wk_ref[...], preferred_element_type=F32).astype(BF16)
        vt_ref[0, :, rows] = lax.dot_general(wvt_ref[...], hb, nt,
                                             preferred_element_type=F32).astype(BF16)
        gate_b = proj(0)
        u = proj(512) * proj(1024)
        row = lax.broadcasted_iota(I32, u.shape, 0)
        u1 = pltpu.roll(u, 1, axis=0)
        u2 = pltpu.roll(u, 2, axis=0)
        u1 = jnp.where(row == 0, prev[7:8, :], u1)
        u2 = jnp.where(row == 0, prev[6:7, :], jnp.where(row == 1, prev[7:8, :], u2))
        conv = cw_ref[0:1, :] * u2 + cw_ref[1:2, :] * u1 + cw_ref[2:3, :] * u
        conv_ref[0, rows, :] = (gate_b * conv).astype(BF16)
        prev = u[sub - 8:sub, :]
    carry_ref[...] = prev


def _inproj(x, sc1, sh1, g1n, w_in, conv_w):
    B, S, D = x.shape
    tm = ROW_TILE
    row_out = jax.ShapeDtypeStruct((B, S, 512), BF16)
    col_out = jax.ShapeDtypeStruct((B, 512, S), BF16)
    row_spec = pl.BlockSpec((1, tm, 512), lambda b, j: (b, j, 0))
    col_spec = pl.BlockSpec((1, 512, tm), lambda b, j: (b, 0, j))
    mod_spec = pl.BlockSpec((1, 1, D), lambda b, j: (b, 0, 0))
    const2 = lambda b, j: (0, 0)
    return pl.pallas_call(
        _inproj_kernel,
        grid=(B, S // tm),
        in_specs=[pl.BlockSpec((1, tm, D), lambda b, j: (b, j, 0)),
                  mod_spec, mod_spec,
                  pl.BlockSpec((1, D), const2),
                  pl.BlockSpec(memory_space=pl.ANY),
                  pl.BlockSpec((CONV_K, CONV_WIDTH), const2)],
        out_specs=[col_spec, row_spec, col_spec, row_spec],
        out_shape=[col_out, row_out, col_out, row_out],
        scratch_shapes=[pltpu.VMEM((8, CONV_WIDTH), F32),
                        pltpu.VMEM((ATTN_WIDTH, D), BF16),
                        pltpu.VMEM((D, ATTN_WIDTH), BF16),
                        pltpu.VMEM((ATTN_WIDTH, D), BF16),
                        pltpu.VMEM((D, 3 * CONV_WIDTH), BF16),
                        pltpu.VMEM((D, 512), F32),
                        pltpu.SemaphoreType.DMA((1,))],
        compiler_params=pltpu.CompilerParams(
            dimension_semantics=("arbitrary", "arbitrary"), vmem_limit_bytes=VMEM_LIMIT),
        name="inproj",
    )(x, sc1, sh1, g1n, w_in, conv_w)


def _attn_kernel(qmin_ref, kmax_ref, consec_ref, qt_ref, k_ref, vt_ref, pr_ref, pr_all_ref, lut_ref,
                 lam_ref, sg_ref, o_ref, acc_ref, tz_ref, *, n_q_tiles):
    b = pl.program_id(0)
    qi = pl.program_id(2)
    nq = pl.num_programs(2)
    tq, tk = ATT_TQ, ATT_TK
    hw = ATT_CHAIN
    n_half = tq // hw
    nk = nq * (tq // tk)
    qt = qt_ref[0]
    feat = lax.broadcasted_iota(I32, qt.shape, 0)
    zero = jnp.zeros_like(qt)
    qts = (jnp.where(feat < DIFF_HEAD_DIM, qt, zero), jnp.where(feat >= DIFF_HEAD_DIM, qt, zero))
    luts = (lut_ref[0, 0:1, :], lut_ref[0, 1:2, :])
    fars = tuple(t[:, BIAS_LUT - 1:BIAS_LUT] for t in luts)
    pq = pr_ref[0]
    qmin = qmin_ref[b * nq + qi]
    czero = jnp.zeros((1, 1), F32)
    ones_rows = jnp.ones((16, tk), BF16)
    chains = [(mi, hi) for mi in range(2) for hi in range(n_half)]

    def gather_bias(mi, dist):
        table = jnp.broadcast_to(luts[mi], (tk, BIAS_LUT))
        return jnp.concatenate([jnp.take_along_axis(table, dist[:, o:o + 128], axis=1)
                                for o in range(0, hw, 128)], axis=1)

    @pl.when(qi == 0)
    def _():
        delta = (lax.broadcasted_iota(I32, (tk, hw), 1) - lax.broadcasted_iota(I32, (tk, hw), 0))
        for mi in range(2):
            diag_bias = gather_bias(mi, jnp.clip(delta, 0, BIAS_LUT - 1))
            tz_ref[mi, 0] = jnp.where(delta >= 0, diag_bias, NEG_INF)
            tz_ref[mi, 1] = gather_bias(mi, jnp.clip(delta + tk, 0, BIAS_LUT - 1))

    def run_blocks(blocks, state):
        loaded = []
        for (j, kinds) in blocks:
            ks = pl.multiple_of(j * tk, tk)
            kb = k_ref[0, pl.ds(ks, tk), :]
            vtb = jnp.concatenate([vt_ref[0, :, pl.ds(ks, tk)], ones_rows], axis=0)
            dist = None
            if any(kd is not None and kd.startswith("near") for kd in kinds):
                pk_rows = jnp.broadcast_to(pr_all_ref[0, :, pl.ds(ks, tk)], (8, tk))
                pk = pk_rows.T[:, 0:1]
                dist = jnp.clip(pq - pk, 0, BIAS_LUT - 1)
            loaded.append((kb, vtb, dist))
        items = [(bi, n) for bi, blk in enumerate(blocks) for n, (mi, hi) in enumerate(chains)
                 if blk[1][hi] is not None]
        scores = {}

        def issue_qk(t):
            bi, n = items[t]
            mi, hi = chains[n]
            scores[t] = jnp.dot(loaded[bi][0], qts[mi][:, hi * hw:(hi + 1) * hw],
                                preferred_element_type=F32)

        state = list(state)
        for t in range(min(ATT_LOOKAHEAD, len(items))):
            issue_qk(t)
        for t, (bi, n) in enumerate(items):
            if t + ATT_LOOKAHEAD < len(items):
                issue_qk(t + ATT_LOOKAHEAD)
            _, vtb, dist = loaded[bi]
            mi, hi = chains[n]
            kind = blocks[bi][1][hi]
            cols = slice(hi * hw, (hi + 1) * hw)
            m, l = state[n]
            s = scores.pop(t)
            c = czero
            if kind == "far":
                c = fars[mi]
            elif kind == "tz_diag":
                s = tz_ref[mi, 0] + s
            elif kind == "tz_sub":
                s = tz_ref[mi, 1] + s
            else:
                s = gather_bias(mi, dist[:, cols]) + s
                if kind == "near_masked":
                    keep = (lax.broadcasted_iota(I32, (tk, hw), 0)
                            <= lax.broadcasted_iota(I32, (tk, hw), 1))
                    s = jnp.where(keep, s, NEG_INF)
            mn = jnp.maximum(m, jnp.max(s, axis=0, keepdims=True) + c)
            alpha = jnp.exp2(m - mn)
            p = jnp.exp2(s - (mn - c))
            pv = jnp.dot(vtb, p.astype(BF16), preferred_element_type=F32)
            l = alpha * l + pv[V_HEAD_DIM:V_HEAD_DIM + 1, :]
            acc_ref[mi, :, cols] = alpha * acc_ref[mi, :, cols] + pv[:V_HEAD_DIM, :]
            state[n] = (mn, l)
        return tuple(state)

    def block_is_far(j):
        return qmin - kmax_ref[b * nk + j] >= BIAS_LUT - 1

    def one_block(j, state):
        return lax.cond(block_is_far(j), lambda st: run_blocks([(j, ("far",) * n_half)], st),
                        lambda st: run_blocks([(j, ("near",) * n_half)], st), state)

    def group_body(width):
        def body(g, carry):
            j0, state = carry
            all_far = block_is_far(j0)
            for u in range(1, width):
                all_far = jnp.logical_and(all_far, block_is_far(j0 + u))
            far_blocks = [(j0 + u, ("far",) * n_half) for u in range(width)]
            state = lax.cond(
                all_far, lambda st: run_blocks(far_blocks, st),
                lambda st: lax.fori_loop(0, width, lambda u, s2: one_block(j0 + u, s2), st), state)
            return j0 + width, state
        return body

    acc_ref[...] = jnp.zeros_like(acc_ref)
    m0 = jnp.full((1, hw), NEG_INF, F32)
    l0 = jnp.zeros((1, hw), F32)
    state = tuple((m0, l0) for _ in chains)
    assert hw == tk and n_half % 2 == 0
    n_full = n_half * qi
    n_main = jnp.maximum(n_full - 2, 0)
    n_groups = n_main // ATT_GROUP
    j0, state = lax.fori_loop(0, n_groups, group_body(ATT_GROUP), (jnp.int32(0), state))
    rem = n_main - n_groups * ATT_GROUP

    consec = consec_ref[b * nq + qi] == 1

    def diag_kinds(d, on_diag, below, further):
        return tuple(None if hi < d else on_diag if hi == d else below if hi == d + 1 else further
                     for hi in range(n_half))

    diag_fast = [(n_full + d, diag_kinds(d, "tz_diag", "tz_sub", "far")) for d in range(n_half)]
    diag_any = [(n_full + d, diag_kinds(d, "near_masked", "near", "near")) for d in range(n_half)]
    below_fast = [(n_full - 2, ("far",) * n_half),
                  (n_full - 1, ("tz_sub",) + ("far",) * (n_half - 1))]
    below_any = [(n_full - 2, ("near",) * n_half), (n_full - 1, ("near",) * n_half)]

    def first_tile(st):
        return lax.cond(consec, lambda s2: run_blocks(diag_fast, s2),
                        lambda s2: run_blocks(diag_any, s2), st)

    def slow_rest(st):
        st = lax.fori_loop(0, rem, lambda u, s2: one_block(j0 + u, s2), st)
        return lax.cond(consec, lambda s2: run_blocks(below_fast + diag_fast, s2),
                        lambda s2: run_blocks(below_any + diag_any, s2), st)

    def fast_rest(r):
        left = [(j0 + u, ("far",) * n_half) for u in range(r)]
        return lambda st: run_blocks(left + below_fast + diag_fast, st)

    left_counts = sorted({(n_half * q - 2) % ATT_GROUP for q in range(1, n_q_tiles)})
    left_far = consec
    for u in range(max(left_counts, default=0)):
        far_u = block_is_far(jnp.minimum(j0 + u, nk - 1))
        left_far = jnp.logical_and(left_far, jnp.logical_or(u >= rem, far_u))

    def later_tile(st):
        out = slow_rest
        for r in left_counts:
            out = (lambda r, nxt: lambda s2: lax.cond(
                jnp.logical_and(left_far, rem == r), fast_rest(r), nxt, s2))(r, out)
        return out(st)

    state = lax.cond(qi > 0, later_tile, first_tile, state)

    l1 = jnp.concatenate([state[n][1] for n, (mi, hi) in enumerate(chains) if mi == 0], axis=1)
    l2 = jnp.concatenate([state[n][1] for n, (mi, hi) in enumerate(chains) if mi == 1], axis=1)
    lam = (jnp.exp(jnp.sum(lam_ref[0:1, :] * lam_ref[1:2, :], axis=-1, keepdims=True))
           - jnp.exp(jnp.sum(lam_ref[2:3, :] * lam_ref[3:4, :], axis=-1, keepdims=True))
           + LAMBDA_INIT)
    ot = acc_ref[0] * (1.0 / l1) - (lam * (1.0 / l2)) * acc_ref[1]
    ot = ot * lax.rsqrt(jnp.mean(ot * ot, axis=0, keepdims=True) + SUBLN_EPS)
    ot = ot * (sg_ref[...] * (1.0 - LAMBDA_INIT))
    o_ref[0] = ot.T.astype(BF16)


def _attention(qt, k, vt, positions, lut, lam_params, subln_g_col):
    B, S, _ = k.shape
    tq = ATT_TQ
    nq = S // tq
    pos_row = positions.reshape(B, 1, S)
    qmin = jnp.min(positions.reshape(B * nq, tq), axis=1)
    kmax = jnp.max(positions.reshape(B * (S // ATT_TK), ATT_TK), axis=1)
    step_ok = jnp.concatenate([positions[:, 1:] - positions[:, :-1] == 1,
                               jnp.ones((B, 1), jnp.bool_)], axis=1).reshape(B, nq, tq)
    inner_ok = jnp.all(step_ok[:, :, :tq - 1], axis=2)
    link_ok = jnp.concatenate([jnp.ones((B, 1), jnp.bool_), step_ok[:, :-1, tq - 1]], axis=1)
    prev_ok = jnp.concatenate([jnp.ones((B, 1), jnp.bool_), inner_ok[:, :-1]], axis=1)
    consec = (inner_ok & link_ok & prev_ok).astype(I32).reshape(B * nq)
    grid_spec = pltpu.PrefetchScalarGridSpec(
        num_scalar_prefetch=3,
        grid=(B, N_DIFF_HEADS, nq),
        in_specs=[pl.BlockSpec((1, 128, tq), lambda b, h, i, *_: (b, h, i)),
                  pl.BlockSpec((1, S, 128), lambda b, h, i, *_: (b, 0, h)),
                  pl.BlockSpec((1, 128, S), lambda b, h, i, *_: (b, h, 0)),
                  pl.BlockSpec((1, 1, tq), lambda b, h, i, *_: (b, 0, i)),
                  pl.BlockSpec((1, 1, S), lambda b, h, i, *_: (b, 0, 0)),
                  pl.BlockSpec((1, 2, BIAS_LUT), lambda b, h, i, *_: (h, 0, 0)),
                  pl.BlockSpec((4, DIFF_HEAD_DIM), lambda b, h, i, *_: (0, 0)),
                  pl.BlockSpec((V_HEAD_DIM, 1), lambda b, h, i, *_: (0, 0))],
        out_specs=pl.BlockSpec((1, tq, 128), lambda b, h, i, *_: (b, i, h)),
        scratch_shapes=[pltpu.VMEM((2, V_HEAD_DIM, tq), F32),
                        pltpu.VMEM((2, 2, ATT_TK, ATT_CHAIN), F32)],
    )
    return pl.pallas_call(
        functools.partial(_attn_kernel, n_q_tiles=nq),
        grid_spec=grid_spec,
        out_shape=jax.ShapeDtypeStruct((B, S, ATTN_WIDTH), BF16),
        compiler_params=pltpu.CompilerParams(
            dimension_semantics=("arbitrary", "arbitrary", "arbitrary"),
            vmem_limit_bytes=VMEM_LIMIT),
        name="diffattn",
    )(qmin, kmax, consec, qt, k, vt, pos_row, pos_row, lut, lam_params, subln_g_col)


def _outproj_kernel(at_ref, cv_ref, x_ref, g1_ref, sc_ref, sh_ref, gn_ref, wo_ref, wr_ref, rb_ref,
                    x1_ref, hp_ref, ri_ref, rw_ref):
    sub = x_ref.shape[1] // OUT_SUBTILES
    groups = [slice(s * sub, (s + 1) * sub) for s in range(OUT_SUBTILES)]
    mixes = [jnp.dot(at_ref[0, rows, :], wo_ref[0:ATTN_WIDTH, :], preferred_element_type=F32)
             + jnp.dot(cv_ref[0, rows, :], wo_ref[ATTN_WIDTH:, :], preferred_element_type=F32)
             for rows in groups]
    for rows, mix in zip(groups, mixes):
        _outproj_rows(rows, mix, x_ref, g1_ref, sc_ref, sh_ref, gn_ref, wr_ref, rb_ref,
                      x1_ref, hp_ref, ri_ref, rw_ref)


def _outproj_rows(rows, mix, x_ref, g1_ref, sc_ref, sh_ref, gn_ref, wr_ref, rb_ref,
                  x1_ref, hp_ref, ri_ref, rw_ref):
    tm = rows.stop - rows.start
    x1 = x_ref[0, rows, :] + g1_ref[0] * mix
    x1_ref[0, rows, :] = x1
    ms = jnp.mean(x1 * x1, axis=-1, keepdims=True)
    h = x1 * lax.rsqrt(ms + NORM_EPS) * gn_ref[...]
    h = h * (1.0 + sc_ref[0]) + sh_ref[0]
    hb = h.astype(BF16)

    _store_packed_rows(hp_ref.at[0, pl.ds(rows.start * ROW_SLAB, tm * ROW_SLAB)], h)

    lg_all = lax.dot_general(wr_ref[...], hb, (((1,), (1,)), ((), ())),
                             preferred_element_type=F32) + rb_ref[...]
    lg = lg_all[0:N_GROUPS, :]
    le = lg_all[N_GROUPS:N_GROUPS + N_EXPERTS, :]
    row4 = lax.broadcasted_iota(I32, (N_GROUPS, tm), 0)
    gmax = jnp.max(lg, axis=0, keepdims=True)
    pg_sel = 1.0 / jnp.sum(jnp.exp(lg - gmax), axis=0, keepdims=True)
    gsel = jnp.min(jnp.where(lg == gmax, row4, N_GROUPS), axis=0, keepdims=True)
    sel = jnp.zeros((EXPERTS_PER_GROUP, tm), F32)
    for g in range(N_GROUPS):
        sel = jnp.where(gsel == g, le[g * EXPERTS_PER_GROUP:(g + 1) * EXPERTS_PER_GROUP, :], sel)
    v1 = jnp.max(sel, axis=0, keepdims=True)
    i1 = jnp.min(jnp.where(sel == v1, row4, EXPERTS_PER_GROUP), axis=0, keepdims=True)
    rest = jnp.where(row4 == i1, -jnp.inf, sel)
    v2 = jnp.max(rest, axis=0, keepdims=True)
    i2 = jnp.min(jnp.where(rest == v2, row4, EXPERTS_PER_GROUP), axis=0, keepdims=True)
    e2 = jnp.exp(v2 - v1)
    w1 = pg_sel / (1.0 + e2)
    w2 = pg_sel * e2 / (1.0 + e2)
    row8 = lax.broadcasted_iota(I32, (8, tm), 0)
    eid1 = gsel * EXPERTS_PER_GROUP + i1
    eid2 = gsel * EXPERTS_PER_GROUP + i2
    ri_ref[0, :, rows] = jnp.where(row8 == 0, eid1, jnp.where(row8 == 1, eid2, 0))
    rw_ref[0, :, rows] = jnp.where(row8 == 0, w1, jnp.where(row8 == 1, w2, 0.0))


def _outproj(attn, conv, x, g1, sc2, sh2, g2n, w_out_bf, wr_t, rb):
    B, S, D = x.shape
    tm = OUT_TILE
    half_spec = pl.BlockSpec((1, tm, 512), lambda b, j: (b, j, 0))
    full_spec = pl.BlockSpec((1, tm, D), lambda b, j: (b, j, 0))
    mod_spec = pl.BlockSpec((1, 1, D), lambda b, j: (b, 0, 0))
    rt_spec = pl.BlockSpec((1, 8, tm), lambda b, j: (b, 0, j))
    return pl.pallas_call(
        _outproj_kernel,
        grid=(B, S // tm),
        in_specs=[half_spec, half_spec, full_spec, mod_spec, mod_spec, mod_spec,
                  pl.BlockSpec((1, D), lambda b, j: (0, 0)),
                  pl.BlockSpec((D, D), lambda b, j: (0, 0)),
                  pl.BlockSpec((ROUTER_ROWS, D), lambda b, j: (0, 0)),
                  pl.BlockSpec((ROUTER_ROWS, 1), lambda b, j: (0, 0))],
        out_specs=[full_spec, pl.BlockSpec((1, tm * ROW_SLAB, 128), lambda b, j: (b, j, 0)),
                   rt_spec, rt_spec],
        out_shape=[jax.ShapeDtypeStruct((B, S, D), F32),
                   jax.ShapeDtypeStruct((B, S * ROW_SLAB, 128), U32),
                   jax.ShapeDtypeStruct((B, 8, S), I32),
                   jax.ShapeDtypeStruct((B, 8, S), F32)],
        compiler_params=pltpu.CompilerParams(
            dimension_semantics=("arbitrary", "arbitrary"), vmem_limit_bytes=VMEM_LIMIT),
        name="outproj",
    )(attn, conv, x, g1, sc2, sh2, g2n, w_out_bf, wr_t, rb)


def _store_packed_rows(dst, x):
    half = D_MODEL // 2
    xb = x.astype(BF16).astype(F32)
    packed = (pltpu.bitcast(xb[:, :half], U32) >> 16) | (
        pltpu.bitcast(xb[:, half:], U32) & jnp.uint32(0xFFFF0000))
    rows = x.shape[0]
    for c in range(ROW_SLAB):
        dst[pl.ds(c, rows, stride=ROW_SLAB), :] = packed[:, c * 128:(c + 1) * 128]


def _load_packed_rows(src, row0, rows):
    packed = jnp.concatenate(
        [src[pl.ds(row0 * ROW_SLAB + c, rows, stride=ROW_SLAB), :] for c in range(ROW_SLAB)], axis=1)
    lo = pltpu.bitcast(packed << 16, F32)
    hi = pltpu.bitcast(packed & jnp.uint32(0xFFFF0000), F32)
    return lo, hi


def _row_gather_start(src_hbm, idx_ref, dst, sem, n_rows):
    for r in range(n_rows):
        off = pl.multiple_of(idx_ref[0, 0, r], ROW_SLAB)
        pltpu.make_async_copy(src_hbm.at[pl.ds(off, ROW_SLAB)],
                              dst.at[pl.ds(r * ROW_SLAB, ROW_SLAB)], sem).start(priority=r % 2)


def _row_gather_wait(src_hbm, dst, sem, n_rows):
    pltpu.make_async_copy(src_hbm.at[pl.ds(0, n_rows * ROW_SLAB)], dst, sem).wait()


def _dispatch_kernel(fill_ref, pos_ref, hp_ref, xs_hbm, zbuf, sem):
    i = pl.program_id(0)
    tm = COMB_TM
    tile_rows = MOE_TM * ROW_SLAB

    @pl.when(i == 0)
    def _():
        zbuf[...] = jnp.zeros_like(zbuf)

        def fill_copy(t):
            return pltpu.make_async_copy(
                zbuf, xs_hbm.at[pl.ds(pl.multiple_of(t * tile_rows, tile_rows), tile_rows)],
                sem.at[0])

        def start(t, c):
            @pl.when(fill_ref[t] == 1)
            def _():
                fill_copy(t).start()
            return c

        def wait(t, c):
            @pl.when(fill_ref[t] == 1)
            def _():
                fill_copy(t).wait()
            return c

        lax.fori_loop(0, fill_ref.shape[0], start, 0)
        lax.fori_loop(0, fill_ref.shape[0], wait, 0)

    par = lax.rem(i, 2)
    base = pl.multiple_of(i * (tm * ROW_SLAB), tm * ROW_SLAB)
    for r in range(2 * tm):
        off = pl.multiple_of(pos_ref[0, 0, r], ROW_SLAB)
        pltpu.make_async_copy(hp_ref.at[pl.ds(base + (r % tm) * ROW_SLAB, ROW_SLAB)],
                              xs_hbm.at[pl.ds(off, ROW_SLAB)],
                              sem.at[1 + par]).start(priority=r % 2)

    def drain(parity):
        for _ in range(2):
            pltpu.make_async_copy(hp_ref.at[pl.ds(0, tm * ROW_SLAB)],
                                  xs_hbm.at[pl.ds(0, tm * ROW_SLAB)], sem.at[1 + parity]).wait()

    @pl.when(i > 0)
    def _():
        drain(1 - par)

    @pl.when(i == pl.num_programs(0) - 1)
    def _():
        drain(par)


def _dispatch(tile_fill, pos_tiles, hp, n_tiles):
    T = hp.shape[0] // ROW_SLAB
    tm = COMB_TM
    grid_spec = pltpu.PrefetchScalarGridSpec(
        num_scalar_prefetch=1,
        grid=(T // tm,),
        in_specs=[pl.BlockSpec((1, 1, 2 * tm), lambda i, f: (i, 0, 0), memory_space=pltpu.SMEM),
                  pl.BlockSpec(memory_space=pltpu.VMEM)],
        out_specs=pl.BlockSpec(memory_space=pl.ANY),
        scratch_shapes=[pltpu.VMEM((MOE_TM * ROW_SLAB, 128), U32),
                        pltpu.SemaphoreType.DMA((3,))],
    )
    return pl.pallas_call(
        _dispatch_kernel,
        grid_spec=grid_spec,
        out_shape=jax.ShapeDtypeStruct((n_tiles * MOE_TM * ROW_SLAB, 128), U32),
        compiler_params=pltpu.CompilerParams(
            dimension_semantics=("arbitrary",), vmem_limit_bytes=VMEM_LIMIT),
        name="dispatch",
    )(tile_fill, pos_tiles, hp)


def _moe_kernel(te_ref, tv_ref, ne_ref, xs_ref, wg_hbm, wu_hbm, wd_hbm, y_ref,
                wg_bf, wu_bf, wd_bf, wg_st, wu_st, wd_st, sem):
    i = pl.program_id(0)
    tm = MOE_TM

    def weight_copies(e):
        return (pltpu.make_async_copy(wg_hbm.at[e], wg_st, sem.at[0]),
                pltpu.make_async_copy(wu_hbm.at[e], wu_st, sem.at[1]),
                pltpu.make_async_copy(wd_hbm.at[e], wd_st, sem.at[2]))

    @pl.when(i == 0)
    def _():
        for cp in weight_copies(te_ref[0]):
            cp.start()

    @pl.when(jnp.logical_or(i == 0, te_ref[i] != te_ref[jnp.maximum(i - 1, 0)]))
    def _():
        for cp in weight_copies(te_ref[i]):
            cp.wait()
        wg_bf[...] = wg_st[...].astype(BF16)
        wu_bf[...] = wu_st[...].astype(BF16)
        wd_bf[...] = wd_st[...].astype(BF16)

        @pl.when(ne_ref[i] >= 0)
        def _():
            for cp in weight_copies(ne_ref[i]):
                cp.start()

    @pl.when(tv_ref[i] == 1)
    def _():
        half = D_MODEL // 2
        lo, hi = _load_packed_rows(xs_ref, 0, tm)
        lo = lo.astype(BF16)
        hi = hi.astype(BF16)
        g = (jnp.dot(lo, wg_bf[0:half, :], preferred_element_type=F32)
             + jnp.dot(hi, wg_bf[half:, :], preferred_element_type=F32))
        u = (jnp.dot(lo, wu_bf[0:half, :], preferred_element_type=F32)
             + jnp.dot(hi, wu_bf[half:, :], preferred_element_type=F32))
        hid = (_silu(g) * u).astype(BF16)
        _store_packed_rows(y_ref, jnp.dot(hid, wd_bf[...], preferred_element_type=F32))

    @pl.when(tv_ref[i] == 0)
    def _():
        y_ref[...] = jnp.zeros_like(y_ref)


def _moe(tile_expert, tile_valid, tile_next_expert, xs, wg, wu, wd):
    nt = tile_expert.shape[0]
    tm = MOE_TM
    D = D_MODEL
    any_spec = pl.BlockSpec(memory_space=pl.ANY)
    grid_spec = pltpu.PrefetchScalarGridSpec(
        num_scalar_prefetch=3,
        grid=(nt,),
        in_specs=[pl.BlockSpec((tm * ROW_SLAB, 128), lambda i, *_: (i, 0)),
                  any_spec, any_spec, any_spec],
        out_specs=pl.BlockSpec((tm * ROW_SLAB, 128), lambda i, *_: (i, 0)),
        scratch_shapes=[pltpu.VMEM((D, D_EXPERT), BF16), pltpu.VMEM((D, D_EXPERT), BF16),
                        pltpu.VMEM((D_EXPERT, D), BF16),
                        pltpu.VMEM((D, D_EXPERT), F32), pltpu.VMEM((D, D_EXPERT), F32),
                        pltpu.VMEM((D_EXPERT, D), F32),
                        pltpu.SemaphoreType.DMA((3,))],
    )
    return pl.pallas_call(
        _moe_kernel,
        grid_spec=grid_spec,
        out_shape=jax.ShapeDtypeStruct((nt * tm * ROW_SLAB, 128), U32),
        compiler_params=pltpu.CompilerParams(
            dimension_semantics=("arbitrary",), vmem_limit_bytes=VMEM_LIMIT),
        name="moe",
    )(tile_expert, tile_valid, tile_next_expert, xs, wg, wu, wd)


def _combine_kernel(pos_ref, posn_ref, ys_hbm, x1_ref, g2_ref, w_ref, fg_ref, o_ref, rbuf, sem):
    i = pl.program_id(0)
    n = pl.num_programs(0)
    slot = lax.rem(i, 2)
    nslot = 1 - slot
    tm = COMB_TM

    @pl.when(i == 0)
    def _():
        _row_gather_start(ys_hbm, pos_ref, rbuf.at[0], sem.at[0], 2 * tm)

    @pl.when(i + 1 < n)
    def _():
        _row_gather_start(ys_hbm, posn_ref, rbuf.at[nslot], sem.at[nslot], 2 * tm)

    _row_gather_wait(ys_hbm, rbuf.at[slot], sem.at[slot], 2 * tm)
    w = w_ref[...]
    r1 = jnp.concatenate(_load_packed_rows(rbuf.at[slot], 0, tm), axis=1)
    r2 = jnp.concatenate(_load_packed_rows(rbuf.at[slot], tm, tm), axis=1)
    moe = w[:, 0:1] * r1 + w[:, 1:2] * r2
    y = x1_ref[...] + g2_ref[0] * moe
    ms = jnp.mean(y * y, axis=-1, keepdims=True)
    o_ref[...] = y * lax.rsqrt(ms + NORM_EPS) * fg_ref[...]


def _combine(pos_tiles, ys, x1, g2, w_tok, final_g, seq_len):
    T, D = x1.shape
    tm = COMB_TM
    nt = T // tm
    per_b = seq_len // tm
    return pl.pallas_call(
        _combine_kernel,
        grid=(nt,),
        in_specs=[pl.BlockSpec((1, 1, 2 * tm), lambda i: (i, 0, 0), memory_space=pltpu.SMEM),
                  pl.BlockSpec((1, 1, 2 * tm), lambda i: (jnp.minimum(i + 1, nt - 1), 0, 0),
                               memory_space=pltpu.SMEM),
                  pl.BlockSpec(memory_space=pl.ANY),
                  pl.BlockSpec((tm, D), lambda i: (i, 0)),
                  pl.BlockSpec((1, 1, D), lambda i: (i // per_b, 0, 0)),
                  pl.BlockSpec((tm, 2), lambda i: (i, 0)),
                  pl.BlockSpec((1, D), lambda i: (0, 0))],
        out_specs=pl.BlockSpec((tm, D), lambda i: (i, 0)),
        out_shape=jax.ShapeDtypeStruct((T, D), F32),
        scratch_shapes=[pltpu.VMEM((2, 2 * tm * ROW_SLAB, 128), U32),
                        pltpu.SemaphoreType.DMA((2,))],
        compiler_params=pltpu.CompilerParams(
            dimension_semantics=("arbitrary",), vmem_limit_bytes=VMEM_LIMIT),
        name="combine",
    )(pos_tiles, pos_tiles, ys, x1, g2, w_tok, final_g)


def _rel_bucket_table():
    n = jnp.arange(BIAS_LUT, dtype=I32)
    max_exact = N_BUCKETS // 2
    nf = jnp.maximum(n, 1).astype(F32)
    large = max_exact + (jnp.log(nf / max_exact) / math.log(MAX_DISTANCE / max_exact)
                         * (N_BUCKETS - max_exact)).astype(I32)
    large = jnp.minimum(large, N_BUCKETS - 1)
    return jnp.where(n < max_exact, n, large)


def _route_plan(eid, n_tiles, tm):
    two, T = eid.shape
    e_flat = eid.reshape(-1)
    onehot = (e_flat[:, None] == jnp.arange(N_EXPERTS, dtype=I32)[None, :]).astype(I32)
    csum = jnp.cumsum(onehot, axis=0)
    rank = jnp.sum((csum - onehot) * onehot, axis=1)
    counts = csum[-1]
    ptiles = (counts + tm - 1) // tm
    tend = jnp.cumsum(ptiles)
    tstart = tend - ptiles
    slot = jnp.sum(onehot * tstart[None, :], axis=1) * tm + rank
    total = tend[-1]
    tile_ids = jnp.arange(n_tiles, dtype=I32)
    tile_valid = (tile_ids < total).astype(I32)
    tile_expert = jnp.sum((tile_ids[:, None] >= tend[None, :]).astype(I32), axis=1)
    last_expert = jnp.sum((total - 1 >= tend).astype(I32))
    tile_expert = jnp.minimum(tile_expert, last_expert).astype(I32)
    partial_last = jnp.any((tile_ids[:, None] == tend[None, :] - 1)
                           & (ptiles[None, :] > 0) & (counts[None, :] % tm != 0), axis=1)
    tile_fill = (partial_last | (tile_ids >= total)).astype(I32)
    experts = jnp.arange(N_EXPERTS, dtype=I32)
    later = (experts[None, :] > tile_expert[:, None]) & (ptiles[None, :] > 0)
    tile_next_expert = jnp.min(jnp.where(later, experts[None, :], N_EXPERTS), axis=1)
    tile_next_expert = jnp.where(tile_next_expert == N_EXPERTS, -1, tile_next_expert).astype(I32)
    return slot.astype(I32), tile_expert, tile_valid, tile_fill, tile_next_expert


def kernel(x, c, positions, rel_bias, ada_w, ada_b, norm1_g, w_in, lambda_q1, lambda_k1, lambda_q2,
           lambda_k2, subln_g, conv_w, w_out, norm2_g, router_group_w, router_group_b,
           router_expert_w, router_expert_b, expert_w_gate, expert_w_up, expert_w_down, final_g):
    B, S, D = x.shape
    T = B * S
    l = 0

    ada = _ada(c.reshape(B, D, 1), ada_w[l], ada_b[l].reshape(1, -1))
    sh1, sc1, g1, sh2, sc2, g2 = jnp.split(ada, 6, axis=-1)

    qt, k, vt, conv = _inproj(x, sc1, sh1, norm1_g[l].reshape(1, D), w_in[l], conv_w[l])
    lut = (rel_bias.astype(F32)[_rel_bucket_table(), :].T * LOG2E).reshape(N_DIFF_HEADS, 2, BIAS_LUT)
    lam_params = jnp.stack([lambda_q1[l], lambda_k1[l], lambda_q2[l], lambda_k2[l]]).astype(F32)
    attn = _attention(qt, k, vt, positions, lut, lam_params, subln_g[l].reshape(V_HEAD_DIM, 1))

    wr_t = jnp.zeros((ROUTER_ROWS, D), F32)
    wr_t = wr_t.at[0:N_GROUPS].set(router_group_w[l].T)
    wr_t = wr_t.at[N_GROUPS:N_GROUPS + N_EXPERTS].set(router_expert_w[l].T).astype(BF16)
    rb = jnp.zeros((ROUTER_ROWS, 1), F32)
    rb = rb.at[0:N_GROUPS, 0].set(router_group_b[l])
    rb = rb.at[N_GROUPS:N_GROUPS + N_EXPERTS, 0].set(router_expert_b[l])
    x1, hp, ri, rw = _outproj(attn, conv, x, g1, sc2, sh2, norm2_g[l].reshape(1, D),
                              w_out[l].astype(BF16), wr_t, rb)

    eid = ri[:, 0:2, :].transpose(1, 0, 2).reshape(2, T)
    n_tiles = 2 * T // MOE_TM + N_EXPERTS
    slot, tile_expert, tile_valid, tile_fill, tile_next = _route_plan(eid, n_tiles, MOE_TM)
    nct = T // COMB_TM
    pos = (slot * ROW_SLAB).reshape(2, nct, 1, COMB_TM)
    pos_tiles = jnp.concatenate([pos[0], pos[1]], axis=2)

    xs = _dispatch(tile_fill, pos_tiles, hp.reshape(T * ROW_SLAB, 128), n_tiles)
    ys = _moe(tile_expert, tile_valid, tile_next, xs, expert_w_gate[l], expert_w_up[l],
              expert_w_down[l])

    w_tok = rw[:, 0:2, :].transpose(0, 2, 1).reshape(T, 2)
    out = _combine(pos_tiles, ys, x1.reshape(T, D), g2, w_tok, final_g.reshape(1, D), S)
    return out.reshape(B, S, D)
```

```python
import functools
import math

import jax
import jax.numpy as jnp
from jax import lax
from jax.experimental import pallas as pl
from jax.experimental.pallas import tpu as pltpu

F32 = jnp.float32
BF16 = jnp.bfloat16
I32 = jnp.int32
U32 = jnp.uint32

D_MODEL = 1024
ATTN_WIDTH = 512
CONV_WIDTH = 512
N_DIFF_HEADS = 4
DIFF_HEAD_DIM = 64
V_HEAD_DIM = 128
IN_PROJ_WIDTH = 3 * ATTN_WIDTH + 3 * CONV_WIDTH
CONV_K = 3
N_BUCKETS = 32
MAX_DISTANCE = 128
N_GROUPS = 4
EXPERTS_PER_GROUP = 4
N_EXPERTS = 16
D_EXPERT = 512
NORM_EPS = 1e-6
SUBLN_EPS = 1e-5
NEG_INF = -1e30
LAMBDA_INIT = 0.8 - 0.6 * math.exp(-0.3 * 0)
QK_SCALE = DIFF_HEAD_DIM ** -0.5
LOG2E = math.log2(math.e)

BIAS_LUT = 128

ROW_TILE = 512
PROJ_SUBTILES = 2
OUT_TILE = 1024
OUT_SUBTILES = 8
ATT_TQ = 1024
ATT_CHAIN = 256
ATT_GROUP = 16
ATT_LOOKAHEAD = 8
ATT_TK = 256
MOE_TM = 256
COMB_TM = 256
ROUTER_ROWS = 32
ROW_SLAB = D_MODEL // 256
VMEM_LIMIT = 56 * 1024 * 1024


def _silu(x):
    return x * (1.0 / (1.0 + jnp.exp(-x)))


def _ada_kernel(c_ref, w_ref, b_ref, o_ref):
    for bi in range(c_ref.shape[0]):
        s = _silu(c_ref[bi])
        o_ref[bi] = jnp.sum(s * w_ref[...], axis=0, keepdims=True) + b_ref[...]


def _ada(c_col, w, b):
    nb, n = c_col.shape[0], w.shape[1]
    bn = 1024
    return pl.pallas_call(
        _ada_kernel,
        grid=(n // bn,),
        in_specs=[pl.BlockSpec((nb, D_MODEL, 1), lambda j: (0, 0, 0)),
                  pl.BlockSpec((D_MODEL, bn), lambda j: (0, j)),
                  pl.BlockSpec((1, bn), lambda j: (0, j))],
        out_specs=pl.BlockSpec((nb, 1, bn), lambda j: (0, 0, j)),
        out_shape=jax.ShapeDtypeStruct((nb, 1, n), F32),
        name="ada",
    )(c_col, w, b)


def _inproj_kernel(x_ref, sc_ref, sh_ref, g_ref, w_hbm, cw_ref,
                   qt_ref, k_ref, vt_ref, conv_ref,
                   carry_ref, wqt_ref, wk_ref, wvt_ref, wc_ref, stage_ref, sem):
    j = pl.program_id(1)
    tm = x_ref.shape[1]
    nt = (((1,), (1,)), ((), ()))

    @pl.when(jnp.logical_and(pl.program_id(0) == 0, j == 0))
    def _():
        for c in range(IN_PROJ_WIDTH // 512):
            cp = pltpu.make_async_copy(w_hbm.at[:, pl.ds(c * 512, 512)], stage_ref, sem.at[0])
            cp.start()
            cp.wait()
            if c == 0:
                wqt_ref[...] = stage_ref[...].T.astype(BF16)
            elif c == 1:
                wk_ref[...] = stage_ref[...].astype(BF16)
            elif c == 2:
                wvt_ref[...] = stage_ref[...].T.astype(BF16)
            else:
                wc_ref[:, (c - 3) * 512:(c - 2) * 512] = stage_ref[...].astype(BF16)

    @pl.when(j == 0)
    def _():
        carry_ref[...] = jnp.zeros_like(carry_ref)

    prev = carry_ref[...]
    sub = tm // PROJ_SUBTILES
    for s in range(PROJ_SUBTILES):
        rows = slice(s * sub, (s + 1) * sub)
        x = x_ref[0, rows, :]
        ms = jnp.mean(x * x, axis=-1, keepdims=True)
        h = x * lax.rsqrt(ms + NORM_EPS) * g_ref[...]
        h = h * (1.0 + sc_ref[0]) + sh_ref[0]
        hb = h.astype(BF16)

        def proj(c0):
            return jnp.dot(hb, wc_ref[:, c0:c0 + 512], preferred_element_type=F32)

        qt = lax.dot_general(wqt_ref[...], hb, nt, preferred_element_type=F32)
        qt_ref[0, :, rows] = (qt * (QK_SCALE * LOG2E)).astype(BF16)
        k_ref[0, rows, :] = jnp.dot(hb, wk_ref[...], preferred_element_type=F32).astype(BF16)
        vt_ref[0, :, rows] = lax.dot_general(wvt_ref[...], hb, nt,
                                             preferred_element_type=F32).astype(BF16)
        gate_b = proj(0)
        u = proj(512) * proj(1024)
        row = lax.broadcasted_iota(I32, u.shape, 0)
        u1 = pltpu.roll(u, 1, axis=0)
        u2 = pltpu.roll(u, 2, axis=0)
        u1 = jnp.where(row == 0, prev[7:8, :], u1)
        u2 = jnp.where(row == 0, prev[6:7, :], jnp.where(row == 1, prev[7:8, :], u2))
        conv = cw_ref[0:1, :] * u2 + cw_ref[1:2, :] * u1 + cw_ref[2:3, :] * u
        conv_ref[0, rows, :] = (gate_b * conv).astype(BF16)
        prev = u[sub - 8:sub, :]
    carry_ref[...] = prev


def _inproj(x, sc1, sh1, g1n, w_in, conv_w):
    B, S, D = x.shape
    tm = ROW_TILE
    row_out = jax.ShapeDtypeStruct((B, S, 512), BF16)
    col_out = jax.ShapeDtypeStruct((B, 512, S), BF16)
    row_spec = pl.BlockSpec((1, tm, 512), lambda b, j: (b, j, 0))
    col_spec = pl.BlockSpec((1, 512, tm), lambda b, j: (b, 0, j))
    mod_spec = pl.BlockSpec((1, 1, D), lambda b, j: (b, 0, 0))
    const2 = lambda b, j: (0, 0)
    return pl.pallas_call(
        _inproj_kernel,
        grid=(B, S // tm),
        in_specs=[pl.BlockSpec((1, tm, D), lambda b, j: (b, j, 0)),
                  mod_spec, mod_spec,
                  pl.BlockSpec((1, D), const2),
                  pl.BlockSpec(memory_space=pl.ANY),
                  pl.BlockSpec((CONV_K, CONV_WIDTH), const2)],
        out_specs=[col_spec, row_spec, col_spec, row_spec],
        out_shape=[col_out, row_out, col_out, row_out],
        scratch_shapes=[pltpu.VMEM((8, CONV_WIDTH), F32),
                        pltpu.VMEM((ATTN_WIDTH, D), BF16),
                        pltpu.VMEM((D, ATTN_WIDTH), BF16),
                        pltpu.VMEM((ATTN_WIDTH, D), BF16),
                        pltpu.VMEM((D, 3 * CONV_WIDTH), BF16),
                        pltpu.VMEM((D, 512), F32),
                        pltpu.SemaphoreType.DMA((1,))],
        compiler_params=pltpu.CompilerParams(
            dimension_semantics=("arbitrary", "arbitrary"), vmem_limit_bytes=VMEM_LIMIT),
        name="inproj",
    )(x, sc1, sh1, g1n, w_in, conv_w)


def _attn_kernel(qmin_ref, kmax_ref, consec_ref, qt_ref, k_ref, vt_ref, pr_ref, pr_all_ref, lut_ref,
                 lam_ref, sg_ref, o_ref, acc_ref, tz_ref, *, n_q_tiles):
    b = pl.program_id(0)
    qi = pl.program_id(2)
    nq = pl.num_programs(2)
    tq, tk = ATT_TQ, ATT_TK
    hw = ATT_CHAIN
    n_half = tq // hw
    nk = nq * (tq // tk)
    qt = qt_ref[0]
    feat = lax.broadcasted_iota(I32, qt.shape, 0)
    zero = jnp.zeros_like(qt)
    qts = (jnp.where(feat < DIFF_HEAD_DIM, qt, zero), jnp.where(feat >= DIFF_HEAD_DIM, qt, zero))
    luts = (lut_ref[0, 0:1, :], lut_ref[0, 1:2, :])
    fars = tuple(t[:, BIAS_LUT - 1:BIAS_LUT] for t in luts)
    pq = pr_ref[0]
    qmin = qmin_ref[b * nq + qi]
    czero = jnp.zeros((1, 1), F32)
    ones_rows = jnp.ones((16, tk), BF16)
    chains = [(mi, hi) for mi in range(2) for hi in range(n_half)]

    def gather_bias(mi, dist):
        table = jnp.broadcast_to(luts[mi], (tk, BIAS_LUT))
        return jnp.concatenate([jnp.take_along_axis(table, dist[:, o:o + 128], axis=1)
                                for o in range(0, hw, 128)], axis=1)

    @pl.when(qi == 0)
    def _():
        delta = (lax.broadcasted_iota(I32, (tk, hw), 1) - lax.broadcasted_iota(I32, (tk, hw), 0))
        for mi in range(2):
            diag_bias = gather_bias(mi, jnp.clip(delta, 0, BIAS_LUT - 1))
            tz_ref[mi, 0] = jnp.where(delta >= 0, diag_bias, NEG_INF)
            tz_ref[mi, 1] = gather_bias(mi, jnp.clip(delta + tk, 0, BIAS_LUT - 1))

    def run_blocks(blocks, state):
        loaded = []
        for (j, kinds) in blocks:
            ks = pl.multiple_of(j * tk, tk)
            kb = k_ref[0, pl.ds(ks, tk), :]
            vtb = jnp.concatenate([vt_ref[0, :, pl.ds(ks, tk)], ones_rows], axis=0)
            dist = None
            if any(kd is not None and kd.startswith("near") for kd in kinds):
                pk_rows = jnp.broadcast_to(pr_all_ref[0, :, pl.ds(ks, tk)], (8, tk))
                pk = pk_rows.T[:, 0:1]
                dist = jnp.clip(pq - pk, 0, BIAS_LUT - 1)
            loaded.append((kb, vtb, dist))
        items = [(bi, n) for bi, blk in enumerate(blocks) for n, (mi, hi) in enumerate(chains)
                 if blk[1][hi] is not None]
        scores = {}

        def issue_qk(t):
            bi, n = items[t]
            mi, hi = chains[n]
            scores[t] = jnp.dot(loaded[bi][0], qts[mi][:, hi * hw:(hi + 1) * hw],
                                preferred_element_type=F32)

        state = list(state)
        for t in range(min(ATT_LOOKAHEAD, len(items))):
            issue_qk(t)
        for t, (bi, n) in enumerate(items):
            if t + ATT_LOOKAHEAD < len(items):
                issue_qk(t + ATT_LOOKAHEAD)
            _, vtb, dist = loaded[bi]
            mi, hi = chains[n]
            kind = blocks[bi][1][hi]
            cols = slice(hi * hw, (hi + 1) * hw)
            m, l = state[n]
            s = scores.pop(t)
            c = czero
            if kind == "far":
                c = fars[mi]
            elif kind == "tz_diag":
                s = tz_ref[mi, 0] + s
            elif kind == "tz_sub":
                s = tz_ref[mi, 1] + s
            else:
                s = gather_bias(mi, dist[:, cols]) + s
                if kind == "near_masked":
                    keep = (lax.broadcasted_iota(I32, (tk, hw), 0)
                            <= lax.broadcasted_iota(I32, (tk, hw), 1))
                    s = jnp.where(keep, s, NEG_INF)
            mn = jnp.maximum(m, jnp.max(s, axis=0, keepdims=True) + c)
            alpha = jnp.exp2(m - mn)
            p = jnp.exp2(s - (mn - c))
            pv = jnp.dot(vtb, p.astype(BF16), preferred_element_type=F32)
            l = alpha * l + pv[V_HEAD_DIM:V_HEAD_DIM + 1, :]
            acc_ref[mi, :, cols] = alpha * acc_ref[mi, :, cols] + pv[:V_HEAD_DIM, :]
            state[n] = (mn, l)
        return tuple(state)

    def block_is_far(j):
        return qmin - kmax_ref[b * nk + j] >= BIAS_LUT - 1

    def one_block(j, state):
        return lax.cond(block_is_far(j), lambda st: run_blocks([(j, ("far",) * n_half)], st),
                        lambda st: run_blocks([(j, ("near",) * n_half)], st), state)

    def group_body(width):
        def body(g, carry):
            j0, state = carry
            all_far = block_is_far(j0)
            for u in range(1, width):
                all_far = jnp.logical_and(all_far, block_is_far(j0 + u))
            far_blocks = [(j0 + u, ("far",) * n_half) for u in range(width)]
            state = lax.cond(
                all_far, lambda st: run_blocks(far_blocks, st),
                lambda st: lax.fori_loop(0, width, lambda u, s2: one_block(j0 + u, s2), st), state)
            return j0 + width, state
        return body

    acc_ref[...] = jnp.zeros_like(acc_ref)
    m0 = jnp.full((1, hw), NEG_INF, F32)
    l0 = jnp.zeros((1, hw), F32)
    state = tuple((m0, l0) for _ in chains)
    assert hw == tk and n_half % 2 == 0
    n_full = n_half * qi
    n_main = jnp.maximum(n_full - 2, 0)
    n_groups = n_main // ATT_GROUP
    j0, state = lax.fori_loop(0, n_groups, group_body(ATT_GROUP), (jnp.int32(0), state))
    rem = n_main - n_groups * ATT_GROUP

    consec = consec_ref[b * nq + qi] == 1

    def diag_kinds(d, on_diag, below, further):
        return tuple(None if hi < d else on_diag if hi == d else below if hi == d + 1 else further
                     for hi in range(n_half))

    diag_fast = [(n_full + d, diag_kinds(d, "tz_diag", "tz_sub", "far")) for d in range(n_half)]
    diag_any = [(n_full + d, diag_kinds(d, "near_masked", "near", "near")) for d in range(n_half)]
    below_fast = [(n_full - 2, ("far",) * n_half),
                  (n_full - 1, ("tz_sub",) + ("far",) * (n_half - 1))]
    below_any = [(n_full - 2, ("near",) * n_half), (n_full - 1, ("near",) * n_half)]

    def first_tile(st):
        return lax.cond(consec, lambda s2: run_blocks(diag_fast, s2),
                        lambda s2: run_blocks(diag_any, s2), st)

    def slow_rest(st):
        st = lax.fori_loop(0, rem, lambda u, s2: one_block(j0 + u, s2), st)
        return lax.cond(consec, lambda s2: run_blocks(below_fast + diag_fast, s2),
                        lambda s2: run_blocks(below_any + diag_any, s2), st)

    def fast_rest(r):
        left = [(j0 + u, ("far",) * n_half) for u in range(r)]
        return lambda st: run_blocks(left + below_fast + diag_fast, st)

    left_counts = sorted({(n_half * q - 2) % ATT_GROUP for q in range(1, n_q_tiles)})
    left_far = consec
    for u in range(max(left_counts, default=0)):
        far_u = block_is_far(jnp.minimum(j0 + u, nk - 1))
        left_far = jnp.logical_and(left_far, jnp.logical_or(u >= rem, far_u))

    def later_tile(st):
        out = slow_rest
        for r in left_counts:
            out = (lambda r, nxt: lambda s2: lax.cond(
                jnp.logical_and(left_far, rem == r), fast_rest(r), nxt, s2))(r, out)
        return out(st)

    state = lax.cond(qi > 0, later_tile, first_tile, state)

    l1 = jnp.concatenate([state[n][1] for n, (mi, hi) in enumerate(chains) if mi == 0], axis=1)
    l2 = jnp.concatenate([state[n][1] for n, (mi, hi) in enumerate(chains) if mi == 1], axis=1)
    lam = (jnp.exp(jnp.sum(lam_ref[0:1, :] * lam_ref[1:2, :], axis=-1, keepdims=True))
           - jnp.exp(jnp.sum(lam_ref[2:3, :] * lam_ref[3:4, :], axis=-1, keepdims=True))
           + LAMBDA_INIT)
    ot = acc_ref[0] * (1.0 / l1) - (lam * (1.0 / l2)) * acc_ref[1]
    ot = ot * lax.rsqrt(jnp.mean(ot * ot, axis=0, keepdims=True) + SUBLN_EPS)
    ot = ot * (sg_ref[...] * (1.0 - LAMBDA_INIT))
    o_ref[0] = ot.T.astype(BF16)


def _attention(qt, k, vt, positions, lut, lam_params, subln_g_col):
    B, S, _ = k.shape
    tq = ATT_TQ
    nq = S // tq
    pos_row = positions.reshape(B, 1, S)
    qmin = jnp.min(positions.reshape(B * nq, tq), axis=1)
    kmax = jnp.max(positions.reshape(B * (S // ATT_TK), ATT_TK), axis=1)
    step_ok = jnp.concatenate([positions[:, 1:] - positions[:, :-1] == 1,
                               jnp.ones((B, 1), jnp.bool_)], axis=1).reshape(B, nq, tq)
    inner_ok = jnp.all(step_ok[:, :, :tq - 1], axis=2)
    link_ok = jnp.concatenate([jnp.ones((B, 1), jnp.bool_), step_ok[:, :-1, tq - 1]], axis=1)
    prev_ok = jnp.concatenate([jnp.ones((B, 1), jnp.bool_), inner_ok[:, :-1]], axis=1)
    consec = (inner_ok & link_ok & prev_ok).astype(I32).reshape(B * nq)
    grid_spec = pltpu.PrefetchScalarGridSpec(
        num_scalar_prefetch=3,
        grid=(B, N_DIFF_HEADS, nq),
        in_specs=[pl.BlockSpec((1, 128, tq), lambda b, h, i, *_: (b, h, i)),
                  pl.BlockSpec((1, S, 128), lambda b, h, i, *_: (b, 0, h)),
                  pl.BlockSpec((1, 128, S), lambda b, h, i, *_: (b, h, 0)),
                  pl.BlockSpec((1, 1, tq), lambda b, h, i, *_: (b, 0, i)),
                  pl.BlockSpec((1, 1, S), lambda b, h, i, *_: (b, 0, 0)),
                  pl.BlockSpec((1, 2, BIAS_LUT), lambda b, h, i, *_: (h, 0, 0)),
                  pl.BlockSpec((4, DIFF_HEAD_DIM), lambda b, h, i, *_: (0, 0)),
                  pl.BlockSpec((V_HEAD_DIM, 1), lambda b, h, i, *_: (0, 0))],
        out_specs=pl.BlockSpec((1, tq, 128), lambda b, h, i, *_: (b, i, h)),
        scratch_shapes=[pltpu.VMEM((2, V_HEAD_DIM, tq), F32),
                        pltpu.VMEM((2, 2, ATT_TK, ATT_CHAIN), F32)],
    )
    return pl.pallas_call(
        functools.partial(_attn_kernel, n_q_tiles=nq),
        grid_spec=grid_spec,
        out_shape=jax.ShapeDtypeStruct((B, S, ATTN_WIDTH), BF16),
        compiler_params=pltpu.CompilerParams(
            dimension_semantics=("arbitrary", "arbitrary", "arbitrary"),
            vmem_limit_bytes=VMEM_LIMIT),
        name="diffattn",
    )(qmin, kmax, consec, qt, k, vt, pos_row, pos_row, lut, lam_params, subln_g_col)


def _outproj_kernel(at_ref, cv_ref, x_ref, g1_ref, sc_ref, sh_ref, gn_ref, wo_ref, wr_ref, rb_ref,
                    x1_ref, hp_ref, ri_ref, rw_ref):
    sub = x_ref.shape[1] // OUT_SUBTILES
    groups = [slice(s * sub, (s + 1) * sub) for s in range(OUT_SUBTILES)]
    mixes = [jnp.dot(at_ref[0, rows, :], wo_ref[0:ATTN_WIDTH, :], preferred_element_type=F32)
             + jnp.dot(cv_ref[0, rows, :], wo_ref[ATTN_WIDTH:, :], preferred_element_type=F32)
             for rows in groups]
    for rows, mix in zip(groups, mixes):
        _outproj_rows(rows, mix, x_ref, g1_ref, sc_ref, sh_ref, gn_ref, wr_ref, rb_ref,
                      x1_ref, hp_ref, ri_ref, rw_ref)


def _outproj_rows(rows, mix, x_ref, g1_ref, sc_ref, sh_ref, gn_ref, wr_ref, rb_ref,
                  x1_ref, hp_ref, ri_ref, rw_ref):
    tm = rows.stop - rows.start
    x1 = x_ref[0, rows, :] + g1_ref[0] * mix
    x1_ref[0, rows, :] = x1
    ms = jnp.mean(x1 * x1, axis=-1, keepdims=True)
    h = x1 * lax.rsqrt(ms + NORM_EPS) * gn_ref[...]
    h = h * (1.0 + sc_ref[0]) + sh_ref[0]
    hb = h.astype(BF16)

    _store_packed_rows(hp_ref.at[0, pl.ds(rows.start * ROW_SLAB, tm * ROW_SLAB)], h)

    lg_all = lax.dot_general(wr_ref[...], hb, (((1,), (1,)), ((), ())),
                             preferred_element_type=F32) + rb_ref[...]
    lg = lg_all[0:N_GROUPS, :]
    le = lg_all[N_GROUPS:N_GROUPS + N_EXPERTS, :]
    row4 = lax.broadcasted_iota(I32, (N_GROUPS, tm), 0)
    gmax = jnp.max(lg, axis=0, keepdims=True)
    pg_sel = 1.0 / jnp.sum(jnp.exp(lg - gmax), axis=0, keepdims=True)
    gsel = jnp.min(jnp.where(lg == gmax, row4, N_GROUPS), axis=0, keepdims=True)
    sel = jnp.zeros((EXPERTS_PER_GROUP, tm), F32)
    for g in range(N_GROUPS):
        sel = jnp.where(gsel == g, le[g * EXPERTS_PER_GROUP:(g + 1) * EXPERTS_PER_GROUP, :], sel)
    v1 = jnp.max(sel, axis=0, keepdims=True)
    i1 = jnp.min(jnp.where(sel == v1, row4, EXPERTS_PER_GROUP), axis=0, keepdims=True)
    rest = jnp.where(row4 == i1, -jnp.inf, sel)
    v2 = jnp.max(rest, axis=0, keepdims=True)
    i2 = jnp.min(jnp.where(rest == v2, row4, EXPERTS_PER_GROUP), axis=0, keepdims=True)
    e2 = jnp.exp(v2 - v1)
    w1 = pg_sel / (1.0 + e2)
    w2 = pg_sel * e2 / (1.0 + e2)
    row8 = lax.broadcasted_iota(I32, (8, tm), 0)
    eid1 = gsel * EXPERTS_PER_GROUP + i1
    eid2 = gsel * EXPERTS_PER_GROUP + i2
    ri_ref[0, :, rows] = jnp.where(row8 == 0, eid1, jnp.where(row8 == 1, eid2, 0))
    rw_ref[0, :, rows] = jnp.where(row8 == 0, w1, jnp.where(row8 == 1, w2, 0.0))


def _outproj(attn, conv, x, g1, sc2, sh2, g2n, w_out_bf, wr_t, rb):
    B, S, D = x.shape
    tm = OUT_TILE
    half_spec = pl.BlockSpec((1, tm, 512), lambda b, j: (b, j, 0))
    full_spec = pl.BlockSpec((1, tm, D), lambda b, j: (b, j, 0))
    mod_spec = pl.BlockSpec((1, 1, D), lambda b, j: (b, 0, 0))
    rt_spec = pl.BlockSpec((1, 8, tm), lambda b, j: (b, 0, j))
    return pl.pallas_call(
        _outproj_kernel,
        grid=(B, S // tm),
        in_specs=[half_spec, half_spec, full_spec, mod_spec, mod_spec, mod_spec,
                  pl.BlockSpec((1, D), lambda b, j: (0, 0)),
                  pl.BlockSpec((D, D), lambda b, j: (0, 0)),
                  pl.BlockSpec((ROUTER_ROWS, D), lambda b, j: (0, 0)),
                  pl.BlockSpec((ROUTER_ROWS, 1), lambda b, j: (0, 0))],
        out_specs=[full_spec, pl.BlockSpec((1, tm * ROW_SLAB, 128), lambda b, j: (b, j, 0)),
                   rt_spec, rt_spec],
        out_shape=[jax.ShapeDtypeStruct((B, S, D), F32),
                   jax.ShapeDtypeStruct((B, S * ROW_SLAB, 128), U32),
                   jax.ShapeDtypeStruct((B, 8, S), I32),
                   jax.ShapeDtypeStruct((B, 8, S), F32)],
        compiler_params=pltpu.CompilerParams(
            dimension_semantics=("arbitrary", "arbitrary"), vmem_limit_bytes=VMEM_LIMIT),
        name="outproj",
    )(attn, conv, x, g1, sc2, sh2, g2n, w_out_bf, wr_t, rb)


def _store_packed_rows(dst, x):
    half = D_MODEL // 2
    xb = x.astype(BF16).astype(F32)
    packed = (pltpu.bitcast(xb[:, :half], U32) >> 16) | (
        pltpu.bitcast(xb[:, half:], U32) & jnp.uint32(0xFFFF0000))
    rows = x.shape[0]
    for c in range(ROW_SLAB):
        dst[pl.ds(c, rows, stride=ROW_SLAB), :] = packed[:, c * 128:(c + 1) * 128]


def _load_packed_rows(src, row0, rows):
    packed = jnp.concatenate(
        [src[pl.ds(row0 * ROW_SLAB + c, rows, stride=ROW_SLAB), :] for c in range(ROW_SLAB)], axis=1)
    lo = pltpu.bitcast(packed << 16, F32)
    hi = pltpu.bitcast(packed & jnp.uint32(0xFFFF0000), F32)
    return lo, hi


def _row_gather_start(src_hbm, idx_ref, dst, sem, n_rows):
    for r in range(n_rows):
        off = pl.multiple_of(idx_ref[0, 0, r], ROW_SLAB)
        pltpu.make_async_copy(src_hbm.at[pl.ds(off, ROW_SLAB)],
                              dst.at[pl.ds(r * ROW_SLAB, ROW_SLAB)], sem).start(priority=r % 2)


def _row_gather_wait(src_hbm, dst, sem, n_rows):
    pltpu.make_async_copy(src_hbm.at[pl.ds(0, n_rows * ROW_SLAB)], dst, sem).wait()


def _dispatch_kernel(fill_ref, pos_ref, hp_ref, xs_hbm, zbuf, sem):
    i = pl.program_id(0)
    tm = COMB_TM
    tile_rows = MOE_TM * ROW_SLAB

    @pl.when(i == 0)
    def _():
        zbuf[...] = jnp.zeros_like(zbuf)

        def fill_copy(t):
            return pltpu.make_async_copy(
                zbuf, xs_hbm.at[pl.ds(pl.multiple_of(t * tile_rows, tile_rows), tile_rows)],
                sem.at[0])

        def start(t, c):
            @pl.when(fill_ref[t] == 1)
            def _():
                fill_copy(t).start()
            return c

        def wait(t, c):
            @pl.when(fill_ref[t] == 1)
            def _():
                fill_copy(t).wait()
            return c

        lax.fori_loop(0, fill_ref.shape[0], start, 0)
        lax.fori_loop(0, fill_ref.shape[0], wait, 0)

    par = lax.rem(i, 2)
    base = pl.multiple_of(i * (tm * ROW_SLAB), tm * ROW_SLAB)
    for r in range(2 * tm):
        off = pl.multiple_of(pos_ref[0, 0, r], ROW_SLAB)
        pltpu.make_async_copy(hp_ref.at[pl.ds(base + (r % tm) * ROW_SLAB, ROW_SLAB)],
                              xs_hbm.at[pl.ds(off, ROW_SLAB)],
                              sem.at[1 + par]).start(priority=r % 2)

    def drain(parity):
        for _ in range(2):
            pltpu.make_async_copy(hp_ref.at[pl.ds(0, tm * ROW_SLAB)],
                                  xs_hbm.at[pl.ds(0, tm * ROW_SLAB)], sem.at[1 + parity]).wait()

    @pl.when(i > 0)
    def _():
        drain(1 - par)

    @pl.when(i == pl.num_programs(0) - 1)
    def _():
        drain(par)


def _dispatch(tile_fill, pos_tiles, hp, n_tiles):
    T = hp.shape[0] // ROW_SLAB
    tm = COMB_TM
    grid_spec = pltpu.PrefetchScalarGridSpec(
        num_scalar_prefetch=1,
        grid=(T // tm,),
        in_specs=[pl.BlockSpec((1, 1, 2 * tm), lambda i, f: (i, 0, 0), memory_space=pltpu.SMEM),
                  pl.BlockSpec(memory_space=pltpu.VMEM)],
        out_specs=pl.BlockSpec(memory_space=pl.ANY),
        scratch_shapes=[pltpu.VMEM((MOE_TM * ROW_SLAB, 128), U32),
                        pltpu.SemaphoreType.DMA((3,))],
    )
    return pl.pallas_call(
        _dispatch_kernel,
        grid_spec=grid_spec,
        out_shape=jax.ShapeDtypeStruct((n_tiles * MOE_TM * ROW_SLAB, 128), U32),
        compiler_params=pltpu.CompilerParams(
            dimension_semantics=("arbitrary",), vmem_limit_bytes=VMEM_LIMIT),
        name="dispatch",
    )(tile_fill, pos_tiles, hp)


def _moe_kernel(te_ref, tv_ref, ne_ref, xs_ref, wg_hbm, wu_hbm, wd_hbm, y_ref,
                wg_bf, wu_bf, wd_bf, wg_st, wu_st, wd_st, sem):
    i = pl.program_id(0)
    tm = MOE_TM

    def weight_copies(e):
        return (pltpu.make_async_copy(wg_hbm.at[e], wg_st, sem.at[0]),
                pltpu.make_async_copy(wu_hbm.at[e], wu_st, sem.at[1]),
                pltpu.make_async_copy(wd_hbm.at[e], wd_st, sem.at[2]))

    @pl.when(i == 0)
    def _():
        for cp in weight_copies(te_ref[0]):
            cp.start()

    @pl.when(jnp.logical_or(i == 0, te_ref[i] != te_ref[jnp.maximum(i - 1, 0)]))
    def _():
        for cp in weight_copies(te_ref[i]):
            cp.wait()
        wg_bf[...] = wg_st[...].astype(BF16)
        wu_bf[...] = wu_st[...].astype(BF16)
        wd_bf[...] = wd_st[...].astype(BF16)

        @pl.when(ne_ref[i] >= 0)
        def _():
            for cp in weight_copies(ne_ref[i]):
                cp.start()

    @pl.when(tv_ref[i] == 1)
    def _():
        half = D_MODEL // 2
        lo, hi = _load_packed_rows(xs_ref, 0, tm)
        lo = lo.astype(BF16)
        hi = hi.astype(BF16)
        g = (jnp.dot(lo, wg_bf[0:half, :], preferred_element_type=F32)
             + jnp.dot(hi, wg_bf[half:, :], preferred_element_type=F32))
        u = (jnp.dot(lo, wu_bf[0:half, :], preferred_element_type=F32)
             + jnp.dot(hi, wu_bf[half:, :], preferred_element_type=F32))
        hid = (_silu(g) * u).astype(BF16)
        _store_packed_rows(y_ref, jnp.dot(hid, wd_bf[...], preferred_element_type=F32))

    @pl.when(tv_ref[i] == 0)
    def _():
        y_ref[...] = jnp.zeros_like(y_ref)


def _moe(tile_expert, tile_valid, tile_next_expert, xs, wg, wu, wd):
    nt = tile_expert.shape[0]
    tm = MOE_TM
    D = D_MODEL
    any_spec = pl.BlockSpec(memory_space=pl.ANY)
    grid_spec = pltpu.PrefetchScalarGridSpec(
        num_scalar_prefetch=3,
        grid=(nt,),
        in_specs=[pl.BlockSpec((tm * ROW_SLAB, 128), lambda i, *_: (i, 0)),
                  any_spec, any_spec, any_spec],
        out_specs=pl.BlockSpec((tm * ROW_SLAB, 128), lambda i, *_: (i, 0)),
        scratch_shapes=[pltpu.VMEM((D, D_EXPERT), BF16), pltpu.VMEM((D, D_EXPERT), BF16),
                        pltpu.VMEM((D_EXPERT, D), BF16),
                        pltpu.VMEM((D, D_EXPERT), F32), pltpu.VMEM((D, D_EXPERT), F32),
                        pltpu.VMEM((D_EXPERT, D), F32),
                        pltpu.SemaphoreType.DMA((3,))],
    )
    return pl.pallas_call(
        _moe_kernel,
        grid_spec=grid_spec,
        out_shape=jax.ShapeDtypeStruct((nt * tm * ROW_SLAB, 128), U32),
        compiler_params=pltpu.CompilerParams(
            dimension_semantics=("arbitrary",), vmem_limit_bytes=VMEM_LIMIT),
        name="moe",
    )(tile_expert, tile_valid, tile_next_expert, xs, wg, wu, wd)


def _combine_kernel(pos_ref, posn_ref, ys_hbm, x1_ref, g2_ref, w_ref, fg_ref, o_ref, rbuf, sem):
    i = pl.program_id(0)
    n = pl.num_programs(0)
    slot = lax.rem(i, 2)
    nslot = 1 - slot
    tm = COMB_TM

    @pl.when(i == 0)
    def _():
        _row_gather_start(ys_hbm, pos_ref, rbuf.at[0], sem.at[0], 2 * tm)

    @pl.when(i + 1 < n)
    def _():
        _row_gather_start(ys_hbm, posn_ref, rbuf.at[nslot], sem.at[nslot], 2 * tm)

    _row_gather_wait(ys_hbm, rbuf.at[slot], sem.at[slot], 2 * tm)
    w = w_ref[...]
    r1 = jnp.concatenate(_load_packed_rows(rbuf.at[slot], 0, tm), axis=1)
    r2 = jnp.concatenate(_load_packed_rows(rbuf.at[slot], tm, tm), axis=1)
    moe = w[:, 0:1] * r1 + w[:, 1:2] * r2
    y = x1_ref[...] + g2_ref[0] * moe
    ms = jnp.mean(y * y, axis=-1, keepdims=True)
    o_ref[...] = y * lax.rsqrt(ms + NORM_EPS) * fg_ref[...]


def _combine(pos_tiles, ys, x1, g2, w_tok, final_g, seq_len):
    T, D = x1.shape
    tm = COMB_TM
    nt = T // tm
    per_b = seq_len // tm
    return pl.pallas_call(
        _combine_kernel,
        grid=(nt,),
        in_specs=[pl.BlockSpec((1, 1, 2 * tm), lambda i: (i, 0, 0), memory_space=pltpu.SMEM),
                  pl.BlockSpec((1, 1, 2 * tm), lambda i: (jnp.minimum(i + 1, nt - 1), 0, 0),
                               memory_space=pltpu.SMEM),
                  pl.BlockSpec(memory_space=pl.ANY),
                  pl.BlockSpec((tm, D), lambda i: (i, 0)),
                  pl.BlockSpec((1, 1, D), lambda i: (i // per_b, 0, 0)),
                  pl.BlockSpec((tm, 2), lambda i: (i, 0)),
                  pl.BlockSpec((1, D), lambda i: (0, 0))],
        out_specs=pl.BlockSpec((tm, D), lambda i: (i, 0)),
        out_shape=jax.ShapeDtypeStruct((T, D), F32),
        scratch_shapes=[pltpu.VMEM((2, 2 * tm * ROW_SLAB, 128), U32),
                        pltpu.SemaphoreType.DMA((2,))],
        compiler_params=pltpu.CompilerParams(
            dimension_semantics=("arbitrary",), vmem_limit_bytes=VMEM_LIMIT),
        name="combine",
    )(pos_tiles, pos_tiles, ys, x1, g2, w_tok, final_g)


def _rel_bucket_table():
    n = jnp.arange(BIAS_LUT, dtype=I32)
    max_exact = N_BUCKETS // 2
    nf = jnp.maximum(n, 1).astype(F32)
    large = max_exact + (jnp.log(nf / max_exact) / math.log(MAX_DISTANCE / max_exact)
                         * (N_BUCKETS - max_exact)).astype(I32)
    large = jnp.minimum(large, N_BUCKETS - 1)
    return jnp.where(n < max_exact, n, large)


def _route_plan(eid, n_tiles, tm):
    two, T = eid.shape
    e_flat = eid.reshape(-1)
    onehot = (e_flat[:, None] == jnp.arange(N_EXPERTS, dtype=I32)[None, :]).astype(I32)
    csum = jnp.cumsum(onehot, axis=0)
    rank = jnp.sum((csum - onehot) * onehot, axis=1)
    counts = csum[-1]
    ptiles = (counts + tm - 1) // tm
    tend = jnp.cumsum(ptiles)
    tstart = tend - ptiles
    slot = jnp.sum(onehot * tstart[None, :], axis=1) * tm + rank
    total = tend[-1]
    tile_ids = jnp.arange(n_tiles, dtype=I32)
    tile_valid = (tile_ids < total).astype(I32)
    tile_expert = jnp.sum((tile_ids[:, None] >= tend[None, :]).astype(I32), axis=1)
    last_expert = jnp.sum((total - 1 >= tend).astype(I32))
    tile_expert = jnp.minimum(tile_expert, last_expert).astype(I32)
    partial_last = jnp.any((tile_ids[:, None] == tend[None, :] - 1)
                           & (ptiles[None, :] > 0) & (counts[None, :] % tm != 0), axis=1)
    tile_fill = (partial_last | (tile_ids >= total)).astype(I32)
    experts = jnp.arange(N_EXPERTS, dtype=I32)
    later = (experts[None, :] > tile_expert[:, None]) & (ptiles[None, :] > 0)
    tile_next_expert = jnp.min(jnp.where(later, experts[None, :], N_EXPERTS), axis=1)
    tile_next_expert = jnp.where(tile_next_expert == N_EXPERTS, -1, tile_next_expert).astype(I32)
    return slot.astype(I32), tile_expert, tile_valid, tile_fill, tile_next_expert


def kernel(x, c, positions, rel_bias, ada_w, ada_b, norm1_g, w_in, lambda_q1, lambda_k1, lambda_q2,
           lambda_k2, subln_g, conv_w, w_out, norm2_g, router_group_w, router_group_b,
           router_expert_w, router_expert_b, expert_w_gate, expert_w_up, expert_w_down, final_g):
    B, S, D = x.shape
    T = B * S
    l = 0

    ada = _ada(c.reshape(B, D, 1), ada_w[l], ada_b[l].reshape(1, -1))
    sh1, sc1, g1, sh2, sc2, g2 = jnp.split(ada, 6, axis=-1)

    qt, k, vt, conv = _inproj(x, sc1, sh1, norm1_g[l].reshape(1, D), w_in[l], conv_w[l])
    lut = (rel_bias.astype(F32)[_rel_bucket_table(), :].T * LOG2E).reshape(N_DIFF_HEADS, 2, BIAS_LUT)
    lam_params = jnp.stack([lambda_q1[l], lambda_k1[l], lambda_q2[l], lambda_k2[l]]).astype(F32)
    attn = _attention(qt, k, vt, positions, lut, lam_params, subln_g[l].reshape(V_HEAD_DIM, 1))

    wr_t = jnp.zeros((ROUTER_ROWS, D), F32)
    wr_t = wr_t.at[0:N_GROUPS].set(router_group_w[l].T)
    wr_t = wr_t.at[N_GROUPS:N_GROUPS + N_EXPERTS].set(router_expert_w[l].T).astype(BF16)
    rb = jnp.zeros((ROUTER_ROWS, 1), F32)
    rb = rb.at[0:N_GROUPS, 0].set(router_group_b[l])
    rb = rb.at[N_GROUPS:N_GROUPS + N_EXPERTS, 0].set(router_expert_b[l])
    x1, hp, ri, rw = _outproj(attn, conv, x, g1, sc2, sh2, norm2_g[l].reshape(1, D),
                              w_out[l].astype(BF16), wr_t, rb)

    eid = ri[:, 0:2, :].transpose(1, 0, 2).reshape(2, T)
    n_tiles = 2 * T // MOE_TM + N_EXPERTS
    slot, tile_expert, tile_valid, tile_fill, tile_next = _route_plan(eid, n_tiles, MOE_TM)
    nct = T // COMB_TM
    pos = (slot * ROW_SLAB).reshape(2, nct, 1, COMB_TM)
    pos_tiles = jnp.concatenate([pos[0], pos[1]], axis=2)

    xs = _dispatch(tile_fill, pos_tiles, hp.reshape(T * ROW_SLAB, 128), n_tiles)
    ys = _moe(tile_expert, tile_valid, tile_next, xs, expert_w_gate[l], expert_w_up[l],
              expert_w_down[l])

    w_tok = rw[:, 0:2, :].transpose(0, 2, 1).reshape(T, 2)
    out = _combine(pos_tiles, ys, x1.reshape(T, D), g2, w_tok, final_g.reshape(1, D), S)
    return out.reshape(B, S, D)
```

```python
import functools
import math

import jax
import jax.numpy as jnp
from jax import lax
from jax.experimental import pallas as pl
from jax.experimental.pallas import tpu as pltpu

F32 = jnp.float32
BF16 = jnp.bfloat16
I32 = jnp.int32
U32 = jnp.uint32

D_MODEL = 1024
ATTN_WIDTH = 512
CONV_WIDTH = 512
N_DIFF_HEADS = 4
DIFF_HEAD_DIM = 64
V_HEAD_DIM = 128
IN_PROJ_WIDTH = 3 * ATTN_WIDTH + 3 * CONV_WIDTH
CONV_K = 3
N_BUCKETS = 32
MAX_DISTANCE = 128
N_GROUPS = 4
EXPERTS_PER_GROUP = 4
N_EXPERTS = 16
D_EXPERT = 512
NORM_EPS = 1e-6
SUBLN_EPS = 1e-5
NEG_INF = -1e30
LAMBDA_INIT = 0.8 - 0.6 * math.exp(-0.3 * 0)
QK_SCALE = DIFF_HEAD_DIM ** -0.5
LOG2E = math.log2(math.e)

BIAS_LUT = 128

ROW_TILE = 512
PROJ_SUBTILES = 2
OUT_TILE = 1024
OUT_SUBTILES = 8
ATT_TQ = 1024
ATT_CHAIN = 256
ATT_GROUP = 8
ATT_LOOKAHEAD = 8
ATT_TK = 256
MOE_TM = 256
COMB_TM = 256
DISPATCH_CHUNKS = 8
ROUTER_ROWS = 32
ROW_SLAB = D_MODEL // 256
VMEM_LIMIT = 56 * 1024 * 1024


def _silu(x):
    return x * (1.0 / (1.0 + jnp.exp(-x)))


def _ada_kernel(c_ref, w_ref, b_ref, o_ref):
    for bi in range(c_ref.shape[0]):
        s = _silu(c_ref[bi])
        o_ref[bi] = jnp.sum(s * w_ref[...], axis=0, keepdims=True) + b_ref[...]


def _ada(c_col, w, b):
    nb, n = c_col.shape[0], w.shape[1]
    bn = 1024
    return pl.pallas_call(
        _ada_kernel,
        grid=(n // bn,),
        in_specs=[pl.BlockSpec((nb, D_MODEL, 1), lambda j: (0, 0, 0)),
                  pl.BlockSpec((D_MODEL, bn), lambda j: (0, j)),
                  pl.BlockSpec((1, bn), lambda j: (0, j))],
        out_specs=pl.BlockSpec((nb, 1, bn), lambda j: (0, 0, j)),
        out_shape=jax.ShapeDtypeStruct((nb, 1, n), F32),
        name="ada",
    )(c_col, w, b)


def _inproj_kernel(x_ref, sc_ref, sh_ref, g_ref, w_hbm, cw_ref,
                   qt_ref, k_ref, vt_ref, conv_ref,
                   carry_ref, wqt_ref, wk_ref, wvt_ref, wc_ref, stage_ref, sem):
    j = pl.program_id(1)
    tm = x_ref.shape[1]
    nt = (((1,), (1,)), ((), ()))

    @pl.when(jnp.logical_and(pl.program_id(0) == 0, j == 0))
    def _():
        for c in range(IN_PROJ_WIDTH // 512):
            cp = pltpu.make_async_copy(w_hbm.at[:, pl.ds(c * 512, 512)], stage_ref, sem.at[0])
            cp.start()
            cp.wait()
            if c == 0:
                wqt_ref[...] = stage_ref[...].T.astype(BF16)
            elif c == 1:
                wk_ref[...] = stage_ref[...].astype(BF16)
            elif c == 2:
                wvt_ref[...] = stage_ref[...].T.astype(BF16)
            else:
                wc_ref[:, (c - 3) * 512:(c - 2) * 512] = stage_ref[...].astype(BF16)

    @pl.when(j == 0)
    def _():
        carry_ref[...] = jnp.zeros_like(carry_ref)

    prev = carry_ref[...]
    sub = tm // PROJ_SUBTILES
    for s in range(PROJ_SUBTILES):
        rows = slice(s * sub, (s + 1) * sub)
        x = x_ref[0, rows, :]
        ms = jnp.mean(x * x, axis=-1, keepdims=True)
        h = x * lax.rsqrt(ms + NORM_EPS) * g_ref[...]
        h = h * (1.0 + sc_ref[0]) + sh_ref[0]
        hb = h.astype(BF16)

        def proj(c0):
            return jnp.dot(hb, wc_ref[:, c0:c0 + 512], preferred_element_type=F32)

        qt = lax.dot_general(wqt_ref[...], hb, nt, preferred_element_type=F32)
        qt_ref[0, :, rows] = (qt * (QK_SCALE * LOG2E)).astype(BF16)
        k_ref[0, rows, :] = jnp.dot(hb, wk_ref[...], preferred_element_type=F32).astype(BF16)
        vt_ref[0, :, rows] = lax.dot_general(wvt_ref[...], hb, nt,
                                             preferred_element_type=F32).astype(BF16)
        gate_b = proj(0)
        u = proj(512) * proj(1024)
        row = lax.broadcasted_iota(I32, u.shape, 0)
        u1 = pltpu.roll(u, 1, axis=0)
        u2 = pltpu.roll(u, 2, axis=0)
        u1 = jnp.where(row == 0, prev[7:8, :], u1)
        u2 = jnp.where(row == 0, prev[6:7, :], jnp.where(row == 1, prev[7:8, :], u2))
        conv = cw_ref[0:1, :] * u2 + cw_ref[1:2, :] * u1 + cw_ref[2:3, :] * u
        conv_ref[0, rows, :] = (gate_b * conv).astype(BF16)
        prev = u[sub - 8:sub, :]
    carry_ref[...] = prev


def _inproj(x, sc1, sh1, g1n, w_in, conv_w):
    B, S, D = x.shape
    tm = ROW_TILE
    row_out = jax.ShapeDtypeStruct((B, S, 512), BF16)
    col_out = jax.ShapeDtypeStruct((B, 512, S), BF16)
    row_spec = pl.BlockSpec((1, tm, 512), lambda b, j: (b, j, 0))
    col_spec = pl.BlockSpec((1, 512, tm), lambda b, j: (b, 0, j))
    mod_spec = pl.BlockSpec((1, 1, D), lambda b, j: (b, 0, 0))
    const2 = lambda b, j: (0, 0)
    return pl.pallas_call(
        _inproj_kernel,
        grid=(B, S // tm),
        in_specs=[pl.BlockSpec((1, tm, D), lambda b, j: (b, j, 0)),
                  mod_spec, mod_spec,
                  pl.BlockSpec((1, D), const2),
                  pl.BlockSpec(memory_space=pl.ANY),
                  pl.BlockSpec((CONV_K, CONV_WIDTH), const2)],
        out_specs=[col_spec, row_spec, col_spec, row_spec],
        out_shape=[col_out, row_out, col_out, row_out],
        scratch_shapes=[pltpu.VMEM((8, CONV_WIDTH), F32),
                        pltpu.VMEM((ATTN_WIDTH, D), BF16),
                        pltpu.VMEM((D, ATTN_WIDTH), BF16),
                        pltpu.VMEM((ATTN_WIDTH, D), BF16),
                        pltpu.VMEM((D, 3 * CONV_WIDTH), BF16),
                        pltpu.VMEM((D, 512), F32),
                        pltpu.SemaphoreType.DMA((1,))],
        compiler_params=pltpu.CompilerParams(
            dimension_semantics=("arbitrary", "arbitrary"), vmem_limit_bytes=VMEM_LIMIT),
        name="inproj",
    )(x, sc1, sh1, g1n, w_in, conv_w)


def _attn_kernel(qmin_ref, kmax_ref, consec_ref, qt_ref, k_ref, vt_ref, pr_ref, pr_all_ref, lut_ref,
                 lam_ref, sg_ref, o_ref, acc_ref, tz_ref, *, n_q_tiles):
    b = pl.program_id(0)
    qi = pl.program_id(2)
    nq = pl.num_programs(2)
    tq, tk = ATT_TQ, ATT_TK
    hw = ATT_CHAIN
    n_half = tq // hw
    nk = nq * (tq // tk)
    qt = qt_ref[0]
    feat = lax.broadcasted_iota(I32, qt.shape, 0)
    zero = jnp.zeros_like(qt)
    qts = (jnp.where(feat < DIFF_HEAD_DIM, qt, zero), jnp.where(feat >= DIFF_HEAD_DIM, qt, zero))
    luts = (lut_ref[0, 0:1, :], lut_ref[0, 1:2, :])
    fars = tuple(t[:, BIAS_LUT - 1:BIAS_LUT] for t in luts)
    pq = pr_ref[0]
    qmin = qmin_ref[b * nq + qi]
    czero = jnp.zeros((1, 1), F32)
    ones_rows = jnp.ones((16, tk), BF16)
    chains = [(mi, hi) for mi in range(2) for hi in range(n_half)]

    def gather_bias(mi, dist):
        table = jnp.broadcast_to(luts[mi], (tk, BIAS_LUT))
        return jnp.concatenate([jnp.take_along_axis(table, dist[:, o:o + 128], axis=1)
                                for o in range(0, hw, 128)], axis=1)

    @pl.when(qi == 0)
    def _():
        delta = (lax.broadcasted_iota(I32, (tk, hw), 1) - lax.broadcasted_iota(I32, (tk, hw), 0))
        for mi in range(2):
            diag_bias = gather_bias(mi, jnp.clip(delta, 0, BIAS_LUT - 1))
            tz_ref[mi, 0] = jnp.where(delta >= 0, diag_bias, NEG_INF)
            tz_ref[mi, 1] = gather_bias(mi, jnp.clip(delta + tk, 0, BIAS_LUT - 1))

    def run_blocks(blocks, state):
        loaded = []
        for (j, kinds) in blocks:
            ks = pl.multiple_of(j * tk, tk)
            kb = k_ref[0, pl.ds(ks, tk), :]
            vtb = jnp.concatenate([vt_ref[0, :, pl.ds(ks, tk)], ones_rows], axis=0)
            dist = None
            if any(kd is not None and kd.startswith("near") for kd in kinds):
                pk_rows = jnp.broadcast_to(pr_all_ref[0, :, pl.ds(ks, tk)], (8, tk))
                pk = pk_rows.T[:, 0:1]
                dist = jnp.clip(pq - pk, 0, BIAS_LUT - 1)
            loaded.append((kb, vtb, dist))
        items = [(bi, n) for bi, blk in enumerate(blocks) for n, (mi, hi) in enumerate(chains)
                 if blk[1][hi] is not None]
        scores = {}

        def issue_qk(t):
            bi, n = items[t]
            mi, hi = chains[n]
            scores[t] = jnp.dot(loaded[bi][0], qts[mi][:, hi * hw:(hi + 1) * hw],
                                preferred_element_type=F32)

        state = list(state)
        for t in range(min(ATT_LOOKAHEAD, len(items))):
            issue_qk(t)
        for t, (bi, n) in enumerate(items):
            if t + ATT_LOOKAHEAD < len(items):
                issue_qk(t + ATT_LOOKAHEAD)
            _, vtb, dist = loaded[bi]
            mi, hi = chains[n]
            kind = blocks[bi][1][hi]
            cols = slice(hi * hw, (hi + 1) * hw)
            m, l = state[n]
            s = scores.pop(t)
            c = czero
            if kind == "far":
                c = fars[mi]
            elif kind == "tz_diag":
                s = tz_ref[mi, 0] + s
            elif kind == "tz_sub":
                s = tz_ref[mi, 1] + s
            else:
                s = gather_bias(mi, dist[:, cols]) + s
                if kind == "near_masked":
                    keep = (lax.broadcasted_iota(I32, (tk, hw), 0)
                            <= lax.broadcasted_iota(I32, (tk, hw), 1))
                    s = jnp.where(keep, s, NEG_INF)
            mn = jnp.maximum(m, jnp.max(s, axis=0, keepdims=True) + c)
            alpha = jnp.exp2(m - mn)
            p = jnp.exp2(s - (mn - c))
            pv = jnp.dot(vtb, p.astype(BF16), preferred_element_type=F32)
            l = alpha * l + pv[V_HEAD_DIM:V_HEAD_DIM + 1, :]
            acc_ref[mi, :, cols] = alpha * acc_ref[mi, :, cols] + pv[:V_HEAD_DIM, :]
            state[n] = (mn, l)
        return tuple(state)

    def block_is_far(j):
        return qmin - kmax_ref[b * nk + j] >= BIAS_LUT - 1

    def one_block(j, state):
        return lax.cond(block_is_far(j), lambda st: run_blocks([(j, ("far",) * n_half)], st),
                        lambda st: run_blocks([(j, ("near",) * n_half)], st), state)

    def group_body(width):
        def body(g, carry):
            j0, state = carry
            all_far = block_is_far(j0)
            for u in range(1, width):
                all_far = jnp.logical_and(all_far, block_is_far(j0 + u))
            far_blocks = [(j0 + u, ("far",) * n_half) for u in range(width)]
            state = lax.cond(
                all_far, lambda st: run_blocks(far_blocks, st),
                lambda st: lax.fori_loop(0, width, lambda u, s2: one_block(j0 + u, s2), st), state)
            return j0 + width, state
        return body

    acc_ref[...] = jnp.zeros_like(acc_ref)
    m0 = jnp.full((1, hw), NEG_INF, F32)
    l0 = jnp.zeros((1, hw), F32)
    state = tuple((m0, l0) for _ in chains)
    assert hw == tk and n_half % 2 == 0
    n_full = n_half * qi
    n_main = jnp.maximum(n_full - 2, 0)
    n_groups = n_main // ATT_GROUP
    j0, state = lax.fori_loop(0, n_groups, group_body(ATT_GROUP), (jnp.int32(0), state))
    rem = n_main - n_groups * ATT_GROUP

    consec = consec_ref[b * nq + qi] == 1

    def diag_kinds(d, on_diag, below, further):
        return tuple(None if hi < d else on_diag if hi == d else below if hi == d + 1 else further
                     for hi in range(n_half))

    diag_fast = [(n_full + d, diag_kinds(d, "tz_diag", "tz_sub", "far")) for d in range(n_half)]
    diag_any = [(n_full + d, diag_kinds(d, "near_masked", "near", "near")) for d in range(n_half)]
    below_fast = [(n_full - 2, ("far",) * n_half),
                  (n_full - 1, ("tz_sub",) + ("far",) * (n_half - 1))]
    below_any = [(n_full - 2, ("near",) * n_half), (n_full - 1, ("near",) * n_half)]

    def first_tile(st):
        return lax.cond(consec, lambda s2: run_blocks(diag_fast, s2),
                        lambda s2: run_blocks(diag_any, s2), st)

    def slow_rest(st):
        st = lax.fori_loop(0, rem, lambda u, s2: one_block(j0 + u, s2), st)
        return lax.cond(consec, lambda s2: run_blocks(below_fast + diag_fast, s2),
                        lambda s2: run_blocks(below_any + diag_any, s2), st)

    def fast_rest(r):
        left = [(j0 + u, ("far",) * n_half) for u in range(r)]
        return lambda st: run_blocks(left + below_fast + diag_fast, st)

    left_counts = sorted({(n_half * q - 2) % ATT_GROUP for q in range(1, n_q_tiles)})
    left_far = consec
    for u in range(max(left_counts, default=0)):
        far_u = block_is_far(jnp.minimum(j0 + u, nk - 1))
        left_far = jnp.logical_and(left_far, jnp.logical_or(u >= rem, far_u))

    def later_tile(st):
        out = slow_rest
        for r in left_counts:
            out = (lambda r, nxt: lambda s2: lax.cond(
                jnp.logical_and(left_far, rem == r), fast_rest(r), nxt, s2))(r, out)
        return out(st)

    state = lax.cond(qi > 0, later_tile, first_tile, state)

    l1 = jnp.concatenate([state[n][1] for n, (mi, hi) in enumerate(chains) if mi == 0], axis=1)
    l2 = jnp.concatenate([state[n][1] for n, (mi, hi) in enumerate(chains) if mi == 1], axis=1)
    lam = (jnp.exp(jnp.sum(lam_ref[0:1, :] * lam_ref[1:2, :], axis=-1, keepdims=True))
           - jnp.exp(jnp.sum(lam_ref[2:3, :] * lam_ref[3:4, :], axis=-1, keepdims=True))
           + LAMBDA_INIT)
    ot = acc_ref[0] * (1.0 / l1) - (lam * (1.0 / l2)) * acc_ref[1]
    ot = ot * lax.rsqrt(jnp.mean(ot * ot, axis=0, keepdims=True) + SUBLN_EPS)
    ot = ot * (sg_ref[...] * (1.0 - LAMBDA_INIT))
    o_ref[0] = ot.T.astype(BF16)


def _attention(qt, k, vt, positions, lut, lam_params, subln_g_col):
    B, S, _ = k.shape
    tq = ATT_TQ
    nq = S // tq
    pos_row = positions.reshape(B, 1, S)
    qmin = jnp.min(positions.reshape(B * nq, tq), axis=1)
    kmax = jnp.max(positions.reshape(B * (S // ATT_TK), ATT_TK), axis=1)
    step_ok = jnp.concatenate([positions[:, 1:] - positions[:, :-1] == 1,
                               jnp.ones((B, 1), jnp.bool_)], axis=1).reshape(B, nq, tq)
    inner_ok = jnp.all(step_ok[:, :, :tq - 1], axis=2)
    link_ok = jnp.concatenate([jnp.ones((B, 1), jnp.bool_), step_ok[:, :-1, tq - 1]], axis=1)
    prev_ok = jnp.concatenate([jnp.ones((B, 1), jnp.bool_), inner_ok[:, :-1]], axis=1)
    consec = (inner_ok & link_ok & prev_ok).astype(I32).reshape(B * nq)
    grid_spec = pltpu.PrefetchScalarGridSpec(
        num_scalar_prefetch=3,
        grid=(B, N_DIFF_HEADS, nq),
        in_specs=[pl.BlockSpec((1, 128, tq), lambda b, h, i, *_: (b, h, i)),
                  pl.BlockSpec((1, S, 128), lambda b, h, i, *_: (b, 0, h)),
                  pl.BlockSpec((1, 128, S), lambda b, h, i, *_: (b, h, 0)),
                  pl.BlockSpec((1, 1, tq), lambda b, h, i, *_: (b, 0, i)),
                  pl.BlockSpec((1, 1, S), lambda b, h, i, *_: (b, 0, 0)),
                  pl.BlockSpec((1, 2, BIAS_LUT), lambda b, h, i, *_: (h, 0, 0)),
                  pl.BlockSpec((4, DIFF_HEAD_DIM), lambda b, h, i, *_: (0, 0)),
                  pl.BlockSpec((V_HEAD_DIM, 1), lambda b, h, i, *_: (0, 0))],
        out_specs=pl.BlockSpec((1, tq, 128), lambda b, h, i, *_: (b, i, h)),
        scratch_shapes=[pltpu.VMEM((2, V_HEAD_DIM, tq), F32),
                        pltpu.VMEM((2, 2, ATT_TK, ATT_CHAIN), F32)],
    )
    return pl.pallas_call(
        functools.partial(_attn_kernel, n_q_tiles=nq),
        grid_spec=grid_spec,
        out_shape=jax.ShapeDtypeStruct((B, S, ATTN_WIDTH), BF16),
        compiler_params=pltpu.CompilerParams(
            dimension_semantics=("arbitrary", "arbitrary", "arbitrary"),
            vmem_limit_bytes=VMEM_LIMIT),
        name="diffattn",
    )(qmin, kmax, consec, qt, k, vt, pos_row, pos_row, lut, lam_params, subln_g_col)


def _outproj_kernel(at_ref, cv_ref, x_ref, g1_ref, sc_ref, sh_ref, gn_ref, wo_ref, wr_ref, rb_ref,
                    x1_ref, hp_ref, ri_ref, rw_ref):
    sub = x_ref.shape[1] // OUT_SUBTILES
    groups = [slice(s * sub, (s + 1) * sub) for s in range(OUT_SUBTILES)]
    mixes = [jnp.dot(at_ref[0, rows, :], wo_ref[0:ATTN_WIDTH, :], preferred_element_type=F32)
             + jnp.dot(cv_ref[0, rows, :], wo_ref[ATTN_WIDTH:, :], preferred_element_type=F32)
             for rows in groups]
    for rows, mix in zip(groups, mixes):
        _outproj_rows(rows, mix, x_ref, g1_ref, sc_ref, sh_ref, gn_ref, wr_ref, rb_ref,
                      x1_ref, hp_ref, ri_ref, rw_ref)


def _outproj_rows(rows, mix, x_ref, g1_ref, sc_ref, sh_ref, gn_ref, wr_ref, rb_ref,
                  x1_ref, hp_ref, ri_ref, rw_ref):
    tm = rows.stop - rows.start
    x1 = x_ref[0, rows, :] + g1_ref[0] * mix
    x1_ref[0, rows, :] = x1
    ms = jnp.mean(x1 * x1, axis=-1, keepdims=True)
    h = x1 * lax.rsqrt(ms + NORM_EPS) * gn_ref[...]
    h = h * (1.0 + sc_ref[0]) + sh_ref[0]
    hb = h.astype(BF16)

    _store_packed_rows(hp_ref.at[0, pl.ds(rows.start * ROW_SLAB, tm * ROW_SLAB)], h)

    lg_all = lax.dot_general(wr_ref[...], hb, (((1,), (1,)), ((), ())),
                             preferred_element_type=F32) + rb_ref[...]
    lg = lg_all[0:N_GROUPS, :]
    le = lg_all[N_GROUPS:N_GROUPS + N_EXPERTS, :]
    row4 = lax.broadcasted_iota(I32, (N_GROUPS, tm), 0)
    gmax = jnp.max(lg, axis=0, keepdims=True)
    pg_sel = 1.0 / jnp.sum(jnp.exp(lg - gmax), axis=0, keepdims=True)
    gsel = jnp.min(jnp.where(lg == gmax, row4, N_GROUPS), axis=0, keepdims=True)
    sel = jnp.zeros((EXPERTS_PER_GROUP, tm), F32)
    for g in range(N_GROUPS):
        sel = jnp.where(gsel == g, le[g * EXPERTS_PER_GROUP:(g + 1) * EXPERTS_PER_GROUP, :], sel)
    v1 = jnp.max(sel, axis=0, keepdims=True)
    i1 = jnp.min(jnp.where(sel == v1, row4, EXPERTS_PER_GROUP), axis=0, keepdims=True)
    rest = jnp.where(row4 == i1, -jnp.inf, sel)
    v2 = jnp.max(rest, axis=0, keepdims=True)
    i2 = jnp.min(jnp.where(rest == v2, row4, EXPERTS_PER_GROUP), axis=0, keepdims=True)
    e2 = jnp.exp(v2 - v1)
    w1 = pg_sel / (1.0 + e2)
    w2 = pg_sel * e2 / (1.0 + e2)
    row8 = lax.broadcasted_iota(I32, (8, tm), 0)
    eid1 = gsel * EXPERTS_PER_GROUP + i1
    eid2 = gsel * EXPERTS_PER_GROUP + i2
    ri_ref[0, :, rows] = jnp.where(row8 == 0, eid1, jnp.where(row8 == 1, eid2, 0))
    rw_ref[0, :, rows] = jnp.where(row8 == 0, w1, jnp.where(row8 == 1, w2, 0.0))


def _outproj(attn, conv, x, g1, sc2, sh2, g2n, w_out_bf, wr_t, rb):
    B, S, D = x.shape
    tm = OUT_TILE
    half_spec = pl.BlockSpec((1, tm, 512), lambda b, j: (b, j, 0))
    full_spec = pl.BlockSpec((1, tm, D), lambda b, j: (b, j, 0))
    mod_spec = pl.BlockSpec((1, 1, D), lambda b, j: (b, 0, 0))
    rt_spec = pl.BlockSpec((1, 8, tm), lambda b, j: (b, 0, j))
    return pl.pallas_call(
        _outproj_kernel,
        grid=(B, S // tm),
        in_specs=[half_spec, half_spec, full_spec, mod_spec, mod_spec, mod_spec,
                  pl.BlockSpec((1, D), lambda b, j: (0, 0)),
                  pl.BlockSpec((D, D), lambda b, j: (0, 0)),
                  pl.BlockSpec((ROUTER_ROWS, D), lambda b, j: (0, 0)),
                  pl.BlockSpec((ROUTER_ROWS, 1), lambda b, j: (0, 0))],
        out_specs=[full_spec, pl.BlockSpec((1, tm * ROW_SLAB, 128), lambda b, j: (b, j, 0)),
                   rt_spec, rt_spec],
        out_shape=[jax.ShapeDtypeStruct((B, S, D), F32),
                   jax.ShapeDtypeStruct((B, S * ROW_SLAB, 128), U32),
                   jax.ShapeDtypeStruct((B, 8, S), I32),
                   jax.ShapeDtypeStruct((B, 8, S), F32)],
        compiler_params=pltpu.CompilerParams(
            dimension_semantics=("arbitrary", "arbitrary"), vmem_limit_bytes=VMEM_LIMIT),
        name="outproj",
    )(attn, conv, x, g1, sc2, sh2, g2n, w_out_bf, wr_t, rb)


def _store_packed_rows(dst, x):
    half = D_MODEL // 2
    xb = x.astype(BF16).astype(F32)
    packed = (pltpu.bitcast(xb[:, :half], U32) >> 16) | (
        pltpu.bitcast(xb[:, half:], U32) & jnp.uint32(0xFFFF0000))
    rows = x.shape[0]
    for c in range(ROW_SLAB):
        dst[pl.ds(c, rows, stride=ROW_SLAB), :] = packed[:, c * 128:(c + 1) * 128]


def _load_packed_rows(src, row0, rows):
    packed = jnp.concatenate(
        [src[pl.ds(row0 * ROW_SLAB + c, rows, stride=ROW_SLAB), :] for c in range(ROW_SLAB)], axis=1)
    lo = pltpu.bitcast(packed << 16, F32)
    hi = pltpu.bitcast(packed & jnp.uint32(0xFFFF0000), F32)
    return lo, hi


def _row_gather_start(src_hbm, idx_ref, dst, sem, n_rows):
    for r in range(n_rows):
        off = pl.multiple_of(idx_ref[0, 0, r], ROW_SLAB)
        pltpu.make_async_copy(src_hbm.at[pl.ds(off, ROW_SLAB)],
                              dst.at[pl.ds(r * ROW_SLAB, ROW_SLAB)], sem).start(priority=r % 2)


def _row_gather_wait(src_hbm, dst, sem, n_rows):
    pltpu.make_async_copy(src_hbm.at[pl.ds(0, n_rows * ROW_SLAB)], dst, sem).wait()


def _dispatch_kernel(fill_ref, pos_ref, hp_hbm, xs_hbm, hp_ref, zbuf, sem, load_sem):
    i = pl.program_id(0)
    tm = COMB_TM
    tile_rows = MOE_TM * ROW_SLAB
    n_steps = hp_ref.shape[0] // (tm * ROW_SLAB)
    chunk_rows = hp_ref.shape[0] // DISPATCH_CHUNKS

    def load(k):
        rows = pl.ds(k * chunk_rows, chunk_rows)
        return pltpu.make_async_copy(hp_hbm.at[rows], hp_ref.at[rows], load_sem.at[k])

    @pl.when(i == 0)
    def _():
        for k in range(DISPATCH_CHUNKS):
            load(k).start()
        zbuf[...] = jnp.zeros_like(zbuf)

        def fill_copy(t):
            return pltpu.make_async_copy(
                zbuf, xs_hbm.at[pl.ds(pl.multiple_of(t * tile_rows, tile_rows), tile_rows)],
                sem.at[0])

        def start(t, c):
            @pl.when(fill_ref[t] == 1)
            def _():
                fill_copy(t).start()
            return c

        def wait(t, c):
            @pl.when(fill_ref[t] == 1)
            def _():
                fill_copy(t).wait()
            return c

        lax.fori_loop(0, fill_ref.shape[0], start, 0)
        lax.fori_loop(0, fill_ref.shape[0], wait, 0)

    for k in range(DISPATCH_CHUNKS):
        @pl.when(i == k * (n_steps // DISPATCH_CHUNKS))
        def _():
            load(k).wait()

    par = lax.rem(i, 2)
    base = pl.multiple_of(i * (tm * ROW_SLAB), tm * ROW_SLAB)
    for r in range(2 * tm):
        off = pl.multiple_of(pos_ref[0, 0, r], ROW_SLAB)
        pltpu.make_async_copy(hp_ref.at[pl.ds(base + (r % tm) * ROW_SLAB, ROW_SLAB)],
                              xs_hbm.at[pl.ds(off, ROW_SLAB)],
                              sem.at[1 + par]).start(priority=r % 2)

    def drain(parity):
        for _ in range(2):
            pltpu.make_async_copy(hp_ref.at[pl.ds(0, tm * ROW_SLAB)],
                                  xs_hbm.at[pl.ds(0, tm * ROW_SLAB)], sem.at[1 + parity]).wait()

    @pl.when(i > 0)
    def _():
        drain(1 - par)

    @pl.when(i == pl.num_programs(0) - 1)
    def _():
        drain(par)


def _dispatch(tile_fill, pos_tiles, hp, n_tiles):
    T = hp.shape[0] // ROW_SLAB
    tm = COMB_TM
    grid_spec = pltpu.PrefetchScalarGridSpec(
        num_scalar_prefetch=1,
        grid=(T // tm,),
        in_specs=[pl.BlockSpec((1, 1, 2 * tm), lambda i, f: (i, 0, 0), memory_space=pltpu.SMEM),
                  pl.BlockSpec(memory_space=pl.ANY)],
        out_specs=pl.BlockSpec(memory_space=pl.ANY),
        scratch_shapes=[pltpu.VMEM(hp.shape, U32),
                        pltpu.VMEM((MOE_TM * ROW_SLAB, 128), U32),
                        pltpu.SemaphoreType.DMA((3,)),
                        pltpu.SemaphoreType.DMA((DISPATCH_CHUNKS,))],
    )
    return pl.pallas_call(
        _dispatch_kernel,
        grid_spec=grid_spec,
        out_shape=jax.ShapeDtypeStruct((n_tiles * MOE_TM * ROW_SLAB, 128), U32),
        compiler_params=pltpu.CompilerParams(
            dimension_semantics=("arbitrary",), vmem_limit_bytes=VMEM_LIMIT),
        name="dispatch",
    )(tile_fill, pos_tiles, hp)


def _moe_kernel(te_ref, tv_ref, ne_ref, xs_ref, wg_hbm, wu_hbm, wd_hbm, y_ref,
                wg_bf, wu_bf, wd_bf, wg_st, wu_st, wd_st, sem):
    i = pl.program_id(0)
    tm = MOE_TM

    def weight_copies(e):
        return (pltpu.make_async_copy(wg_hbm.at[e], wg_st, sem.at[0]),
                pltpu.make_async_copy(wu_hbm.at[e], wu_st, sem.at[1]),
                pltpu.make_async_copy(wd_hbm.at[e], wd_st, sem.at[2]))

    @pl.when(i == 0)
    def _():
        for cp in weight_copies(te_ref[0]):
            cp.start()

    @pl.when(jnp.logical_or(i == 0, te_ref[i] != te_ref[jnp.maximum(i - 1, 0)]))
    def _():
        for cp in weight_copies(te_ref[i]):
            cp.wait()
        wg_bf[...] = wg_st[...].astype(BF16)
        wu_bf[...] = wu_st[...].astype(BF16)
        wd_bf[...] = wd_st[...].astype(BF16)

        @pl.when(ne_ref[i] >= 0)
        def _():
            for cp in weight_copies(ne_ref[i]):
                cp.start()

    @pl.when(tv_ref[i] == 1)
    def _():
        half = D_MODEL // 2
        lo, hi = _load_packed_rows(xs_ref, 0, tm)
        lo = lo.astype(BF16)
        hi = hi.astype(BF16)
        g = (jnp.dot(lo, wg_bf[0:half, :], preferred_element_type=F32)
             + jnp.dot(hi, wg_bf[half:, :], preferred_element_type=F32))
        u = (jnp.dot(lo, wu_bf[0:half, :], preferred_element_type=F32)
             + jnp.dot(hi, wu_bf[half:, :], preferred_element_type=F32))
        hid = (_silu(g) * u).astype(BF16)
        _store_packed_rows(y_ref, jnp.dot(hid, wd_bf[...], preferred_element_type=F32))

    @pl.when(tv_ref[i] == 0)
    def _():
        y_ref[...] = jnp.zeros_like(y_ref)


def _moe(tile_expert, tile_valid, tile_next_expert, xs, wg, wu, wd):
    nt = tile_expert.shape[0]
    tm = MOE_TM
    D = D_MODEL
    any_spec = pl.BlockSpec(memory_space=pl.ANY)
    grid_spec = pltpu.PrefetchScalarGridSpec(
        num_scalar_prefetch=3,
        grid=(nt,),
        in_specs=[pl.BlockSpec((tm * ROW_SLAB, 128), lambda i, *_: (i, 0)),
                  any_spec, any_spec, any_spec],
        out_specs=pl.BlockSpec((tm * ROW_SLAB, 128), lambda i, *_: (i, 0)),
        scratch_shapes=[pltpu.VMEM((D, D_EXPERT), BF16), pltpu.VMEM((D, D_EXPERT), BF16),
                        pltpu.VMEM((D_EXPERT, D), BF16),
                        pltpu.VMEM((D, D_EXPERT), F32), pltpu.VMEM((D, D_EXPERT), F32),
                        pltpu.VMEM((D_EXPERT, D), F32),
                        pltpu.SemaphoreType.DMA((3,))],
    )
    return pl.pallas_call(
        _moe_kernel,
        grid_spec=grid_spec,
        out_shape=jax.ShapeDtypeStruct((nt * tm * ROW_SLAB, 128), U32),
        compiler_params=pltpu.CompilerParams(
            dimension_semantics=("arbitrary",), vmem_limit_bytes=VMEM_LIMIT),
        name="moe",
    )(tile_expert, tile_valid, tile_next_expert, xs, wg, wu, wd)


def _combine_kernel(pos_ref, posn_ref, ys_hbm, x1_ref, g2_ref, w_ref, fg_ref, o_ref, rbuf, sem):
    i = pl.program_id(0)
    n = pl.num_programs(0)
    slot = lax.rem(i, 2)
    nslot = 1 - slot
    tm = COMB_TM

    @pl.when(i == 0)
    def _():
        _row_gather_start(ys_hbm, pos_ref, rbuf.at[0], sem.at[0], 2 * tm)

    @pl.when(i + 1 < n)
    def _():
        _row_gather_start(ys_hbm, posn_ref, rbuf.at[nslot], sem.at[nslot], 2 * tm)

    _row_gather_wait(ys_hbm, rbuf.at[slot], sem.at[slot], 2 * tm)
    w = w_ref[...]
    r1 = jnp.concatenate(_load_packed_rows(rbuf.at[slot], 0, tm), axis=1)
    r2 = jnp.concatenate(_load_packed_rows(rbuf.at[slot], tm, tm), axis=1)
    moe = w[:, 0:1] * r1 + w[:, 1:2] * r2
    y = x1_ref[...] + g2_ref[0] * moe
    ms = jnp.mean(y * y, axis=-1, keepdims=True)
    o_ref[...] = y * lax.rsqrt(ms + NORM_EPS) * fg_ref[...]


def _combine(pos_tiles, ys, x1, g2, w_tok, final_g, seq_len):
    T, D = x1.shape
    tm = COMB_TM
    nt = T // tm
    per_b = seq_len // tm
    return pl.pallas_call(
        _combine_kernel,
        grid=(nt,),
        in_specs=[pl.BlockSpec((1, 1, 2 * tm), lambda i: (i, 0, 0), memory_space=pltpu.SMEM),
                  pl.BlockSpec((1, 1, 2 * tm), lambda i: (jnp.minimum(i + 1, nt - 1), 0, 0),
                               memory_space=pltpu.SMEM),
                  pl.BlockSpec(memory_space=pl.ANY),
                  pl.BlockSpec((tm, D), lambda i: (i, 0)),
                  pl.BlockSpec((1, 1, D), lambda i: (i // per_b, 0, 0)),
                  pl.BlockSpec((tm, 2), lambda i: (i, 0)),
                  pl.BlockSpec((1, D), lambda i: (0, 0))],
        out_specs=pl.BlockSpec((tm, D), lambda i: (i, 0)),
        out_shape=jax.ShapeDtypeStruct((T, D), F32),
        scratch_shapes=[pltpu.VMEM((2, 2 * tm * ROW_SLAB, 128), U32),
                        pltpu.SemaphoreType.DMA((2,))],
        compiler_params=pltpu.CompilerParams(
            dimension_semantics=("arbitrary",), vmem_limit_bytes=VMEM_LIMIT),
        name="combine",
    )(pos_tiles, pos_tiles, ys, x1, g2, w_tok, final_g)


def _rel_bucket_table():
    n = jnp.arange(BIAS_LUT, dtype=I32)
    max_exact = N_BUCKETS // 2
    nf = jnp.maximum(n, 1).astype(F32)
    large = max_exact + (jnp.log(nf / max_exact) / math.log(MAX_DISTANCE / max_exact)
                         * (N_BUCKETS - max_exact)).astype(I32)
    large = jnp.minimum(large, N_BUCKETS - 1)
    return jnp.where(n < max_exact, n, large)


def _route_plan(eid, n_tiles, tm):
    two, T = eid.shape
    e_flat = eid.reshape(-1)
    onehot = (e_flat[:, None] == jnp.arange(N_EXPERTS, dtype=I32)[None, :]).astype(I32)
    csum = jnp.cumsum(onehot, axis=0)
    rank = jnp.sum((csum - onehot) * onehot, axis=1)
    counts = csum[-1]
    ptiles = (counts + tm - 1) // tm
    tend = jnp.cumsum(ptiles)
    tstart = tend - ptiles
    slot = jnp.sum(onehot * tstart[None, :], axis=1) * tm + rank
    total = tend[-1]
    tile_ids = jnp.arange(n_tiles, dtype=I32)
    tile_valid = (tile_ids < total).astype(I32)
    tile_expert = jnp.sum((tile_ids[:, None] >= tend[None, :]).astype(I32), axis=1)
    last_expert = jnp.sum((total - 1 >= tend).astype(I32))
    tile_expert = jnp.minimum(tile_expert, last_expert).astype(I32)
    partial_last = jnp.any((tile_ids[:, None] == tend[None, :] - 1)
                           & (ptiles[None, :] > 0) & (counts[None, :] % tm != 0), axis=1)
    tile_fill = (partial_last | (tile_ids >= total)).astype(I32)
    experts = jnp.arange(N_EXPERTS, dtype=I32)
    later = (experts[None, :] > tile_expert[:, None]) & (ptiles[None, :] > 0)
    tile_next_expert = jnp.min(jnp.where(later, experts[None, :], N_EXPERTS), axis=1)
    tile_next_expert = jnp.where(tile_next_expert == N_EXPERTS, -1, tile_next_expert).astype(I32)
    return slot.astype(I32), tile_expert, tile_valid, tile_fill, tile_next_expert


def kernel(x, c, positions, rel_bias, ada_w, ada_b, norm1_g, w_in, lambda_q1, lambda_k1, lambda_q2,
           lambda_k2, subln_g, conv_w, w_out, norm2_g, router_group_w, router_group_b,
           router_expert_w, router_expert_b, expert_w_gate, expert_w_up, expert_w_down, final_g):
    B, S, D = x.shape
    T = B * S
    l = 0

    ada = _ada(c.reshape(B, D, 1), ada_w[l], ada_b[l].reshape(1, -1))
    sh1, sc1, g1, sh2, sc2, g2 = jnp.split(ada, 6, axis=-1)

    qt, k, vt, conv = _inproj(x, sc1, sh1, norm1_g[l].reshape(1, D), w_in[l], conv_w[l])
    lut = (rel_bias.astype(F32)[_rel_bucket_table(), :].T * LOG2E).reshape(N_DIFF_HEADS, 2, BIAS_LUT)
    lam_params = jnp.stack([lambda_q1[l], lambda_k1[l], lambda_q2[l], lambda_k2[l]]).astype(F32)
    attn = _attention(qt, k, vt, positions, lut, lam_params, subln_g[l].reshape(V_HEAD_DIM, 1))

    wr_t = jnp.zeros((ROUTER_ROWS, D), F32)
    wr_t = wr_t.at[0:N_GROUPS].set(router_group_w[l].T)
    wr_t = wr_t.at[N_GROUPS:N_GROUPS + N_EXPERTS].set(router_expert_w[l].T).astype(BF16)
    rb = jnp.zeros((ROUTER_ROWS, 1), F32)
    rb = rb.at[0:N_GROUPS, 0].set(router_group_b[l])
    rb = rb.at[N_GROUPS:N_GROUPS + N_EXPERTS, 0].set(router_expert_b[l])
    x1, hp, ri, rw = _outproj(attn, conv, x, g1, sc2, sh2, norm2_g[l].reshape(1, D),
                              w_out[l].astype(BF16), wr_t, rb)

    eid = ri[:, 0:2, :].transpose(1, 0, 2).reshape(2, T)
    n_tiles = 2 * T // MOE_TM + N_EXPERTS
    slot, tile_expert, tile_valid, tile_fill, tile_next = _route_plan(eid, n_tiles, MOE_TM)
    nct = T // COMB_TM
    pos = (slot * ROW_SLAB).reshape(2, nct, 1, COMB_TM)
    pos_tiles = jnp.concatenate([pos[0], pos[1]], axis=2)

    xs = _dispatch(tile_fill, pos_tiles, hp.reshape(T * ROW_SLAB, 128), n_tiles)
    ys = _moe(tile_expert, tile_valid, tile_next, xs, expert_w_gate[l], expert_w_up[l],
              expert_w_down[l])

    w_tok = rw[:, 0:2, :].transpose(0, 2, 1).reshape(T, 2)
    out = _combine(pos_tiles, ys, x1.reshape(T, D), g2, w_tok, final_g.reshape(1, D), S)
    return out.reshape(B, S, D)
```

```python
import functools
import math

import jax
import jax.numpy as jnp
from jax import lax
from jax.experimental import pallas as pl
from jax.experimental.pallas import tpu as pltpu

F32 = jnp.float32
BF16 = jnp.bfloat16
I32 = jnp.int32
U32 = jnp.uint32

D_MODEL = 1024
ATTN_WIDTH = 512
CONV_WIDTH = 512
N_DIFF_HEADS = 4
DIFF_HEAD_DIM = 64
V_HEAD_DIM = 128
IN_PROJ_WIDTH = 3 * ATTN_WIDTH + 3 * CONV_WIDTH
CONV_K = 3
N_BUCKETS = 32
MAX_DISTANCE = 128
N_GROUPS = 4
EXPERTS_PER_GROUP = 4
N_EXPERTS = 16
D_EXPERT = 512
NORM_EPS = 1e-6
SUBLN_EPS = 1e-5
NEG_INF = -1e30
LAMBDA_INIT = 0.8 - 0.6 * math.exp(-0.3 * 0)
QK_SCALE = DIFF_HEAD_DIM ** -0.5
LOG2E = math.log2(math.e)

BIAS_LUT = 128

ROW_TILE = 512
PROJ_SUBTILES = 2
OUT_TILE = 1024
OUT_SUBTILES = 8
ATT_TQ = 1024
ATT_CHAIN = 256
ATT_GROUP = 8
ATT_LOOKAHEAD = 8
ATT_TK = 256
MOE_TM = 256
COMB_TM = 256
DISPATCH_CHUNKS = 8
ROUTER_ROWS = 32
ROW_SLAB = D_MODEL // 256
VMEM_LIMIT = 56 * 1024 * 1024


def _silu(x):
    return x * (1.0 / (1.0 + jnp.exp(-x)))


def _ada_kernel(c_ref, w_ref, b_ref, o_ref):
    for bi in range(c_ref.shape[0]):
        s = _silu(c_ref[bi])
        o_ref[bi] = jnp.sum(s * w_ref[...], axis=0, keepdims=True) + b_ref[...]


def _ada(c_col, w, b):
    nb, n = c_col.shape[0], w.shape[1]
    bn = 1024
    return pl.pallas_call(
        _ada_kernel,
        grid=(n // bn,),
        in_specs=[pl.BlockSpec((nb, D_MODEL, 1), lambda j: (0, 0, 0)),
                  pl.BlockSpec((D_MODEL, bn), lambda j: (0, j)),
                  pl.BlockSpec((1, bn), lambda j: (0, j))],
        out_specs=pl.BlockSpec((nb, 1, bn), lambda j: (0, 0, j)),
        out_shape=jax.ShapeDtypeStruct((nb, 1, n), F32),
        name="ada",
    )(c_col, w, b)


def _inproj_kernel(x_ref, sc_ref, sh_ref, g_ref, w_hbm, cw_ref,
                   qt_ref, k_ref, vt_ref, conv_ref,
                   carry_ref, wqt_ref, wk_ref, wvt_ref, wc_ref, stage_ref, sem):
    j = pl.program_id(1)
    tm = x_ref.shape[1]
    nt = (((1,), (1,)), ((), ()))

    @pl.when(jnp.logical_and(pl.program_id(0) == 0, j == 0))
    def _():
        n_chunks = IN_PROJ_WIDTH // 512

        def chunk_copy(c):
            return pltpu.make_async_copy(w_hbm.at[:, pl.ds(c * 512, 512)], stage_ref.at[c % 2],
                                         sem.at[c % 2])

        chunk_copy(0).start()
        for c in range(n_chunks):
            if c + 1 < n_chunks:
                chunk_copy(c + 1).start()
            chunk_copy(c).wait()
            w = stage_ref[c % 2]
            if c == 0:
                wqt_ref[...] = w.T.astype(BF16)
            elif c == 1:
                wk_ref[...] = w.astype(BF16)
            elif c == 2:
                wvt_ref[...] = w.T.astype(BF16)
            else:
                wc_ref[:, (c - 3) * 512:(c - 2) * 512] = w.astype(BF16)

    @pl.when(j == 0)
    def _():
        carry_ref[...] = jnp.zeros_like(carry_ref)

    prev = carry_ref[...]
    sub = tm // PROJ_SUBTILES
    for s in range(PROJ_SUBTILES):
        rows = slice(s * sub, (s + 1) * sub)
        x = x_ref[0, rows, :]
        ms = jnp.mean(x * x, axis=-1, keepdims=True)
        h = x * lax.rsqrt(ms + NORM_EPS) * g_ref[...]
        h = h * (1.0 + sc_ref[0]) + sh_ref[0]
        hb = h.astype(BF16)

        def proj(c0):
            return jnp.dot(hb, wc_ref[:, c0:c0 + 512], preferred_element_type=F32)

        qt = lax.dot_general(wqt_ref[...], hb, nt, preferred_element_type=F32)
        qt_ref[0, :, rows] = (qt * (QK_SCALE * LOG2E)).astype(BF16)
        k_ref[0, rows, :] = jnp.dot(hb, wk_ref[...], preferred_element_type=F32).astype(BF16)
        vt_ref[0, :, rows] = lax.dot_general(wvt_ref[...], hb, nt,
                                             preferred_element_type=F32).astype(BF16)
        gate_b = proj(0)
        u = proj(512) * proj(1024)
        row = lax.broadcasted_iota(I32, u.shape, 0)
        u1 = pltpu.roll(u, 1, axis=0)
        u2 = pltpu.roll(u, 2, axis=0)
        u1 = jnp.where(row == 0, prev[7:8, :], u1)
        u2 = jnp.where(row == 0, prev[6:7, :], jnp.where(row == 1, prev[7:8, :], u2))
        conv = cw_ref[0:1, :] * u2 + cw_ref[1:2, :] * u1 + cw_ref[2:3, :] * u
        conv_ref[0, rows, :] = (gate_b * conv).astype(BF16)
        prev = u[sub - 8:sub, :]
    carry_ref[...] = prev


def _inproj(x, sc1, sh1, g1n, w_in, conv_w):
    B, S, D = x.shape
    tm = ROW_TILE
    row_out = jax.ShapeDtypeStruct((B, S, 512), BF16)
    col_out = jax.ShapeDtypeStruct((B, 512, S), BF16)
    row_spec = pl.BlockSpec((1, tm, 512), lambda b, j: (b, j, 0))
    col_spec = pl.BlockSpec((1, 512, tm), lambda b, j: (b, 0, j))
    mod_spec = pl.BlockSpec((1, 1, D), lambda b, j: (b, 0, 0))
    const2 = lambda b, j: (0, 0)
    return pl.pallas_call(
        _inproj_kernel,
        grid=(B, S // tm),
        in_specs=[pl.BlockSpec((1, tm, D), lambda b, j: (b, j, 0)),
                  mod_spec, mod_spec,
                  pl.BlockSpec((1, D), const2),
                  pl.BlockSpec(memory_space=pl.ANY),
                  pl.BlockSpec((CONV_K, CONV_WIDTH), const2)],
        out_specs=[col_spec, row_spec, col_spec, row_spec],
        out_shape=[col_out, row_out, col_out, row_out],
        scratch_shapes=[pltpu.VMEM((8, CONV_WIDTH), F32),
                        pltpu.VMEM((ATTN_WIDTH, D), BF16),
                        pltpu.VMEM((D, ATTN_WIDTH), BF16),
                        pltpu.VMEM((ATTN_WIDTH, D), BF16),
                        pltpu.VMEM((D, 3 * CONV_WIDTH), BF16),
                        pltpu.VMEM((2, D, 512), F32),
                        pltpu.SemaphoreType.DMA((2,))],
        compiler_params=pltpu.CompilerParams(
            dimension_semantics=("arbitrary", "arbitrary"), vmem_limit_bytes=VMEM_LIMIT),
        name="inproj",
    )(x, sc1, sh1, g1n, w_in, conv_w)


def _attn_kernel(qmin_ref, kmax_ref, consec_ref, qt_ref, k_ref, vt_ref, pr_ref, pr_all_ref, lut_ref,
                 lam_ref, sg_ref, o_ref, acc_ref, tz_ref, *, n_q_tiles):
    b = pl.program_id(0)
    qi = pl.program_id(2)
    nq = pl.num_programs(2)
    tq, tk = ATT_TQ, ATT_TK
    hw = ATT_CHAIN
    n_half = tq // hw
    nk = nq * (tq // tk)
    qt = qt_ref[0]
    feat = lax.broadcasted_iota(I32, qt.shape, 0)
    zero = jnp.zeros_like(qt)
    qts = (jnp.where(feat < DIFF_HEAD_DIM, qt, zero), jnp.where(feat >= DIFF_HEAD_DIM, qt, zero))
    luts = (lut_ref[0, 0:1, :], lut_ref[0, 1:2, :])
    fars = tuple(t[:, BIAS_LUT - 1:BIAS_LUT] for t in luts)
    pq = pr_ref[0]
    qmin = qmin_ref[b * nq + qi]
    czero = jnp.zeros((1, 1), F32)
    ones_rows = jnp.ones((16, tk), BF16)
    chains = [(mi, hi) for mi in range(2) for hi in range(n_half)]

    def gather_bias(mi, dist):
        table = jnp.broadcast_to(luts[mi], (tk, BIAS_LUT))
        return jnp.concatenate([jnp.take_along_axis(table, dist[:, o:o + 128], axis=1)
                                for o in range(0, hw, 128)], axis=1)

    @pl.when(qi == 0)
    def _():
        delta = (lax.broadcasted_iota(I32, (tk, hw), 1) - lax.broadcasted_iota(I32, (tk, hw), 0))
        for mi in range(2):
            diag_bias = gather_bias(mi, jnp.clip(delta, 0, BIAS_LUT - 1))
            tz_ref[mi, 0] = jnp.where(delta >= 0, diag_bias, NEG_INF)
            tz_ref[mi, 1] = gather_bias(mi, jnp.clip(delta + tk, 0, BIAS_LUT - 1))

    def run_blocks(blocks, state):
        loaded = []
        for (j, kinds) in blocks:
            ks = pl.multiple_of(j * tk, tk)
            kb = k_ref[0, pl.ds(ks, tk), :]
            vtb = jnp.concatenate([vt_ref[0, :, pl.ds(ks, tk)], ones_rows], axis=0)
            dist = None
            if any(kd is not None and kd.startswith("near") for kd in kinds):
                pk_rows = jnp.broadcast_to(pr_all_ref[0, :, pl.ds(ks, tk)], (8, tk))
                pk = pk_rows.T[:, 0:1]
                dist = jnp.clip(pq - pk, 0, BIAS_LUT - 1)
            loaded.append((kb, vtb, dist))
        items = [(bi, n) for bi, blk in enumerate(blocks) for n, (mi, hi) in enumerate(chains)
                 if blk[1][hi] is not None]
        scores = {}

        def issue_qk(t):
            bi, n = items[t]
            mi, hi = chains[n]
            scores[t] = jnp.dot(loaded[bi][0], qts[mi][:, hi * hw:(hi + 1) * hw],
                                preferred_element_type=F32)

        state = list(state)
        for t in range(min(ATT_LOOKAHEAD, len(items))):
            issue_qk(t)
        for t, (bi, n) in enumerate(items):
            if t + ATT_LOOKAHEAD < len(items):
                issue_qk(t + ATT_LOOKAHEAD)
            _, vtb, dist = loaded[bi]
            mi, hi = chains[n]
            kind = blocks[bi][1][hi]
            cols = slice(hi * hw, (hi + 1) * hw)
            m, l = state[n]
            s = scores.pop(t)
            c = czero
            if kind == "far":
                c = fars[mi]
            elif kind == "tz_diag":
                s = tz_ref[mi, 0] + s
            elif kind == "tz_sub":
                s = tz_ref[mi, 1] + s
            else:
                s = gather_bias(mi, dist[:, cols]) + s
                if kind == "near_masked":
                    keep = (lax.broadcasted_iota(I32, (tk, hw), 0)
                            <= lax.broadcasted_iota(I32, (tk, hw), 1))
                    s = jnp.where(keep, s, NEG_INF)
            mn = jnp.maximum(m, jnp.max(s, axis=0, keepdims=True) + c)
            alpha = jnp.exp2(m - mn)
            p = jnp.exp2(s - (mn - c))
            pv = jnp.dot(vtb, p.astype(BF16), preferred_element_type=F32)
            l = alpha * l + pv[V_HEAD_DIM:V_HEAD_DIM + 1, :]
            acc_ref[mi, :, cols] = alpha * acc_ref[mi, :, cols] + pv[:V_HEAD_DIM, :]
            state[n] = (mn, l)
        return tuple(state)

    def block_is_far(j):
        return qmin - kmax_ref[b * nk + j] >= BIAS_LUT - 1

    def one_block(j, state):
        return lax.cond(block_is_far(j), lambda st: run_blocks([(j, ("far",) * n_half)], st),
                        lambda st: run_blocks([(j, ("near",) * n_half)], st), state)

    def group_body(width):
        def body(g, carry):
            j0, state = carry
            all_far = block_is_far(j0)
            for u in range(1, width):
                all_far = jnp.logical_and(all_far, block_is_far(j0 + u))
            far_blocks = [(j0 + u, ("far",) * n_half) for u in range(width)]
            state = lax.cond(
                all_far, lambda st: run_blocks(far_blocks, st),
                lambda st: lax.fori_loop(0, width, lambda u, s2: one_block(j0 + u, s2), st), state)
            return j0 + width, state
        return body

    acc_ref[...] = jnp.zeros_like(acc_ref)
    m0 = jnp.full((1, hw), NEG_INF, F32)
    l0 = jnp.zeros((1, hw), F32)
    state = tuple((m0, l0) for _ in chains)
    assert hw == tk and n_half % 2 == 0
    n_full = n_half * qi
    n_main = jnp.maximum(n_full - 2, 0)
    n_groups = n_main // ATT_GROUP
    j0, state = lax.fori_loop(0, n_groups, group_body(ATT_GROUP), (jnp.int32(0), state))
    rem = n_main - n_groups * ATT_GROUP

    consec = consec_ref[b * nq + qi] == 1

    def diag_kinds(d, on_diag, below, further):
        return tuple(None if hi < d else on_diag if hi == d else below if hi == d + 1 else further
                     for hi in range(n_half))

    diag_fast = [(n_full + d, diag_kinds(d, "tz_diag", "tz_sub", "far")) for d in range(n_half)]
    diag_any = [(n_full + d, diag_kinds(d, "near_masked", "near", "near")) for d in range(n_half)]
    below_fast = [(n_full - 2, ("far",) * n_half),
                  (n_full - 1, ("tz_sub",) + ("far",) * (n_half - 1))]
    below_any = [(n_full - 2, ("near",) * n_half), (n_full - 1, ("near",) * n_half)]

    def first_tile(st):
        return lax.cond(consec, lambda s2: run_blocks(diag_fast, s2),
                        lambda s2: run_blocks(diag_any, s2), st)

    def slow_rest(st):
        st = lax.fori_loop(0, rem, lambda u, s2: one_block(j0 + u, s2), st)
        return lax.cond(consec, lambda s2: run_blocks(below_fast + diag_fast, s2),
                        lambda s2: run_blocks(below_any + diag_any, s2), st)

    def fast_rest(r):
        left = [(j0 + u, ("far",) * n_half) for u in range(r)]
        return lambda st: run_blocks(left + below_fast + diag_fast, st)

    left_counts = sorted({(n_half * q - 2) % ATT_GROUP for q in range(1, n_q_tiles)})
    left_far = consec
    for u in range(max(left_counts, default=0)):
        far_u = block_is_far(jnp.minimum(j0 + u, nk - 1))
        left_far = jnp.logical_and(left_far, jnp.logical_or(u >= rem, far_u))

    def later_tile(st):
        out = slow_rest
        for r in left_counts:
            out = (lambda r, nxt: lambda s2: lax.cond(
                jnp.logical_and(left_far, rem == r), fast_rest(r), nxt, s2))(r, out)
        return out(st)

    state = lax.cond(qi > 0, later_tile, first_tile, state)

    l1 = jnp.concatenate([state[n][1] for n, (mi, hi) in enumerate(chains) if mi == 0], axis=1)
    l2 = jnp.concatenate([state[n][1] for n, (mi, hi) in enumerate(chains) if mi == 1], axis=1)
    lam = (jnp.exp(jnp.sum(lam_ref[0:1, :] * lam_ref[1:2, :], axis=-1, keepdims=True))
           - jnp.exp(jnp.sum(lam_ref[2:3, :] * lam_ref[3:4, :], axis=-1, keepdims=True))
           + LAMBDA_INIT)
    ot = acc_ref[0] * (1.0 / l1) - (lam * (1.0 / l2)) * acc_ref[1]
    ot = ot * lax.rsqrt(jnp.mean(ot * ot, axis=0, keepdims=True) + SUBLN_EPS)
    ot = ot * (sg_ref[...] * (1.0 - LAMBDA_INIT))
    o_ref[0] = ot.T.astype(BF16)


def _attention(qt, k, vt, positions, lut, lam_params, subln_g_col):
    B, S, _ = k.shape
    tq = ATT_TQ
    nq = S // tq
    pos_row = positions.reshape(B, 1, S)
    qmin = jnp.min(positions.reshape(B * nq, tq), axis=1)
    kmax = jnp.max(positions.reshape(B * (S // ATT_TK), ATT_TK), axis=1)
    step_ok = jnp.concatenate([positions[:, 1:] - positions[:, :-1] == 1,
                               jnp.ones((B, 1), jnp.bool_)], axis=1).reshape(B, nq, tq)
    inner_ok = jnp.all(step_ok[:, :, :tq - 1], axis=2)
    link_ok = jnp.concatenate([jnp.ones((B, 1), jnp.bool_), step_ok[:, :-1, tq - 1]], axis=1)
    prev_ok = jnp.concatenate([jnp.ones((B, 1), jnp.bool_), inner_ok[:, :-1]], axis=1)
    consec = (inner_ok & link_ok & prev_ok).astype(I32).reshape(B * nq)
    grid_spec = pltpu.PrefetchScalarGridSpec(
        num_scalar_prefetch=3,
        grid=(B, N_DIFF_HEADS, nq),
        in_specs=[pl.BlockSpec((1, 128, tq), lambda b, h, i, *_: (b, h, i)),
                  pl.BlockSpec((1, S, 128), lambda b, h, i, *_: (b, 0, h)),
                  pl.BlockSpec((1, 128, S), lambda b, h, i, *_: (b, h, 0)),
                  pl.BlockSpec((1, 1, tq), lambda b, h, i, *_: (b, 0, i)),
                  pl.BlockSpec((1, 1, S), lambda b, h, i, *_: (b, 0, 0)),
                  pl.BlockSpec((1, 2, BIAS_LUT), lambda b, h, i, *_: (h, 0, 0)),
                  pl.BlockSpec((4, DIFF_HEAD_DIM), lambda b, h, i, *_: (0, 0)),
                  pl.BlockSpec((V_HEAD_DIM, 1), lambda b, h, i, *_: (0, 0))],
        out_specs=pl.BlockSpec((1, tq, 128), lambda b, h, i, *_: (b, i, h)),
        scratch_shapes=[pltpu.VMEM((2, V_HEAD_DIM, tq), F32),
                        pltpu.VMEM((2, 2, ATT_TK, ATT_CHAIN), F32)],
    )
    return pl.pallas_call(
        functools.partial(_attn_kernel, n_q_tiles=nq),
        grid_spec=grid_spec,
        out_shape=jax.ShapeDtypeStruct((B, S, ATTN_WIDTH), BF16),
        compiler_params=pltpu.CompilerParams(
            dimension_semantics=("arbitrary", "arbitrary", "arbitrary"),
            vmem_limit_bytes=VMEM_LIMIT),
        name="diffattn",
    )(qmin, kmax, consec, qt, k, vt, pos_row, pos_row, lut, lam_params, subln_g_col)


def _outproj_kernel(at_ref, cv_ref, x_ref, g1_ref, sc_ref, sh_ref, gn_ref, wo_ref, wr_ref, rb_ref,
                    x1_ref, hp_ref, ri_ref, rw_ref):
    sub = x_ref.shape[1] // OUT_SUBTILES
    groups = [slice(s * sub, (s + 1) * sub) for s in range(OUT_SUBTILES)]
    mixes = [jnp.dot(at_ref[0, rows, :], wo_ref[0:ATTN_WIDTH, :], preferred_element_type=F32)
             + jnp.dot(cv_ref[0, rows, :], wo_ref[ATTN_WIDTH:, :], preferred_element_type=F32)
             for rows in groups]
    for rows, mix in zip(groups, mixes):
        _outproj_rows(rows, mix, x_ref, g1_ref, sc_ref, sh_ref, gn_ref, wr_ref, rb_ref,
                      x1_ref, hp_ref, ri_ref, rw_ref)


def _outproj_rows(rows, mix, x_ref, g1_ref, sc_ref, sh_ref, gn_ref, wr_ref, rb_ref,
                  x1_ref, hp_ref, ri_ref, rw_ref):
    tm = rows.stop - rows.start
    x1 = x_ref[0, rows, :] + g1_ref[0] * mix
    x1_ref[0, rows, :] = x1
    ms = jnp.mean(x1 * x1, axis=-1, keepdims=True)
    h = x1 * lax.rsqrt(ms + NORM_EPS) * gn_ref[...]
    h = h * (1.0 + sc_ref[0]) + sh_ref[0]
    hb = h.astype(BF16)

    _store_packed_rows(hp_ref.at[0, pl.ds(rows.start * ROW_SLAB, tm * ROW_SLAB)], h)

    lg_all = lax.dot_general(wr_ref[...], hb, (((1,), (1,)), ((), ())),
                             preferred_element_type=F32) + rb_ref[...]
    lg = lg_all[0:N_GROUPS, :]
    le = lg_all[N_GROUPS:N_GROUPS + N_EXPERTS, :]
    row4 = lax.broadcasted_iota(I32, (N_GROUPS, tm), 0)
    gmax = jnp.max(lg, axis=0, keepdims=True)
    pg_sel = 1.0 / jnp.sum(jnp.exp(lg - gmax), axis=0, keepdims=True)
    gsel = jnp.min(jnp.where(lg == gmax, row4, N_GROUPS), axis=0, keepdims=True)
    sel = jnp.zeros((EXPERTS_PER_GROUP, tm), F32)
    for g in range(N_GROUPS):
        sel = jnp.where(gsel == g, le[g * EXPERTS_PER_GROUP:(g + 1) * EXPERTS_PER_GROUP, :], sel)
    v1 = jnp.max(sel, axis=0, keepdims=True)
    i1 = jnp.min(jnp.where(sel == v1, row4, EXPERTS_PER_GROUP), axis=0, keepdims=True)
    rest = jnp.where(row4 == i1, -jnp.inf, sel)
    v2 = jnp.max(rest, axis=0, keepdims=True)
    i2 = jnp.min(jnp.where(rest == v2, row4, EXPERTS_PER_GROUP), axis=0, keepdims=True)
    e2 = jnp.exp(v2 - v1)
    w1 = pg_sel / (1.0 + e2)
    w2 = pg_sel * e2 / (1.0 + e2)
    row8 = lax.broadcasted_iota(I32, (8, tm), 0)
    eid1 = gsel * EXPERTS_PER_GROUP + i1
    eid2 = gsel * EXPERTS_PER_GROUP + i2
    ri_ref[0, :, rows] = jnp.where(row8 == 0, eid1, jnp.where(row8 == 1, eid2, 0))
    rw_ref[0, :, rows] = jnp.where(row8 == 0, w1, jnp.where(row8 == 1, w2, 0.0))


def _outproj(attn, conv, x, g1, sc2, sh2, g2n, w_out_bf, wr_t, rb):
    B, S, D = x.shape
    tm = OUT_TILE
    half_spec = pl.BlockSpec((1, tm, 512), lambda b, j: (b, j, 0))
    full_spec = pl.BlockSpec((1, tm, D), lambda b, j: (b, j, 0))
    mod_spec = pl.BlockSpec((1, 1, D), lambda b, j: (b, 0, 0))
    rt_spec = pl.BlockSpec((1, 8, tm), lambda b, j: (b, 0, j))
    return pl.pallas_call(
        _outproj_kernel,
        grid=(B, S // tm),
        in_specs=[half_spec, half_spec, full_spec, mod_spec, mod_spec, mod_spec,
                  pl.BlockSpec((1, D), lambda b, j: (0, 0)),
                  pl.BlockSpec((D, D), lambda b, j: (0, 0)),
                  pl.BlockSpec((ROUTER_ROWS, D), lambda b, j: (0, 0)),
                  pl.BlockSpec((ROUTER_ROWS, 1), lambda b, j: (0, 0))],
        out_specs=[full_spec, pl.BlockSpec((1, tm * ROW_SLAB, 128), lambda b, j: (b, j, 0)),
                   rt_spec, rt_spec],
        out_shape=[jax.ShapeDtypeStruct((B, S, D), F32),
                   jax.ShapeDtypeStruct((B, S * ROW_SLAB, 128), U32),
                   jax.ShapeDtypeStruct((B, 8, S), I32),
                   jax.ShapeDtypeStruct((B, 8, S), F32)],
        compiler_params=pltpu.CompilerParams(
            dimension_semantics=("arbitrary", "arbitrary"), vmem_limit_bytes=VMEM_LIMIT),
        name="outproj",
    )(attn, conv, x, g1, sc2, sh2, g2n, w_out_bf, wr_t, rb)


def _store_packed_rows(dst, x):
    half = D_MODEL // 2
    xb = x.astype(BF16).astype(F32)
    packed = (pltpu.bitcast(xb[:, :half], U32) >> 16) | (
        pltpu.bitcast(xb[:, half:], U32) & jnp.uint32(0xFFFF0000))
    rows = x.shape[0]
    for c in range(ROW_SLAB):
        dst[pl.ds(c, rows, stride=ROW_SLAB), :] = packed[:, c * 128:(c + 1) * 128]


def _load_packed_rows(src, row0, rows):
    packed = jnp.concatenate(
        [src[pl.ds(row0 * ROW_SLAB + c, rows, stride=ROW_SLAB), :] for c in range(ROW_SLAB)], axis=1)
    lo = pltpu.bitcast(packed << 16, F32)
    hi = pltpu.bitcast(packed & jnp.uint32(0xFFFF0000), F32)
    return lo, hi


def _row_gather_start(src_hbm, idx_ref, dst, sem, n_rows):
    for r in range(n_rows):
        off = pl.multiple_of(idx_ref[0, 0, r], ROW_SLAB)
        pltpu.make_async_copy(src_hbm.at[pl.ds(off, ROW_SLAB)],
                              dst.at[pl.ds(r * ROW_SLAB, ROW_SLAB)], sem).start(priority=r % 2)


def _row_gather_wait(src_hbm, dst, sem, n_rows):
    pltpu.make_async_copy(src_hbm.at[pl.ds(0, n_rows * ROW_SLAB)], dst, sem).wait()


def _dispatch_kernel(fill_ref, pos_ref, hp_hbm, xs_hbm, hp_ref, zbuf, sem, load_sem):
    i = pl.program_id(0)
    tm = COMB_TM
    tile_rows = MOE_TM * ROW_SLAB
    n_steps = hp_ref.shape[0] // (tm * ROW_SLAB)
    chunk_rows = hp_ref.shape[0] // DISPATCH_CHUNKS

    def load(k):
        rows = pl.ds(k * chunk_rows, chunk_rows)
        return pltpu.make_async_copy(hp_hbm.at[rows], hp_ref.at[rows], load_sem.at[k])

    @pl.when(i == 0)
    def _():
        for k in range(DISPATCH_CHUNKS):
            load(k).start()
        zbuf[...] = jnp.zeros_like(zbuf)

        def fill_copy(t):
            return pltpu.make_async_copy(
                zbuf, xs_hbm.at[pl.ds(pl.multiple_of(t * tile_rows, tile_rows), tile_rows)],
                sem.at[0])

        def start(t, c):
            @pl.when(fill_ref[t] == 1)
            def _():
                fill_copy(t).start()
            return c

        def wait(t, c):
            @pl.when(fill_ref[t] == 1)
            def _():
                fill_copy(t).wait()
            return c

        lax.fori_loop(0, fill_ref.shape[0], start, 0)
        lax.fori_loop(0, fill_ref.shape[0], wait, 0)

    for k in range(DISPATCH_CHUNKS):
        @pl.when(i == k * (n_steps // DISPATCH_CHUNKS))
        def _():
            load(k).wait()

    par = lax.rem(i, 2)
    base = pl.multiple_of(i * (tm * ROW_SLAB), tm * ROW_SLAB)
    for r in range(2 * tm):
        off = pl.multiple_of(pos_ref[0, 0, r], ROW_SLAB)
        pltpu.make_async_copy(hp_ref.at[pl.ds(base + (r % tm) * ROW_SLAB, ROW_SLAB)],
                              xs_hbm.at[pl.ds(off, ROW_SLAB)],
                              sem.at[1 + par]).start(priority=r % 2)

    def drain(parity):
        for _ in range(2):
            pltpu.make_async_copy(hp_ref.at[pl.ds(0, tm * ROW_SLAB)],
                                  xs_hbm.at[pl.ds(0, tm * ROW_SLAB)], sem.at[1 + parity]).wait()

    @pl.when(i > 0)
    def _():
        drain(1 - par)

    @pl.when(i == pl.num_programs(0) - 1)
    def _():
        drain(par)


def _dispatch(tile_fill, pos_tiles, hp, n_tiles):
    T = hp.shape[0] // ROW_SLAB
    tm = COMB_TM
    grid_spec = pltpu.PrefetchScalarGridSpec(
        num_scalar_prefetch=1,
        grid=(T // tm,),
        in_specs=[pl.BlockSpec((1, 1, 2 * tm), lambda i, f: (i, 0, 0), memory_space=pltpu.SMEM),
                  pl.BlockSpec(memory_space=pl.ANY)],
        out_specs=pl.BlockSpec(memory_space=pl.ANY),
        scratch_shapes=[pltpu.VMEM(hp.shape, U32),
                        pltpu.VMEM((MOE_TM * ROW_SLAB, 128), U32),
                        pltpu.SemaphoreType.DMA((3,)),
                        pltpu.SemaphoreType.DMA((DISPATCH_CHUNKS,))],
    )
    return pl.pallas_call(
        _dispatch_kernel,
        grid_spec=grid_spec,
        out_shape=jax.ShapeDtypeStruct((n_tiles * MOE_TM * ROW_SLAB, 128), U32),
        compiler_params=pltpu.CompilerParams(
            dimension_semantics=("arbitrary",), vmem_limit_bytes=VMEM_LIMIT),
        name="dispatch",
    )(tile_fill, pos_tiles, hp)


def _moe_kernel(te_ref, tv_ref, ne_ref, xs_ref, wg_hbm, wu_hbm, wd_hbm, y_ref,
                wg_bf, wu_bf, wd_bf, wg_st, wu_st, wd_st, sem):
    i = pl.program_id(0)
    tm = MOE_TM

    def weight_copies(e):
        return (pltpu.make_async_copy(wg_hbm.at[e], wg_st, sem.at[0]),
                pltpu.make_async_copy(wu_hbm.at[e], wu_st, sem.at[1]),
                pltpu.make_async_copy(wd_hbm.at[e], wd_st, sem.at[2]))

    @pl.when(i == 0)
    def _():
        for cp in weight_copies(te_ref[0]):
            cp.start()

    @pl.when(jnp.logical_or(i == 0, te_ref[i] != te_ref[jnp.maximum(i - 1, 0)]))
    def _():
        for cp in weight_copies(te_ref[i]):
            cp.wait()
        wg_bf[...] = wg_st[...].astype(BF16)
        wu_bf[...] = wu_st[...].astype(BF16)
        wd_bf[...] = wd_st[...].astype(BF16)

        @pl.when(ne_ref[i] >= 0)
        def _():
            for cp in weight_copies(ne_ref[i]):
                cp.start()

    @pl.when(tv_ref[i] == 1)
    def _():
        half = D_MODEL // 2
        lo, hi = _load_packed_rows(xs_ref, 0, tm)
        lo = lo.astype(BF16)
        hi = hi.astype(BF16)
        g = (jnp.dot(lo, wg_bf[0:half, :], preferred_element_type=F32)
             + jnp.dot(hi, wg_bf[half:, :], preferred_element_type=F32))
        u = (jnp.dot(lo, wu_bf[0:half, :], preferred_element_type=F32)
             + jnp.dot(hi, wu_bf[half:, :], preferred_element_type=F32))
        hid = (_silu(g) * u).astype(BF16)
        _store_packed_rows(y_ref, jnp.dot(hid, wd_bf[...], preferred_element_type=F32))

    @pl.when(tv_ref[i] == 0)
    def _():
        y_ref[...] = jnp.zeros_like(y_ref)


def _moe(tile_expert, tile_valid, tile_next_expert, xs, wg, wu, wd):
    nt = tile_expert.shape[0]
    tm = MOE_TM
    D = D_MODEL
    any_spec = pl.BlockSpec(memory_space=pl.ANY)
    grid_spec = pltpu.PrefetchScalarGridSpec(
        num_scalar_prefetch=3,
        grid=(nt,),
        in_specs=[pl.BlockSpec((tm * ROW_SLAB, 128), lambda i, *_: (i, 0)),
                  any_spec, any_spec, any_spec],
        out_specs=pl.BlockSpec((tm * ROW_SLAB, 128), lambda i, *_: (i, 0)),
        scratch_shapes=[pltpu.VMEM((D, D_EXPERT), BF16), pltpu.VMEM((D, D_EXPERT), BF16),
                        pltpu.VMEM((D_EXPERT, D), BF16),
                        pltpu.VMEM((D, D_EXPERT), F32), pltpu.VMEM((D, D_EXPERT), F32),
                        pltpu.VMEM((D_EXPERT, D), F32),
                        pltpu.SemaphoreType.DMA((3,))],
    )
    return pl.pallas_call(
        _moe_kernel,
        grid_spec=grid_spec,
        out_shape=jax.ShapeDtypeStruct((nt * tm * ROW_SLAB, 128), U32),
        compiler_params=pltpu.CompilerParams(
            dimension_semantics=("arbitrary",), vmem_limit_bytes=VMEM_LIMIT),
        name="moe",
    )(tile_expert, tile_valid, tile_next_expert, xs, wg, wu, wd)


def _combine_kernel(pos_ref, posn_ref, ys_hbm, x1_ref, g2_ref, w_ref, fg_ref, o_ref, rbuf, sem):
    i = pl.program_id(0)
    n = pl.num_programs(0)
    slot = lax.rem(i, 2)
    nslot = 1 - slot
    tm = COMB_TM

    @pl.when(i == 0)
    def _():
        _row_gather_start(ys_hbm, pos_ref, rbuf.at[0], sem.at[0], 2 * tm)

    @pl.when(i + 1 < n)
    def _():
        _row_gather_start(ys_hbm, posn_ref, rbuf.at[nslot], sem.at[nslot], 2 * tm)

    _row_gather_wait(ys_hbm, rbuf.at[slot], sem.at[slot], 2 * tm)
    w = w_ref[...]
    r1 = jnp.concatenate(_load_packed_rows(rbuf.at[slot], 0, tm), axis=1)
    r2 = jnp.concatenate(_load_packed_rows(rbuf.at[slot], tm, tm), axis=1)
    moe = w[:, 0:1] * r1 + w[:, 1:2] * r2
    y = x1_ref[...] + g2_ref[0] * moe
    ms = jnp.mean(y * y, axis=-1, keepdims=True)
    o_ref[...] = y * lax.rsqrt(ms + NORM_EPS) * fg_ref[...]


def _combine(pos_tiles, ys, x1, g2, w_tok, final_g, seq_len):
    T, D = x1.shape
    tm = COMB_TM
    nt = T // tm
    per_b = seq_len // tm
    return pl.pallas_call(
        _combine_kernel,
        grid=(nt,),
        in_specs=[pl.BlockSpec((1, 1, 2 * tm), lambda i: (i, 0, 0), memory_space=pltpu.SMEM),
                  pl.BlockSpec((1, 1, 2 * tm), lambda i: (jnp.minimum(i + 1, nt - 1), 0, 0),
                               memory_space=pltpu.SMEM),
                  pl.BlockSpec(memory_space=pl.ANY),
                  pl.BlockSpec((tm, D), lambda i: (i, 0)),
                  pl.BlockSpec((1, 1, D), lambda i: (i // per_b, 0, 0)),
                  pl.BlockSpec((tm, 2), lambda i: (i, 0)),
                  pl.BlockSpec((1, D), lambda i: (0, 0))],
        out_specs=pl.BlockSpec((tm, D), lambda i: (i, 0)),
        out_shape=jax.ShapeDtypeStruct((T, D), F32),
        scratch_shapes=[pltpu.VMEM((2, 2 * tm * ROW_SLAB, 128), U32),
                        pltpu.SemaphoreType.DMA((2,))],
        compiler_params=pltpu.CompilerParams(
            dimension_semantics=("arbitrary",), vmem_limit_bytes=VMEM_LIMIT),
        name="combine",
    )(pos_tiles, pos_tiles, ys, x1, g2, w_tok, final_g)


def _rel_bucket_table():
    n = jnp.arange(BIAS_LUT, dtype=I32)
    max_exact = N_BUCKETS // 2
    nf = jnp.maximum(n, 1).astype(F32)
    large = max_exact + (jnp.log(nf / max_exact) / math.log(MAX_DISTANCE / max_exact)
                         * (N_BUCKETS - max_exact)).astype(I32)
    large = jnp.minimum(large, N_BUCKETS - 1)
    return jnp.where(n < max_exact, n, large)


def _route_plan(eid, n_tiles, tm):
    two, T = eid.shape
    e_flat = eid.reshape(-1)
    onehot = (e_flat[:, None] == jnp.arange(N_EXPERTS, dtype=I32)[None, :]).astype(I32)
    csum = jnp.cumsum(onehot, axis=0)
    rank = jnp.sum((csum - onehot) * onehot, axis=1)
    counts = csum[-1]
    ptiles = (counts + tm - 1) // tm
    tend = jnp.cumsum(ptiles)
    tstart = tend - ptiles
    slot = jnp.sum(onehot * tstart[None, :], axis=1) * tm + rank
    total = tend[-1]
    tile_ids = jnp.arange(n_tiles, dtype=I32)
    tile_valid = (tile_ids < total).astype(I32)
    tile_expert = jnp.sum((tile_ids[:, None] >= tend[None, :]).astype(I32), axis=1)
    last_expert = jnp.sum((total - 1 >= tend).astype(I32))
    tile_expert = jnp.minimum(tile_expert, last_expert).astype(I32)
    partial_last = jnp.any((tile_ids[:, None] == tend[None, :] - 1)
                           & (ptiles[None, :] > 0) & (counts[None, :] % tm != 0), axis=1)
    tile_fill = (partial_last | (tile_ids >= total)).astype(I32)
    experts = jnp.arange(N_EXPERTS, dtype=I32)
    later = (experts[None, :] > tile_expert[:, None]) & (ptiles[None, :] > 0)
    tile_next_expert = jnp.min(jnp.where(later, experts[None, :], N_EXPERTS), axis=1)
    tile_next_expert = jnp.where(tile_next_expert == N_EXPERTS, -1, tile_next_expert).astype(I32)
    return slot.astype(I32), tile_expert, tile_valid, tile_fill, tile_next_expert


def kernel(x, c, positions, rel_bias, ada_w, ada_b, norm1_g, w_in, lambda_q1, lambda_k1, lambda_q2,
           lambda_k2, subln_g, conv_w, w_out, norm2_g, router_group_w, router_group_b,
           router_expert_w, router_expert_b, expert_w_gate, expert_w_up, expert_w_down, final_g):
    B, S, D = x.shape
    T = B * S
    l = 0

    ada = _ada(c.reshape(B, D, 1), ada_w[l], ada_b[l].reshape(1, -1))
    sh1, sc1, g1, sh2, sc2, g2 = jnp.split(ada, 6, axis=-1)

    qt, k, vt, conv = _inproj(x, sc1, sh1, norm1_g[l].reshape(1, D), w_in[l], conv_w[l])
    lut = (rel_bias.astype(F32)[_rel_bucket_table(), :].T * LOG2E).reshape(N_DIFF_HEADS, 2, BIAS_LUT)
    lam_params = jnp.stack([lambda_q1[l], lambda_k1[l], lambda_q2[l], lambda_k2[l]]).astype(F32)
    attn = _attention(qt, k, vt, positions, lut, lam_params, subln_g[l].reshape(V_HEAD_DIM, 1))

    pad_rows = ROUTER_ROWS - N_GROUPS - N_EXPERTS
    wr_t = jnp.concatenate([router_group_w[l], router_expert_w[l], jnp.zeros((D, pad_rows), F32)],
                           axis=1).T.astype(BF16)
    rb = jnp.concatenate([router_group_b[l], router_expert_b[l],
                          jnp.zeros((pad_rows,), F32)]).reshape(ROUTER_ROWS, 1)
    x1, hp, ri, rw = _outproj(attn, conv, x, g1, sc2, sh2, norm2_g[l].reshape(1, D),
                              w_out[l].astype(BF16), wr_t, rb)

    eid = ri[:, 0:2, :].transpose(1, 0, 2).reshape(2, T)
    n_tiles = 2 * T // MOE_TM + N_EXPERTS
    slot, tile_expert, tile_valid, tile_fill, tile_next = _route_plan(eid, n_tiles, MOE_TM)
    nct = T // COMB_TM
    pos = (slot * ROW_SLAB).reshape(2, nct, 1, COMB_TM)
    pos_tiles = jnp.concatenate([pos[0], pos[1]], axis=2)

    xs = _dispatch(tile_fill, pos_tiles, hp.reshape(T * ROW_SLAB, 128), n_tiles)
    ys = _moe(tile_expert, tile_valid, tile_next, xs, expert_w_gate[l], expert_w_up[l],
              expert_w_down[l])

    w_tok = rw[:, 0:2, :].transpose(0, 2, 1).reshape(T, 2)
    out = _combine(pos_tiles, ys, x1.reshape(T, D), g2, w_tok, final_g.reshape(1, D), S)
    return out.reshape(B, S, D)
```

```python
import functools
import math

import jax
import jax.numpy as jnp
from jax import lax
from jax.experimental import pallas as pl
from jax.experimental.pallas import tpu as pltpu

F32 = jnp.float32
BF16 = jnp.bfloat16
I32 = jnp.int32
U32 = jnp.uint32

D_MODEL = 1024
ATTN_WIDTH = 512
CONV_WIDTH = 512
N_DIFF_HEADS = 4
DIFF_HEAD_DIM = 64
V_HEAD_DIM = 128
IN_PROJ_WIDTH = 3 * ATTN_WIDTH + 3 * CONV_WIDTH
CONV_K = 3
N_BUCKETS = 32
MAX_DISTANCE = 128
N_GROUPS = 4
EXPERTS_PER_GROUP = 4
N_EXPERTS = 16
D_EXPERT = 512
NORM_EPS = 1e-6
SUBLN_EPS = 1e-5
NEG_INF = -1e30
LAMBDA_INIT = 0.8 - 0.6 * math.exp(-0.3 * 0)
QK_SCALE = DIFF_HEAD_DIM ** -0.5
LOG2E = math.log2(math.e)

BIAS_LUT = 128

ROW_TILE = 512
PROJ_SUBTILES = 2
OUT_TILE = 1024
OUT_SUBTILES = 8
ATT_TQ = 1024
ATT_CHAIN = 256
ATT_GROUP = 8
ATT_LOOKAHEAD = 8
ATT_TK = 256
MOE_TM = 256
COMB_TM = 256
DISPATCH_CHUNKS = 8
ROUTER_ROWS = 32
ROW_SLAB = D_MODEL // 256
VMEM_LIMIT = 56 * 1024 * 1024


def _silu(x):
    return x * (1.0 / (1.0 + jnp.exp(-x)))


def _ada_kernel(c_ref, w_ref, b_ref, o_ref):
    for bi in range(c_ref.shape[0]):
        s = _silu(c_ref[bi])
        o_ref[bi] = jnp.sum(s * w_ref[...], axis=0, keepdims=True) + b_ref[...]


def _ada(c_col, w, b):
    nb, n = c_col.shape[0], w.shape[1]
    bn = 2048
    return pl.pallas_call(
        _ada_kernel,
        grid=(n // bn,),
        in_specs=[pl.BlockSpec((nb, D_MODEL, 1), lambda j: (0, 0, 0)),
                  pl.BlockSpec((D_MODEL, bn), lambda j: (0, j)),
                  pl.BlockSpec((1, bn), lambda j: (0, j))],
        out_specs=pl.BlockSpec((nb, 1, bn), lambda j: (0, 0, j)),
        out_shape=jax.ShapeDtypeStruct((nb, 1, n), F32),
        name="ada",
    )(c_col, w, b)


def _inproj_kernel(x_ref, sc_ref, sh_ref, g_ref, w_hbm, cw_ref,
                   qt_ref, k_ref, vt_ref, conv_ref,
                   carry_ref, wqt_ref, wk_ref, wvt_ref, wc_ref, stage_ref, sem):
    j = pl.program_id(1)
    tm = x_ref.shape[1]
    nt = (((1,), (1,)), ((), ()))

    @pl.when(jnp.logical_and(pl.program_id(0) == 0, j == 0))
    def _():
        n_chunks = IN_PROJ_WIDTH // 512

        def chunk_copy(c):
            return pltpu.make_async_copy(w_hbm.at[:, pl.ds(c * 512, 512)], stage_ref.at[c % 2],
                                         sem.at[c % 2])

        chunk_copy(0).start()
        for c in range(n_chunks):
            if c + 1 < n_chunks:
                chunk_copy(c + 1).start()
            chunk_copy(c).wait()
            w = stage_ref[c % 2]
            if c == 0:
                wqt_ref[...] = w.T.astype(BF16)
            elif c == 1:
                wk_ref[...] = w.astype(BF16)
            elif c == 2:
                wvt_ref[...] = w.T.astype(BF16)
            else:
                wc_ref[:, (c - 3) * 512:(c - 2) * 512] = w.astype(BF16)

    @pl.when(j == 0)
    def _():
        carry_ref[...] = jnp.zeros_like(carry_ref)

    prev = carry_ref[...]
    sub = tm // PROJ_SUBTILES
    for s in range(PROJ_SUBTILES):
        rows = slice(s * sub, (s + 1) * sub)
        x = x_ref[0, rows, :]
        ms = jnp.mean(x * x, axis=-1, keepdims=True)
        h = x * lax.rsqrt(ms + NORM_EPS) * g_ref[...]
        h = h * (1.0 + sc_ref[0]) + sh_ref[0]
        hb = h.astype(BF16)

        def proj(c0):
            return jnp.dot(hb, wc_ref[:, c0:c0 + 512], preferred_element_type=F32)

        qt = lax.dot_general(wqt_ref[...], hb, nt, preferred_element_type=F32)
        qt_ref[0, :, rows] = (qt * (QK_SCALE * LOG2E)).astype(BF16)
        k_ref[0, rows, :] = jnp.dot(hb, wk_ref[...], preferred_element_type=F32).astype(BF16)
        vt_ref[0, :, rows] = lax.dot_general(wvt_ref[...], hb, nt,
                                             preferred_element_type=F32).astype(BF16)
        gate_b = proj(0)
        u = proj(512) * proj(1024)
        row = lax.broadcasted_iota(I32, u.shape, 0)
        u1 = pltpu.roll(u, 1, axis=0)
        u2 = pltpu.roll(u, 2, axis=0)
        u1 = jnp.where(row == 0, prev[7:8, :], u1)
        u2 = jnp.where(row == 0, prev[6:7, :], jnp.where(row == 1, prev[7:8, :], u2))
        conv = cw_ref[0:1, :] * u2 + cw_ref[1:2, :] * u1 + cw_ref[2:3, :] * u
        conv_ref[0, rows, :] = (gate_b * conv).astype(BF16)
        prev = u[sub - 8:sub, :]
    carry_ref[...] = prev


def _inproj(x, sc1, sh1, g1n, w_in, conv_w):
    B, S, D = x.shape
    tm = ROW_TILE
    row_out = jax.ShapeDtypeStruct((B, S, 512), BF16)
    col_out = jax.ShapeDtypeStruct((B, 512, S), BF16)
    row_spec = pl.BlockSpec((1, tm, 512), lambda b, j: (b, j, 0))
    col_spec = pl.BlockSpec((1, 512, tm), lambda b, j: (b, 0, j))
    mod_spec = pl.BlockSpec((1, 1, D), lambda b, j: (b, 0, 0))
    const2 = lambda b, j: (0, 0)
    return pl.pallas_call(
        _inproj_kernel,
        grid=(B, S // tm),
        in_specs=[pl.BlockSpec((1, tm, D), lambda b, j: (b, j, 0)),
                  mod_spec, mod_spec,
                  pl.BlockSpec((1, D), const2),
                  pl.BlockSpec(memory_space=pl.ANY),
                  pl.BlockSpec((CONV_K, CONV_WIDTH), const2)],
        out_specs=[col_spec, row_spec, col_spec, row_spec],
        out_shape=[col_out, row_out, col_out, row_out],
        scratch_shapes=[pltpu.VMEM((8, CONV_WIDTH), F32),
                        pltpu.VMEM((ATTN_WIDTH, D), BF16),
                        pltpu.VMEM((D, ATTN_WIDTH), BF16),
                        pltpu.VMEM((ATTN_WIDTH, D), BF16),
                        pltpu.VMEM((D, 3 * CONV_WIDTH), BF16),
                        pltpu.VMEM((2, D, 512), F32),
                        pltpu.SemaphoreType.DMA((2,))],
        compiler_params=pltpu.CompilerParams(
            dimension_semantics=("arbitrary", "arbitrary"), vmem_limit_bytes=VMEM_LIMIT),
        name="inproj",
    )(x, sc1, sh1, g1n, w_in, conv_w)


def _attn_kernel(qmin_ref, kmax_ref, consec_ref, qt_ref, k_ref, vt_ref, pr_ref, pr_all_ref, lut_ref,
                 lam_ref, sg_ref, o_ref, acc_ref, tz_ref, *, n_q_tiles):
    b = pl.program_id(0)
    qi = pl.program_id(2)
    nq = pl.num_programs(2)
    tq, tk = ATT_TQ, ATT_TK
    hw = ATT_CHAIN
    n_half = tq // hw
    nk = nq * (tq // tk)
    qt = qt_ref[0]
    feat = lax.broadcasted_iota(I32, qt.shape, 0)
    zero = jnp.zeros_like(qt)
    qts = (jnp.where(feat < DIFF_HEAD_DIM, qt, zero), jnp.where(feat >= DIFF_HEAD_DIM, qt, zero))
    luts = (lut_ref[0, 0:1, :], lut_ref[0, 1:2, :])
    fars = tuple(t[:, BIAS_LUT - 1:BIAS_LUT] for t in luts)
    pq = pr_ref[0]
    qmin = qmin_ref[b * nq + qi]
    czero = jnp.zeros((1, 1), F32)
    ones_rows = jnp.ones((16, tk), BF16)
    chains = [(mi, hi) for mi in range(2) for hi in range(n_half)]

    def gather_bias(mi, dist):
        table = jnp.broadcast_to(luts[mi], (tk, BIAS_LUT))
        return jnp.concatenate([jnp.take_along_axis(table, dist[:, o:o + 128], axis=1)
                                for o in range(0, hw, 128)], axis=1)

    @pl.when(qi == 0)
    def _():
        delta = (lax.broadcasted_iota(I32, (tk, hw), 1) - lax.broadcasted_iota(I32, (tk, hw), 0))
        for mi in range(2):
            diag_bias = gather_bias(mi, jnp.clip(delta, 0, BIAS_LUT - 1))
            tz_ref[mi, 0] = jnp.where(delta >= 0, diag_bias, NEG_INF)
            tz_ref[mi, 1] = gather_bias(mi, jnp.clip(delta + tk, 0, BIAS_LUT - 1))

    def run_blocks(blocks, state):
        loaded = []
        for (j, kinds) in blocks:
            ks = pl.multiple_of(j * tk, tk)
            kb = k_ref[0, pl.ds(ks, tk), :]
            vtb = jnp.concatenate([vt_ref[0, :, pl.ds(ks, tk)], ones_rows], axis=0)
            dist = None
            if any(kd is not None and kd.startswith("near") for kd in kinds):
                pk_rows = jnp.broadcast_to(pr_all_ref[0, :, pl.ds(ks, tk)], (8, tk))
                pk = pk_rows.T[:, 0:1]
                dist = jnp.clip(pq - pk, 0, BIAS_LUT - 1)
            loaded.append((kb, vtb, dist))
        items = [(bi, n) for bi, blk in enumerate(blocks) for n, (mi, hi) in enumerate(chains)
                 if blk[1][hi] is not None]
        scores = {}

        def issue_qk(t):
            bi, n = items[t]
            mi, hi = chains[n]
            scores[t] = jnp.dot(loaded[bi][0], qts[mi][:, hi * hw:(hi + 1) * hw],
                                preferred_element_type=F32)

        state = list(state)
        for t in range(min(ATT_LOOKAHEAD, len(items))):
            issue_qk(t)
        for t, (bi, n) in enumerate(items):
            if t + ATT_LOOKAHEAD < len(items):
                issue_qk(t + ATT_LOOKAHEAD)
            _, vtb, dist = loaded[bi]
            mi, hi = chains[n]
            kind = blocks[bi][1][hi]
            cols = slice(hi * hw, (hi + 1) * hw)
            m, l = state[n]
            s = scores.pop(t)
            c = czero
            if kind == "far":
                c = fars[mi]
            elif kind == "tz_diag":
                s = tz_ref[mi, 0] + s
            elif kind == "tz_sub":
                s = tz_ref[mi, 1] + s
            else:
                s = gather_bias(mi, dist[:, cols]) + s
                if kind == "near_masked":
                    keep = (lax.broadcasted_iota(I32, (tk, hw), 0)
                            <= lax.broadcasted_iota(I32, (tk, hw), 1))
                    s = jnp.where(keep, s, NEG_INF)
            mn = jnp.maximum(m, jnp.max(s, axis=0, keepdims=True) + c)
            alpha = jnp.exp2(m - mn)
            p = jnp.exp2(s - (mn - c))
            pv = jnp.dot(vtb, p.astype(BF16), preferred_element_type=F32)
            l = alpha * l + pv[V_HEAD_DIM:V_HEAD_DIM + 1, :]
            acc_ref[mi, :, cols] = alpha * acc_ref[mi, :, cols] + pv[:V_HEAD_DIM, :]
            state[n] = (mn, l)
        return tuple(state)

    def block_is_far(j):
        return qmin - kmax_ref[b * nk + j] >= BIAS_LUT - 1

    def one_block(j, state):
        return lax.cond(block_is_far(j), lambda st: run_blocks([(j, ("far",) * n_half)], st),
                        lambda st: run_blocks([(j, ("near",) * n_half)], st), state)

    def group_body(width):
        def body(g, carry):
            j0, state = carry
            all_far = block_is_far(j0)
            for u in range(1, width):
                all_far = jnp.logical_and(all_far, block_is_far(j0 + u))
            far_blocks = [(j0 + u, ("far",) * n_half) for u in range(width)]
            state = lax.cond(
                all_far, lambda st: run_blocks(far_blocks, st),
                lambda st: lax.fori_loop(0, width, lambda u, s2: one_block(j0 + u, s2), st), state)
            return j0 + width, state
        return body

    acc_ref[...] = jnp.zeros_like(acc_ref)
    m0 = jnp.full((1, hw), NEG_INF, F32)
    l0 = jnp.zeros((1, hw), F32)
    state = tuple((m0, l0) for _ in chains)
    assert hw == tk and n_half % 2 == 0
    n_full = n_half * qi
    n_main = jnp.maximum(n_full - 2, 0)
    n_groups = n_main // ATT_GROUP
    j0, state = lax.fori_loop(0, n_groups, group_body(ATT_GROUP), (jnp.int32(0), state))
    rem = n_main - n_groups * ATT_GROUP

    consec = consec_ref[b * nq + qi] == 1

    def diag_kinds(d, on_diag, below, further):
        return tuple(None if hi < d else on_diag if hi == d else below if hi == d + 1 else further
                     for hi in range(n_half))

    diag_fast = [(n_full + d, diag_kinds(d, "tz_diag", "tz_sub", "far")) for d in range(n_half)]
    diag_any = [(n_full + d, diag_kinds(d, "near_masked", "near", "near")) for d in range(n_half)]
    below_fast = [(n_full - 2, ("far",) * n_half),
                  (n_full - 1, ("tz_sub",) + ("far",) * (n_half - 1))]
    below_any = [(n_full - 2, ("near",) * n_half), (n_full - 1, ("near",) * n_half)]

    def first_tile(st):
        return lax.cond(consec, lambda s2: run_blocks(diag_fast, s2),
                        lambda s2: run_blocks(diag_any, s2), st)

    def slow_rest(st):
        st = lax.fori_loop(0, rem, lambda u, s2: one_block(j0 + u, s2), st)
        return lax.cond(consec, lambda s2: run_blocks(below_fast + diag_fast, s2),
                        lambda s2: run_blocks(below_any + diag_any, s2), st)

    def fast_rest(r):
        left = [(j0 + u, ("far",) * n_half) for u in range(r)]
        return lambda st: run_blocks(left + below_fast + diag_fast, st)

    left_counts = sorted({(n_half * q - 2) % ATT_GROUP for q in range(1, n_q_tiles)})
    left_far = consec
    for u in range(max(left_counts, default=0)):
        far_u = block_is_far(jnp.minimum(j0 + u, nk - 1))
        left_far = jnp.logical_and(left_far, jnp.logical_or(u >= rem, far_u))

    def later_tile(st):
        out = slow_rest
        for r in left_counts:
            out = (lambda r, nxt: lambda s2: lax.cond(
                jnp.logical_and(left_far, rem == r), fast_rest(r), nxt, s2))(r, out)
        return out(st)

    state = lax.cond(qi > 0, later_tile, first_tile, state)

    l1 = jnp.concatenate([state[n][1] for n, (mi, hi) in enumerate(chains) if mi == 0], axis=1)
    l2 = jnp.concatenate([state[n][1] for n, (mi, hi) in enumerate(chains) if mi == 1], axis=1)
    lam = (jnp.exp(jnp.sum(lam_ref[0:1, :] * lam_ref[1:2, :], axis=-1, keepdims=True))
           - jnp.exp(jnp.sum(lam_ref[2:3, :] * lam_ref[3:4, :], axis=-1, keepdims=True))
           + LAMBDA_INIT)
    ot = acc_ref[0] * (1.0 / l1) - (lam * (1.0 / l2)) * acc_ref[1]
    ot = ot * lax.rsqrt(jnp.mean(ot * ot, axis=0, keepdims=True) + SUBLN_EPS)
    ot = ot * (sg_ref[...] * (1.0 - LAMBDA_INIT))
    o_ref[0] = ot.T.astype(BF16)


def _attention(qt, k, vt, positions, lut, lam_params, subln_g_col):
    B, S, _ = k.shape
    tq = ATT_TQ
    nq = S // tq
    pos_row = positions.reshape(B, 1, S)
    qmin = jnp.min(positions.reshape(B * nq, tq), axis=1)
    kmax = jnp.max(positions.reshape(B * (S // ATT_TK), ATT_TK), axis=1)
    step_ok = jnp.concatenate([positions[:, 1:] - positions[:, :-1] == 1,
                               jnp.ones((B, 1), jnp.bool_)], axis=1).reshape(B, nq, tq)
    inner_ok = jnp.all(step_ok[:, :, :tq - 1], axis=2)
    link_ok = jnp.concatenate([jnp.ones((B, 1), jnp.bool_), step_ok[:, :-1, tq - 1]], axis=1)
    prev_ok = jnp.concatenate([jnp.ones((B, 1), jnp.bool_), inner_ok[:, :-1]], axis=1)
    consec = (inner_ok & link_ok & prev_ok).astype(I32).reshape(B * nq)
    grid_spec = pltpu.PrefetchScalarGridSpec(
        num_scalar_prefetch=3,
        grid=(B, N_DIFF_HEADS, nq),
        in_specs=[pl.BlockSpec((1, 128, tq), lambda b, h, i, *_: (b, h, i)),
                  pl.BlockSpec((1, S, 128), lambda b, h, i, *_: (b, 0, h)),
                  pl.BlockSpec((1, 128, S), lambda b, h, i, *_: (b, h, 0)),
                  pl.BlockSpec((1, 1, tq), lambda b, h, i, *_: (b, 0, i)),
                  pl.BlockSpec((1, 1, S), lambda b, h, i, *_: (b, 0, 0)),
                  pl.BlockSpec((1, 2, BIAS_LUT), lambda b, h, i, *_: (h, 0, 0)),
                  pl.BlockSpec((4, DIFF_HEAD_DIM), lambda b, h, i, *_: (0, 0)),
                  pl.BlockSpec((V_HEAD_DIM, 1), lambda b, h, i, *_: (0, 0))],
        out_specs=pl.BlockSpec((1, tq, 128), lambda b, h, i, *_: (b, i, h)),
        scratch_shapes=[pltpu.VMEM((2, V_HEAD_DIM, tq), F32),
                        pltpu.VMEM((2, 2, ATT_TK, ATT_CHAIN), F32)],
    )
    return pl.pallas_call(
        functools.partial(_attn_kernel, n_q_tiles=nq),
        grid_spec=grid_spec,
        out_shape=jax.ShapeDtypeStruct((B, S, ATTN_WIDTH), BF16),
        compiler_params=pltpu.CompilerParams(
            dimension_semantics=("arbitrary", "arbitrary", "arbitrary"),
            vmem_limit_bytes=VMEM_LIMIT),
        name="diffattn",
    )(qmin, kmax, consec, qt, k, vt, pos_row, pos_row, lut, lam_params, subln_g_col)


def _outproj_kernel(at_ref, cv_ref, x_ref, g1_ref, sc_ref, sh_ref, gn_ref, wo_ref, wr_ref, rb_ref,
                    x1_ref, hp_ref, ri_ref, rw_ref):
    sub = x_ref.shape[1] // OUT_SUBTILES
    groups = [slice(s * sub, (s + 1) * sub) for s in range(OUT_SUBTILES)]
    mixes = [jnp.dot(at_ref[0, rows, :], wo_ref[0:ATTN_WIDTH, :], preferred_element_type=F32)
             + jnp.dot(cv_ref[0, rows, :], wo_ref[ATTN_WIDTH:, :], preferred_element_type=F32)
             for rows in groups]
    for rows, mix in zip(groups, mixes):
        _outproj_rows(rows, mix, x_ref, g1_ref, sc_ref, sh_ref, gn_ref, wr_ref, rb_ref,
                      x1_ref, hp_ref, ri_ref, rw_ref)


def _outproj_rows(rows, mix, x_ref, g1_ref, sc_ref, sh_ref, gn_ref, wr_ref, rb_ref,
                  x1_ref, hp_ref, ri_ref, rw_ref):
    tm = rows.stop - rows.start
    x1 = x_ref[0, rows, :] + g1_ref[0] * mix
    x1_ref[0, rows, :] = x1
    ms = jnp.mean(x1 * x1, axis=-1, keepdims=True)
    h = x1 * lax.rsqrt(ms + NORM_EPS) * gn_ref[...]
    h = h * (1.0 + sc_ref[0]) + sh_ref[0]
    hb = h.astype(BF16)

    _store_packed_rows(hp_ref.at[0, pl.ds(rows.start * ROW_SLAB, tm * ROW_SLAB)], h)

    lg_all = lax.dot_general(wr_ref[...], hb, (((1,), (1,)), ((), ())),
                             preferred_element_type=F32) + rb_ref[...]
    lg = lg_all[0:N_GROUPS, :]
    le = lg_all[N_GROUPS:N_GROUPS + N_EXPERTS, :]
    row4 = lax.broadcasted_iota(I32, (N_GROUPS, tm), 0)
    gmax = jnp.max(lg, axis=0, keepdims=True)
    pg_sel = 1.0 / jnp.sum(jnp.exp(lg - gmax), axis=0, keepdims=True)
    gsel = jnp.min(jnp.where(lg == gmax, row4, N_GROUPS), axis=0, keepdims=True)
    sel = jnp.zeros((EXPERTS_PER_GROUP, tm), F32)
    for g in range(N_GROUPS):
        sel = jnp.where(gsel == g, le[g * EXPERTS_PER_GROUP:(g + 1) * EXPERTS_PER_GROUP, :], sel)
    v1 = jnp.max(sel, axis=0, keepdims=True)
    i1 = jnp.min(jnp.where(sel == v1, row4, EXPERTS_PER_GROUP), axis=0, keepdims=True)
    rest = jnp.where(row4 == i1, -jnp.inf, sel)
    v2 = jnp.max(rest, axis=0, keepdims=True)
    i2 = jnp.min(jnp.where(rest == v2, row4, EXPERTS_PER_GROUP), axis=0, keepdims=True)
    e2 = jnp.exp(v2 - v1)
    w1 = pg_sel / (1.0 + e2)
    w2 = pg_sel * e2 / (1.0 + e2)
    row8 = lax.broadcasted_iota(I32, (8, tm), 0)
    eid1 = gsel * EXPERTS_PER_GROUP + i1
    eid2 = gsel * EXPERTS_PER_GROUP + i2
    ri_ref[0, :, rows] = jnp.where(row8 == 0, eid1, jnp.where(row8 == 1, eid2, 0))
    rw_ref[0, :, rows] = jnp.where(row8 == 0, w1, jnp.where(row8 == 1, w2, 0.0))


def _outproj(attn, conv, x, g1, sc2, sh2, g2n, w_out_bf, wr_t, rb):
    B, S, D = x.shape
    tm = OUT_TILE
    half_spec = pl.BlockSpec((1, tm, 512), lambda b, j: (b, j, 0))
    full_spec = pl.BlockSpec((1, tm, D), lambda b, j: (b, j, 0))
    mod_spec = pl.BlockSpec((1, 1, D), lambda b, j: (b, 0, 0))
    rt_spec = pl.BlockSpec((1, 8, tm), lambda b, j: (b, 0, j))
    return pl.pallas_call(
        _outproj_kernel,
        grid=(B, S // tm),
        in_specs=[half_spec, half_spec, full_spec, mod_spec, mod_spec, mod_spec,
                  pl.BlockSpec((1, D), lambda b, j: (0, 0)),
                  pl.BlockSpec((D, D), lambda b, j: (0, 0)),
                  pl.BlockSpec((ROUTER_ROWS, D), lambda b, j: (0, 0)),
                  pl.BlockSpec((ROUTER_ROWS, 1), lambda b, j: (0, 0))],
        out_specs=[full_spec, pl.BlockSpec((1, tm * ROW_SLAB, 128), lambda b, j: (b, j, 0)),
                   rt_spec, rt_spec],
        out_shape=[jax.ShapeDtypeStruct((B, S, D), F32),
                   jax.ShapeDtypeStruct((B, S * ROW_SLAB, 128), U32),
                   jax.ShapeDtypeStruct((B, 8, S), I32),
                   jax.ShapeDtypeStruct((B, 8, S), F32)],
        compiler_params=pltpu.CompilerParams(
            dimension_semantics=("arbitrary", "arbitrary"), vmem_limit_bytes=VMEM_LIMIT),
        name="outproj",
    )(attn, conv, x, g1, sc2, sh2, g2n, w_out_bf, wr_t, rb)


def _store_packed_rows(dst, x):
    half = D_MODEL // 2
    xb = x.astype(BF16).astype(F32)
    packed = (pltpu.bitcast(xb[:, :half], U32) >> 16) | (
        pltpu.bitcast(xb[:, half:], U32) & jnp.uint32(0xFFFF0000))
    rows = x.shape[0]
    for c in range(ROW_SLAB):
        dst[pl.ds(c, rows, stride=ROW_SLAB), :] = packed[:, c * 128:(c + 1) * 128]


def _load_packed_rows(src, row0, rows):
    packed = jnp.concatenate(
        [src[pl.ds(row0 * ROW_SLAB + c, rows, stride=ROW_SLAB), :] for c in range(ROW_SLAB)], axis=1)
    lo = pltpu.bitcast(packed << 16, F32)
    hi = pltpu.bitcast(packed & jnp.uint32(0xFFFF0000), F32)
    return lo, hi


def _row_gather_start(src_hbm, idx_ref, dst, sem, n_rows):
    for r in range(n_rows):
        off = pl.multiple_of(idx_ref[0, 0, r], ROW_SLAB)
        pltpu.make_async_copy(src_hbm.at[pl.ds(off, ROW_SLAB)],
                              dst.at[pl.ds(r * ROW_SLAB, ROW_SLAB)], sem).start(priority=r % 2)


def _row_gather_wait(src_hbm, dst, sem, n_rows):
    pltpu.make_async_copy(src_hbm.at[pl.ds(0, n_rows * ROW_SLAB)], dst, sem).wait()


def _dispatch_kernel(fill_ref, pos_ref, hp_hbm, xs_hbm, hp_ref, zbuf, sem, load_sem):
    i = pl.program_id(0)
    tm = COMB_TM
    tile_rows = MOE_TM * ROW_SLAB
    n_steps = hp_ref.shape[0] // (tm * ROW_SLAB)
    chunk_rows = hp_ref.shape[0] // DISPATCH_CHUNKS

    def load(k):
        rows = pl.ds(k * chunk_rows, chunk_rows)
        return pltpu.make_async_copy(hp_hbm.at[rows], hp_ref.at[rows], load_sem.at[k])

    @pl.when(i == 0)
    def _():
        for k in range(DISPATCH_CHUNKS):
            load(k).start()
        zbuf[...] = jnp.zeros_like(zbuf)

        def fill_copy(t):
            return pltpu.make_async_copy(
                zbuf, xs_hbm.at[pl.ds(pl.multiple_of(t * tile_rows, tile_rows), tile_rows)],
                sem.at[0])

        def start(t, c):
            @pl.when(fill_ref[t] == 1)
            def _():
                fill_copy(t).start()
            return c

        def wait(t, c):
            @pl.when(fill_ref[t] == 1)
            def _():
                fill_copy(t).wait()
            return c

        lax.fori_loop(0, fill_ref.shape[0], start, 0)
        lax.fori_loop(0, fill_ref.shape[0], wait, 0)

    for k in range(DISPATCH_CHUNKS):
        @pl.when(i == k * (n_steps // DISPATCH_CHUNKS))
        def _():
            load(k).wait()

    par = lax.rem(i, 2)
    base = pl.multiple_of(i * (tm * ROW_SLAB), tm * ROW_SLAB)
    for r in range(2 * tm):
        off = pl.multiple_of(pos_ref[0, 0, r], ROW_SLAB)
        pltpu.make_async_copy(hp_ref.at[pl.ds(base + (r % tm) * ROW_SLAB, ROW_SLAB)],
                              xs_hbm.at[pl.ds(off, ROW_SLAB)],
                              sem.at[1 + par]).start(priority=r % 2)

    def drain(parity):
        for _ in range(2):
            pltpu.make_async_copy(hp_ref.at[pl.ds(0, tm * ROW_SLAB)],
                                  xs_hbm.at[pl.ds(0, tm * ROW_SLAB)], sem.at[1 + parity]).wait()

    @pl.when(i > 0)
    def _():
        drain(1 - par)

    @pl.when(i == pl.num_programs(0) - 1)
    def _():
        drain(par)


def _dispatch(tile_fill, pos_tiles, hp, n_tiles):
    T = hp.shape[0] // ROW_SLAB
    tm = COMB_TM
    grid_spec = pltpu.PrefetchScalarGridSpec(
        num_scalar_prefetch=1,
        grid=(T // tm,),
        in_specs=[pl.BlockSpec((1, 1, 2 * tm), lambda i, f: (i, 0, 0), memory_space=pltpu.SMEM),
                  pl.BlockSpec(memory_space=pl.ANY)],
        out_specs=pl.BlockSpec(memory_space=pl.ANY),
        scratch_shapes=[pltpu.VMEM(hp.shape, U32),
                        pltpu.VMEM((MOE_TM * ROW_SLAB, 128), U32),
                        pltpu.SemaphoreType.DMA((3,)),
                        pltpu.SemaphoreType.DMA((DISPATCH_CHUNKS,))],
    )
    return pl.pallas_call(
        _dispatch_kernel,
        grid_spec=grid_spec,
        out_shape=jax.ShapeDtypeStruct((n_tiles * MOE_TM * ROW_SLAB, 128), U32),
        compiler_params=pltpu.CompilerParams(
            dimension_semantics=("arbitrary",), vmem_limit_bytes=VMEM_LIMIT),
        name="dispatch",
    )(tile_fill, pos_tiles, hp)


def _moe_kernel(te_ref, tv_ref, ne_ref, xs_ref, wg_hbm, wu_hbm, wd_hbm, y_ref,
                wg_bf, wu_bf, wd_bf, wg_st, wu_st, wd_st, sem):
    i = pl.program_id(0)
    tm = MOE_TM

    def weight_copies(e):
        return (pltpu.make_async_copy(wg_hbm.at[e], wg_st, sem.at[0]),
                pltpu.make_async_copy(wu_hbm.at[e], wu_st, sem.at[1]),
                pltpu.make_async_copy(wd_hbm.at[e], wd_st, sem.at[2]))

    @pl.when(i == 0)
    def _():
        for cp in weight_copies(te_ref[0]):
            cp.start()

    @pl.when(jnp.logical_or(i == 0, te_ref[i] != te_ref[jnp.maximum(i - 1, 0)]))
    def _():
        for cp in weight_copies(te_ref[i]):
            cp.wait()
        wg_bf[...] = wg_st[...].astype(BF16)
        wu_bf[...] = wu_st[...].astype(BF16)
        wd_bf[...] = wd_st[...].astype(BF16)

        @pl.when(ne_ref[i] >= 0)
        def _():
            for cp in weight_copies(ne_ref[i]):
                cp.start()

    @pl.when(tv_ref[i] == 1)
    def _():
        half = D_MODEL // 2
        lo, hi = _load_packed_rows(xs_ref, 0, tm)
        lo = lo.astype(BF16)
        hi = hi.astype(BF16)
        g = (jnp.dot(lo, wg_bf[0:half, :], preferred_element_type=F32)
             + jnp.dot(hi, wg_bf[half:, :], preferred_element_type=F32))
        u = (jnp.dot(lo, wu_bf[0:half, :], preferred_element_type=F32)
             + jnp.dot(hi, wu_bf[half:, :], preferred_element_type=F32))
        hid = (_silu(g) * u).astype(BF16)
        _store_packed_rows(y_ref, jnp.dot(hid, wd_bf[...], preferred_element_type=F32))

    @pl.when(tv_ref[i] == 0)
    def _():
        y_ref[...] = jnp.zeros_like(y_ref)


def _moe(tile_expert, tile_valid, tile_next_expert, xs, wg, wu, wd):
    nt = tile_expert.shape[0]
    tm = MOE_TM
    D = D_MODEL
    any_spec = pl.BlockSpec(memory_space=pl.ANY)
    grid_spec = pltpu.PrefetchScalarGridSpec(
        num_scalar_prefetch=3,
        grid=(nt,),
        in_specs=[pl.BlockSpec((tm * ROW_SLAB, 128), lambda i, *_: (i, 0)),
                  any_spec, any_spec, any_spec],
        out_specs=pl.BlockSpec((tm * ROW_SLAB, 128), lambda i, *_: (i, 0)),
        scratch_shapes=[pltpu.VMEM((D, D_EXPERT), BF16), pltpu.VMEM((D, D_EXPERT), BF16),
                        pltpu.VMEM((D_EXPERT, D), BF16),
                        pltpu.VMEM((D, D_EXPERT), F32), pltpu.VMEM((D, D_EXPERT), F32),
                        pltpu.VMEM((D_EXPERT, D), F32),
                        pltpu.SemaphoreType.DMA((3,))],
    )
    return pl.pallas_call(
        _moe_kernel,
        grid_spec=grid_spec,
        out_shape=jax.ShapeDtypeStruct((nt * tm * ROW_SLAB, 128), U32),
        compiler_params=pltpu.CompilerParams(
            dimension_semantics=("arbitrary",), vmem_limit_bytes=VMEM_LIMIT),
        name="moe",
    )(tile_expert, tile_valid, tile_next_expert, xs, wg, wu, wd)


def _combine_kernel(pos_ref, posn_ref, ys_hbm, x1_ref, g2_ref, w_ref, fg_ref, o_ref, rbuf, sem):
    i = pl.program_id(0)
    n = pl.num_programs(0)
    slot = lax.rem(i, 2)
    nslot = 1 - slot
    tm = COMB_TM

    @pl.when(i == 0)
    def _():
        _row_gather_start(ys_hbm, pos_ref, rbuf.at[0], sem.at[0], 2 * tm)

    @pl.when(i + 1 < n)
    def _():
        _row_gather_start(ys_hbm, posn_ref, rbuf.at[nslot], sem.at[nslot], 2 * tm)

    _row_gather_wait(ys_hbm, rbuf.at[slot], sem.at[slot], 2 * tm)
    w = w_ref[...]
    r1 = jnp.concatenate(_load_packed_rows(rbuf.at[slot], 0, tm), axis=1)
    r2 = jnp.concatenate(_load_packed_rows(rbuf.at[slot], tm, tm), axis=1)
    moe = w[:, 0:1] * r1 + w[:, 1:2] * r2
    y = x1_ref[...] + g2_ref[0] * moe
    ms = jnp.mean(y * y, axis=-1, keepdims=True)
    o_ref[...] = y * lax.rsqrt(ms + NORM_EPS) * fg_ref[...]


def _combine(pos_tiles, ys, x1, g2, w_tok, final_g, seq_len):
    T, D = x1.shape
    tm = COMB_TM
    nt = T // tm
    per_b = seq_len // tm
    return pl.pallas_call(
        _combine_kernel,
        grid=(nt,),
        in_specs=[pl.BlockSpec((1, 1, 2 * tm), lambda i: (i, 0, 0), memory_space=pltpu.SMEM),
                  pl.BlockSpec((1, 1, 2 * tm), lambda i: (jnp.minimum(i + 1, nt - 1), 0, 0),
                               memory_space=pltpu.SMEM),
                  pl.BlockSpec(memory_space=pl.ANY),
                  pl.BlockSpec((tm, D), lambda i: (i, 0)),
                  pl.BlockSpec((1, 1, D), lambda i: (i // per_b, 0, 0)),
                  pl.BlockSpec((tm, 2), lambda i: (i, 0)),
                  pl.BlockSpec((1, D), lambda i: (0, 0))],
        out_specs=pl.BlockSpec((tm, D), lambda i: (i, 0)),
        out_shape=jax.ShapeDtypeStruct((T, D), F32),
        scratch_shapes=[pltpu.VMEM((2, 2 * tm * ROW_SLAB, 128), U32),
                        pltpu.SemaphoreType.DMA((2,))],
        compiler_params=pltpu.CompilerParams(
            dimension_semantics=("arbitrary",), vmem_limit_bytes=VMEM_LIMIT),
        name="combine",
    )(pos_tiles, pos_tiles, ys, x1, g2, w_tok, final_g)


def _rel_bucket_table():
    n = jnp.arange(BIAS_LUT, dtype=I32)
    max_exact = N_BUCKETS // 2
    nf = jnp.maximum(n, 1).astype(F32)
    large = max_exact + (jnp.log(nf / max_exact) / math.log(MAX_DISTANCE / max_exact)
                         * (N_BUCKETS - max_exact)).astype(I32)
    large = jnp.minimum(large, N_BUCKETS - 1)
    return jnp.where(n < max_exact, n, large)


def _route_plan(eid, n_tiles, tm):
    two, T = eid.shape
    e_flat = eid.reshape(-1)
    onehot = (e_flat[:, None] == jnp.arange(N_EXPERTS, dtype=I32)[None, :]).astype(I32)
    csum = jnp.cumsum(onehot, axis=0)
    rank = jnp.sum((csum - onehot) * onehot, axis=1)
    counts = csum[-1]
    ptiles = (counts + tm - 1) // tm
    tend = jnp.cumsum(ptiles)
    tstart = tend - ptiles
    slot = jnp.sum(onehot * tstart[None, :], axis=1) * tm + rank
    total = tend[-1]
    tile_ids = jnp.arange(n_tiles, dtype=I32)
    tile_valid = (tile_ids < total).astype(I32)
    tile_expert = jnp.sum((tile_ids[:, None] >= tend[None, :]).astype(I32), axis=1)
    last_expert = jnp.sum((total - 1 >= tend).astype(I32))
    tile_expert = jnp.minimum(tile_expert, last_expert).astype(I32)
    partial_last = jnp.any((tile_ids[:, None] == tend[None, :] - 1)
                           & (ptiles[None, :] > 0) & (counts[None, :] % tm != 0), axis=1)
    tile_fill = (partial_last | (tile_ids >= total)).astype(I32)
    experts = jnp.arange(N_EXPERTS, dtype=I32)
    later = (experts[None, :] > tile_expert[:, None]) & (ptiles[None, :] > 0)
    tile_next_expert = jnp.min(jnp.where(later, experts[None, :], N_EXPERTS), axis=1)
    tile_next_expert = jnp.where(tile_next_expert == N_EXPERTS, -1, tile_next_expert).astype(I32)
    return slot.astype(I32), tile_expert, tile_valid, tile_fill, tile_next_expert


def kernel(x, c, positions, rel_bias, ada_w, ada_b, norm1_g, w_in, lambda_q1, lambda_k1, lambda_q2,
           lambda_k2, subln_g, conv_w, w_out, norm2_g, router_group_w, router_group_b,
           router_expert_w, router_expert_b, expert_w_gate, expert_w_up, expert_w_down, final_g):
    B, S, D = x.shape
    T = B * S
    l = 0

    ada = _ada(c.reshape(B, D, 1), ada_w[l], ada_b[l].reshape(1, -1))
    sh1, sc1, g1, sh2, sc2, g2 = jnp.split(ada, 6, axis=-1)

    qt, k, vt, conv = _inproj(x, sc1, sh1, norm1_g[l].reshape(1, D), w_in[l], conv_w[l])
    lut = (rel_bias.astype(F32)[_rel_bucket_table(), :].T * LOG2E).reshape(N_DIFF_HEADS, 2, BIAS_LUT)
    lam_params = jnp.stack([lambda_q1[l], lambda_k1[l], lambda_q2[l], lambda_k2[l]]).astype(F32)
    attn = _attention(qt, k, vt, positions, lut, lam_params, subln_g[l].reshape(V_HEAD_DIM, 1))

    pad_rows = ROUTER_ROWS - N_GROUPS - N_EXPERTS
    wr_t = jnp.concatenate([router_group_w[l], router_expert_w[l], jnp.zeros((D, pad_rows), F32)],
                           axis=1).T.astype(BF16)
    rb = jnp.concatenate([router_group_b[l], router_expert_b[l],
                          jnp.zeros((pad_rows,), F32)]).reshape(ROUTER_ROWS, 1)
    x1, hp, ri, rw = _outproj(attn, conv, x, g1, sc2, sh2, norm2_g[l].reshape(1, D),
                              w_out[l].astype(BF16), wr_t, rb)

    eid = ri[:, 0:2, :].transpose(1, 0, 2).reshape(2, T)
    n_tiles = 2 * T // MOE_TM + N_EXPERTS
    slot, tile_expert, tile_valid, tile_fill, tile_next = _route_plan(eid, n_tiles, MOE_TM)
    nct = T // COMB_TM
    pos = (slot * ROW_SLAB).reshape(2, nct, 1, COMB_TM)
    pos_tiles = jnp.concatenate([pos[0], pos[1]], axis=2)

    xs = _dispatch(tile_fill, pos_tiles, hp.reshape(T * ROW_SLAB, 128), n_tiles)
    ys = _moe(tile_expert, tile_valid, tile_next, xs, expert_w_gate[l], expert_w_up[l],
              expert_w_down[l])

    w_tok = rw[:, 0:2, :].transpose(0, 2, 1).reshape(T, 2)
    out = _combine(pos_tiles, ys, x1.reshape(T, D), g2, w_tok, final_g.reshape(1, D), S)
    return out.reshape(B, S, D)
```

```python
import functools
import math

import jax
import jax.numpy as jnp
from jax import lax
from jax.experimental import pallas as pl
from jax.experimental.pallas import tpu as pltpu

F32 = jnp.float32
BF16 = jnp.bfloat16
I32 = jnp.int32
U32 = jnp.uint32

D_MODEL = 1024
ATTN_WIDTH = 512
CONV_WIDTH = 512
N_DIFF_HEADS = 4
DIFF_HEAD_DIM = 64
V_HEAD_DIM = 128
IN_PROJ_WIDTH = 3 * ATTN_WIDTH + 3 * CONV_WIDTH
CONV_K = 3
N_BUCKETS = 32
MAX_DISTANCE = 128
N_GROUPS = 4
EXPERTS_PER_GROUP = 4
N_EXPERTS = 16
D_EXPERT = 512
NORM_EPS = 1e-6
SUBLN_EPS = 1e-5
NEG_INF = -1e30
LAMBDA_INIT = 0.8 - 0.6 * math.exp(-0.3 * 0)
QK_SCALE = DIFF_HEAD_DIM ** -0.5
LOG2E = math.log2(math.e)

BIAS_LUT = 128

ROW_TILE = 512
PROJ_SUBTILES = 2
OUT_TILE = 1024
OUT_SUBTILES = 8
ATT_TQ = 1024
ATT_CHAIN = 256
ATT_GROUP = 8
ATT_LOOKAHEAD = 8
ATT_TK = 256
MOE_TM = 512
MOE_CHAIN = 256
COMB_TM = 256
DISPATCH_CHUNKS = 8
ROUTER_ROWS = 32
ROW_SLAB = D_MODEL // 256
VMEM_LIMIT = 56 * 1024 * 1024


def _silu(x):
    return x * (1.0 / (1.0 + jnp.exp(-x)))


def _ada_kernel(c_ref, w_ref, b_ref, o_ref):
    for bi in range(c_ref.shape[0]):
        s = _silu(c_ref[bi])
        o_ref[bi] = jnp.sum(s * w_ref[...], axis=0, keepdims=True) + b_ref[...]


def _ada(c_col, w, b):
    nb, n = c_col.shape[0], w.shape[1]
    bn = 1024
    return pl.pallas_call(
        _ada_kernel,
        grid=(n // bn,),
        in_specs=[pl.BlockSpec((nb, D_MODEL, 1), lambda j: (0, 0, 0)),
                  pl.BlockSpec((D_MODEL, bn), lambda j: (0, j)),
                  pl.BlockSpec((1, bn), lambda j: (0, j))],
        out_specs=pl.BlockSpec((nb, 1, bn), lambda j: (0, 0, j)),
        out_shape=jax.ShapeDtypeStruct((nb, 1, n), F32),
        name="ada",
    )(c_col, w, b)


def _inproj_kernel(x_ref, sc_ref, sh_ref, g_ref, w_hbm, cw_ref,
                   qt_ref, k_ref, vt_ref, conv_ref,
                   carry_ref, wqt_ref, wk_ref, wvt_ref, wc_ref, stage_ref, sem):
    j = pl.program_id(1)
    tm = x_ref.shape[1]
    nt = (((1,), (1,)), ((), ()))

    @pl.when(jnp.logical_and(pl.program_id(0) == 0, j == 0))
    def _():
        n_chunks = IN_PROJ_WIDTH // 512

        def chunk_copy(c):
            return pltpu.make_async_copy(w_hbm.at[:, pl.ds(c * 512, 512)], stage_ref.at[c % 2],
                                         sem.at[c % 2])

        chunk_copy(0).start()
        for c in range(n_chunks):
            if c + 1 < n_chunks:
                chunk_copy(c + 1).start()
            chunk_copy(c).wait()
            w = stage_ref[c % 2]
            if c == 0:
                wqt_ref[...] = w.T.astype(BF16)
            elif c == 1:
                wk_ref[...] = w.astype(BF16)
            elif c == 2:
                wvt_ref[...] = w.T.astype(BF16)
            else:
                wc_ref[:, (c - 3) * 512:(c - 2) * 512] = w.astype(BF16)

    @pl.when(j == 0)
    def _():
        carry_ref[...] = jnp.zeros_like(carry_ref)

    prev = carry_ref[...]
    sub = tm // PROJ_SUBTILES
    for s in range(PROJ_SUBTILES):
        rows = slice(s * sub, (s + 1) * sub)
        x = x_ref[0, rows, :]
        ms = jnp.mean(x * x, axis=-1, keepdims=True)
        h = x * lax.rsqrt(ms + NORM_EPS) * g_ref[...]
        h = h * (1.0 + sc_ref[0]) + sh_ref[0]
        hb = h.astype(BF16)

        def proj(c0):
            return jnp.dot(hb, wc_ref[:, c0:c0 + 512], preferred_element_type=F32)

        qt = lax.dot_general(wqt_ref[...], hb, nt, preferred_element_type=F32)
        qt_ref[0, :, rows] = (qt * (QK_SCALE * LOG2E)).astype(BF16)
        k_ref[0, rows, :] = jnp.dot(hb, wk_ref[...], preferred_element_type=F32).astype(BF16)
        vt_ref[0, :, rows] = lax.dot_general(wvt_ref[...], hb, nt,
                                             preferred_element_type=F32).astype(BF16)
        gate_b = proj(0)
        u = proj(512) * proj(1024)
        row = lax.broadcasted_iota(I32, u.shape, 0)
        u1 = pltpu.roll(u, 1, axis=0)
        u2 = pltpu.roll(u, 2, axis=0)
        u1 = jnp.where(row == 0, prev[7:8, :], u1)
        u2 = jnp.where(row == 0, prev[6:7, :], jnp.where(row == 1, prev[7:8, :], u2))
        conv = cw_ref[0:1, :] * u2 + cw_ref[1:2, :] * u1 + cw_ref[2:3, :] * u
        conv_ref[0, rows, :] = (gate_b * conv).astype(BF16)
        prev = u[sub - 8:sub, :]
    carry_ref[...] = prev


def _inproj(x, sc1, sh1, g1n, w_in, conv_w):
    B, S, D = x.shape
    tm = ROW_TILE
    row_out = jax.ShapeDtypeStruct((B, S, 512), BF16)
    col_out = jax.ShapeDtypeStruct((B, 512, S), BF16)
    row_spec = pl.BlockSpec((1, tm, 512), lambda b, j: (b, j, 0))
    col_spec = pl.BlockSpec((1, 512, tm), lambda b, j: (b, 0, j))
    mod_spec = pl.BlockSpec((1, 1, D), lambda b, j: (b, 0, 0))
    const2 = lambda b, j: (0, 0)
    return pl.pallas_call(
        _inproj_kernel,
        grid=(B, S // tm),
        in_specs=[pl.BlockSpec((1, tm, D), lambda b, j: (b, j, 0)),
                  mod_spec, mod_spec,
                  pl.BlockSpec((1, D), const2),
                  pl.BlockSpec(memory_space=pl.ANY),
                  pl.BlockSpec((CONV_K, CONV_WIDTH), const2)],
        out_specs=[col_spec, row_spec, col_spec, row_spec],
        out_shape=[col_out, row_out, col_out, row_out],
        scratch_shapes=[pltpu.VMEM((8, CONV_WIDTH), F32),
                        pltpu.VMEM((ATTN_WIDTH, D), BF16),
                        pltpu.VMEM((D, ATTN_WIDTH), BF16),
                        pltpu.VMEM((ATTN_WIDTH, D), BF16),
                        pltpu.VMEM((D, 3 * CONV_WIDTH), BF16),
                        pltpu.VMEM((2, D, 512), F32),
                        pltpu.SemaphoreType.DMA((2,))],
        compiler_params=pltpu.CompilerParams(
            dimension_semantics=("arbitrary", "arbitrary"), vmem_limit_bytes=VMEM_LIMIT),
        name="inproj",
    )(x, sc1, sh1, g1n, w_in, conv_w)


def _attn_kernel(qmin_ref, kmax_ref, consec_ref, qt_ref, k_ref, vt_ref, pr_ref, pr_all_ref, lut_ref,
                 lam_ref, sg_ref, o_ref, acc_ref, tz_ref, *, n_q_tiles):
    b = pl.program_id(0)
    qi = pl.program_id(2)
    nq = pl.num_programs(2)
    tq, tk = ATT_TQ, ATT_TK
    hw = ATT_CHAIN
    n_half = tq // hw
    nk = nq * (tq // tk)
    qt = qt_ref[0]
    feat = lax.broadcasted_iota(I32, qt.shape, 0)
    zero = jnp.zeros_like(qt)
    qts = (jnp.where(feat < DIFF_HEAD_DIM, qt, zero), jnp.where(feat >= DIFF_HEAD_DIM, qt, zero))
    luts = (lut_ref[0, 0:1, :], lut_ref[0, 1:2, :])
    fars = tuple(t[:, BIAS_LUT - 1:BIAS_LUT] for t in luts)
    pq = pr_ref[0]
    qmin = qmin_ref[b * nq + qi]
    czero = jnp.zeros((1, 1), F32)
    ones_rows = jnp.ones((16, tk), BF16)
    chains = [(mi, hi) for mi in range(2) for hi in range(n_half)]

    def gather_bias(mi, dist):
        table = jnp.broadcast_to(luts[mi], (tk, BIAS_LUT))
        return jnp.concatenate([jnp.take_along_axis(table, dist[:, o:o + 128], axis=1)
                                for o in range(0, hw, 128)], axis=1)

    @pl.when(qi == 0)
    def _():
        delta = (lax.broadcasted_iota(I32, (tk, hw), 1) - lax.broadcasted_iota(I32, (tk, hw), 0))
        for mi in range(2):
            diag_bias = gather_bias(mi, jnp.clip(delta, 0, BIAS_LUT - 1))
            tz_ref[mi, 0] = jnp.where(delta >= 0, diag_bias, NEG_INF)
            tz_ref[mi, 1] = gather_bias(mi, jnp.clip(delta + tk, 0, BIAS_LUT - 1))

    def run_blocks(blocks, state):
        loaded = []
        for (j, kinds) in blocks:
            ks = pl.multiple_of(j * tk, tk)
            kb = k_ref[0, pl.ds(ks, tk), :]
            vtb = jnp.concatenate([vt_ref[0, :, pl.ds(ks, tk)], ones_rows], axis=0)
            dist = None
            if any(kd is not None and kd.startswith("near") for kd in kinds):
                pk_rows = jnp.broadcast_to(pr_all_ref[0, :, pl.ds(ks, tk)], (8, tk))
                pk = pk_rows.T[:, 0:1]
                dist = jnp.clip(pq - pk, 0, BIAS_LUT - 1)
            loaded.append((kb, vtb, dist))
        items = [(bi, n) for bi, blk in enumerate(blocks) for n, (mi, hi) in enumerate(chains)
                 if blk[1][hi] is not None]
        scores = {}

        def issue_qk(t):
            bi, n = items[t]
            mi, hi = chains[n]
            scores[t] = jnp.dot(loaded[bi][0], qts[mi][:, hi * hw:(hi + 1) * hw],
                                preferred_element_type=F32)

        state = list(state)
        for t in range(min(ATT_LOOKAHEAD, len(items))):
            issue_qk(t)
        for t, (bi, n) in enumerate(items):
            if t + ATT_LOOKAHEAD < len(items):
                issue_qk(t + ATT_LOOKAHEAD)
            _, vtb, dist = loaded[bi]
            mi, hi = chains[n]
            kind = blocks[bi][1][hi]
            cols = slice(hi * hw, (hi + 1) * hw)
            m, l = state[n]
            s = scores.pop(t)
            c = czero
            if kind == "far":
                c = fars[mi]
            elif kind == "tz_diag":
                s = tz_ref[mi, 0] + s
            elif kind == "tz_sub":
                s = tz_ref[mi, 1] + s
            else:
                s = gather_bias(mi, dist[:, cols]) + s
                if kind == "near_masked":
                    keep = (lax.broadcasted_iota(I32, (tk, hw), 0)
                            <= lax.broadcasted_iota(I32, (tk, hw), 1))
                    s = jnp.where(keep, s, NEG_INF)
            mn = jnp.maximum(m, jnp.max(s, axis=0, keepdims=True) + c)
            alpha = jnp.exp2(m - mn)
            p = jnp.exp2(s - (mn - c))
            pv = jnp.dot(vtb, p.astype(BF16), preferred_element_type=F32)
            l = alpha * l + pv[V_HEAD_DIM:V_HEAD_DIM + 1, :]
            acc_ref[mi, :, cols] = alpha * acc_ref[mi, :, cols] + pv[:V_HEAD_DIM, :]
            state[n] = (mn, l)
        return tuple(state)

    def block_is_far(j):
        return qmin - kmax_ref[b * nk + j] >= BIAS_LUT - 1

    def one_block(j, state):
        return lax.cond(block_is_far(j), lambda st: run_blocks([(j, ("far",) * n_half)], st),
                        lambda st: run_blocks([(j, ("near",) * n_half)], st), state)

    def group_body(width):
        def body(g, carry):
            j0, state = carry
            all_far = block_is_far(j0)
            for u in range(1, width):
                all_far = jnp.logical_and(all_far, block_is_far(j0 + u))
            far_blocks = [(j0 + u, ("far",) * n_half) for u in range(width)]
            state = lax.cond(
                all_far, lambda st: run_blocks(far_blocks, st),
                lambda st: lax.fori_loop(0, width, lambda u, s2: one_block(j0 + u, s2), st), state)
            return j0 + width, state
        return body

    acc_ref[...] = jnp.zeros_like(acc_ref)
    m0 = jnp.full((1, hw), NEG_INF, F32)
    l0 = jnp.zeros((1, hw), F32)
    state = tuple((m0, l0) for _ in chains)
    assert hw == tk and n_half % 2 == 0
    n_full = n_half * qi
    n_main = jnp.maximum(n_full - 2, 0)
    n_groups = n_main // ATT_GROUP
    j0, state = lax.fori_loop(0, n_groups, group_body(ATT_GROUP), (jnp.int32(0), state))
    rem = n_main - n_groups * ATT_GROUP

    consec = consec_ref[b * nq + qi] == 1

    def diag_kinds(d, on_diag, below, further):
        return tuple(None if hi < d else on_diag if hi == d else below if hi == d + 1 else further
                     for hi in range(n_half))

    diag_fast = [(n_full + d, diag_kinds(d, "tz_diag", "tz_sub", "far")) for d in range(n_half)]
    diag_any = [(n_full + d, diag_kinds(d, "near_masked", "near", "near")) for d in range(n_half)]
    below_fast = [(n_full - 2, ("far",) * n_half),
                  (n_full - 1, ("tz_sub",) + ("far",) * (n_half - 1))]
    below_any = [(n_full - 2, ("near",) * n_half), (n_full - 1, ("near",) * n_half)]

    def first_tile(st):
        return lax.cond(consec, lambda s2: run_blocks(diag_fast, s2),
                        lambda s2: run_blocks(diag_any, s2), st)

    def slow_rest(st):
        st = lax.fori_loop(0, rem, lambda u, s2: one_block(j0 + u, s2), st)
        return lax.cond(consec, lambda s2: run_blocks(below_fast + diag_fast, s2),
                        lambda s2: run_blocks(below_any + diag_any, s2), st)

    def fast_rest(r):
        left = [(j0 + u, ("far",) * n_half) for u in range(r)]
        return lambda st: run_blocks(left + below_fast + diag_fast, st)

    left_counts = sorted({(n_half * q - 2) % ATT_GROUP for q in range(1, n_q_tiles)})
    left_far = consec
    for u in range(max(left_counts, default=0)):
        far_u = block_is_far(jnp.minimum(j0 + u, nk - 1))
        left_far = jnp.logical_and(left_far, jnp.logical_or(u >= rem, far_u))

    def later_tile(st):
        out = slow_rest
        for r in left_counts:
            out = (lambda r, nxt: lambda s2: lax.cond(
                jnp.logical_and(left_far, rem == r), fast_rest(r), nxt, s2))(r, out)
        return out(st)

    state = lax.cond(qi > 0, later_tile, first_tile, state)

    l1 = jnp.concatenate([state[n][1] for n, (mi, hi) in enumerate(chains) if mi == 0], axis=1)
    l2 = jnp.concatenate([state[n][1] for n, (mi, hi) in enumerate(chains) if mi == 1], axis=1)
    lam = (jnp.exp(jnp.sum(lam_ref[0:1, :] * lam_ref[1:2, :], axis=-1, keepdims=True))
           - jnp.exp(jnp.sum(lam_ref[2:3, :] * lam_ref[3:4, :], axis=-1, keepdims=True))
           + LAMBDA_INIT)
    ot = acc_ref[0] * (1.0 / l1) - (lam * (1.0 / l2)) * acc_ref[1]
    ot = ot * lax.rsqrt(jnp.mean(ot * ot, axis=0, keepdims=True) + SUBLN_EPS)
    ot = ot * (sg_ref[...] * (1.0 - LAMBDA_INIT))
    o_ref[0] = ot.T.astype(BF16)


def _attention(qt, k, vt, positions, lut, lam_params, subln_g_col):
    B, S, _ = k.shape
    tq = ATT_TQ
    nq = S // tq
    pos_row = positions.reshape(B, 1, S)
    qmin = jnp.min(positions.reshape(B * nq, tq), axis=1)
    kmax = jnp.max(positions.reshape(B * (S // ATT_TK), ATT_TK), axis=1)
    step_ok = jnp.concatenate([positions[:, 1:] - positions[:, :-1] == 1,
                               jnp.ones((B, 1), jnp.bool_)], axis=1).reshape(B, nq, tq)
    inner_ok = jnp.all(step_ok[:, :, :tq - 1], axis=2)
    link_ok = jnp.concatenate([jnp.ones((B, 1), jnp.bool_), step_ok[:, :-1, tq - 1]], axis=1)
    prev_ok = jnp.concatenate([jnp.ones((B, 1), jnp.bool_), inner_ok[:, :-1]], axis=1)
    consec = (inner_ok & link_ok & prev_ok).astype(I32).reshape(B * nq)
    grid_spec = pltpu.PrefetchScalarGridSpec(
        num_scalar_prefetch=3,
        grid=(B, N_DIFF_HEADS, nq),
        in_specs=[pl.BlockSpec((1, 128, tq), lambda b, h, i, *_: (b, h, i)),
                  pl.BlockSpec((1, S, 128), lambda b, h, i, *_: (b, 0, h)),
                  pl.BlockSpec((1, 128, S), lambda b, h, i, *_: (b, h, 0)),
                  pl.BlockSpec((1, 1, tq), lambda b, h, i, *_: (b, 0, i)),
                  pl.BlockSpec((1, 1, S), lambda b, h, i, *_: (b, 0, 0)),
                  pl.BlockSpec((1, 2, BIAS_LUT), lambda b, h, i, *_: (h, 0, 0)),
                  pl.BlockSpec((4, DIFF_HEAD_DIM), lambda b, h, i, *_: (0, 0)),
                  pl.BlockSpec((V_HEAD_DIM, 1), lambda b, h, i, *_: (0, 0))],
        out_specs=pl.BlockSpec((1, tq, 128), lambda b, h, i, *_: (b, i, h)),
        scratch_shapes=[pltpu.VMEM((2, V_HEAD_DIM, tq), F32),
                        pltpu.VMEM((2, 2, ATT_TK, ATT_CHAIN), F32)],
    )
    return pl.pallas_call(
        functools.partial(_attn_kernel, n_q_tiles=nq),
        grid_spec=grid_spec,
        out_shape=jax.ShapeDtypeStruct((B, S, ATTN_WIDTH), BF16),
        compiler_params=pltpu.CompilerParams(
            dimension_semantics=("arbitrary", "arbitrary", "arbitrary"),
            vmem_limit_bytes=VMEM_LIMIT),
        name="diffattn",
    )(qmin, kmax, consec, qt, k, vt, pos_row, pos_row, lut, lam_params, subln_g_col)


def _outproj_kernel(at_ref, cv_ref, x_ref, g1_ref, sc_ref, sh_ref, gn_ref, wo_ref, wr_ref, rb_ref,
                    x1_ref, hp_ref, ri_ref, rw_ref):
    sub = x_ref.shape[1] // OUT_SUBTILES
    groups = [slice(s * sub, (s + 1) * sub) for s in range(OUT_SUBTILES)]
    mixes = [jnp.dot(at_ref[0, rows, :], wo_ref[0:ATTN_WIDTH, :], preferred_element_type=F32)
             + jnp.dot(cv_ref[0, rows, :], wo_ref[ATTN_WIDTH:, :], preferred_element_type=F32)
             for rows in groups]
    for rows, mix in zip(groups, mixes):
        _outproj_rows(rows, mix, x_ref, g1_ref, sc_ref, sh_ref, gn_ref, wr_ref, rb_ref,
                      x1_ref, hp_ref, ri_ref, rw_ref)


def _outproj_rows(rows, mix, x_ref, g1_ref, sc_ref, sh_ref, gn_ref, wr_ref, rb_ref,
                  x1_ref, hp_ref, ri_ref, rw_ref):
    tm = rows.stop - rows.start
    x1 = x_ref[0, rows, :] + g1_ref[0] * mix
    x1_ref[0, rows, :] = x1
    ms = jnp.mean(x1 * x1, axis=-1, keepdims=True)
    h = x1 * lax.rsqrt(ms + NORM_EPS) * gn_ref[...]
    h = h * (1.0 + sc_ref[0]) + sh_ref[0]
    hb = h.astype(BF16)

    _store_packed_rows(hp_ref.at[0, pl.ds(rows.start * ROW_SLAB, tm * ROW_SLAB)], h)

    lg_all = lax.dot_general(wr_ref[...], hb, (((1,), (1,)), ((), ())),
                             preferred_element_type=F32) + rb_ref[...]
    lg = lg_all[0:N_GROUPS, :]
    le = lg_all[N_GROUPS:N_GROUPS + N_EXPERTS, :]
    row4 = lax.broadcasted_iota(I32, (N_GROUPS, tm), 0)
    gmax = jnp.max(lg, axis=0, keepdims=True)
    pg_sel = 1.0 / jnp.sum(jnp.exp(lg - gmax), axis=0, keepdims=True)
    gsel = jnp.min(jnp.where(lg == gmax, row4, N_GROUPS), axis=0, keepdims=True)
    sel = jnp.zeros((EXPERTS_PER_GROUP, tm), F32)
    for g in range(N_GROUPS):
        sel = jnp.where(gsel == g, le[g * EXPERTS_PER_GROUP:(g + 1) * EXPERTS_PER_GROUP, :], sel)
    v1 = jnp.max(sel, axis=0, keepdims=True)
    i1 = jnp.min(jnp.where(sel == v1, row4, EXPERTS_PER_GROUP), axis=0, keepdims=True)
    rest = jnp.where(row4 == i1, -jnp.inf, sel)
    v2 = jnp.max(rest, axis=0, keepdims=True)
    i2 = jnp.min(jnp.where(rest == v2, row4, EXPERTS_PER_GROUP), axis=0, keepdims=True)
    e2 = jnp.exp(v2 - v1)
    w1 = pg_sel / (1.0 + e2)
    w2 = pg_sel * e2 / (1.0 + e2)
    row8 = lax.broadcasted_iota(I32, (8, tm), 0)
    eid1 = gsel * EXPERTS_PER_GROUP + i1
    eid2 = gsel * EXPERTS_PER_GROUP + i2
    ri_ref[0, :, rows] = jnp.where(row8 == 0, eid1, jnp.where(row8 == 1, eid2, 0))
    rw_ref[0, :, rows] = jnp.where(row8 == 0, w1, jnp.where(row8 == 1, w2, 0.0))


def _outproj(attn, conv, x, g1, sc2, sh2, g2n, w_out_bf, wr_t, rb):
    B, S, D = x.shape
    tm = OUT_TILE
    half_spec = pl.BlockSpec((1, tm, 512), lambda b, j: (b, j, 0))
    full_spec = pl.BlockSpec((1, tm, D), lambda b, j: (b, j, 0))
    mod_spec = pl.BlockSpec((1, 1, D), lambda b, j: (b, 0, 0))
    rt_spec = pl.BlockSpec((1, 8, tm), lambda b, j: (b, 0, j))
    return pl.pallas_call(
        _outproj_kernel,
        grid=(B, S // tm),
        in_specs=[half_spec, half_spec, full_spec, mod_spec, mod_spec, mod_spec,
                  pl.BlockSpec((1, D), lambda b, j: (0, 0)),
                  pl.BlockSpec((D, D), lambda b, j: (0, 0)),
                  pl.BlockSpec((ROUTER_ROWS, D), lambda b, j: (0, 0)),
                  pl.BlockSpec((ROUTER_ROWS, 1), lambda b, j: (0, 0))],
        out_specs=[full_spec, pl.BlockSpec((1, tm * ROW_SLAB, 128), lambda b, j: (b, j, 0)),
                   rt_spec, rt_spec],
        out_shape=[jax.ShapeDtypeStruct((B, S, D), F32),
                   jax.ShapeDtypeStruct((B, S * ROW_SLAB, 128), U32),
                   jax.ShapeDtypeStruct((B, 8, S), I32),
                   jax.ShapeDtypeStruct((B, 8, S), F32)],
        compiler_params=pltpu.CompilerParams(
            dimension_semantics=("arbitrary", "arbitrary"), vmem_limit_bytes=VMEM_LIMIT),
        name="outproj",
    )(attn, conv, x, g1, sc2, sh2, g2n, w_out_bf, wr_t, rb)


def _store_packed_rows(dst, x):
    half = D_MODEL // 2
    xb = x.astype(BF16).astype(F32)
    packed = (pltpu.bitcast(xb[:, :half], U32) >> 16) | (
        pltpu.bitcast(xb[:, half:], U32) & jnp.uint32(0xFFFF0000))
    rows = x.shape[0]
    for c in range(ROW_SLAB):
        dst[pl.ds(c, rows, stride=ROW_SLAB), :] = packed[:, c * 128:(c + 1) * 128]


def _load_packed_rows(src, row0, rows):
    packed = jnp.concatenate(
        [src[pl.ds(row0 * ROW_SLAB + c, rows, stride=ROW_SLAB), :] for c in range(ROW_SLAB)], axis=1)
    lo = pltpu.bitcast(packed << 16, F32)
    hi = pltpu.bitcast(packed & jnp.uint32(0xFFFF0000), F32)
    return lo, hi


def _row_gather_start(src_hbm, idx_ref, dst, sem, n_rows):
    for r in range(n_rows):
        off = pl.multiple_of(idx_ref[0, 0, r], ROW_SLAB)
        pltpu.make_async_copy(src_hbm.at[pl.ds(off, ROW_SLAB)],
                              dst.at[pl.ds(r * ROW_SLAB, ROW_SLAB)], sem).start(priority=r % 2)


def _row_gather_wait(src_hbm, dst, sem, n_rows):
    pltpu.make_async_copy(src_hbm.at[pl.ds(0, n_rows * ROW_SLAB)], dst, sem).wait()


def _dispatch_kernel(fill_ref, pos_ref, hp_hbm, xs_hbm, hp_ref, zbuf, sem, load_sem):
    i = pl.program_id(0)
    tm = COMB_TM
    tile_rows = MOE_TM * ROW_SLAB
    n_steps = hp_ref.shape[0] // (tm * ROW_SLAB)
    chunk_rows = hp_ref.shape[0] // DISPATCH_CHUNKS

    def load(k):
        rows = pl.ds(k * chunk_rows, chunk_rows)
        return pltpu.make_async_copy(hp_hbm.at[rows], hp_ref.at[rows], load_sem.at[k])

    @pl.when(i == 0)
    def _():
        for k in range(DISPATCH_CHUNKS):
            load(k).start()
        zbuf[...] = jnp.zeros_like(zbuf)

        def fill_copy(t):
            return pltpu.make_async_copy(
                zbuf, xs_hbm.at[pl.ds(pl.multiple_of(t * tile_rows, tile_rows), tile_rows)],
                sem.at[0])

        def start(t, c):
            @pl.when(fill_ref[t] == 1)
            def _():
                fill_copy(t).start()
            return c

        def wait(t, c):
            @pl.when(fill_ref[t] == 1)
            def _():
                fill_copy(t).wait()
            return c

        lax.fori_loop(0, fill_ref.shape[0], start, 0)
        lax.fori_loop(0, fill_ref.shape[0], wait, 0)

    for k in range(DISPATCH_CHUNKS):
        @pl.when(i == k * (n_steps // DISPATCH_CHUNKS))
        def _():
            load(k).wait()

    par = lax.rem(i, 2)
    base = pl.multiple_of(i * (tm * ROW_SLAB), tm * ROW_SLAB)
    for r in range(2 * tm):
        off = pl.multiple_of(pos_ref[0, 0, r], ROW_SLAB)
        pltpu.make_async_copy(hp_ref.at[pl.ds(base + (r % tm) * ROW_SLAB, ROW_SLAB)],
                              xs_hbm.at[pl.ds(off, ROW_SLAB)],
                              sem.at[1 + par]).start(priority=r % 2)

    def drain(parity):
        for _ in range(2):
            pltpu.make_async_copy(hp_ref.at[pl.ds(0, tm * ROW_SLAB)],
                                  xs_hbm.at[pl.ds(0, tm * ROW_SLAB)], sem.at[1 + parity]).wait()

    @pl.when(i > 0)
    def _():
        drain(1 - par)

    @pl.when(i == pl.num_programs(0) - 1)
    def _():
        drain(par)


def _dispatch(tile_fill, pos_tiles, hp, n_tiles):
    T = hp.shape[0] // ROW_SLAB
    tm = COMB_TM
    grid_spec = pltpu.PrefetchScalarGridSpec(
        num_scalar_prefetch=1,
        grid=(T // tm,),
        in_specs=[pl.BlockSpec((1, 1, 2 * tm), lambda i, f: (i, 0, 0), memory_space=pltpu.SMEM),
                  pl.BlockSpec(memory_space=pl.ANY)],
        out_specs=pl.BlockSpec(memory_space=pl.ANY),
        scratch_shapes=[pltpu.VMEM(hp.shape, U32),
                        pltpu.VMEM((MOE_TM * ROW_SLAB, 128), U32),
                        pltpu.SemaphoreType.DMA((3,)),
                        pltpu.SemaphoreType.DMA((DISPATCH_CHUNKS,))],
    )
    return pl.pallas_call(
        _dispatch_kernel,
        grid_spec=grid_spec,
        out_shape=jax.ShapeDtypeStruct((n_tiles * MOE_TM * ROW_SLAB, 128), U32),
        compiler_params=pltpu.CompilerParams(
            dimension_semantics=("arbitrary",), vmem_limit_bytes=VMEM_LIMIT),
        name="dispatch",
    )(tile_fill, pos_tiles, hp)


def _moe_kernel(te_ref, tv_ref, ne_ref, xs_ref, wg_hbm, wu_hbm, wd_hbm, y_ref,
                wg_bf, wu_bf, wd_bf, wg_st, wu_st, wd_st, sem):
    i = pl.program_id(0)
    tm = MOE_TM

    def weight_copies(e):
        return (pltpu.make_async_copy(wg_hbm.at[e], wg_st, sem.at[0]),
                pltpu.make_async_copy(wu_hbm.at[e], wu_st, sem.at[1]),
                pltpu.make_async_copy(wd_hbm.at[e], wd_st, sem.at[2]))

    @pl.when(i == 0)
    def _():
        for cp in weight_copies(te_ref[0]):
            cp.start()

    @pl.when(jnp.logical_or(i == 0, te_ref[i] != te_ref[jnp.maximum(i - 1, 0)]))
    def _():
        for cp in weight_copies(te_ref[i]):
            cp.wait()
        wg_bf[...] = wg_st[...].astype(BF16)
        wu_bf[...] = wu_st[...].astype(BF16)
        wd_bf[...] = wd_st[...].astype(BF16)

        @pl.when(ne_ref[i] >= 0)
        def _():
            for cp in weight_copies(ne_ref[i]):
                cp.start()

    @pl.when(tv_ref[i] == 1)
    def _():
        half = D_MODEL // 2
        for r0 in range(0, tm, MOE_CHAIN):
            lo, hi = _load_packed_rows(xs_ref, r0, MOE_CHAIN)
            lo = lo.astype(BF16)
            hi = hi.astype(BF16)
            g = (jnp.dot(lo, wg_bf[0:half, :], preferred_element_type=F32)
                 + jnp.dot(hi, wg_bf[half:, :], preferred_element_type=F32))
            u = (jnp.dot(lo, wu_bf[0:half, :], preferred_element_type=F32)
                 + jnp.dot(hi, wu_bf[half:, :], preferred_element_type=F32))
            hid = (_silu(g) * u).astype(BF16)
            _store_packed_rows(y_ref.at[pl.ds(r0 * ROW_SLAB, MOE_CHAIN * ROW_SLAB)],
                               jnp.dot(hid, wd_bf[...], preferred_element_type=F32))

    @pl.when(tv_ref[i] == 0)
    def _():
        y_ref[...] = jnp.zeros_like(y_ref)


def _moe(tile_expert, tile_valid, tile_next_expert, xs, wg, wu, wd):
    nt = tile_expert.shape[0]
    tm = MOE_TM
    D = D_MODEL
    any_spec = pl.BlockSpec(memory_space=pl.ANY)
    grid_spec = pltpu.PrefetchScalarGridSpec(
        num_scalar_prefetch=3,
        grid=(nt,),
        in_specs=[pl.BlockSpec((tm * ROW_SLAB, 128), lambda i, *_: (i, 0)),
                  any_spec, any_spec, any_spec],
        out_specs=pl.BlockSpec((tm * ROW_SLAB, 128), lambda i, *_: (i, 0)),
        scratch_shapes=[pltpu.VMEM((D, D_EXPERT), BF16), pltpu.VMEM((D, D_EXPERT), BF16),
                        pltpu.VMEM((D_EXPERT, D), BF16),
                        pltpu.VMEM((D, D_EXPERT), F32), pltpu.VMEM((D, D_EXPERT), F32),
                        pltpu.VMEM((D_EXPERT, D), F32),
                        pltpu.SemaphoreType.DMA((3,))],
    )
    return pl.pallas_call(
        _moe_kernel,
        grid_spec=grid_spec,
        out_shape=jax.ShapeDtypeStruct((nt * tm * ROW_SLAB, 128), U32),
        compiler_params=pltpu.CompilerParams(
            dimension_semantics=("arbitrary",), vmem_limit_bytes=VMEM_LIMIT),
        name="moe",
    )(tile_expert, tile_valid, tile_next_expert, xs, wg, wu, wd)


def _combine_kernel(pos_ref, posn_ref, ys_hbm, x1_ref, g2_ref, w_ref, fg_ref, o_ref, rbuf, sem):
    i = pl.program_id(0)
    n = pl.num_programs(0)
    slot = lax.rem(i, 2)
    nslot = 1 - slot
    tm = COMB_TM

    @pl.when(i == 0)
    def _():
        _row_gather_start(ys_hbm, pos_ref, rbuf.at[0], sem.at[0], 2 * tm)

    @pl.when(i + 1 < n)
    def _():
        _row_gather_start(ys_hbm, posn_ref, rbuf.at[nslot], sem.at[nslot], 2 * tm)

    _row_gather_wait(ys_hbm, rbuf.at[slot], sem.at[slot], 2 * tm)
    w = w_ref[...]
    r1 = jnp.concatenate(_load_packed_rows(rbuf.at[slot], 0, tm), axis=1)
    r2 = jnp.concatenate(_load_packed_rows(rbuf.at[slot], tm, tm), axis=1)
    moe = w[:, 0:1] * r1 + w[:, 1:2] * r2
    y = x1_ref[...] + g2_ref[0] * moe
    ms = jnp.mean(y * y, axis=-1, keepdims=True)
    o_ref[...] = y * lax.rsqrt(ms + NORM_EPS) * fg_ref[...]


def _combine(pos_tiles, ys, x1, g2, w_tok, final_g, seq_len):
    T, D = x1.shape
    tm = COMB_TM
    nt = T // tm
    per_b = seq_len // tm
    return pl.pallas_call(
        _combine_kernel,
        grid=(nt,),
        in_specs=[pl.BlockSpec((1, 1, 2 * tm), lambda i: (i, 0, 0), memory_space=pltpu.SMEM),
                  pl.BlockSpec((1, 1, 2 * tm), lambda i: (jnp.minimum(i + 1, nt - 1), 0, 0),
                               memory_space=pltpu.SMEM),
                  pl.BlockSpec(memory_space=pl.ANY),
                  pl.BlockSpec((tm, D), lambda i: (i, 0)),
                  pl.BlockSpec((1, 1, D), lambda i: (i // per_b, 0, 0)),
                  pl.BlockSpec((tm, 2), lambda i: (i, 0)),
                  pl.BlockSpec((1, D), lambda i: (0, 0))],
        out_specs=pl.BlockSpec((tm, D), lambda i: (i, 0)),
        out_shape=jax.ShapeDtypeStruct((T, D), F32),
        scratch_shapes=[pltpu.VMEM((2, 2 * tm * ROW_SLAB, 128), U32),
                        pltpu.SemaphoreType.DMA((2,))],
        compiler_params=pltpu.CompilerParams(
            dimension_semantics=("arbitrary",), vmem_limit_bytes=VMEM_LIMIT),
        name="combine",
    )(pos_tiles, pos_tiles, ys, x1, g2, w_tok, final_g)


def _rel_bucket_table():
    n = jnp.arange(BIAS_LUT, dtype=I32)
    max_exact = N_BUCKETS // 2
    nf = jnp.maximum(n, 1).astype(F32)
    large = max_exact + (jnp.log(nf / max_exact) / math.log(MAX_DISTANCE / max_exact)
                         * (N_BUCKETS - max_exact)).astype(I32)
    large = jnp.minimum(large, N_BUCKETS - 1)
    return jnp.where(n < max_exact, n, large)


def _route_plan(eid, n_tiles, tm):
    two, T = eid.shape
    e_flat = eid.reshape(-1)
    onehot = (e_flat[:, None] == jnp.arange(N_EXPERTS, dtype=I32)[None, :]).astype(I32)
    csum = jnp.cumsum(onehot, axis=0)
    rank = jnp.sum((csum - onehot) * onehot, axis=1)
    counts = csum[-1]
    ptiles = (counts + tm - 1) // tm
    tend = jnp.cumsum(ptiles)
    tstart = tend - ptiles
    slot = jnp.sum(onehot * tstart[None, :], axis=1) * tm + rank
    total = tend[-1]
    tile_ids = jnp.arange(n_tiles, dtype=I32)
    tile_valid = (tile_ids < total).astype(I32)
    tile_expert = jnp.sum((tile_ids[:, None] >= tend[None, :]).astype(I32), axis=1)
    last_expert = jnp.sum((total - 1 >= tend).astype(I32))
    tile_expert = jnp.minimum(tile_expert, last_expert).astype(I32)
    partial_last = jnp.any((tile_ids[:, None] == tend[None, :] - 1)
                           & (ptiles[None, :] > 0) & (counts[None, :] % tm != 0), axis=1)
    tile_fill = (partial_last | (tile_ids >= total)).astype(I32)
    experts = jnp.arange(N_EXPERTS, dtype=I32)
    later = (experts[None, :] > tile_expert[:, None]) & (ptiles[None, :] > 0)
    tile_next_expert = jnp.min(jnp.where(later, experts[None, :], N_EXPERTS), axis=1)
    tile_next_expert = jnp.where(tile_next_expert == N_EXPERTS, -1, tile_next_expert).astype(I32)
    return slot.astype(I32), tile_expert, tile_valid, tile_fill, tile_next_expert


def kernel(x, c, positions, rel_bias, ada_w, ada_b, norm1_g, w_in, lambda_q1, lambda_k1, lambda_q2,
           lambda_k2, subln_g, conv_w, w_out, norm2_g, router_group_w, router_group_b,
           router_expert_w, router_expert_b, expert_w_gate, expert_w_up, expert_w_down, final_g):
    B, S, D = x.shape
    T = B * S
    l = 0

    ada = _ada(c.reshape(B, D, 1), ada_w[l], ada_b[l].reshape(1, -1))
    sh1, sc1, g1, sh2, sc2, g2 = jnp.split(ada, 6, axis=-1)

    qt, k, vt, conv = _inproj(x, sc1, sh1, norm1_g[l].reshape(1, D), w_in[l], conv_w[l])
    lut = (rel_bias.astype(F32)[_rel_bucket_table(), :].T * LOG2E).reshape(N_DIFF_HEADS, 2, BIAS_LUT)
    lam_params = jnp.stack([lambda_q1[l], lambda_k1[l], lambda_q2[l], lambda_k2[l]]).astype(F32)
    attn = _attention(qt, k, vt, positions, lut, lam_params, subln_g[l].reshape(V_HEAD_DIM, 1))

    pad_rows = ROUTER_ROWS - N_GROUPS - N_EXPERTS
    wr_t = jnp.concatenate([router_group_w[l], router_expert_w[l], jnp.zeros((D, pad_rows), F32)],
                           axis=1).T.astype(BF16)
    rb = jnp.concatenate([router_group_b[l], router_expert_b[l],
                          jnp.zeros((pad_rows,), F32)]).reshape(ROUTER_ROWS, 1)
    x1, hp, ri, rw = _outproj(attn, conv, x, g1, sc2, sh2, norm2_g[l].reshape(1, D),
                              w_out[l].astype(BF16), wr_t, rb)

    eid = ri[:, 0:2, :].transpose(1, 0, 2).reshape(2, T)
    n_tiles = 2 * T // MOE_TM + N_EXPERTS
    slot, tile_expert, tile_valid, tile_fill, tile_next = _route_plan(eid, n_tiles, MOE_TM)
    nct = T // COMB_TM
    pos = (slot * ROW_SLAB).reshape(2, nct, 1, COMB_TM)
    pos_tiles = jnp.concatenate([pos[0], pos[1]], axis=2)

    xs = _dispatch(tile_fill, pos_tiles, hp.reshape(T * ROW_SLAB, 128), n_tiles)
    ys = _moe(tile_expert, tile_valid, tile_next, xs, expert_w_gate[l], expert_w_up[l],
              expert_w_down[l])

    w_tok = rw[:, 0:2, :].transpose(0, 2, 1).reshape(T, 2)
    out = _combine(pos_tiles, ys, x1.reshape(T, D), g2, w_tok, final_g.reshape(1, D), S)
    return out.reshape(B, S, D)
```
